```python
import math
import jax, jax.numpy as jnp
from jax import lax
import numpy as np

D_MODEL = 1024
BATCH = 16
SEQ = 2048
DEPTH = 1

N_Q_HEADS = 16
N_KV_HEADS = 2
HEAD_DIM = 64
WINDOW = 128
ROPE_THETA = 10000.0
Q_WIDTH = N_Q_HEADS * HEAD_DIM
KV_WIDTH = N_KV_HEADS * HEAD_DIM
SSM_GROUP = 16
SSM_GROUPS = 32
SSM_WIDTH = SSM_GROUP * SSM_GROUPS
SSM_STATE = 64
DT_MIN = 0.001
DT_MAX = 0.1
IN_WIDTH = Q_WIDTH + 2 * KV_WIDTH + SSM_WIDTH + 2 * D_MODEL
N_MEM = 256
N_CROSS_HEADS = 4
CROSS_HEAD_DIM = D_MODEL // N_CROSS_HEADS
N_EXPERTS = 32
TOP_K = 4
D_EXPERT = D_MODEL
SWIGLU_ALPHA = 1.702
SWIGLU_LIMIT = 7.0
MOE_BLOCK = 256
LN_EPS = 1e-5
DEEPNORM_ALPHA = (2 * DEPTH) ** 0.25
DEEPNORM_BETA = (8 * DEPTH) ** -0.25

kernel_name = "hybrid_swa_s5_moe_deepnorm"


def layer_norm(x, g, b):
    xf = x.astype(jnp.float32)
    mu = jnp.mean(xf, axis=-1, keepdims=True)
    var = jnp.mean(jnp.square(xf - mu), axis=-1, keepdims=True)
    return ((xf - mu) * lax.rsqrt(var + LN_EPS)).astype(x.dtype) * g + b


def rope(t, positions):
    half = HEAD_DIM // 2
    inv_freq = jnp.power(ROPE_THETA, -jnp.arange(half, dtype=jnp.float32) / half)
    ang = positions.astype(jnp.float32)[..., None] * inv_freq
    cos = jnp.cos(ang)[:, :, None, :]
    sin = jnp.sin(ang)[:, :, None, :]
    t1 = t[..., :half].astype(jnp.float32)
    t2 = t[..., half:].astype(jnp.float32)
    out = jnp.concatenate([t1 * cos - t2 * sin, t2 * cos + t1 * sin], axis=-1)
    return out.astype(t.dtype)


def sliding_window_gqa(q, k, v, sinks):
    bsz, seq = q.shape[0], q.shape[1]
    nb = seq // WINDOW
    rep = N_Q_HEADS // N_KV_HEADS
    qb = q.reshape(bsz, nb, WINDOW, N_KV_HEADS, rep, HEAD_DIM)

    def band(t):
        tb = t.reshape(bsz, nb, WINDOW, N_KV_HEADS, HEAD_DIM)
        prev = jnp.pad(tb, ((0, 0), (1, 0), (0, 0), (0, 0), (0, 0)))[:, :-1]
        return jnp.concatenate([prev, tb], axis=2)

    kb, vb = band(k), band(v)
    scores = jnp.einsum('bnqkrd,bnckd->bnkrqc', qb, kb).astype(jnp.float32) * (HEAD_DIM ** -0.5)
    qi = jnp.arange(WINDOW)[:, None]
    ci = jnp.arange(2 * WINDOW)[None, :]
    diff = qi + WINDOW - ci
    local = (diff >= 0) & (diff < WINDOW)
    kpos = jnp.arange(nb)[:, None] * WINDOW + ci - WINDOW
    mask = local[None] & (kpos >= 0)[:, None, :]
    scores = jnp.where(mask[None, :, None, None], scores, -jnp.inf)
    sink = sinks.astype(jnp.float32).reshape(N_KV_HEADS, rep)[None, None, :, :, None, None]
    m = jnp.maximum(jnp.max(scores, axis=-1, keepdims=True), sink)
    p = jnp.exp(scores - m)
    w = p / (jnp.sum(p, axis=-1, keepdims=True) + jnp.exp(sink - m))
    out = jnp.einsum('bnkrqc,bnckd->bnqkrd', w.astype(v.dtype), vb)
    return out.reshape(bsz, seq, Q_WIDTH)


def s5_ssm(u, lam_re, lam_im, log_dt, b_re, b_im, c_re, c_im, d_skip):
    bsz, seq = u.shape[0], u.shape[1]
    uf = u.astype(jnp.float32).reshape(bsz, seq, SSM_GROUPS, SSM_GROUP)
    lr = lam_re.astype(jnp.float32)
    li = lam_im.astype(jnp.float32)
    dt = jnp.exp(log_dt.astype(jnp.float32))[:, None]
    mag = jnp.exp(lr * dt)
    abar_re = mag * jnp.cos(li * dt)
    abar_im = mag * jnp.sin(li * dt)
    den = lr * lr + li * li
    f_re = ((abar_re - 1.0) * lr + abar_im * li) / den
    f_im = (abar_im * lr - (abar_re - 1.0) * li) / den
    br = b_re.astype(jnp.float32)
    bi = b_im.astype(jnp.float32)
    bbar_re = f_re[..., None] * br - f_im[..., None] * bi
    bbar_im = f_re[..., None] * bi + f_im[..., None] * br
    bu_re = jnp.einsum('bsgh,gph->bsgp', uf, bbar_re)
    bu_im = jnp.einsum('bsgh,gph->bsgp', uf, bbar_im)
    a_re = jnp.broadcast_to(abar_re, (1, seq, SSM_GROUPS, SSM_STATE))
    a_im = jnp.broadcast_to(abar_im, (1, seq, SSM_GROUPS, SSM_STATE))

    def combine(e1, e2):
        a1r, a1i, b1r, b1i = e1
        a2r, a2i, b2r, b2i = e2
        return (a2r * a1r - a2i * a1i,
                a2r * a1i + a2i * a1r,
                a2r * b1r - a2i * b1i + b2r,
                a2r * b1i + a2i * b1r + b2i)

    _, _, h_re, h_im = lax.associative_scan(combine, (a_re, a_im, bu_re, bu_im), axis=1)
    y = (jnp.einsum('bsgp,ghp->bsgh', h_re, c_re.astype(jnp.float32))
         - jnp.einsum('bsgp,ghp->bsgh', h_im, c_im.astype(jnp.float32)))
    y = y + d_skip.astype(jnp.float32).reshape(SSM_GROUPS, SSM_GROUP) * uf
    return y.reshape(bsz, seq, SSM_WIDTH).astype(u.dtype)


def memory_cross_attention(x, mem, wq, wk, wv, wo):
    bsz, seq = x.shape[0], x.shape[1]
    n_mem = mem.shape[1]
    q = (x @ wq).reshape(bsz, seq, N_CROSS_HEADS, CROSS_HEAD_DIM)
    k = (mem @ wk).reshape(bsz, n_mem, N_CROSS_HEADS, CROSS_HEAD_DIM)
    v = (mem @ wv).reshape(bsz, n_mem, N_CROSS_HEADS, CROSS_HEAD_DIM)
    s = jnp.einsum('bshd,bmhd->bhsm', q, k).astype(jnp.float32) * (CROSS_HEAD_DIM ** -0.5)
    w = jax.nn.softmax(s, axis=-1).astype(x.dtype)
    o = jnp.einsum('bhsm,bmhd->bshd', w, v).reshape(bsz, seq, D_MODEL)
    return o @ wo


def clamped_swiglu(gu):
    g = jnp.minimum(gu[..., ::2], SWIGLU_LIMIT)
    lin = jnp.clip(gu[..., 1::2], -SWIGLU_LIMIT, SWIGLU_LIMIT)
    return g * jax.nn.sigmoid(SWIGLU_ALPHA * g) * (lin + 1.0)


def moe_ffn(h, w_router, b_router, w_gate_up, b_gate_up, w_down, b_down):
    n_tok = h.shape[0]
    logits = (h @ w_router + b_router).astype(jnp.float32)
    top_val, top_idx = lax.top_k(logits, TOP_K)
    gates = jax.nn.softmax(top_val, axis=-1).astype(h.dtype)
    n_assign = n_tok * TOP_K
    flat_e = top_idx.reshape(-1)
    flat_tok = jnp.repeat(jnp.arange(n_tok, dtype=jnp.int32), TOP_K)
    flat_gate = gates.reshape(-1)
    order = jnp.argsort(flat_e)
    sorted_e = flat_e[order]
    counts = jnp.bincount(flat_e, length=N_EXPERTS)
    padded = (counts + MOE_BLOCK - 1) // MOE_BLOCK * MOE_BLOCK
    ends_pad = jnp.cumsum(padded)
    start_pad = ends_pad - padded
    start = jnp.cumsum(counts) - counts
    dest = start_pad[sorted_e] + jnp.arange(n_assign, dtype=jnp.int32) - start[sorted_e]
    n_blocks = -(-n_assign // MOE_BLOCK) + N_EXPERTS
    n_slots = n_blocks * MOE_BLOCK
    slot_tok = jnp.zeros((n_slots,), jnp.int32).at[dest].set(flat_tok[order])
    slot_gate = jnp.zeros((n_slots,), h.dtype).at[dest].set(flat_gate[order])
    block_start = jnp.arange(n_blocks, dtype=ends_pad.dtype) * MOE_BLOCK
    block_expert = jnp.minimum(jnp.searchsorted(ends_pad, block_start, side='right'), N_EXPERTS - 1)
    xb = h[slot_tok].reshape(n_blocks, MOE_BLOCK, h.shape[1])

    def run_block(args):
        xblk, e = args
        gu = xblk @ w_gate_up[e] + b_gate_up[e]
        return clamped_swiglu(gu) @ w_down[e] + b_down[e]

    yb = lax.map(run_block, (xb, block_expert))
    y = yb.reshape(n_slots, h.shape[1]) * slot_gate[:, None]
    return jax.ops.segment_sum(y, slot_tok, num_segments=n_tok)


def setup_inputs(seed: int = 0) -> dict:
    key = jax.random.key(seed)
    ks = iter(jax.random.split(key, 40))
    L = DEPTH

    def nrm(shape, scale):
        return jax.random.normal(next(ks), shape, jnp.float32) * scale

    x = nrm((BATCH, SEQ, D_MODEL), 1.0)
    mem = nrm((BATCH, N_MEM, D_MODEL), 1.0)
    offset = jax.random.randint(next(ks), (BATCH, 1), 0, 1024, dtype=jnp.int32)
    positions = (jnp.arange(SEQ, dtype=jnp.int32)[None, :] + offset).astype(jnp.int32)

    w_in = nrm((L, D_MODEL, IN_WIDTH), D_MODEL ** -0.5)
    sinks = nrm((L, N_Q_HEADS), 0.5)
    w_attn_o = nrm((L, Q_WIDTH, D_MODEL), Q_WIDTH ** -0.5)
    lam_re = -0.5 + nrm((L, SSM_GROUPS, SSM_STATE), 0.01)
    lam_im = (math.pi * jnp.arange(SSM_STATE, dtype=jnp.float32))[None, None, :] + nrm((L, SSM_GROUPS, SSM_STATE), 0.01)
    log_dt = jax.random.uniform(next(ks), (L, SSM_GROUPS), jnp.float32, math.log(DT_MIN), math.log(DT_MAX))
    b_re = nrm((L, SSM_GROUPS, SSM_STATE, SSM_GROUP), (2 * SSM_GROUP) ** -0.5)
    b_im = nrm((L, SSM_GROUPS, SSM_STATE, SSM_GROUP), (2 * SSM_GROUP) ** -0.5)
    c_re = nrm((L, SSM_GROUPS, SSM_GROUP, SSM_STATE), (2 * SSM_STATE) ** -0.5)
    c_im = nrm((L, SSM_GROUPS, SSM_GROUP, SSM_STATE), (2 * SSM_STATE) ** -0.5)
    d_skip = nrm((L, SSM_WIDTH), 1.0)
    w_glu_a = nrm((L, SSM_WIDTH, D_MODEL), SSM_WIDTH ** -0.5)
    w_glu_b = nrm((L, SSM_WIDTH, D_MODEL), SSM_WIDTH ** -0.5)
    w_out = nrm((L, D_MODEL, D_MODEL), D_MODEL ** -0.5 * DEEPNORM_BETA)
    ln1_g = 1.0 + nrm((L, D_MODEL), 0.05)
    ln1_b = nrm((L, D_MODEL), 0.02)

    wq_c = nrm((L, D_MODEL, D_MODEL), D_MODEL ** -0.5)
    wk_c = nrm((L, D_MODEL, D_MODEL), D_MODEL ** -0.5)
    wv_c = nrm((L, D_MODEL, D_MODEL), D_MODEL ** -0.5)
    wo_c = nrm((L, D_MODEL, D_MODEL), D_MODEL ** -0.5 * DEEPNORM_BETA)
    ln2_g = 1.0 + nrm((L, D_MODEL), 0.05)
    ln2_b = nrm((L, D_MODEL), 0.02)

    w_router = nrm((L, D_MODEL, N_EXPERTS), D_MODEL ** -0.5)
    b_router = nrm((L, N_EXPERTS), 0.01)
    w_gate_up = nrm((L, N_EXPERTS, D_MODEL, 2 * D_EXPERT), D_MODEL ** -0.5)
    b_gate_up = nrm((L, N_EXPERTS, 2 * D_EXPERT), 0.02)
    w_down = nrm((L, N_EXPERTS, D_EXPERT, D_MODEL), D_EXPERT ** -0.5 * DEEPNORM_BETA)
    b_down = nrm((L, N_EXPERTS, D_MODEL), 0.02)
    ln3_g = 1.0 + nrm((L, D_MODEL), 0.05)
    ln3_b = nrm((L, D_MODEL), 0.02)

    return {"x": x, "mem": mem, "positions": positions,
            "w_in": w_in, "sinks": sinks, "w_attn_o": w_attn_o,
            "lam_re": lam_re, "lam_im": lam_im, "log_dt": log_dt,
            "b_re": b_re, "b_im": b_im, "c_re": c_re, "c_im": c_im, "d_skip": d_skip,
            "w_glu_a": w_glu_a, "w_glu_b": w_glu_b, "w_out": w_out,
            "ln1_g": ln1_g, "ln1_b": ln1_b,
            "wq_c": wq_c, "wk_c": wk_c, "wv_c": wv_c, "wo_c": wo_c,
            "ln2_g": ln2_g, "ln2_b": ln2_b,
            "w_router": w_router, "b_router": b_router,
            "w_gate_up": w_gate_up, "b_gate_up": b_gate_up,
            "w_down": w_down, "b_down": b_down,
            "ln3_g": ln3_g, "ln3_b": ln3_b}


def reference(x, mem, positions, w_in, sinks, w_attn_o, lam_re, lam_im, log_dt,
              b_re, b_im, c_re, c_im, d_skip, w_glu_a, w_glu_b, w_out, ln1_g, ln1_b,
              wq_c, wk_c, wv_c, wo_c, ln2_g, ln2_b, w_router, b_router,
              w_gate_up, b_gate_up, w_down, b_down, ln3_g, ln3_b):
    bsz, seq, d = x.shape
    o_k = Q_WIDTH
    o_v = o_k + KV_WIDTH
    o_s = o_v + KV_WIDTH
    o_ga = o_s + SSM_WIDTH
    o_gs = o_ga + D_MODEL
    for l in range(DEPTH):
        proj = x @ w_in[l]
        q = rope(proj[..., :o_k].reshape(bsz, seq, N_Q_HEADS, HEAD_DIM), positions)
        k = rope(proj[..., o_k:o_v].reshape(bsz, seq, N_KV_HEADS, HEAD_DIM), positions)
        v = proj[..., o_v:o_s].reshape(bsz, seq, N_KV_HEADS, HEAD_DIM)
        u_ssm = proj[..., o_s:o_ga]
        gate_a = jax.nn.sigmoid(proj[..., o_ga:o_gs])
        gate_s = jax.nn.sigmoid(proj[..., o_gs:])

        attn_out = sliding_window_gqa(q, k, v, sinks[l]) @ w_attn_o[l]

        y_ssm = s5_ssm(u_ssm, lam_re[l], lam_im[l], log_dt[l], b_re[l], b_im[l],
                       c_re[l], c_im[l], d_skip[l])
        z = jax.nn.gelu(y_ssm)
        ssm_out = (z @ w_glu_a[l]) * jax.nn.sigmoid(z @ w_glu_b[l])

        mixed = (gate_a * attn_out + gate_s * ssm_out) @ w_out[l]
        x = layer_norm(DEEPNORM_ALPHA * x + mixed, ln1_g[l], ln1_b[l])

        cross = memory_cross_attention(x, mem, wq_c[l], wk_c[l], wv_c[l], wo_c[l])
        x = layer_norm(DEEPNORM_ALPHA * x + cross, ln2_g[l], ln2_b[l])

        ffn = moe_ffn(x.reshape(bsz * seq, d), w_router[l], b_router[l], w_gate_up[l],
                      b_gate_up[l], w_down[l], b_down[l]).reshape(bsz, seq, d)
        x = layer_norm(DEEPNORM_ALPHA * x + ffn, ln3_g[l], ln3_b[l])
    return x
```

```python
import functools

import jax
import jax.numpy as jnp
from jax import lax
from jax.experimental import pallas as pl
from jax.experimental.pallas import tpu as pltpu

N_Q_HEADS = 16
N_KV_HEADS = 2
HEAD_DIM = 64
WINDOW = 128
ROPE_THETA = 10000.0
SSM_GROUP = 16
SSM_GROUPS = 32
SSM_STATE = 64
N_CROSS_HEADS = 4
N_EXPERTS = 32
TOP_K = 4
SWIGLU_ALPHA = 1.702
SWIGLU_LIMIT = 7.0
MOE_BLOCK = 256
LN_EPS = 1e-5

LANES = 128
VMEM_LIMIT_BYTES = 56 * 1024 * 1024

NEG_BIG = -1e30
BF16 = jnp.bfloat16
F32 = jnp.float32
HI = lax.Precision.HIGHEST


def _cparams(n_axes=1):
    return pltpu.CompilerParams(dimension_semantics=("arbitrary",) * n_axes,
                                vmem_limit_bytes=VMEM_LIMIT_BYTES)


def _full_spec(shape):
    n = len(shape)
    return pl.BlockSpec(shape, lambda *_: (0,) * n)


def _dot(a, b):
    return jnp.dot(a, b, preferred_element_type=F32)


def _dot_nt(a, b):
    return lax.dot_general(a, b, (((1,), (1,)), ((), ())), preferred_element_type=F32)


def _layer_norm(y, g, b):
    mu = jnp.mean(y, axis=-1, keepdims=True)
    d = y - mu
    var = jnp.mean(d * d, axis=-1, keepdims=True)
    return d * lax.rsqrt(var + LN_EPS) * g + b


def _rope(t, cos, sin_signed, first_half):
    half = HEAD_DIM // 2
    partner = jnp.where(first_half, pltpu.roll(t, LANES - half, axis=1), pltpu.roll(t, half, axis=1))
    return t * cos + partner * sin_signed


def _inproj_kernel(x_ref, pos_ref, invf_ref, wq_ref, wk_ref, wv_ref, wut_ref, wga_ref, wgs_ref,
                   q2_ref, k_ref, v_ref, ut_ref, ga_ref, gs_ref):
    tm = x_ref.shape[0]
    xb = x_ref[...].astype(BF16)
    ang = pos_ref[...].astype(F32) * invf_ref[...]
    cos = jnp.cos(ang)
    sin = jnp.sin(ang)
    first_half = (lax.broadcasted_iota(jnp.int32, (tm, LANES), 1) % HEAD_DIM) < (HEAD_DIM // 2)
    first_half_w = (lax.broadcasted_iota(jnp.int32, (WINDOW, LANES), 1) % HEAD_DIM) < (HEAD_DIM // 2)
    sin_signed = jnp.where(first_half, -sin, sin)

    q = _dot(xb, wq_ref[...])
    n_rep = q.shape[1] // LANES
    scale = HEAD_DIM ** -0.5
    for j in range(tm // WINDOW):
        rows = slice(j * WINDOW, (j + 1) * WINDOW)
        for r in range(n_rep):
            t = _rope(q[rows, r * LANES:(r + 1) * LANES], cos[rows], sin_signed[rows], first_half_w)
            base = (j * n_rep + r) * WINDOW
            q2_ref[base:base + WINDOW, :] = (t * scale).astype(BF16)
    k_ref[...] = _rope(_dot(xb, wk_ref[...]), cos, sin_signed, first_half).astype(BF16)
    v_ref[...] = _dot(xb, wv_ref[...]).astype(BF16)
    for j in range(tm // LANES):
        ut_ref[j] = _dot_nt(wut_ref[...], xb[j * LANES:(j + 1) * LANES, :])
    ga_ref[...] = jax.nn.sigmoid(_dot(xb, wga_ref[...])).astype(BF16)
    gs_ref[...] = jax.nn.sigmoid(_dot(xb, wgs_ref[...])).astype(BF16)


def _inproj(x2d, pos2d, invf, wq, wk, wv, wut, wga, wgs, tm):
    T, D = x2d.shape
    n_rep = wq.shape[1] // LANES
    ssm_w = wut.shape[0]
    row = lambda i: (i, 0)
    out_shape = (
        jax.ShapeDtypeStruct((T * n_rep, LANES), BF16),
        jax.ShapeDtypeStruct((T, LANES), BF16),
        jax.ShapeDtypeStruct((T, LANES), BF16),
        jax.ShapeDtypeStruct((T // LANES, ssm_w, LANES), F32),
        jax.ShapeDtypeStruct((T, D), BF16),
        jax.ShapeDtypeStruct((T, D), BF16),
    )
    in_specs = [pl.BlockSpec((tm, D), row), pl.BlockSpec((tm, 1), row), _full_spec(invf.shape),
                _full_spec(wq.shape), _full_spec(wk.shape), _full_spec(wv.shape),
                _full_spec(wut.shape), _full_spec(wga.shape), _full_spec(wgs.shape)]
    out_specs = (pl.BlockSpec((tm * n_rep, LANES), row), pl.BlockSpec((tm, LANES), row),
                 pl.BlockSpec((tm, LANES), row),
                 pl.BlockSpec((tm // LANES, ssm_w, LANES), lambda i: (i, 0, 0)),
                 pl.BlockSpec((tm, D), row), pl.BlockSpec((tm, D), row))
    return pl.pallas_call(_inproj_kernel, name="inproj", grid=(T // tm,), in_specs=in_specs, out_specs=out_specs,
                          out_shape=out_shape, compiler_params=_cparams())(
        x2d, pos2d, invf, wq, wk, wv, wut, wga, wgs)


def _swa_kernel(q2_ref, kc_ref, kp_ref, vc_ref, vp_ref, sink_ref, wo_ref, ga_ref, out_ref, cat_ref):
    i = pl.program_id(1)
    tq = kc_ref.shape[0]
    n_sub = tq // WINDOW
    rep = N_Q_HEADS // N_KV_HEADS
    rows_all = rep * WINDOW
    kfull = jnp.concatenate([kp_ref[...], kc_ref[...]], axis=0)
    vfull = jnp.concatenate([vp_ref[...], vc_ref[...]], axis=0)
    lane = lax.broadcasted_iota(jnp.int32, (2 * WINDOW, LANES), 1)
    qi = lax.broadcasted_iota(jnp.int32, (rows_all, 2 * WINDOW), 0) % WINDOW
    ci = lax.broadcasted_iota(jnp.int32, (rows_all, 2 * WINDOW), 1)
    local = (ci > qi) & (ci <= qi + WINDOW)
    out_lane = lax.broadcasted_iota(jnp.int32, (rows_all, LANES), 1)
    for j in range(n_sub):
        qs = q2_ref[j * rows_all:(j + 1) * rows_all, :]
        kb = kfull[j * WINDOW:(j + 2) * WINDOW, :]
        vb = vfull[j * WINDOW:(j + 2) * WINDOW, :]
        mask = local
        if j == 0:
            mask = mask & ((ci >= WINDOW) | (i > 0))
        o = None
        for g in range(N_KV_HEADS):
            in_group = (lane >= g * HEAD_DIM) & (lane < (g + 1) * HEAD_DIM)
            kg = jnp.where(in_group, kb, jnp.zeros_like(kb))
            s = _dot_nt(qs, kg)
            s = jnp.where(mask, s, NEG_BIG)
            sink = sink_ref[g]
            m = jnp.maximum(jnp.max(s, axis=-1, keepdims=True), sink)
            p = jnp.exp(s - m)
            denom = jnp.sum(p, axis=-1, keepdims=True) + jnp.exp(sink - m)
            w = (p / denom).astype(BF16)
            og = _dot(w, vb)
            o = og if o is None else jnp.where(out_lane < g * HEAD_DIM, o, og)
        ob = o.astype(BF16)
        for r in range(rep):
            cat_ref[j * WINDOW:(j + 1) * WINDOW, r * LANES:(r + 1) * LANES] = ob[r * WINDOW:(r + 1) * WINDOW, :]
    attn = _dot(cat_ref[...], wo_ref[...])
    out_ref[...] = (attn * ga_ref[...].astype(F32)).astype(BF16)


def _swa(q2, k, v, sink_cols, wo, ga, B, S, tq):
    T, D = ga.shape
    rep = N_Q_HEADS // N_KV_HEADS
    n_i = S // tq
    n_sub = tq // WINDOW
    in_specs = [
        pl.BlockSpec((tq * rep, LANES), lambda b, i: (b * n_i + i, 0)),
        pl.BlockSpec((tq, LANES), lambda b, i: (b * n_i + i, 0)),
        pl.BlockSpec((WINDOW, LANES), lambda b, i: (b * (S // WINDOW) + jnp.maximum(i * n_sub - 1, 0), 0)),
        pl.BlockSpec((tq, LANES), lambda b, i: (b * n_i + i, 0)),
        pl.BlockSpec((WINDOW, LANES), lambda b, i: (b * (S // WINDOW) + jnp.maximum(i * n_sub - 1, 0), 0)),
        _full_spec(sink_cols.shape), _full_spec(wo.shape),
        pl.BlockSpec((tq, D), lambda b, i: (b * n_i + i, 0)),
    ]
    return pl.pallas_call(
        _swa_kernel, name="swa", grid=(B, n_i), in_specs=in_specs,
        out_specs=pl.BlockSpec((tq, D), lambda b, i: (b * n_i + i, 0)),
        out_shape=jax.ShapeDtypeStruct((T, D), BF16),
        scratch_shapes=[pltpu.VMEM((tq, D), BF16)],
        compiler_params=_cparams(2))(q2, k, k, v, v, sink_cols, wo, ga)


def _ssm_prep_kernel(lam_row_ref, lam_col_ref, dt_ref, bt_a_ref, bt_b_ref, c_re_ref, c_im_ref,
                     ct_a_ref, ct_b_ref, dsk_ref, toep_ref, wst_ref, wout_ref, apow_ref, kt_ref):
    L = LANES
    P = SSM_STATE
    dt = jnp.exp(dt_ref[0])
    lr2 = lam_row_ref[0, 0:1, :]
    li2 = lam_row_ref[0, 1:2, :]
    mag = jnp.exp(lr2 * dt)
    ar = mag * jnp.cos(li2 * dt)
    ai = mag * jnp.sin(li2 * dt)
    den = lr2 * lr2 + li2 * li2
    f_re = ((ar - 1.0) * lr2 + ai * li2) / den
    f_im = (ai * lr2 - (ar - 1.0) * li2) / den
    bt_a = bt_a_ref[0]
    bt_b = bt_b_ref[0]
    bb_a = f_re * bt_a + f_im * bt_b
    bb_b = f_re * bt_b - f_im * bt_a

    tau_rev = (L - 1 - lax.broadcasted_iota(jnp.int32, (L, 2 * P), 0)).astype(F32)
    g_mag = jnp.exp(lr2 * dt * tau_rev)
    g_re = g_mag * jnp.cos(li2 * dt * tau_rev)
    g_im = g_mag * jnp.sin(li2 * dt * tau_rev)
    for i in range(SSM_GROUP):
        wst_ref[0, i * L:(i + 1) * L, :] = (g_re * bb_a[i:i + 1, :] + g_im * bb_b[i:i + 1, :]).astype(BF16)

    lane2 = lax.broadcasted_iota(jnp.int32, (1, 2 * P), 1)
    for kk in range(4):
        n = float(L * (1 << kk))
        pm = jnp.exp(lr2 * dt * n)
        p_re = pm * jnp.cos(li2 * dt * n)
        p_im = pm * jnp.sin(li2 * dt * n)
        apow_ref[0, 2 * kk:2 * kk + 1, :] = p_re
        apow_ref[0, 2 * kk + 1:2 * kk + 2, :] = jnp.where(lane2 < P, -p_im, p_im)

    lrc = lam_col_ref[0, :, 0:1]
    lic = lam_col_ref[0, :, 1:2]
    tau = lax.broadcasted_iota(jnp.int32, (2 * P, L), 1).astype(F32)
    e0_mag = jnp.exp(lrc * dt * tau)
    e0_re = e0_mag * jnp.cos(lic * dt * tau)
    e0_im = e0_mag * jnp.sin(lic * dt * tau)
    e1_mag = jnp.exp(lrc * dt * (tau + 1.0))
    e1_re = e1_mag * jnp.cos(lic * dt * (tau + 1.0))
    e1_im = e1_mag * jnp.sin(lic * dt * (tau + 1.0))

    ct_a = ct_a_ref[0]
    ct_b = ct_b_ref[0]
    for o in range(SSM_GROUP):
        wout_ref[0, :, o * L:(o + 1) * L] = (ct_a[:, o:o + 1] * e1_re + ct_b[:, o:o + 1] * e1_im).astype(BF16)

    c_re = c_re_ref[0]
    c_im = c_im_ref[0]
    bb_re = bb_a[:, :P]
    bb_im = bb_a[:, P:]
    m_re = (c_re[:, None, :] * bb_re[None, :, :] - c_im[:, None, :] * bb_im[None, :, :])
    m_im = (c_re[:, None, :] * bb_im[None, :, :] + c_im[:, None, :] * bb_re[None, :, :])
    m_re = m_re.reshape(SSM_GROUP * SSM_GROUP, P)
    m_im = m_im.reshape(SSM_GROUP * SSM_GROUP, P)
    kt = (jnp.dot(m_re, e0_re[:P, :], precision=HI, preferred_element_type=F32)
          - jnp.dot(m_im, e0_im[:P, :], precision=HI, preferred_element_type=F32))
    rowi = lax.broadcasted_iota(jnp.int32, (SSM_GROUP * SSM_GROUP, L), 0)
    coli = lax.broadcasted_iota(jnp.int32, (SSM_GROUP * SSM_GROUP, L), 1)
    dsk = dsk_ref[0]
    kt_ref[...] = kt + jnp.where((coli == 0) & ((rowi // SSM_GROUP) == (rowi % SSM_GROUP)), dsk, 0.0)

    cc = lax.broadcasted_iota(jnp.int32, (L, L), 0)
    cp = lax.broadcasted_iota(jnp.int32, (L, L), 1)
    causal = cp >= cc
    for o in range(SSM_GROUP):
        def body(i, carry, o=o):
            kv = kt_ref[pl.ds(o * SSM_GROUP + i, 1), :]
            blk = pltpu.roll(jnp.broadcast_to(kv, (L, L)), 0, axis=1, stride=1, stride_axis=0)
            blk = jnp.where(causal, blk, 0.0)
            toep_ref[0, pl.ds(pl.multiple_of(i * L, L), L), o * L:(o + 1) * L] = blk.astype(BF16)
            return carry
        lax.fori_loop(0, SSM_GROUP, body, 0)


def _ssm_prep(lam_row, lam_col, log_dt, bt_a, bt_b, c_re, c_im, ct_a, ct_b, dsk):
    G = SSM_GROUPS
    KW = SSM_GROUP * LANES
    g3 = lambda g: (g, 0, 0)
    ins = [lam_row, lam_col, log_dt, bt_a, bt_b, c_re, c_im, ct_a, ct_b, dsk]
    in_specs = [pl.BlockSpec((1,) + a.shape[1:], g3) for a in ins]
    out_shape = (jax.ShapeDtypeStruct((G, KW, KW), BF16),
                 jax.ShapeDtypeStruct((G, KW, 2 * SSM_STATE), BF16),
                 jax.ShapeDtypeStruct((G, 2 * SSM_STATE, KW), BF16),
                 jax.ShapeDtypeStruct((G, 8, 2 * SSM_STATE), F32))
    out_specs = tuple(pl.BlockSpec((1,) + s.shape[1:], g3) for s in out_shape)
    return pl.pallas_call(_ssm_prep_kernel, name="ssm_prep", grid=(G,), in_specs=in_specs, out_specs=out_specs,
                          out_shape=out_shape,
                          scratch_shapes=[pltpu.VMEM((SSM_GROUP * SSM_GROUP, LANES), F32)],
                          compiler_params=_cparams())(*ins)


def _ssm_scan_kernel(ut_ref, toep_ref, wst_ref, wout_ref, apow_ref, yt_ref, *, n_chunks):
    nb = ut_ref.shape[0]
    P = SSM_STATE
    lhs = jnp.concatenate([ut_ref[:, i, :] for i in range(SSM_GROUP)], axis=1).astype(BF16)
    y = _dot(lhs, toep_ref[0])
    st = _dot(lhs, wst_ref[0])
    srow = lax.broadcasted_iota(jnp.int32, (nb, 2 * P), 0) % n_chunks
    kk = 0
    while (1 << kk) < n_chunks:
        d = 1 << kk
        pa = apow_ref[0, 2 * kk:2 * kk + 1, :]
        pb = apow_ref[0, 2 * kk + 1:2 * kk + 2, :]
        prev = pltpu.roll(st, d, axis=0)
        prev = prev * pa + pltpu.roll(prev, P, axis=1) * pb
        st = st + jnp.where(srow >= d, prev, 0.0)
        kk += 1
    h0 = jnp.where(srow >= 1, pltpu.roll(st, 1, axis=0), 0.0)
    y = y + _dot(h0.astype(BF16), wout_ref[0])
    for o in range(SSM_GROUP):
        yt_ref[o] = y[:, o * LANES:(o + 1) * LANES]


def _ssm_scan(ut, toep, wst, wout, apow, n_chunks):
    nb, ssm_w, _ = ut.shape
    G = SSM_GROUPS
    g3 = lambda g: (g, 0, 0)
    in_specs = [pl.BlockSpec((nb, SSM_GROUP, LANES), lambda g: (0, g, 0)),
                pl.BlockSpec((1,) + toep.shape[1:], g3), pl.BlockSpec((1,) + wst.shape[1:], g3),
                pl.BlockSpec((1,) + wout.shape[1:], g3), pl.BlockSpec((1,) + apow.shape[1:], g3)]
    return pl.pallas_call(
        functools.partial(_ssm_scan_kernel, n_chunks=n_chunks), name="ssm_scan", grid=(G,), in_specs=in_specs,
        out_specs=pl.BlockSpec((SSM_GROUP, nb, LANES), g3),
        out_shape=jax.ShapeDtypeStruct((ssm_w, nb, LANES), F32),
        compiler_params=_cparams())(ut, toep, wst, wout, apow)


def _mix_kernel(yt_ref, att_ref, gs_ref, x_ref, wa_ref, wb_ref, wo_ref, g_ref, b_ref, out_ref, zt_ref,
                *, alpha):
    n_blk = yt_ref.shape[1]
    for j in range(n_blk):
        y = yt_ref[:, j, :]
        z = jax.nn.gelu(y, approximate=True)
        zt_ref[j * LANES:(j + 1) * LANES, :] = z.T.astype(BF16)
    z = zt_ref[...]
    ssm_out = _dot(z, wa_ref[...]) * jax.nn.sigmoid(_dot(z, wb_ref[...]))
    mixed = att_ref[...].astype(F32) + gs_ref[...].astype(F32) * ssm_out
    y = alpha * x_ref[...] + _dot(mixed.astype(BF16), wo_ref[...])
    out_ref[...] = _layer_norm(y, g_ref[...], b_ref[...])


def _mix(yt, att, gs, x2d, wa, wb, wo, g, b, alpha, tm):
    T, D = x2d.shape
    ssm_w = yt.shape[0]
    row = lambda i: (i, 0)
    in_specs = [pl.BlockSpec((ssm_w, tm // LANES, LANES), lambda i: (0, i, 0)),
                pl.BlockSpec((tm, D), row), pl.BlockSpec((tm, D), row), pl.BlockSpec((tm, D), row),
                _full_spec(wa.shape), _full_spec(wb.shape), _full_spec(wo.shape),
                _full_spec(g.shape), _full_spec(b.shape)]
    return pl.pallas_call(functools.partial(_mix_kernel, alpha=alpha), name="mix", grid=(T // tm,), in_specs=in_specs,
                          out_specs=pl.BlockSpec((tm, D), row),
                          out_shape=jax.ShapeDtypeStruct((T, D), F32),
                          scratch_shapes=[pltpu.VMEM((tm, ssm_w), BF16)],
                          compiler_params=_cparams())(yt, att, gs, x2d, wa, wb, wo, g, b)


def _memkv_kernel(m_ref, wk_ref, wv_ref, k_ref, v_ref):
    mb = m_ref[...].astype(BF16)
    k_ref[...] = _dot(mb, wk_ref[...]).astype(BF16)
    v_ref[...] = _dot(mb, wv_ref[...]).astype(BF16)


def _memkv(mem2d, wk, wv, tm):
    R, D = mem2d.shape
    row = lambda i: (i, 0)
    return pl.pallas_call(_memkv_kernel, name="memkv", grid=(R // tm,),
                          in_specs=[pl.BlockSpec((tm, D), row), _full_spec(wk.shape), _full_spec(wv.shape)],
                          out_specs=(pl.BlockSpec((tm, D), row), pl.BlockSpec((tm, D), row)),
                          out_shape=(jax.ShapeDtypeStruct((R, D), BF16), jax.ShapeDtypeStruct((R, D), BF16)),
                          compiler_params=_cparams())(mem2d, wk, wv)


def _cross_kernel(x_ref, k_ref, v_ref, wq_ref, wo_ref, g_ref, b_ref, wr_ref, br_ref, out_ref, logit_ref,
                  cat_ref, *, alpha):
    x = x_ref[...]
    D = x.shape[1]
    hd = D // N_CROSS_HEADS
    q = (_dot(x.astype(BF16), wq_ref[...]) * (hd ** -0.5)).astype(BF16)
    for h in range(N_CROSS_HEADS):
        cols = slice(h * hd, (h + 1) * hd)
        s = _dot_nt(q[:, cols], k_ref[:, cols])
        m = jnp.max(s, axis=-1, keepdims=True)
        p = jnp.exp(s - m)
        w = (p / jnp.sum(p, axis=-1, keepdims=True)).astype(BF16)
        cat_ref[:, cols] = _dot(w, v_ref[:, cols]).astype(BF16)
    y = alpha * x + _dot(cat_ref[...], wo_ref[...])
    x2 = _layer_norm(y, g_ref[...], b_ref[...])
    out_ref[...] = x2
    logit_ref[...] = _dot(x2.astype(BF16), wr_ref[...]) + br_ref[...]


def _cross(x1, kc, vc, wq, wo, g, b, wr, br, alpha, S, tm):
    T, D = x1.shape
    n_mem = kc.shape[0] // (T // S)
    row = lambda i: (i, 0)
    per_b = S // tm
    in_specs = [pl.BlockSpec((tm, D), row),
                pl.BlockSpec((n_mem, D), lambda i: (i // per_b, 0)),
                pl.BlockSpec((n_mem, D), lambda i: (i // per_b, 0)),
                _full_spec(wq.shape), _full_spec(wo.shape), _full_spec(g.shape), _full_spec(b.shape),
                _full_spec(wr.shape), _full_spec(br.shape)]
    return pl.pallas_call(functools.partial(_cross_kernel, alpha=alpha), name="cross", grid=(T // tm,), in_specs=in_specs,
                          out_specs=(pl.BlockSpec((tm, D), row), pl.BlockSpec((tm, LANES), row)),
                          out_shape=(jax.ShapeDtypeStruct((T, D), F32), jax.ShapeDtypeStruct((T, LANES), F32)),
                          scratch_shapes=[pltpu.VMEM((tm, D), BF16)],
                          compiler_params=_cparams())(x1, kc, vc, wq, wo, g, b, wr, br)


def _route_kernel(logit_ref, info_ref, count_ref, carry_ref):
    tm = logit_ref.shape[0]

    @pl.when(pl.program_id(0) == 0)
    def _():
        carry_ref[...] = jnp.zeros_like(carry_ref)

    vals = logit_ref[...]
    lane = lax.broadcasted_iota(jnp.int32, vals.shape, 1).astype(F32)
    sels, tops = [], []
    for _ in range(TOP_K):
        m = jnp.max(vals, axis=-1, keepdims=True)
        idx = jnp.min(jnp.where(vals == m, lane, float(LANES)), axis=-1, keepdims=True)
        sel = lane == idx
        vals = jnp.where(sel, -jnp.inf, vals)
        sels.append((sel, idx))
        tops.append(m)
    es = [jnp.exp(t - tops[0]) for t in tops]
    denom = es[0]
    for e in es[1:]:
        denom = denom + e
    onehot = jnp.zeros(vals.shape, F32)
    for sel, _ in sels:
        onehot = onehot + jnp.where(sel, 1.0, 0.0)
    r = lax.broadcasted_iota(jnp.int32, (tm, tm), 0)
    c = lax.broadcasted_iota(jnp.int32, (tm, tm), 1)
    tri = jnp.where(c < r, 1.0, 0.0).astype(BF16)
    cum = _dot(tri, onehot.astype(BF16)) + carry_ref[...]
    info = jnp.zeros(vals.shape, F32)
    for k, (sel, idx) in enumerate(sels):
        rank = jnp.sum(jnp.where(sel, cum, 0.0), axis=-1, keepdims=True)
        info = jnp.where(lane == k, idx, info)
        info = jnp.where(lane == TOP_K + k, rank, info)
        info = jnp.where(lane == 2 * TOP_K + k, es[k] / denom, info)
    info_ref[...] = info
    carry_ref[...] = carry_ref[...] + jnp.sum(onehot, axis=0, keepdims=True)
    count_ref[...] = carry_ref[...]


def _route(logits, tm):
    T = logits.shape[0]
    row = lambda i: (i, 0)
    return pl.pallas_call(_route_kernel, name="route", grid=(T // tm,),
                          in_specs=[pl.BlockSpec((tm, LANES), row)],
                          out_specs=(pl.BlockSpec((tm, LANES), row), _full_spec((1, LANES))),
                          out_shape=(jax.ShapeDtypeStruct((T, LANES), F32),
                                     jax.ShapeDtypeStruct((1, LANES), F32)),
                          scratch_shapes=[pltpu.VMEM((1, LANES), F32)],
                          compiler_params=_cparams())(logits)


def _moe_kernel(bexp_ref, nused_ref, src_cur_ref, src_nxt_ref, dst_ref, x_hbm, wg_ref, wl_ref, bg_ref,
                bl_ref, wd_ref, bd_ref, y_hbm, xbuf, ybuf, gsem, ssem):
    j = pl.program_id(0)
    n_used = nused_ref[0]
    slot = j % 2
    rows = xbuf.shape[1]

    def gather(idx_ref, s):
        def body(r, c):
            tok = idx_ref[0, 0, r]
            pltpu.make_async_copy(x_hbm.at[pl.ds(tok, 1)], xbuf.at[s, pl.ds(r, 1)], gsem.at[s]).start()
            return c
        lax.fori_loop(0, rows, body, 0)

    def wait_gather(s):
        pltpu.make_async_copy(x_hbm.at[pl.ds(0, rows)], xbuf.at[s], gsem.at[s]).wait()

    def wait_scatter(s):
        pltpu.make_async_copy(ybuf.at[s], y_hbm.at[pl.ds(0, rows)], ssem.at[s]).wait()

    @pl.when(j == 0)
    def _():
        gather(src_cur_ref, 0)
        ybuf[...] = jnp.zeros_like(ybuf)
        n_real = y_hbm.shape[0] - 2 * rows
        for s in range(2):
            cp = pltpu.make_async_copy(ybuf.at[s], y_hbm.at[pl.ds(n_real + s * rows, rows)], ssem.at[s])
            cp.start()
            cp.wait()

    @pl.when(j < n_used)
    def _():
        wait_gather(slot)

        @pl.when(j + 1 < n_used)
        def _():
            gather(src_nxt_ref, 1 - slot)

        @pl.when(j >= 2)
        def _():
            wait_scatter(slot)

        xb = xbuf[slot].astype(BF16)
        gate = jnp.minimum(_dot(xb, wg_ref[0]) + bg_ref[0], SWIGLU_LIMIT)
        lin = jnp.clip(_dot(xb, wl_ref[0]) + bl_ref[0], -SWIGLU_LIMIT, SWIGLU_LIMIT)
        h = gate * jax.nn.sigmoid(SWIGLU_ALPHA * gate) * (lin + 1.0)
        ybuf[slot] = _dot(h.astype(BF16), wd_ref[0]) + bd_ref[0]

        def body(r, c):
            d = dst_ref[0, 0, r]
            pltpu.make_async_copy(ybuf.at[slot, pl.ds(r, 1)], y_hbm.at[pl.ds(d, 1)], ssem.at[slot]).start()
            return c
        lax.fori_loop(0, rows, body, 0)

        @pl.when(j == n_used - 1)
        def _():
            wait_scatter(slot)

            @pl.when(j >= 1)
            def _():
                wait_scatter(1 - slot)


def _moe(bexp, n_used, slot_src, slot_dst, x2, wg, wl, bg, bl, wd, bd, n_out_rows):
    n_blocks = bexp.shape[0]
    T, D = x2.shape
    F = wg.shape[2]
    wmap = lambda j, be, nu: (be[j], 0, 0)
    smem = pltpu.SMEM
    in_specs = [
        pl.BlockSpec((1, 1, MOE_BLOCK), lambda j, be, nu: (j, 0, 0), memory_space=smem),
        pl.BlockSpec((1, 1, MOE_BLOCK), lambda j, be, nu: (jnp.minimum(j + 1, n_blocks - 1), 0, 0),
                     memory_space=smem),
        pl.BlockSpec((1, 1, MOE_BLOCK), lambda j, be, nu: (j, 0, 0), memory_space=smem),
        pl.BlockSpec(memory_space=pl.ANY),
        pl.BlockSpec((1, D, F), wmap), pl.BlockSpec((1, D, F), wmap),
        pl.BlockSpec((1, 1, F), wmap), pl.BlockSpec((1, 1, F), wmap),
        pl.BlockSpec((1, F, D), wmap), pl.BlockSpec((1, 1, D), wmap),
    ]
    grid_spec = pltpu.PrefetchScalarGridSpec(
        num_scalar_prefetch=2, grid=(n_blocks,), in_specs=in_specs,
        out_specs=pl.BlockSpec(memory_space=pl.ANY),
        scratch_shapes=[pltpu.VMEM((2, MOE_BLOCK, D), F32), pltpu.VMEM((2, MOE_BLOCK, D), F32),
                        pltpu.SemaphoreType.DMA((2,)), pltpu.SemaphoreType.DMA((2,))])
    return pl.pallas_call(_moe_kernel, name="moe", grid_spec=grid_spec,
                          out_shape=jax.ShapeDtypeStruct((n_out_rows, D), F32),
                          compiler_params=_cparams())(
        bexp, n_used, slot_src, slot_src, slot_dst, x2, wg, wl, bg, bl, wd, bd)


def _combine_kernel(y0_ref, y1_ref, y2_ref, y3_ref, info_ref, x_ref, g_ref, b_ref, out_ref, *, alpha):
    info = info_ref[...]
    acc = alpha * x_ref[...]
    for k, y_ref in enumerate((y0_ref, y1_ref, y2_ref, y3_ref)):
        acc = acc + info[:, 2 * TOP_K + k:2 * TOP_K + k + 1] * y_ref[...]
    out_ref[...] = _layer_norm(acc, g_ref[...], b_ref[...])


def _combine(yk, info, x2, g, b, alpha, tm):
    T, D = x2.shape
    n_t = T // tm
    row = lambda i: (i, 0)
    in_specs = [pl.BlockSpec((tm, D), (lambda i, k=k: (k * n_t + i, 0))) for k in range(TOP_K)]
    in_specs += [pl.BlockSpec((tm, LANES), row), pl.BlockSpec((tm, D), row),
                 _full_spec(g.shape), _full_spec(b.shape)]
    return pl.pallas_call(functools.partial(_combine_kernel, alpha=alpha), name="combine", grid=(n_t,), in_specs=in_specs,
                          out_specs=pl.BlockSpec((tm, D), row),
                          out_shape=jax.ShapeDtypeStruct((T, D), F32),
                          compiler_params=_cparams())(yk, yk, yk, yk, info, x2, g, b)


def _tile(n, pref):
    return pref if n % pref == 0 else n


def _layer(x2d, mem2d, pos2d, B, S, depth, w_in, sinks, w_attn_o, lam_re, lam_im, log_dt, b_re, b_im,
           c_re, c_im, d_skip, w_glu_a, w_glu_b, w_out, ln1_g, ln1_b, wq_c, wk_c, wv_c, wo_c, ln2_g,
           ln2_b, w_router, b_router, w_gate_up, b_gate_up, w_down, b_down, ln3_g, ln3_b):
    T, D = x2d.shape
    alpha = (2 * depth) ** 0.25
    rep = N_Q_HEADS // N_KV_HEADS
    q_w = N_Q_HEADS * HEAD_DIM
    kv_w = N_KV_HEADS * HEAD_DIM
    ssm_w = SSM_GROUP * SSM_GROUPS
    P = SSM_STATE

    o_k, o_v, o_s = q_w, q_w + kv_w, q_w + 2 * kv_w
    o_ga, o_gs = o_s + ssm_w, o_s + ssm_w + D
    wq = w_in[:, :o_k].reshape(D, N_KV_HEADS, rep, HEAD_DIM).transpose(0, 2, 1, 3).reshape(D, q_w).astype(BF16)
    wk = w_in[:, o_k:o_v].astype(BF16)
    wv = w_in[:, o_v:o_s].astype(BF16)
    wut = w_in[:, o_s:o_ga].T.astype(BF16)
    wga = w_in[:, o_ga:o_gs].astype(BF16)
    wgs = w_in[:, o_gs:].astype(BF16)
    wo_attn = w_attn_o.reshape(N_KV_HEADS, rep, HEAD_DIM, D).transpose(1, 0, 2, 3).reshape(q_w, D).astype(BF16)
    half = HEAD_DIM // 2
    inv_freq = jnp.power(ROPE_THETA, -jnp.arange(half, dtype=F32) / half)
    invf = jnp.tile(inv_freq, LANES // half)[None, :]
    sink_cols = jnp.repeat(sinks.astype(F32).reshape(N_KV_HEADS, rep), WINDOW, axis=1)[:, :, None]

    tm1 = _tile(T, 512)
    q2, k, v, ut, ga, gs = _inproj(x2d, pos2d, invf, wq, wk, wv, wut, wga, wgs, tm1)
    att = _swa(q2, k, v, sink_cols, wo_attn, ga, B, S, _tile(S, 512))

    lam_row = jnp.stack([jnp.concatenate([lam_re, lam_re], -1), jnp.concatenate([lam_im, lam_im], -1)], 1)
    lam_col = jnp.swapaxes(lam_row, 1, 2)
    bt_re = jnp.swapaxes(b_re, 1, 2)
    bt_im = jnp.swapaxes(b_im, 1, 2)
    bt_a = jnp.concatenate([bt_re, bt_im], -1)
    bt_b = jnp.concatenate([-bt_im, bt_re], -1)
    ct_re = jnp.swapaxes(c_re, 1, 2)
    ct_im = jnp.swapaxes(c_im, 1, 2)
    ct_a = jnp.concatenate([ct_re, -ct_im], 1)
    ct_b = jnp.concatenate([-ct_im, -ct_re], 1)
    dsk = jnp.repeat(d_skip.reshape(SSM_GROUPS, SSM_GROUP), SSM_GROUP, axis=1)[:, :, None]
    toep, wst, wout, apow = _ssm_prep(lam_row, lam_col, log_dt.reshape(SSM_GROUPS, 1, 1), bt_a, bt_b,
                                      c_re, c_im, ct_a, ct_b, dsk)
    yt = _ssm_scan(ut, toep, wst, wout, apow, S // LANES)

    x1 = _mix(yt, att, gs, x2d, w_glu_a.astype(BF16), w_glu_b.astype(BF16), w_out.astype(BF16),
              ln1_g[None, :], ln1_b[None, :], alpha, _tile(T, 1024))

    kc, vc = _memkv(mem2d, wk_c.astype(BF16), wv_c.astype(BF16), _tile(mem2d.shape[0], 512))
    wr = jnp.zeros((D, LANES), F32).at[:, :N_EXPERTS].set(w_router).astype(BF16)
    br = jnp.full((1, LANES), NEG_BIG, F32).at[0, :N_EXPERTS].set(b_router)
    x2, logits = _cross(x1, kc, vc, wq_c.astype(BF16), wo_c.astype(BF16), ln2_g[None, :], ln2_b[None, :],
                        wr, br, alpha, S, _tile(S, 512))

    info, counts = _route(logits, _tile(T, 512))
    top_idx = info[:, :TOP_K].astype(jnp.int32)
    rank = info[:, TOP_K:2 * TOP_K].astype(jnp.int32)
    counts = counts[0, :N_EXPERTS].astype(jnp.int32)
    padded = (counts + MOE_BLOCK - 1) // MOE_BLOCK * MOE_BLOCK
    ends_pad = jnp.cumsum(padded)
    start_pad = ends_pad - padded
    n_assign = T * TOP_K
    n_blocks = -(-n_assign // MOE_BLOCK) + N_EXPERTS
    n_slots = n_blocks * MOE_BLOCK
    dest = (start_pad[top_idx] + rank).reshape(-1)
    slot_a = jnp.full((n_slots,), -1, jnp.int32).at[dest].set(jnp.arange(n_assign, dtype=jnp.int32),
                                                              unique_indices=True)
    slot_src = jnp.maximum(slot_a, 0) // TOP_K
    slot_id = jnp.arange(n_slots, dtype=jnp.int32)
    spare = n_assign + (slot_id // MOE_BLOCK % 2) * MOE_BLOCK + slot_id % MOE_BLOCK
    slot_dst = jnp.where(slot_a < 0, spare, (slot_a % TOP_K) * T + slot_a // TOP_K)
    block_start = jnp.arange(n_blocks, dtype=jnp.int32) * MOE_BLOCK
    bexp = jnp.minimum(jnp.sum(block_start[:, None] >= ends_pad[None, :], axis=1), N_EXPERTS - 1).astype(jnp.int32)
    n_used = (ends_pad[-1] // MOE_BLOCK).astype(jnp.int32)[None]
    F = w_down.shape[1]
    wg = w_gate_up[:, :, 0::2].astype(BF16)
    wl = w_gate_up[:, :, 1::2].astype(BF16)
    bg = b_gate_up[:, None, 0::2]
    bl = b_gate_up[:, None, 1::2]
    yk = _moe(bexp, n_used, slot_src.reshape(n_blocks, 1, MOE_BLOCK), slot_dst.reshape(n_blocks, 1, MOE_BLOCK),
              x2, wg, wl, bg, bl, w_down.astype(BF16), b_down[:, None, :], n_assign + 2 * MOE_BLOCK)
    return _combine(yk, info, x2, ln3_g[None, :], ln3_b[None, :], alpha, _tile(T, 512))


def kernel(x, mem, positions, w_in, sinks, w_attn_o, lam_re, lam_im, log_dt, b_re, b_im, c_re, c_im, d_skip,
           w_glu_a, w_glu_b, w_out, ln1_g, ln1_b, wq_c, wk_c, wv_c, wo_c, ln2_g, ln2_b, w_router, b_router,
           w_gate_up, b_gate_up, w_down, b_down, ln3_g, ln3_b):
    B, S, D = x.shape
    depth = w_in.shape[0]
    x2d = x.reshape(B * S, D)
    mem2d = mem.reshape(-1, D)
    pos2d = positions.reshape(B * S, 1)
    per_layer = (w_in, sinks, w_attn_o, lam_re, lam_im, log_dt, b_re, b_im, c_re, c_im, d_skip, w_glu_a,
                 w_glu_b, w_out, ln1_g, ln1_b, wq_c, wk_c, wv_c, wo_c, ln2_g, ln2_b, w_router, b_router,
                 w_gate_up, b_gate_up, w_down, b_down, ln3_g, ln3_b)
    for l in range(depth):
        x2d = _layer(x2d, mem2d, pos2d, B, S, depth, *(w[l] for w in per_layer))
    return x2d.reshape(B, S, D)
```

```python
import functools

import jax
import jax.numpy as jnp
from jax import lax
from jax.experimental import pallas as pl
from jax.experimental.pallas import tpu as pltpu

N_Q_HEADS = 16
N_KV_HEADS = 2
HEAD_DIM = 64
WINDOW = 128
ROPE_THETA = 10000.0
SSM_GROUP = 16
SSM_GROUPS = 32
SSM_STATE = 64
N_CROSS_HEADS = 4
N_EXPERTS = 32
TOP_K = 4
SWIGLU_ALPHA = 1.702
SWIGLU_LIMIT = 7.0
MOE_BLOCK = 256
LN_EPS = 1e-5

LANES = 128
VMEM_LIMIT_BYTES = 56 * 1024 * 1024

NEG_BIG = -1e30
BF16 = jnp.bfloat16
F32 = jnp.float32
HI = lax.Precision.HIGHEST


def _cparams(n_axes=1):
    return pltpu.CompilerParams(dimension_semantics=("arbitrary",) * n_axes,
                                vmem_limit_bytes=VMEM_LIMIT_BYTES)


def _full_spec(shape):
    n = len(shape)
    return pl.BlockSpec(shape, lambda *_: (0,) * n)


def _dot(a, b):
    return jnp.dot(a, b, preferred_element_type=F32)


def _dot_nt(a, b):
    return lax.dot_general(a, b, (((1,), (1,)), ((), ())), preferred_element_type=F32)


def _layer_norm(y, g, b):
    mu = jnp.mean(y, axis=-1, keepdims=True)
    d = y - mu
    var = jnp.mean(d * d, axis=-1, keepdims=True)
    return d * lax.rsqrt(var + LN_EPS) * g + b


def _rope(t, cos, sin_signed, first_half):
    half = HEAD_DIM // 2
    partner = jnp.where(first_half, pltpu.roll(t, LANES - half, axis=1), pltpu.roll(t, half, axis=1))
    return t * cos + partner * sin_signed


def _inproj_kernel(x_ref, pos_ref, invf_ref, wq_ref, wk_ref, wv_ref, wut_ref, wga_ref, wgs_ref,
                   q2_ref, k_ref, v_ref, ut_ref, ga_ref, gs_ref):
    tm = x_ref.shape[0]
    xb = x_ref[...].astype(BF16)
    ang = pos_ref[...].astype(F32) * invf_ref[...]
    cos = jnp.cos(ang)
    sin = jnp.sin(ang)
    first_half = (lax.broadcasted_iota(jnp.int32, (tm, LANES), 1) % HEAD_DIM) < (HEAD_DIM // 2)
    first_half_w = (lax.broadcasted_iota(jnp.int32, (WINDOW, LANES), 1) % HEAD_DIM) < (HEAD_DIM // 2)
    sin_signed = jnp.where(first_half, -sin, sin)

    q = _dot(xb, wq_ref[...])
    n_rep = q.shape[1] // LANES
    scale = HEAD_DIM ** -0.5
    for j in range(tm // WINDOW):
        rows = slice(j * WINDOW, (j + 1) * WINDOW)
        for r in range(n_rep):
            t = _rope(q[rows, r * LANES:(r + 1) * LANES], cos[rows], sin_signed[rows], first_half_w)
            base = (j * n_rep + r) * WINDOW
            q2_ref[base:base + WINDOW, :] = (t * scale).astype(BF16)
    k_ref[...] = _rope(_dot(xb, wk_ref[...]), cos, sin_signed, first_half).astype(BF16)
    v_ref[...] = _dot(xb, wv_ref[...]).astype(BF16)
    for j in range(tm // LANES):
        ut_ref[j] = _dot_nt(wut_ref[...], xb[j * LANES:(j + 1) * LANES, :])
    ga_ref[...] = jax.nn.sigmoid(_dot(xb, wga_ref[...])).astype(BF16)
    gs_ref[...] = jax.nn.sigmoid(_dot(xb, wgs_ref[...])).astype(BF16)


def _inproj(x2d, pos2d, invf, wq, wk, wv, wut, wga, wgs, tm):
    T, D = x2d.shape
    n_rep = wq.shape[1] // LANES
    ssm_w = wut.shape[0]
    row = lambda i: (i, 0)
    out_shape = (
        jax.ShapeDtypeStruct((T * n_rep, LANES), BF16),
        jax.ShapeDtypeStruct((T, LANES), BF16),
        jax.ShapeDtypeStruct((T, LANES), BF16),
        jax.ShapeDtypeStruct((T // LANES, ssm_w, LANES), F32),
        jax.ShapeDtypeStruct((T, D), BF16),
        jax.ShapeDtypeStruct((T, D), BF16),
    )
    in_specs = [pl.BlockSpec((tm, D), row), pl.BlockSpec((tm, 1), row), _full_spec(invf.shape),
                _full_spec(wq.shape), _full_spec(wk.shape), _full_spec(wv.shape),
                _full_spec(wut.shape), _full_spec(wga.shape), _full_spec(wgs.shape)]
    out_specs = (pl.BlockSpec((tm * n_rep, LANES), row), pl.BlockSpec((tm, LANES), row),
                 pl.BlockSpec((tm, LANES), row),
                 pl.BlockSpec((tm // LANES, ssm_w, LANES), lambda i: (i, 0, 0)),
                 pl.BlockSpec((tm, D), row), pl.BlockSpec((tm, D), row))
    return pl.pallas_call(_inproj_kernel, name="inproj", grid=(T // tm,), in_specs=in_specs, out_specs=out_specs,
                          out_shape=out_shape, compiler_params=_cparams())(
        x2d, pos2d, invf, wq, wk, wv, wut, wga, wgs)


def _swa_kernel(q2_ref, kc_ref, kp_ref, vc_ref, vp_ref, sink_ref, wo_ref, ga_ref, out_ref, cat_ref):
    i = pl.program_id(1)
    tq = kc_ref.shape[0]
    n_sub = tq // WINDOW
    rep = N_Q_HEADS // N_KV_HEADS
    rows_all = rep * WINDOW
    kfull = jnp.concatenate([kp_ref[...], kc_ref[...]], axis=0)
    vfull = jnp.concatenate([vp_ref[...], vc_ref[...]], axis=0)
    lane = lax.broadcasted_iota(jnp.int32, (2 * WINDOW, LANES), 1)
    qi = lax.broadcasted_iota(jnp.int32, (rows_all, 2 * WINDOW), 0) % WINDOW
    ci = lax.broadcasted_iota(jnp.int32, (rows_all, 2 * WINDOW), 1)
    local = (ci > qi) & (ci <= qi + WINDOW)
    out_lane = lax.broadcasted_iota(jnp.int32, (rows_all, LANES), 1)
    for j in range(n_sub):
        qs = q2_ref[j * rows_all:(j + 1) * rows_all, :]
        kb = kfull[j * WINDOW:(j + 2) * WINDOW, :]
        vb = vfull[j * WINDOW:(j + 2) * WINDOW, :]
        mask = local
        if j == 0:
            mask = mask & ((ci >= WINDOW) | (i > 0))
        o = None
        for g in range(N_KV_HEADS):
            in_group = (lane >= g * HEAD_DIM) & (lane < (g + 1) * HEAD_DIM)
            kg = jnp.where(in_group, kb, jnp.zeros_like(kb))
            s = _dot_nt(qs, kg)
            s = jnp.where(mask, s, NEG_BIG)
            sink = sink_ref[g]
            m = jnp.maximum(jnp.max(s, axis=-1, keepdims=True), sink)
            p = jnp.exp(s - m)
            denom = jnp.sum(p, axis=-1, keepdims=True) + jnp.exp(sink - m)
            w = (p / denom).astype(BF16)
            og = _dot(w, vb)
            o = og if o is None else jnp.where(out_lane < g * HEAD_DIM, o, og)
        ob = o.astype(BF16)
        for r in range(rep):
            cat_ref[j * WINDOW:(j + 1) * WINDOW, r * LANES:(r + 1) * LANES] = ob[r * WINDOW:(r + 1) * WINDOW, :]
    attn = _dot(cat_ref[...], wo_ref[...])
    out_ref[...] = (attn * ga_ref[...].astype(F32)).astype(BF16)


def _swa(q2, k, v, sink_cols, wo, ga, B, S, tq):
    T, D = ga.shape
    rep = N_Q_HEADS // N_KV_HEADS
    n_i = S // tq
    n_sub = tq // WINDOW
    in_specs = [
        pl.BlockSpec((tq * rep, LANES), lambda b, i: (b * n_i + i, 0)),
        pl.BlockSpec((tq, LANES), lambda b, i: (b * n_i + i, 0)),
        pl.BlockSpec((WINDOW, LANES), lambda b, i: (b * (S // WINDOW) + jnp.maximum(i * n_sub - 1, 0), 0)),
        pl.BlockSpec((tq, LANES), lambda b, i: (b * n_i + i, 0)),
        pl.BlockSpec((WINDOW, LANES), lambda b, i: (b * (S // WINDOW) + jnp.maximum(i * n_sub - 1, 0), 0)),
        _full_spec(sink_cols.shape), _full_spec(wo.shape),
        pl.BlockSpec((tq, D), lambda b, i: (b * n_i + i, 0)),
    ]
    return pl.pallas_call(
        _swa_kernel, name="swa", grid=(B, n_i), in_specs=in_specs,
        out_specs=pl.BlockSpec((tq, D), lambda b, i: (b * n_i + i, 0)),
        out_shape=jax.ShapeDtypeStruct((T, D), BF16),
        scratch_shapes=[pltpu.VMEM((tq, D), BF16)],
        compiler_params=_cparams(2))(q2, k, k, v, v, sink_cols, wo, ga)


def _ssm_prep_kernel(lam_row_ref, lam_col_ref, dt_ref, bt_a_ref, bt_b_ref, c_re_ref, c_im_ref,
                     ct_a_ref, ct_b_ref, dsk_ref, toep_ref, wst_ref, wout_ref, apow_ref, kt_ref):
    L = LANES
    P = SSM_STATE
    dt = jnp.exp(dt_ref[0])
    lr2 = lam_row_ref[0, 0:1, :]
    li2 = lam_row_ref[0, 1:2, :]
    mag = jnp.exp(lr2 * dt)
    ar = mag * jnp.cos(li2 * dt)
    ai = mag * jnp.sin(li2 * dt)
    den = lr2 * lr2 + li2 * li2
    f_re = ((ar - 1.0) * lr2 + ai * li2) / den
    f_im = (ai * lr2 - (ar - 1.0) * li2) / den
    bt_a = bt_a_ref[0]
    bt_b = bt_b_ref[0]
    bb_a = f_re * bt_a + f_im * bt_b
    bb_b = f_re * bt_b - f_im * bt_a

    tau_rev = (L - 1 - lax.broadcasted_iota(jnp.int32, (L, 2 * P), 0)).astype(F32)
    g_mag = jnp.exp(lr2 * dt * tau_rev)
    g_re = g_mag * jnp.cos(li2 * dt * tau_rev)
    g_im = g_mag * jnp.sin(li2 * dt * tau_rev)
    for i in range(SSM_GROUP):
        wst_ref[0, i * L:(i + 1) * L, :] = (g_re * bb_a[i:i + 1, :] + g_im * bb_b[i:i + 1, :]).astype(BF16)

    lane2 = lax.broadcasted_iota(jnp.int32, (1, 2 * P), 1)
    for kk in range(4):
        n = float(L * (1 << kk))
        pm = jnp.exp(lr2 * dt * n)
        p_re = pm * jnp.cos(li2 * dt * n)
        p_im = pm * jnp.sin(li2 * dt * n)
        apow_ref[0, 2 * kk:2 * kk + 1, :] = p_re
        apow_ref[0, 2 * kk + 1:2 * kk + 2, :] = jnp.where(lane2 < P, -p_im, p_im)

    lrc = lam_col_ref[0, :, 0:1]
    lic = lam_col_ref[0, :, 1:2]
    tau = lax.broadcasted_iota(jnp.int32, (2 * P, L), 1).astype(F32)
    e0_mag = jnp.exp(lrc * dt * tau)
    e0_re = e0_mag * jnp.cos(lic * dt * tau)
    e0_im = e0_mag * jnp.sin(lic * dt * tau)
    e1_mag = jnp.exp(lrc * dt * (tau + 1.0))
    e1_re = e1_mag * jnp.cos(lic * dt * (tau + 1.0))
    e1_im = e1_mag * jnp.sin(lic * dt * (tau + 1.0))

    ct_a = ct_a_ref[0]
    ct_b = ct_b_ref[0]
    for o in range(SSM_GROUP):
        wout_ref[0, :, o * L:(o + 1) * L] = (ct_a[:, o:o + 1] * e1_re + ct_b[:, o:o + 1] * e1_im).astype(BF16)

    c_re = c_re_ref[0]
    c_im = c_im_ref[0]
    bb_re = bb_a[:, :P]
    bb_im = bb_a[:, P:]
    m_re = (c_re[:, None, :] * bb_re[None, :, :] - c_im[:, None, :] * bb_im[None, :, :])
    m_im = (c_re[:, None, :] * bb_im[None, :, :] + c_im[:, None, :] * bb_re[None, :, :])
    m_re = m_re.reshape(SSM_GROUP * SSM_GROUP, P)
    m_im = m_im.reshape(SSM_GROUP * SSM_GROUP, P)
    kt = (jnp.dot(m_re, e0_re[:P, :], precision=HI, preferred_element_type=F32)
          - jnp.dot(m_im, e0_im[:P, :], precision=HI, preferred_element_type=F32))
    rowi = lax.broadcasted_iota(jnp.int32, (SSM_GROUP * SSM_GROUP, L), 0)
    coli = lax.broadcasted_iota(jnp.int32, (SSM_GROUP * SSM_GROUP, L), 1)
    dsk = dsk_ref[0]
    kt_ref[...] = kt + jnp.where((coli == 0) & ((rowi // SSM_GROUP) == (rowi % SSM_GROUP)), dsk, 0.0)

    cc = lax.broadcasted_iota(jnp.int32, (L, L), 0)
    cp = lax.broadcasted_iota(jnp.int32, (L, L), 1)
    causal = cp >= cc
    for o in range(SSM_GROUP):
        def body(i, carry, o=o):
            kv = kt_ref[pl.ds(o * SSM_GROUP + i, 1), :]
            blk = pltpu.roll(jnp.broadcast_to(kv, (L, L)), 0, axis=1, stride=1, stride_axis=0)
            blk = jnp.where(causal, blk, 0.0)
            toep_ref[0, pl.ds(pl.multiple_of(i * L, L), L), o * L:(o + 1) * L] = blk.astype(BF16)
            return carry
        lax.fori_loop(0, SSM_GROUP, body, 0)


def _ssm_prep(lam_row, lam_col, log_dt, bt_a, bt_b, c_re, c_im, ct_a, ct_b, dsk):
    G = SSM_GROUPS
    KW = SSM_GROUP * LANES
    g3 = lambda g: (g, 0, 0)
    ins = [lam_row, lam_col, log_dt, bt_a, bt_b, c_re, c_im, ct_a, ct_b, dsk]
    in_specs = [pl.BlockSpec((1,) + a.shape[1:], g3) for a in ins]
    out_shape = (jax.ShapeDtypeStruct((G, KW, KW), BF16),
                 jax.ShapeDtypeStruct((G, KW, 2 * SSM_STATE), BF16),
                 jax.ShapeDtypeStruct((G, 2 * SSM_STATE, KW), BF16),
                 jax.ShapeDtypeStruct((G, 8, 2 * SSM_STATE), F32))
    out_specs = tuple(pl.BlockSpec((1,) + s.shape[1:], g3) for s in out_shape)
    return pl.pallas_call(_ssm_prep_kernel, name="ssm_prep", grid=(G,), in_specs=in_specs, out_specs=out_specs,
                          out_shape=out_shape,
                          scratch_shapes=[pltpu.VMEM((SSM_GROUP * SSM_GROUP, LANES), F32)],
                          compiler_params=_cparams())(*ins)


def _ssm_scan_kernel(ut_ref, toep_ref, wst_ref, wout_ref, apow_ref, yt_ref, *, n_chunks):
    nb = ut_ref.shape[0]
    P = SSM_STATE
    lhs = jnp.concatenate([ut_ref[:, i, :] for i in range(SSM_GROUP)], axis=1).astype(BF16)
    y = _dot(lhs, toep_ref[0])
    st = _dot(lhs, wst_ref[0])
    srow = lax.broadcasted_iota(jnp.int32, (nb, 2 * P), 0) % n_chunks
    kk = 0
    while (1 << kk) < n_chunks:
        d = 1 << kk
        pa = apow_ref[0, 2 * kk:2 * kk + 1, :]
        pb = apow_ref[0, 2 * kk + 1:2 * kk + 2, :]
        prev = pltpu.roll(st, d, axis=0)
        prev = prev * pa + pltpu.roll(prev, P, axis=1) * pb
        st = st + jnp.where(srow >= d, prev, 0.0)
        kk += 1
    h0 = jnp.where(srow >= 1, pltpu.roll(st, 1, axis=0), 0.0)
    y = y + _dot(h0.astype(BF16), wout_ref[0])
    for o in range(SSM_GROUP):
        yt_ref[o] = y[:, o * LANES:(o + 1) * LANES]


def _ssm_scan(ut, toep, wst, wout, apow, n_chunks):
    nb, ssm_w, _ = ut.shape
    G = SSM_GROUPS
    g3 = lambda g: (g, 0, 0)
    in_specs = [pl.BlockSpec((nb, SSM_GROUP, LANES), lambda g: (0, g, 0)),
                pl.BlockSpec((1,) + toep.shape[1:], g3), pl.BlockSpec((1,) + wst.shape[1:], g3),
                pl.BlockSpec((1,) + wout.shape[1:], g3), pl.BlockSpec((1,) + apow.shape[1:], g3)]
    return pl.pallas_call(
        functools.partial(_ssm_scan_kernel, n_chunks=n_chunks), name="ssm_scan", grid=(G,), in_specs=in_specs,
        out_specs=pl.BlockSpec((SSM_GROUP, nb, LANES), g3),
        out_shape=jax.ShapeDtypeStruct((ssm_w, nb, LANES), F32),
        compiler_params=_cparams())(ut, toep, wst, wout, apow)


def _mix_kernel(yt_ref, att_ref, gs_ref, x_ref, wa_ref, wb_ref, wo_ref, g_ref, b_ref, out_ref, zt_ref,
                *, alpha):
    n_blk = yt_ref.shape[1]
    for j in range(n_blk):
        y = yt_ref[:, j, :]
        z = jax.nn.gelu(y, approximate=True)
        zt_ref[j * LANES:(j + 1) * LANES, :] = z.T.astype(BF16)
    z = zt_ref[...]
    ssm_out = _dot(z, wa_ref[...]) * jax.nn.sigmoid(_dot(z, wb_ref[...]))
    mixed = att_ref[...].astype(F32) + gs_ref[...].astype(F32) * ssm_out
    y = alpha * x_ref[...] + _dot(mixed.astype(BF16), wo_ref[...])
    out_ref[...] = _layer_norm(y, g_ref[...], b_ref[...])


def _mix(yt, att, gs, x2d, wa, wb, wo, g, b, alpha, tm):
    T, D = x2d.shape
    ssm_w = yt.shape[0]
    row = lambda i: (i, 0)
    in_specs = [pl.BlockSpec((ssm_w, tm // LANES, LANES), lambda i: (0, i, 0)),
                pl.BlockSpec((tm, D), row), pl.BlockSpec((tm, D), row), pl.BlockSpec((tm, D), row),
                _full_spec(wa.shape), _full_spec(wb.shape), _full_spec(wo.shape),
                _full_spec(g.shape), _full_spec(b.shape)]
    return pl.pallas_call(functools.partial(_mix_kernel, alpha=alpha), name="mix", grid=(T // tm,), in_specs=in_specs,
                          out_specs=pl.BlockSpec((tm, D), row),
                          out_shape=jax.ShapeDtypeStruct((T, D), F32),
                          scratch_shapes=[pltpu.VMEM((tm, ssm_w), BF16)],
                          compiler_params=_cparams())(yt, att, gs, x2d, wa, wb, wo, g, b)


def _memkv_kernel(m_ref, wk_ref, wv_ref, k_ref, v_ref):
    mb = m_ref[...].astype(BF16)
    k_ref[...] = _dot(mb, wk_ref[...]).astype(BF16)
    v_ref[...] = _dot(mb, wv_ref[...]).astype(BF16)


def _memkv(mem2d, wk, wv, tm):
    R, D = mem2d.shape
    row = lambda i: (i, 0)
    return pl.pallas_call(_memkv_kernel, name="memkv", grid=(R // tm,),
                          in_specs=[pl.BlockSpec((tm, D), row), _full_spec(wk.shape), _full_spec(wv.shape)],
                          out_specs=(pl.BlockSpec((tm, D), row), pl.BlockSpec((tm, D), row)),
                          out_shape=(jax.ShapeDtypeStruct((R, D), BF16), jax.ShapeDtypeStruct((R, D), BF16)),
                          compiler_params=_cparams())(mem2d, wk, wv)


def _cross_kernel(x_ref, k_ref, v_ref, wq_ref, wo_ref, g_ref, b_ref, wr_ref, br_ref, out_ref, logit_ref,
                  cat_ref, *, alpha):
    x = x_ref[...]
    D = x.shape[1]
    hd = D // N_CROSS_HEADS
    q = (_dot(x.astype(BF16), wq_ref[...]) * (hd ** -0.5)).astype(BF16)
    for h in range(N_CROSS_HEADS):
        cols = slice(h * hd, (h + 1) * hd)
        s = _dot_nt(q[:, cols], k_ref[:, cols])
        m = jnp.max(s, axis=-1, keepdims=True)
        p = jnp.exp(s - m)
        w = (p / jnp.sum(p, axis=-1, keepdims=True)).astype(BF16)
        cat_ref[:, cols] = _dot(w, v_ref[:, cols]).astype(BF16)
    y = alpha * x + _dot(cat_ref[...], wo_ref[...])
    x2 = _layer_norm(y, g_ref[...], b_ref[...])
    out_ref[...] = x2
    logit_ref[...] = _dot(x2.astype(BF16), wr_ref[...]) + br_ref[...]


def _cross(x1, kc, vc, wq, wo, g, b, wr, br, alpha, S, tm):
    T, D = x1.shape
    n_mem = kc.shape[0] // (T // S)
    row = lambda i: (i, 0)
    per_b = S // tm
    in_specs = [pl.BlockSpec((tm, D), row),
                pl.BlockSpec((n_mem, D), lambda i: (i // per_b, 0)),
                pl.BlockSpec((n_mem, D), lambda i: (i // per_b, 0)),
                _full_spec(wq.shape), _full_spec(wo.shape), _full_spec(g.shape), _full_spec(b.shape),
                _full_spec(wr.shape), _full_spec(br.shape)]
    return pl.pallas_call(functools.partial(_cross_kernel, alpha=alpha), name="cross", grid=(T // tm,), in_specs=in_specs,
                          out_specs=(pl.BlockSpec((tm, D), row), pl.BlockSpec((tm, LANES), row)),
                          out_shape=(jax.ShapeDtypeStruct((T, D), F32), jax.ShapeDtypeStruct((T, LANES), F32)),
                          scratch_shapes=[pltpu.VMEM((tm, D), BF16)],
                          compiler_params=_cparams())(x1, kc, vc, wq, wo, g, b, wr, br)


def _route_kernel(logit_ref, info_ref, count_ref, carry_ref):
    tm = logit_ref.shape[0]

    @pl.when(pl.program_id(0) == 0)
    def _():
        carry_ref[...] = jnp.zeros_like(carry_ref)

    vals = logit_ref[...]
    lane = lax.broadcasted_iota(jnp.int32, vals.shape, 1).astype(F32)
    sels, tops = [], []
    for _ in range(TOP_K):
        m = jnp.max(vals, axis=-1, keepdims=True)
        idx = jnp.min(jnp.where(vals == m, lane, float(LANES)), axis=-1, keepdims=True)
        sel = lane == idx
        vals = jnp.where(sel, -jnp.inf, vals)
        sels.append((sel, idx))
        tops.append(m)
    es = [jnp.exp(t - tops[0]) for t in tops]
    denom = es[0]
    for e in es[1:]:
        denom = denom + e
    onehot = jnp.zeros(vals.shape, F32)
    for sel, _ in sels:
        onehot = onehot + jnp.where(sel, 1.0, 0.0)
    r = lax.broadcasted_iota(jnp.int32, (tm, tm), 0)
    c = lax.broadcasted_iota(jnp.int32, (tm, tm), 1)
    tri = jnp.where(c < r, 1.0, 0.0).astype(BF16)
    cum = _dot(tri, onehot.astype(BF16)) + carry_ref[...]
    info = jnp.zeros(vals.shape, F32)
    for k, (sel, idx) in enumerate(sels):
        rank = jnp.sum(jnp.where(sel, cum, 0.0), axis=-1, keepdims=True)
        info = jnp.where(lane == k, idx, info)
        info = jnp.where(lane == TOP_K + k, rank, info)
        info = jnp.where(lane == 2 * TOP_K + k, es[k] / denom, info)
    info_ref[...] = info
    carry_ref[...] = carry_ref[...] + jnp.sum(onehot, axis=0, keepdims=True)
    count_ref[...] = carry_ref[...]


def _route(logits, tm):
    T = logits.shape[0]
    row = lambda i: (i, 0)
    return pl.pallas_call(_route_kernel, name="route", grid=(T // tm,),
                          in_specs=[pl.BlockSpec((tm, LANES), row)],
                          out_specs=(pl.BlockSpec((tm, LANES), row), _full_spec((1, LANES))),
                          out_shape=(jax.ShapeDtypeStruct((T, LANES), F32),
                                     jax.ShapeDtypeStruct((1, LANES), F32)),
                          scratch_shapes=[pltpu.VMEM((1, LANES), F32)],
                          compiler_params=_cparams())(logits)


GU_TILE = 2 * LANES


def _moe_kernel(bexp_ref, nused_ref, src_cur_ref, src_nxt_ref, dst_ref, x_hbm, wgu_ref, bgu_ref, wd_ref,
                bd_ref, y_hbm, xbuf, ybuf, wgu_s, wd_s, gsem, ssem):
    j = pl.program_id(0)
    n_used = nused_ref[0]
    slot = j % 2
    rows = xbuf.shape[1]
    n_tiles = wgu_s.shape[1] // GU_TILE

    def gather(idx_ref, s):
        for r in range(rows):
            tok = idx_ref[0, 0, r]
            pltpu.make_async_copy(x_hbm.at[pl.ds(tok, 1)], xbuf.at[s, pl.ds(r, 1)], gsem.at[s]).start()

    def wait_gather(s):
        pltpu.make_async_copy(x_hbm.at[pl.ds(0, rows)], xbuf.at[s], gsem.at[s]).wait()

    def wait_scatter(s):
        pltpu.make_async_copy(ybuf.at[s], y_hbm.at[pl.ds(0, rows)], ssem.at[s]).wait()

    @pl.when(j == 0)
    def _():
        gather(src_cur_ref, 0)
        ybuf[...] = jnp.zeros_like(ybuf)
        n_real = y_hbm.shape[0] - 2 * rows
        for s in range(2):
            pltpu.make_async_copy(ybuf.at[s], y_hbm.at[pl.ds(n_real + s * rows, rows)], ssem.at[s]).start()

    @pl.when(j < n_used)
    def _():
        @pl.when((j == 0) | (bexp_ref[j] != bexp_ref[jnp.maximum(j - 1, 0)]))
        def _():
            k = lax.broadcasted_iota(jnp.int32, (GU_TILE, GU_TILE), 0)
            n = lax.broadcasted_iota(jnp.int32, (GU_TILE, GU_TILE), 1)
            perm = jnp.where(k == jnp.where(n < LANES, 2 * n, 2 * (n - LANES) + 1), 1.0, 0.0).astype(BF16)
            for t in range(n_tiles):
                cols = slice(t * GU_TILE, (t + 1) * GU_TILE)
                wgu_s[:, cols] = _dot(wgu_ref[0, :, cols].astype(BF16), perm).astype(BF16)
            wd_s[...] = wd_ref[0].astype(BF16)

        wait_gather(slot)
        wait_scatter(slot)
        gather(src_nxt_ref, 1 - slot)
        xb = xbuf[slot].astype(BF16)
        hs = []
        for t in range(n_tiles):
            cols = slice(t * GU_TILE, (t + 1) * GU_TILE)
            gu = _dot(xb, wgu_s[:, cols]) + bgu_ref[0, :, cols]
            gate = jnp.minimum(gu[:, :LANES], SWIGLU_LIMIT)
            lin = jnp.clip(gu[:, LANES:], -SWIGLU_LIMIT, SWIGLU_LIMIT)
            hs.append((gate * jax.nn.sigmoid(SWIGLU_ALPHA * gate) * (lin + 1.0)).astype(BF16))
        h = jnp.concatenate(hs, axis=1)
        ybuf[slot] = _dot(h, wd_s[...]) + bd_ref[0]
        for r in range(rows):
            d = dst_ref[0, 0, r]
            pltpu.make_async_copy(ybuf.at[slot, pl.ds(r, 1)], y_hbm.at[pl.ds(d, 1)], ssem.at[slot]).start()

        @pl.when(j == n_used - 1)
        def _():
            wait_gather(1 - slot)
            wait_scatter(slot)
            wait_scatter(1 - slot)


def _moe(bexp, n_used, slot_src, slot_dst, x2, wgu, bgu, wd, bd, n_out_rows):
    n_blocks = bexp.shape[0]
    T, D = x2.shape
    F2 = wgu.shape[2]
    F = wd.shape[1]
    wmap = lambda j, be, nu: (be[j], 0, 0)
    smem = pltpu.SMEM
    in_specs = [
        pl.BlockSpec((1, 1, MOE_BLOCK), lambda j, be, nu: (j, 0, 0), memory_space=smem),
        pl.BlockSpec((1, 1, MOE_BLOCK), lambda j, be, nu: (jnp.minimum(j + 1, n_blocks - 1), 0, 0),
                     memory_space=smem),
        pl.BlockSpec((1, 1, MOE_BLOCK), lambda j, be, nu: (j, 0, 0), memory_space=smem),
        pl.BlockSpec(memory_space=pl.ANY),
        pl.BlockSpec((1, D, F2), wmap), pl.BlockSpec((1, 1, F2), wmap),
        pl.BlockSpec((1, F, D), wmap), pl.BlockSpec((1, 1, D), wmap),
    ]
    grid_spec = pltpu.PrefetchScalarGridSpec(
        num_scalar_prefetch=2, grid=(n_blocks,), in_specs=in_specs,
        out_specs=pl.BlockSpec(memory_space=pl.ANY),
        scratch_shapes=[pltpu.VMEM((2, MOE_BLOCK, D), F32), pltpu.VMEM((2, MOE_BLOCK, D), F32),
                        pltpu.VMEM((D, F2), BF16), pltpu.VMEM((F, D), BF16),
                        pltpu.SemaphoreType.DMA((2,)), pltpu.SemaphoreType.DMA((2,))])
    return pl.pallas_call(_moe_kernel, name="moe", grid_spec=grid_spec,
                          out_shape=jax.ShapeDtypeStruct((n_out_rows, D), F32),
                          compiler_params=_cparams())(
        bexp, n_used, slot_src, slot_src, slot_dst, x2, wgu, bgu, wd, bd)


def _combine_kernel(y0_ref, y1_ref, y2_ref, y3_ref, info_ref, x_ref, g_ref, b_ref, out_ref, *, alpha):
    info = info_ref[...]
    acc = alpha * x_ref[...]
    for k, y_ref in enumerate((y0_ref, y1_ref, y2_ref, y3_ref)):
        acc = acc + info[:, 2 * TOP_K + k:2 * TOP_K + k + 1] * y_ref[...]
    out_ref[...] = _layer_norm(acc, g_ref[...], b_ref[...])


def _combine(yk, info, x2, g, b, alpha, tm):
    T, D = x2.shape
    n_t = T // tm
    row = lambda i: (i, 0)
    in_specs = [pl.BlockSpec((tm, D), (lambda i, k=k: (k * n_t + i, 0))) for k in range(TOP_K)]
    in_specs += [pl.BlockSpec((tm, LANES), row), pl.BlockSpec((tm, D), row),
                 _full_spec(g.shape), _full_spec(b.shape)]
    return pl.pallas_call(functools.partial(_combine_kernel, alpha=alpha), name="combine", grid=(n_t,), in_specs=in_specs,
                          out_specs=pl.BlockSpec((tm, D), row),
                          out_shape=jax.ShapeDtypeStruct((T, D), F32),
                          compiler_params=_cparams())(yk, yk, yk, yk, info, x2, g, b)


def _tile(n, pref):
    return pref if n % pref == 0 else n


def _layer(x2d, mem2d, pos2d, B, S, depth, w_in, sinks, w_attn_o, lam_re, lam_im, log_dt, b_re, b_im,
           c_re, c_im, d_skip, w_glu_a, w_glu_b, w_out, ln1_g, ln1_b, wq_c, wk_c, wv_c, wo_c, ln2_g,
           ln2_b, w_router, b_router, w_gate_up, b_gate_up, w_down, b_down, ln3_g, ln3_b):
    T, D = x2d.shape
    alpha = (2 * depth) ** 0.25
    rep = N_Q_HEADS // N_KV_HEADS
    q_w = N_Q_HEADS * HEAD_DIM
    kv_w = N_KV_HEADS * HEAD_DIM
    ssm_w = SSM_GROUP * SSM_GROUPS
    P = SSM_STATE

    o_k, o_v, o_s = q_w, q_w + kv_w, q_w + 2 * kv_w
    o_ga, o_gs = o_s + ssm_w, o_s + ssm_w + D
    wq = w_in[:, :o_k].reshape(D, N_KV_HEADS, rep, HEAD_DIM).transpose(0, 2, 1, 3).reshape(D, q_w).astype(BF16)
    wk = w_in[:, o_k:o_v].astype(BF16)
    wv = w_in[:, o_v:o_s].astype(BF16)
    wut = w_in[:, o_s:o_ga].T.astype(BF16)
    wga = w_in[:, o_ga:o_gs].astype(BF16)
    wgs = w_in[:, o_gs:].astype(BF16)
    wo_attn = w_attn_o.reshape(N_KV_HEADS, rep, HEAD_DIM, D).transpose(1, 0, 2, 3).reshape(q_w, D).astype(BF16)
    half = HEAD_DIM // 2
    inv_freq = jnp.power(ROPE_THETA, -jnp.arange(half, dtype=F32) / half)
    invf = jnp.tile(inv_freq, LANES // half)[None, :]
    sink_cols = jnp.repeat(sinks.astype(F32).reshape(N_KV_HEADS, rep), WINDOW, axis=1)[:, :, None]

    tm1 = _tile(T, 512)
    q2, k, v, ut, ga, gs = _inproj(x2d, pos2d, invf, wq, wk, wv, wut, wga, wgs, tm1)
    att = _swa(q2, k, v, sink_cols, wo_attn, ga, B, S, _tile(S, 512))

    lam_row = jnp.stack([jnp.concatenate([lam_re, lam_re], -1), jnp.concatenate([lam_im, lam_im], -1)], 1)
    lam_col = jnp.swapaxes(lam_row, 1, 2)
    bt_re = jnp.swapaxes(b_re, 1, 2)
    bt_im = jnp.swapaxes(b_im, 1, 2)
    bt_a = jnp.concatenate([bt_re, bt_im], -1)
    bt_b = jnp.concatenate([-bt_im, bt_re], -1)
    ct_re = jnp.swapaxes(c_re, 1, 2)
    ct_im = jnp.swapaxes(c_im, 1, 2)
    ct_a = jnp.concatenate([ct_re, -ct_im], 1)
    ct_b = jnp.concatenate([-ct_im, -ct_re], 1)
    dsk = jnp.repeat(d_skip.reshape(SSM_GROUPS, SSM_GROUP), SSM_GROUP, axis=1)[:, :, None]
    toep, wst, wout, apow = _ssm_prep(lam_row, lam_col, log_dt.reshape(SSM_GROUPS, 1, 1), bt_a, bt_b,
                                      c_re, c_im, ct_a, ct_b, dsk)
    yt = _ssm_scan(ut, toep, wst, wout, apow, S // LANES)

    x1 = _mix(yt, att, gs, x2d, w_glu_a.astype(BF16), w_glu_b.astype(BF16), w_out.astype(BF16),
              ln1_g[None, :], ln1_b[None, :], alpha, _tile(T, 1024))

    kc, vc = _memkv(mem2d, wk_c.astype(BF16), wv_c.astype(BF16), _tile(mem2d.shape[0], 512))
    wr = jnp.zeros((D, LANES), F32).at[:, :N_EXPERTS].set(w_router).astype(BF16)
    br = jnp.full((1, LANES), NEG_BIG, F32).at[0, :N_EXPERTS].set(b_router)
    x2, logits = _cross(x1, kc, vc, wq_c.astype(BF16), wo_c.astype(BF16), ln2_g[None, :], ln2_b[None, :],
                        wr, br, alpha, S, _tile(S, 512))

    info, counts = _route(logits, _tile(T, 512))
    top_idx = info[:, :TOP_K].astype(jnp.int32)
    rank = info[:, TOP_K:2 * TOP_K].astype(jnp.int32)
    counts = counts[0, :N_EXPERTS].astype(jnp.int32)
    padded = (counts + MOE_BLOCK - 1) // MOE_BLOCK * MOE_BLOCK
    ends_pad = jnp.cumsum(padded)
    start_pad = ends_pad - padded
    n_assign = T * TOP_K
    n_blocks = -(-n_assign // MOE_BLOCK) + N_EXPERTS
    n_slots = n_blocks * MOE_BLOCK
    dest = (start_pad[top_idx] + rank).reshape(-1)
    slot_a = jnp.full((n_slots,), -1, jnp.int32).at[dest].set(jnp.arange(n_assign, dtype=jnp.int32),
                                                              unique_indices=True)
    slot_src = jnp.maximum(slot_a, 0) // TOP_K
    slot_id = jnp.arange(n_slots, dtype=jnp.int32)
    spare = n_assign + (slot_id // MOE_BLOCK % 2) * MOE_BLOCK + slot_id % MOE_BLOCK
    slot_dst = jnp.where(slot_a < 0, spare, (slot_a % TOP_K) * T + slot_a // TOP_K)
    block_start = jnp.arange(n_blocks, dtype=jnp.int32) * MOE_BLOCK
    bexp = jnp.minimum(jnp.sum(block_start[:, None] >= ends_pad[None, :], axis=1), N_EXPERTS - 1).astype(jnp.int32)
    n_used = (ends_pad[-1] // MOE_BLOCK).astype(jnp.int32)[None]
    n_gu_tiles = b_gate_up.shape[1] // GU_TILE
    bgu = b_gate_up.reshape(N_EXPERTS, n_gu_tiles, LANES, 2).transpose(0, 1, 3, 2).reshape(N_EXPERTS, 1, -1)
    yk = _moe(bexp, n_used, slot_src.reshape(n_blocks, 1, MOE_BLOCK), slot_dst.reshape(n_blocks, 1, MOE_BLOCK),
              x2, w_gate_up, bgu, w_down, b_down[:, None, :], n_assign + 2 * MOE_BLOCK)
    return _combine(yk, info, x2, ln3_g[None, :], ln3_b[None, :], alpha, _tile(T, 512))


def kernel(x, mem, positions, w_in, sinks, w_attn_o, lam_re, lam_im, log_dt, b_re, b_im, c_re, c_im, d_skip,
           w_glu_a, w_glu_b, w_out, ln1_g, ln1_b, wq_c, wk_c, wv_c, wo_c, ln2_g, ln2_b, w_router, b_router,
           w_gate_up, b_gate_up, w_down, b_down, ln3_g, ln3_b):
    B, S, D = x.shape
    depth = w_in.shape[0]
    x2d = x.reshape(B * S, D)
    mem2d = mem.reshape(-1, D)
    pos2d = positions.reshape(B * S, 1)
    per_layer = (w_in, sinks, w_attn_o, lam_re, lam_im, log_dt, b_re, b_im, c_re, c_im, d_skip, w_glu_a,
                 w_glu_b, w_out, ln1_g, ln1_b, wq_c, wk_c, wv_c, wo_c, ln2_g, ln2_b, w_router, b_router,
                 w_gate_up, b_gate_up, w_down, b_down, ln3_g, ln3_b)
    for l in range(depth):
        x2d = _layer(x2d, mem2d, pos2d, B, S, depth, *(w[l] for w in per_layer))
    return x2d.reshape(B, S, D)
```

```python
import functools

import jax
import jax.numpy as jnp
from jax import lax
from jax.experimental import pallas as pl
from jax.experimental.pallas import tpu as pltpu

N_Q_HEADS = 16
N_KV_HEADS = 2
HEAD_DIM = 64
WINDOW = 128
ROPE_THETA = 10000.0
SSM_GROUP = 16
SSM_GROUPS = 32
SSM_STATE = 64
N_CROSS_HEADS = 4
N_EXPERTS = 32
TOP_K = 4
SWIGLU_ALPHA = 1.702
SWIGLU_LIMIT = 7.0
MOE_BLOCK = 256
LN_EPS = 1e-5

LANES = 128
VMEM_LIMIT_BYTES = 56 * 1024 * 1024

NEG_BIG = -1e30
BF16 = jnp.bfloat16
F32 = jnp.float32
HI = lax.Precision.HIGHEST


def _cparams(n_axes=1):
    return pltpu.CompilerParams(dimension_semantics=("arbitrary",) * n_axes,
                                vmem_limit_bytes=VMEM_LIMIT_BYTES)


def _full_spec(shape):
    n = len(shape)
    return pl.BlockSpec(shape, lambda *_: (0,) * n)


def _dot(a, b):
    return jnp.dot(a, b, preferred_element_type=F32)


def _dot_nt(a, b):
    return lax.dot_general(a, b, (((1,), (1,)), ((), ())), preferred_element_type=F32)


def _layer_norm(y, g, b):
    mu = jnp.mean(y, axis=-1, keepdims=True)
    d = y - mu
    var = jnp.mean(d * d, axis=-1, keepdims=True)
    return d * lax.rsqrt(var + LN_EPS) * g + b


def _rope(t, cos, sin_signed, first_half):
    half = HEAD_DIM // 2
    partner = jnp.where(first_half, pltpu.roll(t, LANES - half, axis=1), pltpu.roll(t, half, axis=1))
    return t * cos + partner * sin_signed


def _inproj_kernel(x_ref, pos_ref, invf_ref, wq_ref, wk_ref, wv_ref, wut_ref, wga_ref, wgs_ref,
                   q2_ref, k_ref, v_ref, ut_ref, ga_ref, gs_ref):
    tm = x_ref.shape[0]
    xb = x_ref[...].astype(BF16)
    ang = pos_ref[...].astype(F32) * invf_ref[...]
    cos = jnp.cos(ang)
    sin = jnp.sin(ang)
    first_half = (lax.broadcasted_iota(jnp.int32, (tm, LANES), 1) % HEAD_DIM) < (HEAD_DIM // 2)
    first_half_w = (lax.broadcasted_iota(jnp.int32, (WINDOW, LANES), 1) % HEAD_DIM) < (HEAD_DIM // 2)
    sin_signed = jnp.where(first_half, -sin, sin)

    q = _dot(xb, wq_ref[...])
    n_rep = q.shape[1] // LANES
    scale = HEAD_DIM ** -0.5
    for j in range(tm // WINDOW):
        rows = slice(j * WINDOW, (j + 1) * WINDOW)
        for r in range(n_rep):
            t = _rope(q[rows, r * LANES:(r + 1) * LANES], cos[rows], sin_signed[rows], first_half_w)
            base = (j * n_rep + r) * WINDOW
            q2_ref[base:base + WINDOW, :] = (t * scale).astype(BF16)
    k_ref[...] = _rope(_dot(xb, wk_ref[...]), cos, sin_signed, first_half).astype(BF16)
    v_ref[...] = _dot(xb, wv_ref[...]).astype(BF16)
    for j in range(tm // LANES):
        ut_ref[j] = _dot_nt(wut_ref[...], xb[j * LANES:(j + 1) * LANES, :])
    ga_ref[...] = jax.nn.sigmoid(_dot(xb, wga_ref[...])).astype(BF16)
    gs_ref[...] = jax.nn.sigmoid(_dot(xb, wgs_ref[...])).astype(BF16)


def _inproj(x2d, pos2d, invf, wq, wk, wv, wut, wga, wgs, tm):
    T, D = x2d.shape
    n_rep = wq.shape[1] // LANES
    ssm_w = wut.shape[0]
    row = lambda i: (i, 0)
    out_shape = (
        jax.ShapeDtypeStruct((T * n_rep, LANES), BF16),
        jax.ShapeDtypeStruct((T, LANES), BF16),
        jax.ShapeDtypeStruct((T, LANES), BF16),
        jax.ShapeDtypeStruct((T // LANES, ssm_w, LANES), F32),
        jax.ShapeDtypeStruct((T, D), BF16),
        jax.ShapeDtypeStruct((T, D), BF16),
    )
    in_specs = [pl.BlockSpec((tm, D), row), pl.BlockSpec((tm, 1), row), _full_spec(invf.shape),
                _full_spec(wq.shape), _full_spec(wk.shape), _full_spec(wv.shape),
                _full_spec(wut.shape), _full_spec(wga.shape), _full_spec(wgs.shape)]
    out_specs = (pl.BlockSpec((tm * n_rep, LANES), row), pl.BlockSpec((tm, LANES), row),
                 pl.BlockSpec((tm, LANES), row),
                 pl.BlockSpec((tm // LANES, ssm_w, LANES), lambda i: (i, 0, 0)),
                 pl.BlockSpec((tm, D), row), pl.BlockSpec((tm, D), row))
    return pl.pallas_call(_inproj_kernel, name="inproj", grid=(T // tm,), in_specs=in_specs, out_specs=out_specs,
                          out_shape=out_shape, compiler_params=_cparams())(
        x2d, pos2d, invf, wq, wk, wv, wut, wga, wgs)


def _swa_kernel(q2_ref, kc_ref, kp_ref, vc_ref, vp_ref, sink_ref, wo_ref, ga_ref, out_ref, cat_ref):
    i = pl.program_id(1)
    tq = kc_ref.shape[0]
    n_sub = tq // WINDOW
    rep = N_Q_HEADS // N_KV_HEADS
    rows_all = rep * WINDOW
    kfull = jnp.concatenate([kp_ref[...], kc_ref[...]], axis=0)
    vfull = jnp.concatenate([vp_ref[...], vc_ref[...]], axis=0)
    lane = lax.broadcasted_iota(jnp.int32, (2 * WINDOW, LANES), 1)
    qi = lax.broadcasted_iota(jnp.int32, (rows_all, 2 * WINDOW), 0) % WINDOW
    ci = lax.broadcasted_iota(jnp.int32, (rows_all, 2 * WINDOW), 1)
    local = (ci > qi) & (ci <= qi + WINDOW)
    out_lane = lax.broadcasted_iota(jnp.int32, (rows_all, LANES), 1)
    for j in range(n_sub):
        qs = q2_ref[j * rows_all:(j + 1) * rows_all, :]
        kb = kfull[j * WINDOW:(j + 2) * WINDOW, :]
        vb = vfull[j * WINDOW:(j + 2) * WINDOW, :]
        mask = local
        if j == 0:
            mask = mask & ((ci >= WINDOW) | (i > 0))
        o = None
        for g in range(N_KV_HEADS):
            in_group = (lane >= g * HEAD_DIM) & (lane < (g + 1) * HEAD_DIM)
            kg = jnp.where(in_group, kb, jnp.zeros_like(kb))
            s = _dot_nt(qs, kg)
            s = jnp.where(mask, s, NEG_BIG)
            sink = sink_ref[g]
            m = jnp.maximum(jnp.max(s, axis=-1, keepdims=True), sink)
            p = jnp.exp(s - m)
            denom = jnp.sum(p, axis=-1, keepdims=True) + jnp.exp(sink - m)
            w = (p / denom).astype(BF16)
            og = _dot(w, vb)
            o = og if o is None else jnp.where(out_lane < g * HEAD_DIM, o, og)
        ob = o.astype(BF16)
        for r in range(rep):
            cat_ref[j * WINDOW:(j + 1) * WINDOW, r * LANES:(r + 1) * LANES] = ob[r * WINDOW:(r + 1) * WINDOW, :]
    attn = _dot(cat_ref[...], wo_ref[...])
    out_ref[...] = (attn * ga_ref[...].astype(F32)).astype(BF16)


def _swa(q2, k, v, sink_cols, wo, ga, B, S, tq):
    T, D = ga.shape
    rep = N_Q_HEADS // N_KV_HEADS
    n_i = S // tq
    n_sub = tq // WINDOW
    in_specs = [
        pl.BlockSpec((tq * rep, LANES), lambda b, i: (b * n_i + i, 0)),
        pl.BlockSpec((tq, LANES), lambda b, i: (b * n_i + i, 0)),
        pl.BlockSpec((WINDOW, LANES), lambda b, i: (b * (S // WINDOW) + jnp.maximum(i * n_sub - 1, 0), 0)),
        pl.BlockSpec((tq, LANES), lambda b, i: (b * n_i + i, 0)),
        pl.BlockSpec((WINDOW, LANES), lambda b, i: (b * (S // WINDOW) + jnp.maximum(i * n_sub - 1, 0), 0)),
        _full_spec(sink_cols.shape), _full_spec(wo.shape),
        pl.BlockSpec((tq, D), lambda b, i: (b * n_i + i, 0)),
    ]
    return pl.pallas_call(
        _swa_kernel, name="swa", grid=(B, n_i), in_specs=in_specs,
        out_specs=pl.BlockSpec((tq, D), lambda b, i: (b * n_i + i, 0)),
        out_shape=jax.ShapeDtypeStruct((T, D), BF16),
        scratch_shapes=[pltpu.VMEM((tq, D), BF16)],
        compiler_params=_cparams(2))(q2, k, k, v, v, sink_cols, wo, ga)


def _ssm_prep_kernel(lam_row_ref, lam_col_ref, dt_ref, bt_a_ref, bt_b_ref, c_re_ref, c_im_ref,
                     ct_a_ref, ct_b_ref, dsk_ref, toep_ref, wst_ref, wout_ref, apow_ref):
    L = LANES
    P = SSM_STATE
    dt = jnp.exp(dt_ref[0])
    lr2 = lam_row_ref[0, 0:1, :]
    li2 = lam_row_ref[0, 1:2, :]
    mag = jnp.exp(lr2 * dt)
    ar = mag * jnp.cos(li2 * dt)
    ai = mag * jnp.sin(li2 * dt)
    den = lr2 * lr2 + li2 * li2
    f_re = ((ar - 1.0) * lr2 + ai * li2) / den
    f_im = (ai * lr2 - (ar - 1.0) * li2) / den
    bt_a = bt_a_ref[0]
    bt_b = bt_b_ref[0]
    bb_a = f_re * bt_a + f_im * bt_b
    bb_b = f_re * bt_b - f_im * bt_a

    tau_rev = (L - 1 - lax.broadcasted_iota(jnp.int32, (L, 2 * P), 0)).astype(F32)
    g_mag = jnp.exp(lr2 * dt * tau_rev)
    g_re = g_mag * jnp.cos(li2 * dt * tau_rev)
    g_im = g_mag * jnp.sin(li2 * dt * tau_rev)
    for i in range(SSM_GROUP):
        wst_ref[0, i * L:(i + 1) * L, :] = (g_re * bb_a[i:i + 1, :] + g_im * bb_b[i:i + 1, :]).astype(BF16)

    lane2 = lax.broadcasted_iota(jnp.int32, (1, 2 * P), 1)
    for kk in range(4):
        n = float(L * (1 << kk))
        pm = jnp.exp(lr2 * dt * n)
        p_re = pm * jnp.cos(li2 * dt * n)
        p_im = pm * jnp.sin(li2 * dt * n)
        apow_ref[0, 2 * kk:2 * kk + 1, :] = p_re
        apow_ref[0, 2 * kk + 1:2 * kk + 2, :] = jnp.where(lane2 < P, -p_im, p_im)

    lrc = lam_col_ref[0, :, 0:1]
    lic = lam_col_ref[0, :, 1:2]
    tau = lax.broadcasted_iota(jnp.int32, (2 * P, L), 1).astype(F32)
    e0_mag = jnp.exp(lrc * dt * tau)
    e0_re = e0_mag * jnp.cos(lic * dt * tau)
    e0_im = e0_mag * jnp.sin(lic * dt * tau)
    e1_mag = jnp.exp(lrc * dt * (tau + 1.0))
    e1_re = e1_mag * jnp.cos(lic * dt * (tau + 1.0))
    e1_im = e1_mag * jnp.sin(lic * dt * (tau + 1.0))

    ct_a = ct_a_ref[0]
    ct_b = ct_b_ref[0]
    for o in range(SSM_GROUP):
        wout_ref[0, :, o * L:(o + 1) * L] = (ct_a[:, o:o + 1] * e1_re + ct_b[:, o:o + 1] * e1_im).astype(BF16)

    c_re = c_re_ref[0]
    c_im = c_im_ref[0]
    bb_re = bb_a[:, :P]
    bb_im = bb_a[:, P:]
    m_re = (c_re[:, None, :] * bb_re[None, :, :] - c_im[:, None, :] * bb_im[None, :, :])
    m_im = (c_re[:, None, :] * bb_im[None, :, :] + c_im[:, None, :] * bb_re[None, :, :])
    m_re = m_re.reshape(SSM_GROUP * SSM_GROUP, P)
    m_im = m_im.reshape(SSM_GROUP * SSM_GROUP, P)
    kt = (jnp.dot(m_re, e0_re[:P, :], precision=HI, preferred_element_type=F32)
          - jnp.dot(m_im, e0_im[:P, :], precision=HI, preferred_element_type=F32))
    rowi = lax.broadcasted_iota(jnp.int32, (SSM_GROUP * SSM_GROUP, L), 0)
    coli = lax.broadcasted_iota(jnp.int32, (SSM_GROUP * SSM_GROUP, L), 1)
    dsk = dsk_ref[0]
    kt = kt + jnp.where((coli == 0) & ((rowi // SSM_GROUP) == (rowi % SSM_GROUP)), dsk, 0.0)

    cc = lax.broadcasted_iota(jnp.int32, (L, L), 0)
    cp = lax.broadcasted_iota(jnp.int32, (L, L), 1)
    causal = cp >= cc
    for o in range(SSM_GROUP):
        for i in range(SSM_GROUP):
            kv = kt[o * SSM_GROUP + i:o * SSM_GROUP + i + 1, :]
            blk = pltpu.roll(jnp.broadcast_to(kv, (L, L)), 0, axis=1, stride=1, stride_axis=0)
            blk = jnp.where(causal, blk, 0.0)
            toep_ref[0, i * L:(i + 1) * L, o * L:(o + 1) * L] = blk.astype(BF16)


def _ssm_prep(lam_row, lam_col, log_dt, bt_a, bt_b, c_re, c_im, ct_a, ct_b, dsk):
    G = SSM_GROUPS
    KW = SSM_GROUP * LANES
    g3 = lambda g: (g, 0, 0)
    ins = [lam_row, lam_col, log_dt, bt_a, bt_b, c_re, c_im, ct_a, ct_b, dsk]
    in_specs = [pl.BlockSpec((1,) + a.shape[1:], g3) for a in ins]
    out_shape = (jax.ShapeDtypeStruct((G, KW, KW), BF16),
                 jax.ShapeDtypeStruct((G, KW, 2 * SSM_STATE), BF16),
                 jax.ShapeDtypeStruct((G, 2 * SSM_STATE, KW), BF16),
                 jax.ShapeDtypeStruct((G, 8, 2 * SSM_STATE), F32))
    out_specs = tuple(pl.BlockSpec((1,) + s.shape[1:], g3) for s in out_shape)
    return pl.pallas_call(_ssm_prep_kernel, name="ssm_prep", grid=(G,), in_specs=in_specs, out_specs=out_specs,
                          out_shape=out_shape, compiler_params=_cparams())(*ins)


def _ssm_scan_kernel(ut_ref, toep_ref, wst_ref, wout_ref, apow_ref, yt_ref, *, n_chunks):
    nb = ut_ref.shape[0]
    P = SSM_STATE
    lhs = jnp.concatenate([ut_ref[:, i, :] for i in range(SSM_GROUP)], axis=1).astype(BF16)
    y = _dot(lhs, toep_ref[0])
    st = _dot(lhs, wst_ref[0])
    srow = lax.broadcasted_iota(jnp.int32, (nb, 2 * P), 0) % n_chunks
    kk = 0
    while (1 << kk) < n_chunks:
        d = 1 << kk
        pa = apow_ref[0, 2 * kk:2 * kk + 1, :]
        pb = apow_ref[0, 2 * kk + 1:2 * kk + 2, :]
        prev = pltpu.roll(st, d, axis=0)
        prev = prev * pa + pltpu.roll(prev, P, axis=1) * pb
        st = st + jnp.where(srow >= d, prev, 0.0)
        kk += 1
    h0 = jnp.where(srow >= 1, pltpu.roll(st, 1, axis=0), 0.0)
    y = y + _dot(h0.astype(BF16), wout_ref[0])
    for o in range(SSM_GROUP):
        yt_ref[o] = y[:, o * LANES:(o + 1) * LANES]


def _ssm_scan(ut, toep, wst, wout, apow, n_chunks):
    nb, ssm_w, _ = ut.shape
    G = SSM_GROUPS
    g3 = lambda g: (g, 0, 0)
    in_specs = [pl.BlockSpec((nb, SSM_GROUP, LANES), lambda g: (0, g, 0)),
                pl.BlockSpec((1,) + toep.shape[1:], g3), pl.BlockSpec((1,) + wst.shape[1:], g3),
                pl.BlockSpec((1,) + wout.shape[1:], g3), pl.BlockSpec((1,) + apow.shape[1:], g3)]
    return pl.pallas_call(
        functools.partial(_ssm_scan_kernel, n_chunks=n_chunks), name="ssm_scan", grid=(G,), in_specs=in_specs,
        out_specs=pl.BlockSpec((SSM_GROUP, nb, LANES), g3),
        out_shape=jax.ShapeDtypeStruct((ssm_w, nb, LANES), F32),
        compiler_params=_cparams())(ut, toep, wst, wout, apow)


def _mix_kernel(yt_ref, att_ref, gs_ref, x_ref, wa_ref, wb_ref, wo_ref, g_ref, b_ref, out_ref, zt_ref,
                *, alpha):
    n_blk = yt_ref.shape[1]
    for j in range(n_blk):
        y = yt_ref[:, j, :]
        z = jax.nn.gelu(y, approximate=True)
        zt_ref[j * LANES:(j + 1) * LANES, :] = z.T.astype(BF16)
    z = zt_ref[...]
    ssm_out = _dot(z, wa_ref[...]) * jax.nn.sigmoid(_dot(z, wb_ref[...]))
    mixed = att_ref[...].astype(F32) + gs_ref[...].astype(F32) * ssm_out
    y = alpha * x_ref[...] + _dot(mixed.astype(BF16), wo_ref[...])
    out_ref[...] = _layer_norm(y, g_ref[...], b_ref[...])


def _mix(yt, att, gs, x2d, wa, wb, wo, g, b, alpha, tm):
    T, D = x2d.shape
    ssm_w = yt.shape[0]
    row = lambda i: (i, 0)
    in_specs = [pl.BlockSpec((ssm_w, tm // LANES, LANES), lambda i: (0, i, 0)),
                pl.BlockSpec((tm, D), row), pl.BlockSpec((tm, D), row), pl.BlockSpec((tm, D), row),
                _full_spec(wa.shape), _full_spec(wb.shape), _full_spec(wo.shape),
                _full_spec(g.shape), _full_spec(b.shape)]
    return pl.pallas_call(functools.partial(_mix_kernel, alpha=alpha), name="mix", grid=(T // tm,), in_specs=in_specs,
                          out_specs=pl.BlockSpec((tm, D), row),
                          out_shape=jax.ShapeDtypeStruct((T, D), F32),
                          scratch_shapes=[pltpu.VMEM((tm, ssm_w), BF16)],
                          compiler_params=_cparams())(yt, att, gs, x2d, wa, wb, wo, g, b)


def _memkv_kernel(m_ref, wk_ref, wv_ref, k_ref, v_ref):
    mb = m_ref[...].astype(BF16)
    k_ref[...] = _dot(mb, wk_ref[...]).astype(BF16)
    v_ref[...] = _dot(mb, wv_ref[...]).astype(BF16)


def _memkv(mem2d, wk, wv, tm):
    R, D = mem2d.shape
    row = lambda i: (i, 0)
    return pl.pallas_call(_memkv_kernel, name="memkv", grid=(R // tm,),
                          in_specs=[pl.BlockSpec((tm, D), row), _full_spec(wk.shape), _full_spec(wv.shape)],
                          out_specs=(pl.BlockSpec((tm, D), row), pl.BlockSpec((tm, D), row)),
                          out_shape=(jax.ShapeDtypeStruct((R, D), BF16), jax.ShapeDtypeStruct((R, D), BF16)),
                          compiler_params=_cparams())(mem2d, wk, wv)


def _cross_kernel(x_ref, k_ref, v_ref, wq_ref, wo_ref, g_ref, b_ref, wr_ref, br_ref, out_ref, logit_ref,
                  cat_ref, *, alpha):
    x = x_ref[...]
    D = x.shape[1]
    hd = D // N_CROSS_HEADS
    q = (_dot(x.astype(BF16), wq_ref[...]) * (hd ** -0.5)).astype(BF16)
    for h in range(N_CROSS_HEADS):
        cols = slice(h * hd, (h + 1) * hd)
        s = _dot_nt(q[:, cols], k_ref[:, cols])
        m = jnp.max(s, axis=-1, keepdims=True)
        p = jnp.exp(s - m)
        w = (p / jnp.sum(p, axis=-1, keepdims=True)).astype(BF16)
        cat_ref[:, cols] = _dot(w, v_ref[:, cols]).astype(BF16)
    y = alpha * x + _dot(cat_ref[...], wo_ref[...])
    x2 = _layer_norm(y, g_ref[...], b_ref[...])
    out_ref[...] = x2
    logit_ref[...] = _dot(x2.astype(BF16), wr_ref[...]) + br_ref[...]


def _cross(x1, kc, vc, wq, wo, g, b, wr, br, alpha, S, tm):
    T, D = x1.shape
    n_mem = kc.shape[0] // (T // S)
    row = lambda i: (i, 0)
    per_b = S // tm
    in_specs = [pl.BlockSpec((tm, D), row),
                pl.BlockSpec((n_mem, D), lambda i: (i // per_b, 0)),
                pl.BlockSpec((n_mem, D), lambda i: (i // per_b, 0)),
                _full_spec(wq.shape), _full_spec(wo.shape), _full_spec(g.shape), _full_spec(b.shape),
                _full_spec(wr.shape), _full_spec(br.shape)]
    return pl.pallas_call(functools.partial(_cross_kernel, alpha=alpha), name="cross", grid=(T // tm,), in_specs=in_specs,
                          out_specs=(pl.BlockSpec((tm, D), row), pl.BlockSpec((tm, LANES), row)),
                          out_shape=(jax.ShapeDtypeStruct((T, D), F32), jax.ShapeDtypeStruct((T, LANES), F32)),
                          scratch_shapes=[pltpu.VMEM((tm, D), BF16)],
                          compiler_params=_cparams())(x1, kc, vc, wq, wo, g, b, wr, br)


def _route_kernel(logit_ref, info_ref, count_ref, carry_ref):
    tm = logit_ref.shape[0]

    @pl.when(pl.program_id(0) == 0)
    def _():
        carry_ref[...] = jnp.zeros_like(carry_ref)

    vals = logit_ref[...]
    lane = lax.broadcasted_iota(jnp.int32, vals.shape, 1).astype(F32)
    sels, tops = [], []
    for _ in range(TOP_K):
        m = jnp.max(vals, axis=-1, keepdims=True)
        idx = jnp.min(jnp.where(vals == m, lane, float(LANES)), axis=-1, keepdims=True)
        sel = lane == idx
        vals = jnp.where(sel, -jnp.inf, vals)
        sels.append((sel, idx))
        tops.append(m)
    es = [jnp.exp(t - tops[0]) for t in tops]
    denom = es[0]
    for e in es[1:]:
        denom = denom + e
    onehot = jnp.zeros(vals.shape, F32)
    for sel, _ in sels:
        onehot = onehot + jnp.where(sel, 1.0, 0.0)
    r = lax.broadcasted_iota(jnp.int32, (tm, tm), 0)
    c = lax.broadcasted_iota(jnp.int32, (tm, tm), 1)
    tri = jnp.where(c < r, 1.0, 0.0).astype(BF16)
    cum = _dot(tri, onehot.astype(BF16)) + carry_ref[...]
    info = jnp.zeros(vals.shape, F32)
    for k, (sel, idx) in enumerate(sels):
        rank = jnp.sum(jnp.where(sel, cum, 0.0), axis=-1, keepdims=True)
        info = jnp.where(lane == k, idx, info)
        info = jnp.where(lane == TOP_K + k, rank, info)
        info = jnp.where(lane == 2 * TOP_K + k, es[k] / denom, info)
    info_ref[...] = info
    carry_ref[...] = carry_ref[...] + jnp.sum(onehot, axis=0, keepdims=True)
    count_ref[...] = carry_ref[...]


def _route(logits, tm):
    T = logits.shape[0]
    row = lambda i: (i, 0)
    return pl.pallas_call(_route_kernel, name="route", grid=(T // tm,),
                          in_specs=[pl.BlockSpec((tm, LANES), row)],
                          out_specs=(pl.BlockSpec((tm, LANES), row), _full_spec((1, LANES))),
                          out_shape=(jax.ShapeDtypeStruct((T, LANES), F32),
                                     jax.ShapeDtypeStruct((1, LANES), F32)),
                          scratch_shapes=[pltpu.VMEM((1, LANES), F32)],
                          compiler_params=_cparams())(logits)


GU_TILE = 2 * LANES


def _moe_kernel(bexp_ref, nused_ref, src_cur_ref, src_nxt_ref, dst_prev_ref, dst_cur_ref, x_hbm, wgu_ref,
                bgu_ref, wd_ref, bd_ref, y_hbm, xbuf0, xbuf1, ybuf0, ybuf1, xb_s, wgu_s, wd_s, gsem, ssem):
    j = pl.program_id(0)
    n_used = nused_ref[0]
    xbufs, ybufs = (xbuf0, xbuf1), (ybuf0, ybuf1)
    rows = xbuf0.shape[0]
    n_tiles = wgu_s.shape[1] // GU_TILE

    def gather(idx_ref, s):
        for r in range(rows):
            tok = idx_ref[0, 0, r]
            pltpu.make_async_copy(x_hbm.at[pl.ds(tok, 1)], xbufs[s].at[pl.ds(r, 1)], gsem.at[s]).start()

    def scatter(idx_ref, s):
        for r in range(rows):
            d = idx_ref[0, 0, r]
            pltpu.make_async_copy(ybufs[s].at[pl.ds(r, 1)], y_hbm.at[pl.ds(d, 1)], ssem.at[s]).start()

    def wait_gather(s):
        pltpu.make_async_copy(x_hbm.at[pl.ds(0, rows)], xbufs[s], gsem.at[s]).wait()

    def wait_scatter(s):
        pltpu.make_async_copy(ybufs[s], y_hbm.at[pl.ds(0, rows)], ssem.at[s]).wait()

    @pl.when(j == 0)
    def _():
        gather(src_cur_ref, 0)
        ybuf0[...] = jnp.zeros_like(ybuf0)
        ybuf1[...] = jnp.zeros_like(ybuf1)
        n_real = y_hbm.shape[0] - 2 * rows
        pltpu.make_async_copy(ybuf0, y_hbm.at[pl.ds(n_real, rows)], ssem.at[0]).start()

    @pl.when(j < n_used)
    def _():
        @pl.when((j == 0) | (bexp_ref[j] != bexp_ref[jnp.maximum(j - 1, 0)]))
        def _():
            k = lax.broadcasted_iota(jnp.int32, (GU_TILE, GU_TILE), 0)
            n = lax.broadcasted_iota(jnp.int32, (GU_TILE, GU_TILE), 1)
            perm = jnp.where(k == jnp.where(n < LANES, 2 * n, 2 * (n - LANES) + 1), 1.0, 0.0).astype(BF16)
            for t in range(n_tiles):
                cols = slice(t * GU_TILE, (t + 1) * GU_TILE)
                wgu_s[:, cols] = _dot(wgu_ref[0, :, cols].astype(BF16), perm).astype(BF16)
            wd_s[...] = wd_ref[0].astype(BF16)

        def step(cur):
            nxt = 1 - cur
            wait_gather(cur)
            wait_scatter(cur)
            xb_s[...] = xbufs[cur][...].astype(BF16)
            gather(src_nxt_ref, nxt)
            scatter(dst_prev_ref, nxt)
            xb = xb_s[...]
            hs = []
            for t in range(n_tiles):
                cols = slice(t * GU_TILE, (t + 1) * GU_TILE)
                gu = _dot(xb, wgu_s[:, cols]) + bgu_ref[0, :, cols]
                gate = jnp.minimum(gu[:, :LANES], SWIGLU_LIMIT)
                lin = jnp.clip(gu[:, LANES:], -SWIGLU_LIMIT, SWIGLU_LIMIT)
                hs.append((gate * jax.nn.sigmoid(SWIGLU_ALPHA * gate) * (lin + 1.0)).astype(BF16))
            h = jnp.concatenate(hs, axis=1)
            ybufs[cur][...] = _dot(h, wd_s[...]) + bd_ref[0]

            @pl.when(j == n_used - 1)
            def _():
                scatter(dst_cur_ref, cur)
                wait_gather(nxt)
                wait_scatter(nxt)
                wait_scatter(cur)

        @pl.when(j % 2 == 0)
        def _():
            step(0)

        @pl.when(j % 2 == 1)
        def _():
            step(1)


def _moe(bexp, n_used, slot_src, slot_dst, slot_dst_prev, x2, wgu, bgu, wd, bd, n_out_rows):
    n_blocks = bexp.shape[0]
    T, D = x2.shape
    F2 = wgu.shape[2]
    F = wd.shape[1]
    wmap = lambda j, be, nu: (be[j], 0, 0)
    cur = lambda j, be, nu: (j, 0, 0)
    smem = pltpu.SMEM
    in_specs = [
        pl.BlockSpec((1, 1, MOE_BLOCK), cur, memory_space=smem),
        pl.BlockSpec((1, 1, MOE_BLOCK), lambda j, be, nu: (jnp.minimum(j + 1, n_blocks - 1), 0, 0),
                     memory_space=smem),
        pl.BlockSpec((1, 1, MOE_BLOCK), cur, memory_space=smem),
        pl.BlockSpec((1, 1, MOE_BLOCK), cur, memory_space=smem),
        pl.BlockSpec(memory_space=pl.ANY),
        pl.BlockSpec((1, D, F2), wmap), pl.BlockSpec((1, 1, F2), wmap),
        pl.BlockSpec((1, F, D), wmap), pl.BlockSpec((1, 1, D), wmap),
    ]
    grid_spec = pltpu.PrefetchScalarGridSpec(
        num_scalar_prefetch=2, grid=(n_blocks,), in_specs=in_specs,
        out_specs=pl.BlockSpec(memory_space=pl.ANY),
        scratch_shapes=[pltpu.VMEM((MOE_BLOCK, D), F32), pltpu.VMEM((MOE_BLOCK, D), F32),
                        pltpu.VMEM((MOE_BLOCK, D), F32), pltpu.VMEM((MOE_BLOCK, D), F32),
                        pltpu.VMEM((MOE_BLOCK, D), BF16),
                        pltpu.VMEM((D, F2), BF16), pltpu.VMEM((F, D), BF16),
                        pltpu.SemaphoreType.DMA((2,)), pltpu.SemaphoreType.DMA((2,))])
    return pl.pallas_call(_moe_kernel, name="moe", grid_spec=grid_spec,
                          out_shape=jax.ShapeDtypeStruct((n_out_rows, D), F32),
                          compiler_params=_cparams())(
        bexp, n_used, slot_src, slot_src, slot_dst_prev, slot_dst, x2, wgu, bgu, wd, bd)


def _combine_kernel(y0_ref, y1_ref, y2_ref, y3_ref, info_ref, x_ref, g_ref, b_ref, out_ref, *, alpha):
    info = info_ref[...]
    acc = alpha * x_ref[...]
    for k, y_ref in enumerate((y0_ref, y1_ref, y2_ref, y3_ref)):
        acc = acc + info[:, 2 * TOP_K + k:2 * TOP_K + k + 1] * y_ref[...]
    out_ref[...] = _layer_norm(acc, g_ref[...], b_ref[...])


def _combine(yk, info, x2, g, b, alpha, tm):
    T, D = x2.shape
    n_t = T // tm
    row = lambda i: (i, 0)
    in_specs = [pl.BlockSpec((tm, D), (lambda i, k=k: (k * n_t + i, 0))) for k in range(TOP_K)]
    in_specs += [pl.BlockSpec((tm, LANES), row), pl.BlockSpec((tm, D), row),
                 _full_spec(g.shape), _full_spec(b.shape)]
    return pl.pallas_call(functools.partial(_combine_kernel, alpha=alpha), name="combine", grid=(n_t,), in_specs=in_specs,
                          out_specs=pl.BlockSpec((tm, D), row),
                          out_shape=jax.ShapeDtypeStruct((T, D), F32),
                          compiler_params=_cparams())(yk, yk, yk, yk, info, x2, g, b)


def _tile(n, pref):
    return pref if n % pref == 0 else n


def _layer(x2d, mem2d, pos2d, B, S, depth, w_in, sinks, w_attn_o, lam_re, lam_im, log_dt, b_re, b_im,
           c_re, c_im, d_skip, w_glu_a, w_glu_b, w_out, ln1_g, ln1_b, wq_c, wk_c, wv_c, wo_c, ln2_g,
           ln2_b, w_router, b_router, w_gate_up, b_gate_up, w_down, b_down, ln3_g, ln3_b):
    T, D = x2d.shape
    alpha = (2 * depth) ** 0.25
    rep = N_Q_HEADS // N_KV_HEADS
    q_w = N_Q_HEADS * HEAD_DIM
    kv_w = N_KV_HEADS * HEAD_DIM
    ssm_w = SSM_GROUP * SSM_GROUPS
    P = SSM_STATE

    o_k, o_v, o_s = q_w, q_w + kv_w, q_w + 2 * kv_w
    o_ga, o_gs = o_s + ssm_w, o_s + ssm_w + D
    wq = w_in[:, :o_k].reshape(D, N_KV_HEADS, rep, HEAD_DIM).transpose(0, 2, 1, 3).reshape(D, q_w).astype(BF16)
    wk = w_in[:, o_k:o_v].astype(BF16)
    wv = w_in[:, o_v:o_s].astype(BF16)
    wut = w_in[:, o_s:o_ga].T.astype(BF16)
    wga = w_in[:, o_ga:o_gs].astype(BF16)
    wgs = w_in[:, o_gs:].astype(BF16)
    wo_attn = w_attn_o.reshape(N_KV_HEADS, rep, HEAD_DIM, D).transpose(1, 0, 2, 3).reshape(q_w, D).astype(BF16)
    half = HEAD_DIM // 2
    inv_freq = jnp.power(ROPE_THETA, -jnp.arange(half, dtype=F32) / half)
    invf = jnp.tile(inv_freq, LANES // half)[None, :]
    sink_cols = jnp.repeat(sinks.astype(F32).reshape(N_KV_HEADS, rep), WINDOW, axis=1)[:, :, None]

    tm1 = _tile(T, 512)
    q2, k, v, ut, ga, gs = _inproj(x2d, pos2d, invf, wq, wk, wv, wut, wga, wgs, tm1)
    att = _swa(q2, k, v, sink_cols, wo_attn, ga, B, S, _tile(S, 512))

    lam_row = jnp.stack([jnp.concatenate([lam_re, lam_re], -1), jnp.concatenate([lam_im, lam_im], -1)], 1)
    lam_col = jnp.swapaxes(lam_row, 1, 2)
    bt_re = jnp.swapaxes(b_re, 1, 2)
    bt_im = jnp.swapaxes(b_im, 1, 2)
    bt_a = jnp.concatenate([bt_re, bt_im], -1)
    bt_b = jnp.concatenate([-bt_im, bt_re], -1)
    ct_re = jnp.swapaxes(c_re, 1, 2)
    ct_im = jnp.swapaxes(c_im, 1, 2)
    ct_a = jnp.concatenate([ct_re, -ct_im], 1)
    ct_b = jnp.concatenate([-ct_im, -ct_re], 1)
    dsk = jnp.repeat(d_skip.reshape(SSM_GROUPS, SSM_GROUP), SSM_GROUP, axis=1)[:, :, None]
    toep, wst, wout, apow = _ssm_prep(lam_row, lam_col, log_dt.reshape(SSM_GROUPS, 1, 1), bt_a, bt_b,
                                      c_re, c_im, ct_a, ct_b, dsk)
    yt = _ssm_scan(ut, toep, wst, wout, apow, S // LANES)

    x1 = _mix(yt, att, gs, x2d, w_glu_a.astype(BF16), w_glu_b.astype(BF16), w_out.astype(BF16),
              ln1_g[None, :], ln1_b[None, :], alpha, _tile(T, 1024))

    kc, vc = _memkv(mem2d, wk_c.astype(BF16), wv_c.astype(BF16), _tile(mem2d.shape[0], 512))
    wr = jnp.zeros((D, LANES), F32).at[:, :N_EXPERTS].set(w_router).astype(BF16)
    br = jnp.full((1, LANES), NEG_BIG, F32).at[0, :N_EXPERTS].set(b_router)
    x2, logits = _cross(x1, kc, vc, wq_c.astype(BF16), wo_c.astype(BF16), ln2_g[None, :], ln2_b[None, :],
                        wr, br, alpha, S, _tile(S, 512))

    info, counts = _route(logits, _tile(T, 512))
    top_idx = info[:, :TOP_K].astype(jnp.int32)
    rank = info[:, TOP_K:2 * TOP_K].astype(jnp.int32)
    counts = counts[0, :N_EXPERTS].astype(jnp.int32)
    padded = (counts + MOE_BLOCK - 1) // MOE_BLOCK * MOE_BLOCK
    ends_pad = jnp.cumsum(padded)
    start_pad = ends_pad - padded
    n_assign = T * TOP_K
    n_blocks = -(-n_assign // MOE_BLOCK) + N_EXPERTS
    n_slots = n_blocks * MOE_BLOCK
    dest = (start_pad[top_idx] + rank).reshape(-1)
    slot_a = jnp.full((n_slots,), -1, jnp.int32).at[dest].set(jnp.arange(n_assign, dtype=jnp.int32),
                                                              unique_indices=True)
    slot_src = jnp.maximum(slot_a, 0) // TOP_K
    slot_id = jnp.arange(n_slots, dtype=jnp.int32)
    spare = n_assign + (slot_id // MOE_BLOCK % 2) * MOE_BLOCK + slot_id % MOE_BLOCK
    slot_dst = jnp.where(slot_a < 0, spare, (slot_a % TOP_K) * T + slot_a // TOP_K)
    block_start = jnp.arange(n_blocks, dtype=jnp.int32) * MOE_BLOCK
    bexp = jnp.minimum(jnp.sum(block_start[:, None] >= ends_pad[None, :], axis=1), N_EXPERTS - 1).astype(jnp.int32)
    n_used = (ends_pad[-1] // MOE_BLOCK).astype(jnp.int32)[None]
    n_gu_tiles = b_gate_up.shape[1] // GU_TILE
    bgu = b_gate_up.reshape(N_EXPERTS, n_gu_tiles, LANES, 2).transpose(0, 1, 3, 2).reshape(N_EXPERTS, 1, -1)
    slot_dst_prev = jnp.concatenate([n_assign + MOE_BLOCK + jnp.arange(MOE_BLOCK, dtype=jnp.int32),
                                     slot_dst[:-MOE_BLOCK]])
    yk = _moe(bexp, n_used, slot_src.reshape(n_blocks, 1, MOE_BLOCK), slot_dst.reshape(n_blocks, 1, MOE_BLOCK),
              slot_dst_prev.reshape(n_blocks, 1, MOE_BLOCK), x2, w_gate_up, bgu, w_down, b_down[:, None, :],
              n_assign + 2 * MOE_BLOCK)
    return _combine(yk, info, x2, ln3_g[None, :], ln3_b[None, :], alpha, _tile(T, 512))


def kernel(x, mem, positions, w_in, sinks, w_attn_o, lam_re, lam_im, log_dt, b_re, b_im, c_re, c_im, d_skip,
           w_glu_a, w_glu_b, w_out, ln1_g, ln1_b, wq_c, wk_c, wv_c, wo_c, ln2_g, ln2_b, w_router, b_router,
           w_gate_up, b_gate_up, w_down, b_down, ln3_g, ln3_b):
    B, S, D = x.shape
    depth = w_in.shape[0]
    x2d = x.reshape(B * S, D)
    mem2d = mem.reshape(-1, D)
    pos2d = positions.reshape(B * S, 1)
    per_layer = (w_in, sinks, w_attn_o, lam_re, lam_im, log_dt, b_re, b_im, c_re, c_im, d_skip, w_glu_a,
                 w_glu_b, w_out, ln1_g, ln1_b, wq_c, wk_c, wv_c, wo_c, ln2_g, ln2_b, w_router, b_router,
                 w_gate_up, b_gate_up, w_down, b_down, ln3_g, ln3_b)
    for l in range(depth):
        x2d = _layer(x2d, mem2d, pos2d, B, S, depth, *(w[l] for w in per_layer))
    return x2d.reshape(B, S, D)
```

```python
import functools

import jax
import jax.numpy as jnp
from jax import lax
from jax.experimental import pallas as pl
from jax.experimental.pallas import tpu as pltpu

N_Q_HEADS = 16
N_KV_HEADS = 2
HEAD_DIM = 64
WINDOW = 128
ROPE_THETA = 10000.0
SSM_GROUP = 16
SSM_GROUPS = 32
SSM_STATE = 64
N_CROSS_HEADS = 4
N_EXPERTS = 32
TOP_K = 4
SWIGLU_ALPHA = 1.702
SWIGLU_LIMIT = 7.0
MOE_BLOCK = 256
LN_EPS = 1e-5

LANES = 128
VMEM_LIMIT_BYTES = 56 * 1024 * 1024

NEG_BIG = -1e30
BF16 = jnp.bfloat16
F32 = jnp.float32
HI = lax.Precision.HIGHEST


def _cparams(n_axes=1):
    return pltpu.CompilerParams(dimension_semantics=("arbitrary",) * n_axes,
                                vmem_limit_bytes=VMEM_LIMIT_BYTES)


def _full_spec(shape):
    n = len(shape)
    return pl.BlockSpec(shape, lambda *_: (0,) * n)


def _dot(a, b):
    return jnp.dot(a, b, preferred_element_type=F32)


def _dot_nt(a, b):
    return lax.dot_general(a, b, (((1,), (1,)), ((), ())), preferred_element_type=F32)


def _store_token_major(ref, val):
    for c in range(ref.shape[1]):
        ref[:, c, :] = val[:, c * LANES:(c + 1) * LANES]


def _load_token_major(ref):
    return jnp.concatenate([ref[:, c, :] for c in range(ref.shape[1])], axis=1)


def _layer_norm(y, g, b):
    mu = jnp.mean(y, axis=-1, keepdims=True)
    d = y - mu
    var = jnp.mean(d * d, axis=-1, keepdims=True)
    return d * lax.rsqrt(var + LN_EPS) * g + b


def _rope(t, cos, sin_signed, first_half):
    half = HEAD_DIM // 2
    partner = jnp.where(first_half, pltpu.roll(t, LANES - half, axis=1), pltpu.roll(t, half, axis=1))
    return t * cos + partner * sin_signed


def _inproj_kernel(x_ref, pos_ref, invf_ref, wq_ref, wk_ref, wv_ref, wut_ref, wga_ref, wgs_ref,
                   q2_ref, k_ref, v_ref, ut_ref, ga_ref, gs_ref):
    tm = x_ref.shape[0]
    xb = x_ref[...].astype(BF16)
    ang = pos_ref[...].astype(F32) * invf_ref[...]
    cos = jnp.cos(ang)
    sin = jnp.sin(ang)
    first_half = (lax.broadcasted_iota(jnp.int32, (tm, LANES), 1) % HEAD_DIM) < (HEAD_DIM // 2)
    first_half_w = (lax.broadcasted_iota(jnp.int32, (WINDOW, LANES), 1) % HEAD_DIM) < (HEAD_DIM // 2)
    sin_signed = jnp.where(first_half, -sin, sin)

    q = _dot(xb, wq_ref[...])
    n_rep = q.shape[1] // LANES
    scale = HEAD_DIM ** -0.5
    for j in range(tm // WINDOW):
        rows = slice(j * WINDOW, (j + 1) * WINDOW)
        for r in range(n_rep):
            t = _rope(q[rows, r * LANES:(r + 1) * LANES], cos[rows], sin_signed[rows], first_half_w)
            base = (j * n_rep + r) * WINDOW
            q2_ref[base:base + WINDOW, :] = (t * scale).astype(BF16)
    k_ref[...] = _rope(_dot(xb, wk_ref[...]), cos, sin_signed, first_half).astype(BF16)
    v_ref[...] = _dot(xb, wv_ref[...]).astype(BF16)
    for j in range(tm // LANES):
        ut_ref[j] = _dot_nt(wut_ref[...], xb[j * LANES:(j + 1) * LANES, :])
    ga_ref[...] = jax.nn.sigmoid(_dot(xb, wga_ref[...])).astype(BF16)
    gs_ref[...] = jax.nn.sigmoid(_dot(xb, wgs_ref[...])).astype(BF16)


def _inproj(x2d, pos2d, invf, wq, wk, wv, wut, wga, wgs, tm):
    T, D = x2d.shape
    n_rep = wq.shape[1] // LANES
    ssm_w = wut.shape[0]
    row = lambda i: (i, 0)
    out_shape = (
        jax.ShapeDtypeStruct((T * n_rep, LANES), BF16),
        jax.ShapeDtypeStruct((T, LANES), BF16),
        jax.ShapeDtypeStruct((T, LANES), BF16),
        jax.ShapeDtypeStruct((T // LANES, ssm_w, LANES), F32),
        jax.ShapeDtypeStruct((T, D), BF16),
        jax.ShapeDtypeStruct((T, D), BF16),
    )
    in_specs = [pl.BlockSpec((tm, D), row), pl.BlockSpec((tm, 1), row), _full_spec(invf.shape),
                _full_spec(wq.shape), _full_spec(wk.shape), _full_spec(wv.shape),
                _full_spec(wut.shape), _full_spec(wga.shape), _full_spec(wgs.shape)]
    out_specs = (pl.BlockSpec((tm * n_rep, LANES), row), pl.BlockSpec((tm, LANES), row),
                 pl.BlockSpec((tm, LANES), row),
                 pl.BlockSpec((tm // LANES, ssm_w, LANES), lambda i: (i, 0, 0)),
                 pl.BlockSpec((tm, D), row), pl.BlockSpec((tm, D), row))
    return pl.pallas_call(_inproj_kernel, name="inproj", grid=(T // tm,), in_specs=in_specs, out_specs=out_specs,
                          out_shape=out_shape, compiler_params=_cparams())(
        x2d, pos2d, invf, wq, wk, wv, wut, wga, wgs)


def _swa_kernel(q2_ref, kc_ref, kp_ref, vc_ref, vp_ref, fill_ref, wo_ref, ga_ref, out_ref, cat_ref):
    i = pl.program_id(1)
    tq = kc_ref.shape[0]
    n_sub = tq // WINDOW
    rep = N_Q_HEADS // N_KV_HEADS
    rows_all = rep * WINDOW
    kfull = jnp.concatenate([kp_ref[...], kc_ref[...]], axis=0)
    vfull = jnp.concatenate([vp_ref[...], vc_ref[...]], axis=0)
    lane = lax.broadcasted_iota(jnp.int32, (2 * WINDOW, LANES), 1)
    qi = lax.broadcasted_iota(jnp.int32, (rows_all, 2 * WINDOW), 0) % WINDOW
    ci = lax.broadcasted_iota(jnp.int32, (rows_all, 2 * WINDOW), 1)
    local = (ci > qi) & (ci <= qi + WINDOW)
    out_lane = lax.broadcasted_iota(jnp.int32, (rows_all, LANES), 1)
    band_row = lax.broadcasted_iota(jnp.int32, (2 * WINDOW, LANES), 0)
    for j in range(n_sub):
        qs = q2_ref[j * rows_all:(j + 1) * rows_all, :]
        kb = kfull[j * WINDOW:(j + 2) * WINDOW, :]
        vb = vfull[j * WINDOW:(j + 2) * WINDOW, :]
        vb = jnp.where(band_row == 0, jnp.zeros_like(vb), vb)
        mask = local
        if j == 0:
            mask = mask & ((ci >= WINDOW) | (i > 0))
        o = None
        for g in range(N_KV_HEADS):
            in_group = (lane >= g * HEAD_DIM) & (lane < (g + 1) * HEAD_DIM)
            kg = jnp.where(in_group, kb, jnp.zeros_like(kb))
            s = _dot_nt(qs, kg)
            s = jnp.where(mask, s, fill_ref[g])
            m = jnp.max(s, axis=-1, keepdims=True)
            p = jnp.exp(s - m)
            denom = jnp.sum(p, axis=-1, keepdims=True)
            og = _dot(p.astype(BF16), vb) * (1.0 / denom)
            o = og if o is None else jnp.where(out_lane < g * HEAD_DIM, o, og)
        ob = o.astype(BF16)
        for r in range(rep):
            cat_ref[j * WINDOW:(j + 1) * WINDOW, r * LANES:(r + 1) * LANES] = ob[r * WINDOW:(r + 1) * WINDOW, :]
    attn = _dot(cat_ref[...], wo_ref[...])
    out_ref[...] = (attn * ga_ref[...].astype(F32)).astype(BF16)


def _swa(q2, k, v, sink_cols, wo, ga, B, S, tq):
    T, D = ga.shape
    rep = N_Q_HEADS // N_KV_HEADS
    n_i = S // tq
    n_sub = tq // WINDOW
    in_specs = [
        pl.BlockSpec((tq * rep, LANES), lambda b, i: (b * n_i + i, 0)),
        pl.BlockSpec((tq, LANES), lambda b, i: (b * n_i + i, 0)),
        pl.BlockSpec((WINDOW, LANES), lambda b, i: (b * (S // WINDOW) + jnp.maximum(i * n_sub - 1, 0), 0)),
        pl.BlockSpec((tq, LANES), lambda b, i: (b * n_i + i, 0)),
        pl.BlockSpec((WINDOW, LANES), lambda b, i: (b * (S // WINDOW) + jnp.maximum(i * n_sub - 1, 0), 0)),
        _full_spec(sink_cols.shape), _full_spec(wo.shape),
        pl.BlockSpec((tq, D), lambda b, i: (b * n_i + i, 0)),
    ]
    return pl.pallas_call(
        _swa_kernel, name="swa", grid=(B, n_i), in_specs=in_specs,
        out_specs=pl.BlockSpec((tq, D), lambda b, i: (b * n_i + i, 0)),
        out_shape=jax.ShapeDtypeStruct((T, D), BF16),
        scratch_shapes=[pltpu.VMEM((tq, D), BF16)],
        compiler_params=_cparams(2))(q2, k, k, v, v, sink_cols, wo, ga)


def _ssm_prep_kernel(lam_row_ref, lam_col_ref, dt_ref, bt_a_ref, bt_b_ref, c_re_ref, c_im_ref,
                     ct_a_ref, ct_b_ref, dsk_ref, toep_ref, wst_ref, wout_ref, apow_ref):
    L = LANES
    P = SSM_STATE
    dt = jnp.exp(dt_ref[0])
    lr2 = lam_row_ref[0, 0:1, :]
    li2 = lam_row_ref[0, 1:2, :]
    mag = jnp.exp(lr2 * dt)
    ar = mag * jnp.cos(li2 * dt)
    ai = mag * jnp.sin(li2 * dt)
    den = lr2 * lr2 + li2 * li2
    f_re = ((ar - 1.0) * lr2 + ai * li2) / den
    f_im = (ai * lr2 - (ar - 1.0) * li2) / den
    bt_a = bt_a_ref[0]
    bt_b = bt_b_ref[0]
    bb_a = f_re * bt_a + f_im * bt_b
    bb_b = f_re * bt_b - f_im * bt_a

    tau_rev = (L - 1 - lax.broadcasted_iota(jnp.int32, (L, 2 * P), 0)).astype(F32)
    g_mag = jnp.exp(lr2 * dt * tau_rev)
    g_re = g_mag * jnp.cos(li2 * dt * tau_rev)
    g_im = g_mag * jnp.sin(li2 * dt * tau_rev)
    for i in range(SSM_GROUP):
        wst_ref[0, i * L:(i + 1) * L, :] = (g_re * bb_a[i:i + 1, :] + g_im * bb_b[i:i + 1, :]).astype(BF16)

    lane2 = lax.broadcasted_iota(jnp.int32, (1, 2 * P), 1)
    for kk in range(4):
        n = float(L * (1 << kk))
        pm = jnp.exp(lr2 * dt * n)
        p_re = pm * jnp.cos(li2 * dt * n)
        p_im = pm * jnp.sin(li2 * dt * n)
        apow_ref[0, 2 * kk:2 * kk + 1, :] = p_re
        apow_ref[0, 2 * kk + 1:2 * kk + 2, :] = jnp.where(lane2 < P, -p_im, p_im)

    lrc = lam_col_ref[0, :, 0:1]
    lic = lam_col_ref[0, :, 1:2]
    tau = lax.broadcasted_iota(jnp.int32, (2 * P, L), 1).astype(F32)
    e0_mag = jnp.exp(lrc * dt * tau)
    e0_re = e0_mag * jnp.cos(lic * dt * tau)
    e0_im = e0_mag * jnp.sin(lic * dt * tau)
    e1_mag = jnp.exp(lrc * dt * (tau + 1.0))
    e1_re = e1_mag * jnp.cos(lic * dt * (tau + 1.0))
    e1_im = e1_mag * jnp.sin(lic * dt * (tau + 1.0))

    ct_a = ct_a_ref[0]
    ct_b = ct_b_ref[0]
    for o in range(SSM_GROUP):
        wout_ref[0, :, o * L:(o + 1) * L] = (ct_a[:, o:o + 1] * e1_re + ct_b[:, o:o + 1] * e1_im).astype(BF16)

    c_re = c_re_ref[0]
    c_im = c_im_ref[0]
    bb_re = bb_a[:, :P]
    bb_im = bb_a[:, P:]
    m_re = (c_re[:, None, :] * bb_re[None, :, :] - c_im[:, None, :] * bb_im[None, :, :])
    m_im = (c_re[:, None, :] * bb_im[None, :, :] + c_im[:, None, :] * bb_re[None, :, :])
    m_re = m_re.reshape(SSM_GROUP * SSM_GROUP, P)
    m_im = m_im.reshape(SSM_GROUP * SSM_GROUP, P)
    kt = (jnp.dot(m_re, e0_re[:P, :], precision=HI, preferred_element_type=F32)
          - jnp.dot(m_im, e0_im[:P, :], precision=HI, preferred_element_type=F32))
    rowi = lax.broadcasted_iota(jnp.int32, (SSM_GROUP * SSM_GROUP, L), 0)
    coli = lax.broadcasted_iota(jnp.int32, (SSM_GROUP * SSM_GROUP, L), 1)
    dsk = dsk_ref[0]
    kt = kt + jnp.where((coli == 0) & ((rowi // SSM_GROUP) == (rowi % SSM_GROUP)), dsk, 0.0)

    cc = lax.broadcasted_iota(jnp.int32, (L, L), 0)
    cp = lax.broadcasted_iota(jnp.int32, (L, L), 1)
    causal = cp >= cc
    for o in range(SSM_GROUP):
        for i in range(SSM_GROUP):
            kv = kt[o * SSM_GROUP + i:o * SSM_GROUP + i + 1, :]
            blk = pltpu.roll(jnp.broadcast_to(kv, (L, L)), 0, axis=1, stride=1, stride_axis=0)
            blk = jnp.where(causal, blk, 0.0)
            toep_ref[0, i * L:(i + 1) * L, o * L:(o + 1) * L] = blk.astype(BF16)


def _ssm_prep(lam_row, lam_col, log_dt, bt_a, bt_b, c_re, c_im, ct_a, ct_b, dsk):
    G = SSM_GROUPS
    KW = SSM_GROUP * LANES
    g3 = lambda g: (g, 0, 0)
    ins = [lam_row, lam_col, log_dt, bt_a, bt_b, c_re, c_im, ct_a, ct_b, dsk]
    in_specs = [pl.BlockSpec((1,) + a.shape[1:], g3) for a in ins]
    out_shape = (jax.ShapeDtypeStruct((G, KW, KW), BF16),
                 jax.ShapeDtypeStruct((G, KW, 2 * SSM_STATE), BF16),
                 jax.ShapeDtypeStruct((G, 2 * SSM_STATE, KW), BF16),
                 jax.ShapeDtypeStruct((G, 8, 2 * SSM_STATE), F32))
    out_specs = tuple(pl.BlockSpec((1,) + s.shape[1:], g3) for s in out_shape)
    return pl.pallas_call(_ssm_prep_kernel, name="ssm_prep", grid=(G,), in_specs=in_specs, out_specs=out_specs,
                          out_shape=out_shape, compiler_params=_cparams())(*ins)


def _ssm_scan_kernel(ut_ref, toep_ref, wst_ref, wout_ref, apow_ref, yt_ref, *, n_chunks):
    nb = ut_ref.shape[0]
    P = SSM_STATE
    lhs = jnp.concatenate([ut_ref[:, i, :] for i in range(SSM_GROUP)], axis=1).astype(BF16)
    y = _dot(lhs, toep_ref[0])
    st = _dot(lhs, wst_ref[0])
    srow = lax.broadcasted_iota(jnp.int32, (nb, 2 * P), 0) % n_chunks
    kk = 0
    while (1 << kk) < n_chunks:
        d = 1 << kk
        pa = apow_ref[0, 2 * kk:2 * kk + 1, :]
        pb = apow_ref[0, 2 * kk + 1:2 * kk + 2, :]
        prev = pltpu.roll(st, d, axis=0)
        prev = prev * pa + pltpu.roll(prev, P, axis=1) * pb
        st = st + jnp.where(srow >= d, prev, 0.0)
        kk += 1
    h0 = jnp.where(srow >= 1, pltpu.roll(st, 1, axis=0), 0.0)
    y = y + _dot(h0.astype(BF16), wout_ref[0])
    for o in range(SSM_GROUP):
        yt_ref[o] = y[:, o * LANES:(o + 1) * LANES]


def _ssm_scan(ut, toep, wst, wout, apow, n_chunks):
    nb, ssm_w, _ = ut.shape
    G = SSM_GROUPS
    g3 = lambda g: (g, 0, 0)
    in_specs = [pl.BlockSpec((nb, SSM_GROUP, LANES), lambda g: (0, g, 0)),
                pl.BlockSpec((1,) + toep.shape[1:], g3), pl.BlockSpec((1,) + wst.shape[1:], g3),
                pl.BlockSpec((1,) + wout.shape[1:], g3), pl.BlockSpec((1,) + apow.shape[1:], g3)]
    return pl.pallas_call(
        functools.partial(_ssm_scan_kernel, n_chunks=n_chunks), name="ssm_scan", grid=(G,), in_specs=in_specs,
        out_specs=pl.BlockSpec((SSM_GROUP, nb, LANES), g3),
        out_shape=jax.ShapeDtypeStruct((ssm_w, nb, LANES), F32),
        compiler_params=_cparams())(ut, toep, wst, wout, apow)


def _mix_kernel(yt_ref, att_ref, gs_ref, x_ref, wa_ref, wb_ref, wo_ref, g_ref, b_ref, out_ref, zt_ref,
                *, alpha):
    n_blk = yt_ref.shape[1]
    for j in range(n_blk):
        y = yt_ref[:, j, :]
        z = jax.nn.gelu(y, approximate=True)
        zt_ref[j * LANES:(j + 1) * LANES, :] = z.T.astype(BF16)
    z = zt_ref[...]
    ssm_out = _dot(z, wa_ref[...]) * jax.nn.sigmoid(_dot(z, wb_ref[...]))
    mixed = att_ref[...].astype(F32) + gs_ref[...].astype(F32) * ssm_out
    y = alpha * x_ref[...] + _dot(mixed.astype(BF16), wo_ref[...])
    out_ref[...] = _layer_norm(y, g_ref[...], b_ref[...])


def _mix(yt, att, gs, x2d, wa, wb, wo, g, b, alpha, tm):
    T, D = x2d.shape
    ssm_w = yt.shape[0]
    row = lambda i: (i, 0)
    in_specs = [pl.BlockSpec((ssm_w, tm // LANES, LANES), lambda i: (0, i, 0)),
                pl.BlockSpec((tm, D), row), pl.BlockSpec((tm, D), row), pl.BlockSpec((tm, D), row),
                _full_spec(wa.shape), _full_spec(wb.shape), _full_spec(wo.shape),
                _full_spec(g.shape), _full_spec(b.shape)]
    return pl.pallas_call(functools.partial(_mix_kernel, alpha=alpha), name="mix", grid=(T // tm,), in_specs=in_specs,
                          out_specs=pl.BlockSpec((tm, D), row),
                          out_shape=jax.ShapeDtypeStruct((T, D), F32),
                          scratch_shapes=[pltpu.VMEM((tm, ssm_w), BF16)],
                          compiler_params=_cparams())(yt, att, gs, x2d, wa, wb, wo, g, b)


def _memkv_kernel(m_ref, wk_ref, wv_ref, k_ref, v_ref):
    mb = m_ref[...].astype(BF16)
    k_ref[...] = _dot(mb, wk_ref[...]).astype(BF16)
    v_ref[...] = _dot(mb, wv_ref[...]).astype(BF16)


def _memkv(mem2d, wk, wv, tm):
    R, D = mem2d.shape
    row = lambda i: (i, 0)
    return pl.pallas_call(_memkv_kernel, name="memkv", grid=(R // tm,),
                          in_specs=[pl.BlockSpec((tm, D), row), _full_spec(wk.shape), _full_spec(wv.shape)],
                          out_specs=(pl.BlockSpec((tm, D), row), pl.BlockSpec((tm, D), row)),
                          out_shape=(jax.ShapeDtypeStruct((R, D), BF16), jax.ShapeDtypeStruct((R, D), BF16)),
                          compiler_params=_cparams())(mem2d, wk, wv)


def _cross_kernel(x_ref, k_ref, v_ref, wq_ref, wo_ref, g_ref, b_ref, wr_ref, br_ref, out_ref, logit_ref,
                  cat_ref, *, alpha):
    x = x_ref[...]
    D = x.shape[1]
    hd = D // N_CROSS_HEADS
    q = (_dot(x.astype(BF16), wq_ref[...]) * (hd ** -0.5)).astype(BF16)
    for h in range(N_CROSS_HEADS):
        cols = slice(h * hd, (h + 1) * hd)
        s = _dot_nt(q[:, cols], k_ref[:, cols])
        m = jnp.max(s, axis=-1, keepdims=True)
        p = jnp.exp(s - m)
        w = (p / jnp.sum(p, axis=-1, keepdims=True)).astype(BF16)
        cat_ref[:, cols] = _dot(w, v_ref[:, cols]).astype(BF16)
    y = alpha * x + _dot(cat_ref[...], wo_ref[...])
    x2 = _layer_norm(y, g_ref[...], b_ref[...])
    _store_token_major(out_ref, x2)
    logit_ref[...] = _dot(x2.astype(BF16), wr_ref[...]) + br_ref[...]


def _cross(x1, kc, vc, wq, wo, g, b, wr, br, alpha, S, tm):
    T, D = x1.shape
    n_mem = kc.shape[0] // (T // S)
    row = lambda i: (i, 0)
    per_b = S // tm
    in_specs = [pl.BlockSpec((tm, D), row),
                pl.BlockSpec((n_mem, D), lambda i: (i // per_b, 0)),
                pl.BlockSpec((n_mem, D), lambda i: (i // per_b, 0)),
                _full_spec(wq.shape), _full_spec(wo.shape), _full_spec(g.shape), _full_spec(b.shape),
                _full_spec(wr.shape), _full_spec(br.shape)]
    return pl.pallas_call(functools.partial(_cross_kernel, alpha=alpha), name="cross", grid=(T // tm,), in_specs=in_specs,
                          out_specs=(pl.BlockSpec((tm, D // LANES, LANES), lambda i: (i, 0, 0)),
                                     pl.BlockSpec((tm, LANES), row)),
                          out_shape=(jax.ShapeDtypeStruct((T, D // LANES, LANES), F32),
                                     jax.ShapeDtypeStruct((T, LANES), F32)),
                          scratch_shapes=[pltpu.VMEM((tm, D), BF16)],
                          compiler_params=_cparams())(x1, kc, vc, wq, wo, g, b, wr, br)


def _route_kernel(logit_ref, info_ref, count_ref, carry_ref):
    tm = logit_ref.shape[0]

    @pl.when(pl.program_id(0) == 0)
    def _():
        carry_ref[...] = jnp.zeros_like(carry_ref)

    vals = logit_ref[...]
    lane = lax.broadcasted_iota(jnp.int32, vals.shape, 1).astype(F32)
    sels, tops = [], []
    for _ in range(TOP_K):
        m = jnp.max(vals, axis=-1, keepdims=True)
        idx = jnp.min(jnp.where(vals == m, lane, float(LANES)), axis=-1, keepdims=True)
        sel = lane == idx
        vals = jnp.where(sel, -jnp.inf, vals)
        sels.append((sel, idx))
        tops.append(m)
    es = [jnp.exp(t - tops[0]) for t in tops]
    denom = es[0]
    for e in es[1:]:
        denom = denom + e
    onehot = jnp.zeros(vals.shape, F32)
    for sel, _ in sels:
        onehot = onehot + jnp.where(sel, 1.0, 0.0)
    r = lax.broadcasted_iota(jnp.int32, (tm, tm), 0)
    c = lax.broadcasted_iota(jnp.int32, (tm, tm), 1)
    tri = jnp.where(c < r, 1.0, 0.0).astype(BF16)
    cum = _dot(tri, onehot.astype(BF16)) + carry_ref[...]
    info = jnp.zeros(vals.shape, F32)
    for k, (sel, idx) in enumerate(sels):
        rank = jnp.sum(jnp.where(sel, cum, 0.0), axis=-1, keepdims=True)
        info = jnp.where(lane == k, idx, info)
        info = jnp.where(lane == TOP_K + k, rank, info)
        info = jnp.where(lane == 2 * TOP_K + k, es[k] / denom, info)
    info_ref[...] = info
    carry_ref[...] = carry_ref[...] + jnp.sum(onehot, axis=0, keepdims=True)
    count_ref[...] = carry_ref[...]


def _route(logits, tm):
    T = logits.shape[0]
    row = lambda i: (i, 0)
    return pl.pallas_call(_route_kernel, name="route", grid=(T // tm,),
                          in_specs=[pl.BlockSpec((tm, LANES), row)],
                          out_specs=(pl.BlockSpec((tm, LANES), row), _full_spec((1, LANES))),
                          out_shape=(jax.ShapeDtypeStruct((T, LANES), F32),
                                     jax.ShapeDtypeStruct((1, LANES), F32)),
                          scratch_shapes=[pltpu.VMEM((1, LANES), F32)],
                          compiler_params=_cparams())(logits)


GU_TILE = 2 * LANES


def _moe_kernel(bexp_ref, nused_ref, src_cur_ref, src_nxt_ref, dst_prev_ref, dst_cur_ref, x_hbm, wgu_ref,
                bgu_ref, wd_ref, bd_ref, y_hbm, xbuf0, xbuf1, ybuf0, ybuf1, xb_s, wgu_s, wd_s, gsem, ssem):
    j = pl.program_id(0)
    n_used = nused_ref[0]
    xbufs, ybufs = (xbuf0, xbuf1), (ybuf0, ybuf1)
    rows = xbuf0.shape[0]
    n_tiles = wgu_s.shape[1] // GU_TILE

    def gather(idx_ref, s):
        for r in range(rows):
            tok = idx_ref[0, 0, r]
            pltpu.make_async_copy(x_hbm.at[pl.ds(tok, 1)], xbufs[s].at[pl.ds(r, 1)], gsem.at[s]).start()

    def scatter(idx_ref, s):
        for r in range(rows):
            d = idx_ref[0, 0, r]
            pltpu.make_async_copy(ybufs[s].at[pl.ds(r, 1)], y_hbm.at[pl.ds(d, 1)], ssem.at[s]).start()

    def wait_gather(s):
        pltpu.make_async_copy(x_hbm.at[pl.ds(0, rows)], xbufs[s], gsem.at[s]).wait()

    def wait_scatter(s):
        pltpu.make_async_copy(ybufs[s], y_hbm.at[pl.ds(0, rows)], ssem.at[s]).wait()

    @pl.when(j == 0)
    def _():
        gather(src_cur_ref, 0)
        ybuf0[...] = jnp.zeros_like(ybuf0)
        ybuf1[...] = jnp.zeros_like(ybuf1)
        n_real = y_hbm.shape[0] - 2 * rows
        pltpu.make_async_copy(ybuf0, y_hbm.at[pl.ds(n_real, rows)], ssem.at[0]).start()

    @pl.when(j < n_used)
    def _():
        @pl.when((j == 0) | (bexp_ref[j] != bexp_ref[jnp.maximum(j - 1, 0)]))
        def _():
            k = lax.broadcasted_iota(jnp.int32, (GU_TILE, GU_TILE), 0)
            n = lax.broadcasted_iota(jnp.int32, (GU_TILE, GU_TILE), 1)
            perm = jnp.where(k == jnp.where(n < LANES, 2 * n, 2 * (n - LANES) + 1), 1.0, 0.0).astype(BF16)
            for t in range(n_tiles):
                cols = slice(t * GU_TILE, (t + 1) * GU_TILE)
                wgu_s[:, cols] = _dot(wgu_ref[0, :, cols].astype(BF16), perm).astype(BF16)
            wd_s[...] = wd_ref[0].astype(BF16)

        def step(cur):
            nxt = 1 - cur
            wait_gather(cur)
            wait_scatter(cur)
            xb_s[...] = _load_token_major(xbufs[cur]).astype(BF16)
            gather(src_nxt_ref, nxt)
            scatter(dst_prev_ref, nxt)
            xb = xb_s[...]
            hs = []
            for t in range(n_tiles):
                cols = slice(t * GU_TILE, (t + 1) * GU_TILE)
                gu = _dot(xb, wgu_s[:, cols]) + bgu_ref[0, :, cols]
                gate = jnp.minimum(gu[:, :LANES], SWIGLU_LIMIT)
                lin = jnp.clip(gu[:, LANES:], -SWIGLU_LIMIT, SWIGLU_LIMIT)
                hs.append((gate * jax.nn.sigmoid(SWIGLU_ALPHA * gate) * (lin + 1.0)).astype(BF16))
            h = jnp.concatenate(hs, axis=1)
            _store_token_major(ybufs[cur], _dot(h, wd_s[...]) + bd_ref[0])

            @pl.when(j == n_used - 1)
            def _():
                scatter(dst_cur_ref, cur)
                wait_gather(nxt)
                wait_scatter(nxt)
                wait_scatter(cur)

        @pl.when(j % 2 == 0)
        def _():
            step(0)

        @pl.when(j % 2 == 1)
        def _():
            step(1)


def _moe(bexp, n_used, slot_src, slot_dst, slot_dst_prev, x2t, wgu, bgu, wd, bd, n_out_rows):
    n_blocks = bexp.shape[0]
    row_tile = x2t.shape[1:]
    D = wgu.shape[1]
    F2 = wgu.shape[2]
    F = wd.shape[1]
    wmap = lambda j, be, nu: (be[j], 0, 0)
    cur = lambda j, be, nu: (j, 0, 0)
    smem = pltpu.SMEM
    in_specs = [
        pl.BlockSpec((1, 1, MOE_BLOCK), cur, memory_space=smem),
        pl.BlockSpec((1, 1, MOE_BLOCK), lambda j, be, nu: (jnp.minimum(j + 1, n_blocks - 1), 0, 0),
                     memory_space=smem),
        pl.BlockSpec((1, 1, MOE_BLOCK), cur, memory_space=smem),
        pl.BlockSpec((1, 1, MOE_BLOCK), cur, memory_space=smem),
        pl.BlockSpec(memory_space=pl.ANY),
        pl.BlockSpec((1, D, F2), wmap), pl.BlockSpec((1, 1, F2), wmap),
        pl.BlockSpec((1, F, D), wmap), pl.BlockSpec((1, 1, D), wmap),
    ]
    grid_spec = pltpu.PrefetchScalarGridSpec(
        num_scalar_prefetch=2, grid=(n_blocks,), in_specs=in_specs,
        out_specs=pl.BlockSpec(memory_space=pl.ANY),
        scratch_shapes=[pltpu.VMEM((MOE_BLOCK,) + row_tile, F32), pltpu.VMEM((MOE_BLOCK,) + row_tile, F32),
                        pltpu.VMEM((MOE_BLOCK,) + row_tile, F32), pltpu.VMEM((MOE_BLOCK,) + row_tile, F32),
                        pltpu.VMEM((MOE_BLOCK, D), BF16),
                        pltpu.VMEM((D, F2), BF16), pltpu.VMEM((F, D), BF16),
                        pltpu.SemaphoreType.DMA((2,)), pltpu.SemaphoreType.DMA((2,))])
    return pl.pallas_call(_moe_kernel, name="moe", grid_spec=grid_spec,
                          out_shape=jax.ShapeDtypeStruct((n_out_rows,) + row_tile, F32),
                          compiler_params=_cparams())(
        bexp, n_used, slot_src, slot_src, slot_dst_prev, slot_dst, x2t, wgu, bgu, wd, bd)


def _combine_kernel(y0_ref, y1_ref, y2_ref, y3_ref, info_ref, x_ref, g_ref, b_ref, out_ref, *, alpha):
    info = info_ref[...]
    acc = alpha * _load_token_major(x_ref)
    for k, y_ref in enumerate((y0_ref, y1_ref, y2_ref, y3_ref)):
        acc = acc + info[:, 2 * TOP_K + k:2 * TOP_K + k + 1] * _load_token_major(y_ref)
    out_ref[...] = _layer_norm(acc, g_ref[...], b_ref[...])


def _combine(yk, info, x2t, g, b, alpha, tm):
    T = x2t.shape[0]
    row_tile = x2t.shape[1:]
    D = row_tile[0] * row_tile[1]
    n_t = T // tm
    row = lambda i: (i, 0)
    in_specs = [pl.BlockSpec((tm,) + row_tile, (lambda i, k=k: (k * n_t + i, 0, 0))) for k in range(TOP_K)]
    in_specs += [pl.BlockSpec((tm, LANES), row), pl.BlockSpec((tm,) + row_tile, lambda i: (i, 0, 0)),
                 _full_spec(g.shape), _full_spec(b.shape)]
    return pl.pallas_call(functools.partial(_combine_kernel, alpha=alpha), name="combine", grid=(n_t,), in_specs=in_specs,
                          out_specs=pl.BlockSpec((tm, D), row),
                          out_shape=jax.ShapeDtypeStruct((T, D), F32),
                          compiler_params=_cparams())(yk, yk, yk, yk, info, x2t, g, b)


def _tile(n, pref):
    return pref if n % pref == 0 else n


def _layer(x2d, mem2d, pos2d, B, S, depth, w_in, sinks, w_attn_o, lam_re, lam_im, log_dt, b_re, b_im,
           c_re, c_im, d_skip, w_glu_a, w_glu_b, w_out, ln1_g, ln1_b, wq_c, wk_c, wv_c, wo_c, ln2_g,
           ln2_b, w_router, b_router, w_gate_up, b_gate_up, w_down, b_down, ln3_g, ln3_b):
    T, D = x2d.shape
    alpha = (2 * depth) ** 0.25
    rep = N_Q_HEADS // N_KV_HEADS
    q_w = N_Q_HEADS * HEAD_DIM
    kv_w = N_KV_HEADS * HEAD_DIM
    ssm_w = SSM_GROUP * SSM_GROUPS
    P = SSM_STATE

    o_k, o_v, o_s = q_w, q_w + kv_w, q_w + 2 * kv_w
    o_ga, o_gs = o_s + ssm_w, o_s + ssm_w + D
    wq = w_in[:, :o_k].reshape(D, N_KV_HEADS, rep, HEAD_DIM).transpose(0, 2, 1, 3).reshape(D, q_w).astype(BF16)
    wk = w_in[:, o_k:o_v].astype(BF16)
    wv = w_in[:, o_v:o_s].astype(BF16)
    wut = w_in[:, o_s:o_ga].T.astype(BF16)
    wga = w_in[:, o_ga:o_gs].astype(BF16)
    wgs = w_in[:, o_gs:].astype(BF16)
    wo_attn = w_attn_o.reshape(N_KV_HEADS, rep, HEAD_DIM, D).transpose(1, 0, 2, 3).reshape(q_w, D).astype(BF16)
    half = HEAD_DIM // 2
    inv_freq = jnp.power(ROPE_THETA, -jnp.arange(half, dtype=F32) / half)
    invf = jnp.tile(inv_freq, LANES // half)[None, :]
    sink_rows = jnp.repeat(sinks.astype(F32).reshape(N_KV_HEADS, rep), WINDOW, axis=1)
    sink_cols = jnp.full((N_KV_HEADS, rep * WINDOW, 2 * WINDOW), NEG_BIG, F32).at[:, :, 0].set(sink_rows)

    tm1 = _tile(T, 512)
    q2, k, v, ut, ga, gs = _inproj(x2d, pos2d, invf, wq, wk, wv, wut, wga, wgs, tm1)
    att = _swa(q2, k, v, sink_cols, wo_attn, ga, B, S, _tile(S, 512))

    lam_row = jnp.stack([jnp.concatenate([lam_re, lam_re], -1), jnp.concatenate([lam_im, lam_im], -1)], 1)
    lam_col = jnp.swapaxes(lam_row, 1, 2)
    bt_re = jnp.swapaxes(b_re, 1, 2)
    bt_im = jnp.swapaxes(b_im, 1, 2)
    bt_a = jnp.concatenate([bt_re, bt_im], -1)
    bt_b = jnp.concatenate([-bt_im, bt_re], -1)
    ct_re = jnp.swapaxes(c_re, 1, 2)
    ct_im = jnp.swapaxes(c_im, 1, 2)
    ct_a = jnp.concatenate([ct_re, -ct_im], 1)
    ct_b = jnp.concatenate([-ct_im, -ct_re], 1)
    dsk = jnp.repeat(d_skip.reshape(SSM_GROUPS, SSM_GROUP), SSM_GROUP, axis=1)[:, :, None]
    toep, wst, wout, apow = _ssm_prep(lam_row, lam_col, log_dt.reshape(SSM_GROUPS, 1, 1), bt_a, bt_b,
                                      c_re, c_im, ct_a, ct_b, dsk)
    yt = _ssm_scan(ut, toep, wst, wout, apow, S // LANES)

    x1 = _mix(yt, att, gs, x2d, w_glu_a.astype(BF16), w_glu_b.astype(BF16), w_out.astype(BF16),
              ln1_g[None, :], ln1_b[None, :], alpha, _tile(T, 1024))

    kc, vc = _memkv(mem2d, wk_c.astype(BF16), wv_c.astype(BF16), _tile(mem2d.shape[0], 512))
    wr = jnp.zeros((D, LANES), F32).at[:, :N_EXPERTS].set(w_router).astype(BF16)
    br = jnp.full((1, LANES), NEG_BIG, F32).at[0, :N_EXPERTS].set(b_router)
    x2, logits = _cross(x1, kc, vc, wq_c.astype(BF16), wo_c.astype(BF16), ln2_g[None, :], ln2_b[None, :],
                        wr, br, alpha, S, _tile(S, 512))

    info, counts = _route(logits, _tile(T, 512))
    top_idx = info[:, :TOP_K].astype(jnp.int32)
    rank = info[:, TOP_K:2 * TOP_K].astype(jnp.int32)
    counts = counts[0, :N_EXPERTS].astype(jnp.int32)
    padded = (counts + MOE_BLOCK - 1) // MOE_BLOCK * MOE_BLOCK
    ends_pad = jnp.cumsum(padded)
    start_pad = ends_pad - padded
    n_assign = T * TOP_K
    n_blocks = -(-n_assign // MOE_BLOCK) + N_EXPERTS
    n_slots = n_blocks * MOE_BLOCK
    dest = (start_pad[top_idx] + rank).reshape(-1)
    slot_a = jnp.full((n_slots,), -1, jnp.int32).at[dest].set(jnp.arange(n_assign, dtype=jnp.int32),
                                                              unique_indices=True)
    slot_src = jnp.maximum(slot_a, 0) // TOP_K
    slot_id = jnp.arange(n_slots, dtype=jnp.int32)
    spare = n_assign + (slot_id // MOE_BLOCK % 2) * MOE_BLOCK + slot_id % MOE_BLOCK
    slot_dst = jnp.where(slot_a < 0, spare, (slot_a % TOP_K) * T + slot_a // TOP_K)
    block_start = jnp.arange(n_blocks, dtype=jnp.int32) * MOE_BLOCK
    bexp = jnp.minimum(jnp.sum(block_start[:, None] >= ends_pad[None, :], axis=1), N_EXPERTS - 1).astype(jnp.int32)
    n_used = (ends_pad[-1] // MOE_BLOCK).astype(jnp.int32)[None]
    n_gu_tiles = b_gate_up.shape[1] // GU_TILE
    bgu = b_gate_up.reshape(N_EXPERTS, n_gu_tiles, LANES, 2).transpose(0, 1, 3, 2).reshape(N_EXPERTS, 1, -1)
    slot_dst_prev = jnp.concatenate([n_assign + MOE_BLOCK + jnp.arange(MOE_BLOCK, dtype=jnp.int32),
                                     slot_dst[:-MOE_BLOCK]])
    yk = _moe(bexp, n_used, slot_src.reshape(n_blocks, 1, MOE_BLOCK), slot_dst.reshape(n_blocks, 1, MOE_BLOCK),
              slot_dst_prev.reshape(n_blocks, 1, MOE_BLOCK), x2, w_gate_up, bgu, w_down, b_down[:, None, :],
              n_assign + 2 * MOE_BLOCK)
    return _combine(yk, info, x2, ln3_g[None, :], ln3_b[None, :], alpha, _tile(T, 512))


def kernel(x, mem, positions, w_in, sinks, w_attn_o, lam_re, lam_im, log_dt, b_re, b_im, c_re, c_im, d_skip,
           w_glu_a, w_glu_b, w_out, ln1_g, ln1_b, wq_c, wk_c, wv_c, wo_c, ln2_g, ln2_b, w_router, b_router,
           w_gate_up, b_gate_up, w_down, b_down, ln3_g, ln3_b):
    B, S, D = x.shape
    depth = w_in.shape[0]
    x2d = x.reshape(B * S, D)
    mem2d = mem.reshape(-1, D)
    pos2d = positions.reshape(B * S, 1)
    per_layer = (w_in, sinks, w_attn_o, lam_re, lam_im, log_dt, b_re, b_im, c_re, c_im, d_skip, w_glu_a,
                 w_glu_b, w_out, ln1_g, ln1_b, wq_c, wk_c, wv_c, wo_c, ln2_g, ln2_b, w_router, b_router,
                 w_gate_up, b_gate_up, w_down, b_down, ln3_g, ln3_b)
    for l in range(depth):
        x2d = _layer(x2d, mem2d, pos2d, B, S, depth, *(w[l] for w in per_layer))
    return x2d.reshape(B, S, D)
```

```python
import functools

import jax
import jax.numpy as jnp
from jax import lax
from jax.experimental import pallas as pl
from jax.experimental.pallas import tpu as pltpu

N_Q_HEADS = 16
N_KV_HEADS = 2
HEAD_DIM = 64
WINDOW = 128
ROPE_THETA = 10000.0
SSM_GROUP = 16
SSM_GROUPS = 32
SSM_STATE = 64
N_CROSS_HEADS = 4
N_EXPERTS = 32
TOP_K = 4
SWIGLU_ALPHA = 1.702
SWIGLU_LIMIT = 7.0
MOE_BLOCK = 256
LN_EPS = 1e-5

LANES = 128
VMEM_LIMIT_BYTES = 56 * 1024 * 1024

NEG_BIG = -1e30
BF16 = jnp.bfloat16
F32 = jnp.float32
HI = lax.Precision.HIGHEST


def _cparams(n_axes=1):
    return pltpu.CompilerParams(dimension_semantics=("arbitrary",) * n_axes,
                                vmem_limit_bytes=VMEM_LIMIT_BYTES)


def _full_spec(shape):
    n = len(shape)
    return pl.BlockSpec(shape, lambda *_: (0,) * n)


def _dot(a, b):
    return jnp.dot(a, b, preferred_element_type=F32)


def _dot_nt(a, b):
    return lax.dot_general(a, b, (((1,), (1,)), ((), ())), preferred_element_type=F32)


def _layer_norm(y, g, b):
    mu = jnp.mean(y, axis=-1, keepdims=True)
    d = y - mu
    var = jnp.mean(d * d, axis=-1, keepdims=True)
    return d * lax.rsqrt(var + LN_EPS) * g + b


def _rope(t, cos, sin_signed, first_half):
    half = HEAD_DIM // 2
    partner = jnp.where(first_half, pltpu.roll(t, LANES - half, axis=1), pltpu.roll(t, half, axis=1))
    return t * cos + partner * sin_signed


def _inproj_kernel(x_ref, pos_ref, invf_ref, wq_ref, wk_ref, wv_ref, wut_ref, wga_ref, wgs_ref,
                   q2_ref, k_ref, v_ref, ut_ref, ga_ref, gs_ref):
    tm = x_ref.shape[0]
    xb = x_ref[...].astype(BF16)
    ang = pos_ref[...].astype(F32) * invf_ref[...]
    cos = jnp.cos(ang)
    sin = jnp.sin(ang)
    first_half = (lax.broadcasted_iota(jnp.int32, (tm, LANES), 1) % HEAD_DIM) < (HEAD_DIM // 2)
    first_half_w = (lax.broadcasted_iota(jnp.int32, (WINDOW, LANES), 1) % HEAD_DIM) < (HEAD_DIM // 2)
    sin_signed = jnp.where(first_half, -sin, sin)

    q = _dot(xb, wq_ref[...])
    n_rep = q.shape[1] // LANES
    scale = HEAD_DIM ** -0.5
    for j in range(tm // WINDOW):
        rows = slice(j * WINDOW, (j + 1) * WINDOW)
        for r in range(n_rep):
            t = _rope(q[rows, r * LANES:(r + 1) * LANES], cos[rows], sin_signed[rows], first_half_w)
            base = (j * n_rep + r) * WINDOW
            q2_ref[base:base + WINDOW, :] = (t * scale).astype(BF16)
    k_ref[...] = _rope(_dot(xb, wk_ref[...]), cos, sin_signed, first_half).astype(BF16)
    v_ref[...] = _dot(xb, wv_ref[...]).astype(BF16)
    for j in range(tm // LANES):
        ut_ref[j] = _dot_nt(wut_ref[...], xb[j * LANES:(j + 1) * LANES, :])
    ga_ref[...] = jax.nn.sigmoid(_dot(xb, wga_ref[...])).astype(BF16)
    gs_ref[...] = jax.nn.sigmoid(_dot(xb, wgs_ref[...])).astype(BF16)


def _inproj(x2d, pos2d, invf, wq, wk, wv, wut, wga, wgs, tm):
    T, D = x2d.shape
    n_rep = wq.shape[1] // LANES
    ssm_w = wut.shape[0]
    row = lambda i: (i, 0)
    out_shape = (
        jax.ShapeDtypeStruct((T * n_rep, LANES), BF16),
        jax.ShapeDtypeStruct((T, LANES), BF16),
        jax.ShapeDtypeStruct((T, LANES), BF16),
        jax.ShapeDtypeStruct((T // LANES, ssm_w, LANES), F32),
        jax.ShapeDtypeStruct((T, D), BF16),
        jax.ShapeDtypeStruct((T, D), BF16),
    )
    in_specs = [pl.BlockSpec((tm, D), row), pl.BlockSpec((tm, 1), row), _full_spec(invf.shape),
                _full_spec(wq.shape), _full_spec(wk.shape), _full_spec(wv.shape),
                _full_spec(wut.shape), _full_spec(wga.shape), _full_spec(wgs.shape)]
    out_specs = (pl.BlockSpec((tm * n_rep, LANES), row), pl.BlockSpec((tm, LANES), row),
                 pl.BlockSpec((tm, LANES), row),
                 pl.BlockSpec((tm // LANES, ssm_w, LANES), lambda i: (i, 0, 0)),
                 pl.BlockSpec((tm, D), row), pl.BlockSpec((tm, D), row))
    return pl.pallas_call(_inproj_kernel, name="inproj", grid=(T // tm,), in_specs=in_specs, out_specs=out_specs,
                          out_shape=out_shape, compiler_params=_cparams())(
        x2d, pos2d, invf, wq, wk, wv, wut, wga, wgs)


def _swa_kernel(q2_ref, kc_ref, kp_ref, vc_ref, vp_ref, fill_ref, wo_ref, ga_ref, out_ref, cat_ref):
    i = pl.program_id(1)
    tq = kc_ref.shape[0]
    n_sub = tq // WINDOW
    rep = N_Q_HEADS // N_KV_HEADS
    rows_all = rep * WINDOW
    kfull = jnp.concatenate([kp_ref[...], kc_ref[...]], axis=0)
    vfull = jnp.concatenate([vp_ref[...], vc_ref[...]], axis=0)
    lane = lax.broadcasted_iota(jnp.int32, (2 * WINDOW, LANES), 1)
    qi = lax.broadcasted_iota(jnp.int32, (rows_all, 2 * WINDOW), 0) % WINDOW
    ci = lax.broadcasted_iota(jnp.int32, (rows_all, 2 * WINDOW), 1)
    local = (ci > qi) & (ci <= qi + WINDOW)
    out_lane = lax.broadcasted_iota(jnp.int32, (rows_all, LANES), 1)
    band_row = lax.broadcasted_iota(jnp.int32, (2 * WINDOW, LANES), 0)
    for j in range(n_sub):
        qs = q2_ref[j * rows_all:(j + 1) * rows_all, :]
        kb = kfull[j * WINDOW:(j + 2) * WINDOW, :]
        vb = vfull[j * WINDOW:(j + 2) * WINDOW, :]
        vb = jnp.where(band_row == 0, jnp.zeros_like(vb), vb)
        mask = local
        if j == 0:
            mask = mask & ((ci >= WINDOW) | (i > 0))
        o = None
        for g in range(N_KV_HEADS):
            in_group = (lane >= g * HEAD_DIM) & (lane < (g + 1) * HEAD_DIM)
            kg = jnp.where(in_group, kb, jnp.zeros_like(kb))
            s = _dot_nt(qs, kg)
            s = jnp.where(mask, s, fill_ref[g])
            m = jnp.max(s, axis=-1, keepdims=True)
            p = jnp.exp(s - m)
            denom = jnp.sum(p, axis=-1, keepdims=True)
            og = _dot(p.astype(BF16), vb) * (1.0 / denom)
            o = og if o is None else jnp.where(out_lane < g * HEAD_DIM, o, og)
        ob = o.astype(BF16)
        for r in range(rep):
            cat_ref[j * WINDOW:(j + 1) * WINDOW, r * LANES:(r + 1) * LANES] = ob[r * WINDOW:(r + 1) * WINDOW, :]
    attn = _dot(cat_ref[...], wo_ref[...])
    out_ref[...] = (attn * ga_ref[...].astype(F32)).astype(BF16)


def _swa(q2, k, v, sink_cols, wo, ga, B, S, tq):
    T, D = ga.shape
    rep = N_Q_HEADS // N_KV_HEADS
    n_i = S // tq
    n_sub = tq // WINDOW
    in_specs = [
        pl.BlockSpec((tq * rep, LANES), lambda b, i: (b * n_i + i, 0)),
        pl.BlockSpec((tq, LANES), lambda b, i: (b * n_i + i, 0)),
        pl.BlockSpec((WINDOW, LANES), lambda b, i: (b * (S // WINDOW) + jnp.maximum(i * n_sub - 1, 0), 0)),
        pl.BlockSpec((tq, LANES), lambda b, i: (b * n_i + i, 0)),
        pl.BlockSpec((WINDOW, LANES), lambda b, i: (b * (S // WINDOW) + jnp.maximum(i * n_sub - 1, 0), 0)),
        _full_spec(sink_cols.shape), _full_spec(wo.shape),
        pl.BlockSpec((tq, D), lambda b, i: (b * n_i + i, 0)),
    ]
    return pl.pallas_call(
        _swa_kernel, name="swa", grid=(B, n_i), in_specs=in_specs,
        out_specs=pl.BlockSpec((tq, D), lambda b, i: (b * n_i + i, 0)),
        out_shape=jax.ShapeDtypeStruct((T, D), BF16),
        scratch_shapes=[pltpu.VMEM((tq, D), BF16)],
        compiler_params=_cparams(2))(q2, k, k, v, v, sink_cols, wo, ga)


def _ssm_prep_kernel(lam_row_ref, lam_col_ref, dt_ref, bt_a_ref, bt_b_ref, c_re_ref, c_im_ref,
                     ct_a_ref, ct_b_ref, dsk_ref, toep_ref, wst_ref, wout_ref, apow_ref):
    L = LANES
    P = SSM_STATE
    dt = jnp.exp(dt_ref[0])
    lr2 = lam_row_ref[0, 0:1, :]
    li2 = lam_row_ref[0, 1:2, :]
    mag = jnp.exp(lr2 * dt)
    ar = mag * jnp.cos(li2 * dt)
    ai = mag * jnp.sin(li2 * dt)
    den = lr2 * lr2 + li2 * li2
    f_re = ((ar - 1.0) * lr2 + ai * li2) / den
    f_im = (ai * lr2 - (ar - 1.0) * li2) / den
    bt_a = bt_a_ref[0]
    bt_b = bt_b_ref[0]
    bb_a = f_re * bt_a + f_im * bt_b
    bb_b = f_re * bt_b - f_im * bt_a

    tau_rev = (L - 1 - lax.broadcasted_iota(jnp.int32, (L, 2 * P), 0)).astype(F32)
    g_mag = jnp.exp(lr2 * dt * tau_rev)
    g_re = g_mag * jnp.cos(li2 * dt * tau_rev)
    g_im = g_mag * jnp.sin(li2 * dt * tau_rev)
    for i in range(SSM_GROUP):
        wst_ref[0, i * L:(i + 1) * L, :] = (g_re * bb_a[i:i + 1, :] + g_im * bb_b[i:i + 1, :]).astype(BF16)

    lane2 = lax.broadcasted_iota(jnp.int32, (1, 2 * P), 1)
    for kk in range(4):
        n = float(L * (1 << kk))
        pm = jnp.exp(lr2 * dt * n)
        p_re = pm * jnp.cos(li2 * dt * n)
        p_im = pm * jnp.sin(li2 * dt * n)
        apow_ref[0, 2 * kk:2 * kk + 1, :] = p_re
        apow_ref[0, 2 * kk + 1:2 * kk + 2, :] = jnp.where(lane2 < P, -p_im, p_im)

    lrc = lam_col_ref[0, :, 0:1]
    lic = lam_col_ref[0, :, 1:2]
    tau = lax.broadcasted_iota(jnp.int32, (2 * P, L), 1).astype(F32)
    e0_mag = jnp.exp(lrc * dt * tau)
    e0_re = e0_mag * jnp.cos(lic * dt * tau)
    e0_im = e0_mag * jnp.sin(lic * dt * tau)
    e1_mag = jnp.exp(lrc * dt * (tau + 1.0))
    e1_re = e1_mag * jnp.cos(lic * dt * (tau + 1.0))
    e1_im = e1_mag * jnp.sin(lic * dt * (tau + 1.0))

    ct_a = ct_a_ref[0]
    ct_b = ct_b_ref[0]
    for o in range(SSM_GROUP):
        wout_ref[0, :, o * L:(o + 1) * L] = (ct_a[:, o:o + 1] * e1_re + ct_b[:, o:o + 1] * e1_im).astype(BF16)

    c_re = c_re_ref[0]
    c_im = c_im_ref[0]
    bb_re = bb_a[:, :P]
    bb_im = bb_a[:, P:]
    m_re = (c_re[:, None, :] * bb_re[None, :, :] - c_im[:, None, :] * bb_im[None, :, :])
    m_im = (c_re[:, None, :] * bb_im[None, :, :] + c_im[:, None, :] * bb_re[None, :, :])
    m_re = m_re.reshape(SSM_GROUP * SSM_GROUP, P)
    m_im = m_im.reshape(SSM_GROUP * SSM_GROUP, P)
    kt = (jnp.dot(m_re, e0_re[:P, :], precision=HI, preferred_element_type=F32)
          - jnp.dot(m_im, e0_im[:P, :], precision=HI, preferred_element_type=F32))
    rowi = lax.broadcasted_iota(jnp.int32, (SSM_GROUP * SSM_GROUP, L), 0)
    coli = lax.broadcasted_iota(jnp.int32, (SSM_GROUP * SSM_GROUP, L), 1)
    dsk = dsk_ref[0]
    kt = kt + jnp.where((coli == 0) & ((rowi // SSM_GROUP) == (rowi % SSM_GROUP)), dsk, 0.0)

    cc = lax.broadcasted_iota(jnp.int32, (L, L), 0)
    cp = lax.broadcasted_iota(jnp.int32, (L, L), 1)
    causal = cp >= cc
    for o in range(SSM_GROUP):
        for i in range(SSM_GROUP):
            kv = kt[o * SSM_GROUP + i:o * SSM_GROUP + i + 1, :]
            blk = pltpu.roll(jnp.broadcast_to(kv, (L, L)), 0, axis=1, stride=1, stride_axis=0)
            blk = jnp.where(causal, blk, 0.0)
            toep_ref[0, i * L:(i + 1) * L, o * L:(o + 1) * L] = blk.astype(BF16)


def _ssm_prep(lam_row, lam_col, log_dt, bt_a, bt_b, c_re, c_im, ct_a, ct_b, dsk):
    G = SSM_GROUPS
    KW = SSM_GROUP * LANES
    g3 = lambda g: (g, 0, 0)
    ins = [lam_row, lam_col, log_dt, bt_a, bt_b, c_re, c_im, ct_a, ct_b, dsk]
    in_specs = [pl.BlockSpec((1,) + a.shape[1:], g3) for a in ins]
    out_shape = (jax.ShapeDtypeStruct((G, KW, KW), BF16),
                 jax.ShapeDtypeStruct((G, KW, 2 * SSM_STATE), BF16),
                 jax.ShapeDtypeStruct((G, 2 * SSM_STATE, KW), BF16),
                 jax.ShapeDtypeStruct((G, 8, 2 * SSM_STATE), F32))
    out_specs = tuple(pl.BlockSpec((1,) + s.shape[1:], g3) for s in out_shape)
    return pl.pallas_call(_ssm_prep_kernel, name="ssm_prep", grid=(G,), in_specs=in_specs, out_specs=out_specs,
                          out_shape=out_shape, compiler_params=_cparams())(*ins)


def _ssm_scan_kernel(ut_ref, toep_ref, wst_ref, wout_ref, apow_ref, yt_ref, *, n_chunks):
    nb = ut_ref.shape[0]
    P = SSM_STATE
    lhs = jnp.concatenate([ut_ref[:, i, :] for i in range(SSM_GROUP)], axis=1).astype(BF16)
    y = _dot(lhs, toep_ref[0])
    st = _dot(lhs, wst_ref[0])
    srow = lax.broadcasted_iota(jnp.int32, (nb, 2 * P), 0) % n_chunks
    kk = 0
    while (1 << kk) < n_chunks:
        d = 1 << kk
        pa = apow_ref[0, 2 * kk:2 * kk + 1, :]
        pb = apow_ref[0, 2 * kk + 1:2 * kk + 2, :]
        prev = pltpu.roll(st, d, axis=0)
        prev = prev * pa + pltpu.roll(prev, P, axis=1) * pb
        st = st + jnp.where(srow >= d, prev, 0.0)
        kk += 1
    h0 = jnp.where(srow >= 1, pltpu.roll(st, 1, axis=0), 0.0)
    y = y + _dot(h0.astype(BF16), wout_ref[0])
    for o in range(SSM_GROUP):
        yt_ref[o] = y[:, o * LANES:(o + 1) * LANES]


def _ssm_scan(ut, toep, wst, wout, apow, n_chunks):
    nb, ssm_w, _ = ut.shape
    G = SSM_GROUPS
    g3 = lambda g: (g, 0, 0)
    in_specs = [pl.BlockSpec((nb, SSM_GROUP, LANES), lambda g: (0, g, 0)),
                pl.BlockSpec((1,) + toep.shape[1:], g3), pl.BlockSpec((1,) + wst.shape[1:], g3),
                pl.BlockSpec((1,) + wout.shape[1:], g3), pl.BlockSpec((1,) + apow.shape[1:], g3)]
    return pl.pallas_call(
        functools.partial(_ssm_scan_kernel, n_chunks=n_chunks), name="ssm_scan", grid=(G,), in_specs=in_specs,
        out_specs=pl.BlockSpec((SSM_GROUP, nb, LANES), g3),
        out_shape=jax.ShapeDtypeStruct((ssm_w, nb, LANES), F32),
        compiler_params=_cparams())(ut, toep, wst, wout, apow)


def _mix_kernel(yt_ref, att_ref, gs_ref, x_ref, wa_ref, wb_ref, wo_ref, g_ref, b_ref, out_ref, zt_ref,
                *, alpha):
    n_blk = yt_ref.shape[1]
    for j in range(n_blk):
        y = yt_ref[:, j, :]
        z = jax.nn.gelu(y, approximate=True)
        zt_ref[j * LANES:(j + 1) * LANES, :] = z.T.astype(BF16)
    z = zt_ref[...]
    ssm_out = _dot(z, wa_ref[...]) * jax.nn.sigmoid(_dot(z, wb_ref[...]))
    mixed = att_ref[...].astype(F32) + gs_ref[...].astype(F32) * ssm_out
    y = alpha * x_ref[...] + _dot(mixed.astype(BF16), wo_ref[...])
    out_ref[...] = _layer_norm(y, g_ref[...], b_ref[...])


def _mix(yt, att, gs, x2d, wa, wb, wo, g, b, alpha, tm):
    T, D = x2d.shape
    ssm_w = yt.shape[0]
    row = lambda i: (i, 0)
    in_specs = [pl.BlockSpec((ssm_w, tm // LANES, LANES), lambda i: (0, i, 0)),
                pl.BlockSpec((tm, D), row), pl.BlockSpec((tm, D), row), pl.BlockSpec((tm, D), row),
                _full_spec(wa.shape), _full_spec(wb.shape), _full_spec(wo.shape),
                _full_spec(g.shape), _full_spec(b.shape)]
    return pl.pallas_call(functools.partial(_mix_kernel, alpha=alpha), name="mix", grid=(T // tm,), in_specs=in_specs,
                          out_specs=pl.BlockSpec((tm, D), row),
                          out_shape=jax.ShapeDtypeStruct((T, D), F32),
                          scratch_shapes=[pltpu.VMEM((tm, ssm_w), BF16)],
                          compiler_params=_cparams())(yt, att, gs, x2d, wa, wb, wo, g, b)


def _memkv_kernel(m_ref, wk_ref, wv_ref, k_ref, v_ref):
    mb = m_ref[...].astype(BF16)
    k_ref[...] = _dot(mb, wk_ref[...]).astype(BF16)
    v_ref[...] = _dot(mb, wv_ref[...]).astype(BF16)


def _memkv(mem2d, wk, wv, tm):
    R, D = mem2d.shape
    row = lambda i: (i, 0)
    return pl.pallas_call(_memkv_kernel, name="memkv", grid=(R // tm,),
                          in_specs=[pl.BlockSpec((tm, D), row), _full_spec(wk.shape), _full_spec(wv.shape)],
                          out_specs=(pl.BlockSpec((tm, D), row), pl.BlockSpec((tm, D), row)),
                          out_shape=(jax.ShapeDtypeStruct((R, D), BF16), jax.ShapeDtypeStruct((R, D), BF16)),
                          compiler_params=_cparams())(mem2d, wk, wv)


def _cross_kernel(x_ref, k_ref, v_ref, wq_ref, wo_ref, g_ref, b_ref, wr_ref, br_ref, out_ref, logit_ref,
                  cat_ref, *, alpha):
    x = x_ref[...]
    D = x.shape[1]
    hd = D // N_CROSS_HEADS
    q = (_dot(x.astype(BF16), wq_ref[...]) * (hd ** -0.5)).astype(BF16)
    for h in range(N_CROSS_HEADS):
        cols = slice(h * hd, (h + 1) * hd)
        s = _dot_nt(q[:, cols], k_ref[:, cols])
        m = jnp.max(s, axis=-1, keepdims=True)
        p = jnp.exp(s - m)
        w = (p / jnp.sum(p, axis=-1, keepdims=True)).astype(BF16)
        cat_ref[:, cols] = _dot(w, v_ref[:, cols]).astype(BF16)
    y = alpha * x + _dot(cat_ref[...], wo_ref[...])
    x2 = _layer_norm(y, g_ref[...], b_ref[...])
    out_ref[...] = x2
    logit_ref[...] = _dot(x2.astype(BF16), wr_ref[...]) + br_ref[...]


def _cross(x1, kc, vc, wq, wo, g, b, wr, br, alpha, S, tm):
    T, D = x1.shape
    n_mem = kc.shape[0] // (T // S)
    row = lambda i: (i, 0)
    per_b = S // tm
    in_specs = [pl.BlockSpec((tm, D), row),
                pl.BlockSpec((n_mem, D), lambda i: (i // per_b, 0)),
                pl.BlockSpec((n_mem, D), lambda i: (i // per_b, 0)),
                _full_spec(wq.shape), _full_spec(wo.shape), _full_spec(g.shape), _full_spec(b.shape),
                _full_spec(wr.shape), _full_spec(br.shape)]
    return pl.pallas_call(functools.partial(_cross_kernel, alpha=alpha), name="cross", grid=(T // tm,), in_specs=in_specs,
                          out_specs=(pl.BlockSpec((tm, D), row), pl.BlockSpec((tm, LANES), row)),
                          out_shape=(jax.ShapeDtypeStruct((T, D), F32), jax.ShapeDtypeStruct((T, LANES), F32)),
                          scratch_shapes=[pltpu.VMEM((tm, D), BF16)],
                          compiler_params=_cparams())(x1, kc, vc, wq, wo, g, b, wr, br)


SEG_ALIGN = 8


def _route_kernel(logit_ref, info_ref, seg_ref, count_ref, carry_ref):
    tm = logit_ref.shape[0]

    @pl.when(pl.program_id(0) == 0)
    def _():
        carry_ref[...] = jnp.zeros_like(carry_ref)

    vals = logit_ref[...]
    lane = lax.broadcasted_iota(jnp.int32, vals.shape, 1).astype(F32)
    sels, tops = [], []
    for _ in range(TOP_K):
        m = jnp.max(vals, axis=-1, keepdims=True)
        idx = jnp.min(jnp.where(vals == m, lane, float(LANES)), axis=-1, keepdims=True)
        sel = lane == idx
        vals = jnp.where(sel, -jnp.inf, vals)
        sels.append((sel, idx))
        tops.append(m)
    es = [jnp.exp(t - tops[0]) for t in tops]
    denom = es[0]
    for e in es[1:]:
        denom = denom + e
    onehot = jnp.zeros(vals.shape, F32)
    for sel, _ in sels:
        onehot = onehot + jnp.where(sel, 1.0, 0.0)
    r = lax.broadcasted_iota(jnp.int32, (tm, tm), 0)
    c = lax.broadcasted_iota(jnp.int32, (tm, tm), 1)
    tri = jnp.where(c < r, 1.0, 0.0).astype(BF16)
    cum = _dot(tri, onehot.astype(BF16))
    n_seg = jnp.floor((jnp.sum(onehot, axis=0, keepdims=True) + (SEG_ALIGN - 1.0)) * (1.0 / SEG_ALIGN)) * SEG_ALIGN
    a = lax.broadcasted_iota(jnp.int32, (LANES, LANES), 0)
    b = lax.broadcasted_iota(jnp.int32, (LANES, LANES), 1)
    upper = jnp.where(a < b, 1.0, 0.0).astype(BF16)
    seg_off = _dot(jnp.broadcast_to(n_seg, (8, LANES)).astype(BF16), upper)[0:1, :]
    carry = carry_ref[...]
    info = jnp.zeros(vals.shape, F32)
    for k, (sel, idx) in enumerate(sels):
        local = jnp.sum(jnp.where(sel, cum + seg_off, 0.0), axis=-1, keepdims=True)
        info = jnp.where(lane == k, idx, info)
        info = jnp.where(lane == TOP_K + k, local, info)
        info = jnp.where(lane == 2 * TOP_K + k, es[k] / denom, info)
    info_ref[...] = info
    row8 = lax.broadcasted_iota(jnp.int32, (8, LANES), 0)
    seg_ref[0] = jnp.where(row8 == 0, n_seg, jnp.where(row8 == 1, seg_off, jnp.where(row8 == 2, carry, 0.0)))
    carry_ref[...] = carry + n_seg
    count_ref[...] = carry_ref[...]


def _route(logits, tm):
    T = logits.shape[0]
    row = lambda i: (i, 0)
    return pl.pallas_call(_route_kernel, name="route", grid=(T // tm,),
                          in_specs=[pl.BlockSpec((tm, LANES), row)],
                          out_specs=(pl.BlockSpec((tm, LANES), row), pl.BlockSpec((1, 8, LANES), lambda i: (i, 0, 0)),
                                     _full_spec((1, LANES))),
                          out_shape=(jax.ShapeDtypeStruct((T, LANES), F32),
                                     jax.ShapeDtypeStruct((T // tm, 8, LANES), F32),
                                     jax.ShapeDtypeStruct((1, LANES), F32)),
                          scratch_shapes=[pltpu.VMEM((1, LANES), F32)],
                          compiler_params=_cparams())(logits)


GU_TILE = 2 * LANES


SEG_SIZES = (256, 128, 64, 32, 16, 8)


def _segment_copies(seg_ref, local_ref, global_ref, sem, to_global):
    for e in range(N_EXPERTS):
        n = seg_ref[0, 0, e]
        off = seg_ref[0, 1, e]
        start = seg_ref[0, 2, e]
        for sz in SEG_SIZES:
            take = (n & sz) != 0

            @pl.when(take)
            def _(off=off, start=start, sz=sz):
                loc = local_ref.at[pl.ds(pl.multiple_of(off, SEG_ALIGN), sz)]
                glo = global_ref.at[pl.ds(pl.multiple_of(start, SEG_ALIGN), sz)]
                if to_global:
                    pltpu.make_async_copy(loc, glo, sem).start()
                else:
                    pltpu.make_async_copy(glo, loc, sem).start()

            step = jnp.where(take, sz, 0)
            off = off + step
            start = start + step


def _wait_rows(local_ref, global_ref, sem, n_rows):
    @pl.when(n_rows > 0)
    def _():
        n = pl.multiple_of(n_rows, SEG_ALIGN)
        pltpu.make_async_copy(global_ref.at[pl.ds(0, n)], local_ref.at[pl.ds(0, n)], sem).wait()


def _dispatch_kernel(ends_ref, seg_ref, x_ref, info_ref, xs_hbm, comp, tot, sem):
    i = pl.program_id(0)
    n_steps = pl.num_programs(0)
    slot = i % 2
    tm = x_ref.shape[0]
    cap = comp.shape[1]

    @pl.when(i == 0)
    def _():
        comp[1, 0:MOE_BLOCK, :] = jnp.zeros((MOE_BLOCK, comp.shape[2]), F32)
        for e in range(N_EXPERTS):
            end = ends_ref[e]
            prev = ends_ref[e - 1] if e > 0 else 0

            @pl.when(end > prev)
            def _(end=end):
                cp = pltpu.make_async_copy(comp.at[1, pl.ds(0, MOE_BLOCK)],
                                           xs_hbm.at[pl.ds(pl.multiple_of(end - MOE_BLOCK, MOE_BLOCK), MOE_BLOCK)],
                                           sem.at[1])
                cp.start()
                cp.wait()

        def tail(b):
            return pltpu.make_async_copy(comp.at[1, pl.ds(0, MOE_BLOCK)],
                                         xs_hbm.at[pl.ds(pl.multiple_of(b * MOE_BLOCK, MOE_BLOCK), MOE_BLOCK)],
                                         sem.at[0])
        first_unused = ends_ref[N_EXPERTS - 1] // MOE_BLOCK
        n_all = xs_hbm.shape[0] // MOE_BLOCK
        lax.fori_loop(first_unused, n_all, lambda b, c: (tail(b).start(), c)[1], 0)
        lax.fori_loop(first_unused, n_all, lambda b, c: (tail(b).wait(), c)[1], 0)
        tot[0] = 0
        tot[1] = 0

    _wait_rows(comp.at[slot], xs_hbm, sem.at[slot], tot[slot])
    pos = info_ref[...].T
    r = lax.broadcasted_iota(jnp.int32, (cap, tm), 0).astype(F32)
    hit = r == pos[TOP_K:TOP_K + 1, :]
    for k in range(1, TOP_K):
        hit = hit | (r == pos[TOP_K + k:TOP_K + k + 1, :])
    sel = jnp.where(hit, 1.0, 0.0).astype(BF16)
    comp[slot] = _dot(sel, x_ref[...].astype(BF16))
    _segment_copies(seg_ref, comp.at[slot], xs_hbm, sem.at[slot], to_global=True)
    tot[slot] = seg_ref[0, 3, 0]

    @pl.when(i == n_steps - 1)
    def _():
        _wait_rows(comp.at[slot], xs_hbm, sem.at[slot], tot[slot])
        _wait_rows(comp.at[1 - slot], xs_hbm, sem.at[1 - slot], tot[1 - slot])


def _dispatch(ends_pad, seg, x2, info, n_slots, tm):
    T, D = x2.shape
    row = lambda i, ends: (i, 0)
    grid_spec = pltpu.PrefetchScalarGridSpec(
        num_scalar_prefetch=1, grid=(T // tm,),
        in_specs=[pl.BlockSpec((1, 4, LANES), lambda i, ends: (i, 0, 0), memory_space=pltpu.SMEM),
                  pl.BlockSpec((tm, D), row), pl.BlockSpec((tm, LANES), row)],
        out_specs=pl.BlockSpec(memory_space=pl.ANY),
        scratch_shapes=[pltpu.VMEM((2, TOP_K * tm + N_EXPERTS * SEG_ALIGN, D), F32), pltpu.SMEM((2,), jnp.int32),
                        pltpu.SemaphoreType.DMA((2,))])
    return pl.pallas_call(_dispatch_kernel, name="dispatch", grid_spec=grid_spec,
                          out_shape=jax.ShapeDtypeStruct((n_slots, D), F32),
                          compiler_params=_cparams())(ends_pad, seg, x2, info)


def _moe_kernel(bexp_ref, nused_ref, x_ref, wgu_ref, bgu_ref, wd_ref, bd_ref, y_ref, wgu_s, wd_s):
    j = pl.program_id(0)
    n_used = nused_ref[0]
    n_tiles = wgu_s.shape[1] // GU_TILE

    @pl.when(j >= n_used)
    def _():
        y_ref[...] = jnp.zeros_like(y_ref)

    @pl.when(j < n_used)
    def _():
        @pl.when((j == 0) | (bexp_ref[j] != bexp_ref[jnp.maximum(j - 1, 0)]))
        def _():
            k = lax.broadcasted_iota(jnp.int32, (GU_TILE, GU_TILE), 0)
            n = lax.broadcasted_iota(jnp.int32, (GU_TILE, GU_TILE), 1)
            perm = jnp.where(k == jnp.where(n < LANES, 2 * n, 2 * (n - LANES) + 1), 1.0, 0.0).astype(BF16)
            for t in range(n_tiles):
                cols = slice(t * GU_TILE, (t + 1) * GU_TILE)
                wgu_s[:, cols] = _dot(wgu_ref[0, :, cols].astype(BF16), perm).astype(BF16)
            wd_s[...] = wd_ref[0].astype(BF16)

        xb = x_ref[...].astype(BF16)
        hs = []
        for t in range(n_tiles):
            cols = slice(t * GU_TILE, (t + 1) * GU_TILE)
            gu = _dot(xb, wgu_s[:, cols]) + bgu_ref[0, :, cols]
            gate = jnp.minimum(gu[:, :LANES], SWIGLU_LIMIT)
            lin = jnp.clip(gu[:, LANES:], -SWIGLU_LIMIT, SWIGLU_LIMIT)
            hs.append((gate * jax.nn.sigmoid(SWIGLU_ALPHA * gate) * (lin + 1.0)).astype(BF16))
        y_ref[...] = _dot(jnp.concatenate(hs, axis=1), wd_s[...]) + bd_ref[0]


def _moe(bexp, n_used, xs, wgu, bgu, wd, bd):
    n_blocks = bexp.shape[0]
    D = xs.shape[1]
    F2 = wgu.shape[2]
    F = wd.shape[1]
    wmap = lambda j, be, nu: (be[jnp.minimum(j, nu[0] - 1)], 0, 0)
    blk = lambda j, be, nu: (j, 0)
    used = lambda j, be, nu: (jnp.minimum(j, nu[0] - 1), 0)
    in_specs = [pl.BlockSpec((MOE_BLOCK, D), used),
                pl.BlockSpec((1, D, F2), wmap), pl.BlockSpec((1, 1, F2), wmap),
                pl.BlockSpec((1, F, D), wmap), pl.BlockSpec((1, 1, D), wmap)]
    grid_spec = pltpu.PrefetchScalarGridSpec(
        num_scalar_prefetch=2, grid=(n_blocks,), in_specs=in_specs,
        out_specs=pl.BlockSpec((MOE_BLOCK, D), blk),
        scratch_shapes=[pltpu.VMEM((D, F2), BF16), pltpu.VMEM((F, D), BF16)])
    return pl.pallas_call(_moe_kernel, name="moe", grid_spec=grid_spec,
                          out_shape=jax.ShapeDtypeStruct((n_blocks * MOE_BLOCK, D), F32),
                          compiler_params=_cparams())(bexp, n_used, xs, wgu, bgu, wd, bd)


def _combine_kernel(seg_ref, seg_nxt_ref, ys_hbm, info_ref, x_ref, g_ref, b_ref, out_ref, comp, sem, *, alpha):
    i = pl.program_id(0)
    n_steps = pl.num_programs(0)
    slot = i % 2
    tm = x_ref.shape[0]
    cap = comp.shape[1]

    @pl.when(i == 0)
    def _():
        comp[...] = jnp.zeros_like(comp)
        _segment_copies(seg_ref, comp.at[0], ys_hbm, sem.at[0], to_global=False)

    @pl.when(i + 1 < n_steps)
    def _():
        _segment_copies(seg_nxt_ref, comp.at[1 - slot], ys_hbm, sem.at[1 - slot], to_global=False)

    _wait_rows(comp.at[slot], ys_hbm, sem.at[slot], seg_ref[0, 3, 0])
    info = info_ref[...]
    lane = lax.broadcasted_iota(jnp.int32, (tm, cap), 1).astype(F32)
    wmat = jnp.zeros((tm, cap), F32)
    for k in range(TOP_K):
        wmat = jnp.where(lane == info[:, TOP_K + k:TOP_K + k + 1], info[:, 2 * TOP_K + k:2 * TOP_K + k + 1], wmat)
    ffn = _dot(wmat.astype(BF16), comp[slot].astype(BF16))
    out_ref[...] = _layer_norm(alpha * x_ref[...] + ffn, g_ref[...], b_ref[...])


def _combine(seg, ys, info, x2, g, b, alpha, tm):
    T, D = x2.shape
    n_t = T // tm
    row = lambda i: (i, 0)
    seg_spec = lambda m: pl.BlockSpec((1, 4, LANES), m, memory_space=pltpu.SMEM)
    in_specs = [seg_spec(lambda i: (i, 0, 0)), seg_spec(lambda i: (jnp.minimum(i + 1, n_t - 1), 0, 0)),
                pl.BlockSpec(memory_space=pl.ANY),
                pl.BlockSpec((tm, LANES), row), pl.BlockSpec((tm, D), row),
                _full_spec(g.shape), _full_spec(b.shape)]
    cap = TOP_K * tm + N_EXPERTS * SEG_ALIGN
    return pl.pallas_call(functools.partial(_combine_kernel, alpha=alpha), name="combine", grid=(n_t,), in_specs=in_specs,
                          out_specs=pl.BlockSpec((tm, D), row),
                          out_shape=jax.ShapeDtypeStruct((T, D), F32),
                          scratch_shapes=[pltpu.VMEM((2, cap, D), F32), pltpu.SemaphoreType.DMA((2,))],
                          compiler_params=_cparams())(seg, seg, ys, info, x2, g, b)


def _tile(n, pref):
    return pref if n % pref == 0 else n


def _layer(x2d, mem2d, pos2d, B, S, depth, w_in, sinks, w_attn_o, lam_re, lam_im, log_dt, b_re, b_im,
           c_re, c_im, d_skip, w_glu_a, w_glu_b, w_out, ln1_g, ln1_b, wq_c, wk_c, wv_c, wo_c, ln2_g,
           ln2_b, w_router, b_router, w_gate_up, b_gate_up, w_down, b_down, ln3_g, ln3_b):
    T, D = x2d.shape
    alpha = (2 * depth) ** 0.25
    rep = N_Q_HEADS // N_KV_HEADS
    q_w = N_Q_HEADS * HEAD_DIM
    kv_w = N_KV_HEADS * HEAD_DIM
    ssm_w = SSM_GROUP * SSM_GROUPS
    P = SSM_STATE

    o_k, o_v, o_s = q_w, q_w + kv_w, q_w + 2 * kv_w
    o_ga, o_gs = o_s + ssm_w, o_s + ssm_w + D
    wq = w_in[:, :o_k].reshape(D, N_KV_HEADS, rep, HEAD_DIM).transpose(0, 2, 1, 3).reshape(D, q_w).astype(BF16)
    wk = w_in[:, o_k:o_v].astype(BF16)
    wv = w_in[:, o_v:o_s].astype(BF16)
    wut = w_in[:, o_s:o_ga].T.astype(BF16)
    wga = w_in[:, o_ga:o_gs].astype(BF16)
    wgs = w_in[:, o_gs:].astype(BF16)
    wo_attn = w_attn_o.reshape(N_KV_HEADS, rep, HEAD_DIM, D).transpose(1, 0, 2, 3).reshape(q_w, D).astype(BF16)
    half = HEAD_DIM // 2
    inv_freq = jnp.power(ROPE_THETA, -jnp.arange(half, dtype=F32) / half)
    invf = jnp.tile(inv_freq, LANES // half)[None, :]
    sink_rows = jnp.repeat(sinks.astype(F32).reshape(N_KV_HEADS, rep), WINDOW, axis=1)
    sink_cols = jnp.full((N_KV_HEADS, rep * WINDOW, 2 * WINDOW), NEG_BIG, F32).at[:, :, 0].set(sink_rows)

    tm1 = _tile(T, 512)
    q2, k, v, ut, ga, gs = _inproj(x2d, pos2d, invf, wq, wk, wv, wut, wga, wgs, tm1)
    att = _swa(q2, k, v, sink_cols, wo_attn, ga, B, S, _tile(S, 512))

    lam_row = jnp.stack([jnp.concatenate([lam_re, lam_re], -1), jnp.concatenate([lam_im, lam_im], -1)], 1)
    lam_col = jnp.swapaxes(lam_row, 1, 2)
    bt_re = jnp.swapaxes(b_re, 1, 2)
    bt_im = jnp.swapaxes(b_im, 1, 2)
    bt_a = jnp.concatenate([bt_re, bt_im], -1)
    bt_b = jnp.concatenate([-bt_im, bt_re], -1)
    ct_re = jnp.swapaxes(c_re, 1, 2)
    ct_im = jnp.swapaxes(c_im, 1, 2)
    ct_a = jnp.concatenate([ct_re, -ct_im], 1)
    ct_b = jnp.concatenate([-ct_im, -ct_re], 1)
    dsk = jnp.repeat(d_skip.reshape(SSM_GROUPS, SSM_GROUP), SSM_GROUP, axis=1)[:, :, None]
    toep, wst, wout, apow = _ssm_prep(lam_row, lam_col, log_dt.reshape(SSM_GROUPS, 1, 1), bt_a, bt_b,
                                      c_re, c_im, ct_a, ct_b, dsk)
    yt = _ssm_scan(ut, toep, wst, wout, apow, S // LANES)

    x1 = _mix(yt, att, gs, x2d, w_glu_a.astype(BF16), w_glu_b.astype(BF16), w_out.astype(BF16),
              ln1_g[None, :], ln1_b[None, :], alpha, _tile(T, 1024))

    kc, vc = _memkv(mem2d, wk_c.astype(BF16), wv_c.astype(BF16), _tile(mem2d.shape[0], 512))
    wr = jnp.zeros((D, LANES), F32).at[:, :N_EXPERTS].set(w_router).astype(BF16)
    br = jnp.full((1, LANES), NEG_BIG, F32).at[0, :N_EXPERTS].set(b_router)
    x2, logits = _cross(x1, kc, vc, wq_c.astype(BF16), wo_c.astype(BF16), ln2_g[None, :], ln2_b[None, :],
                        wr, br, alpha, S, _tile(S, 512))

    tm_r = _tile(T, MOE_BLOCK)
    n_tiles = T // tm_r
    info, segf, counts = _route(logits, tm_r)
    total = counts[0].astype(jnp.int32)
    padded = (total + MOE_BLOCK - 1) // MOE_BLOCK * MOE_BLOCK
    ends_lane = jnp.cumsum(padded)
    start_lane = ends_lane - padded
    ends_pad = ends_lane[:N_EXPERTS]
    n_blocks = -(-(T * TOP_K + n_tiles * N_EXPERTS * (SEG_ALIGN - 1)) // MOE_BLOCK) + N_EXPERTS
    segi = segf.astype(jnp.int32)
    seg_n, seg_off, seg_start = segi[:, 0, :], segi[:, 1, :], segi[:, 2, :] + start_lane[None, :]
    seg_tot = jnp.broadcast_to(jnp.sum(seg_n, axis=1, keepdims=True), seg_n.shape)
    seg = jnp.stack([seg_n, seg_off, seg_start, seg_tot], axis=1)
    block_start = jnp.arange(n_blocks, dtype=jnp.int32) * MOE_BLOCK
    bexp = jnp.minimum(jnp.sum(block_start[:, None] >= ends_pad[None, :], axis=1), N_EXPERTS - 1).astype(jnp.int32)
    n_used = (ends_pad[-1] // MOE_BLOCK).astype(jnp.int32)[None]
    n_gu_tiles = b_gate_up.shape[1] // GU_TILE
    bgu = b_gate_up.reshape(N_EXPERTS, n_gu_tiles, LANES, 2).transpose(0, 1, 3, 2).reshape(N_EXPERTS, 1, -1)
    xs = _dispatch(ends_pad, seg, x2, info, n_blocks * MOE_BLOCK, tm_r)
    ys = _moe(bexp, n_used, xs, w_gate_up, bgu, w_down, b_down[:, None, :])
    return _combine(seg, ys, info, x2, ln3_g[None, :], ln3_b[None, :], alpha, tm_r)


def kernel(x, mem, positions, w_in, sinks, w_attn_o, lam_re, lam_im, log_dt, b_re, b_im, c_re, c_im, d_skip,
           w_glu_a, w_glu_b, w_out, ln1_g, ln1_b, wq_c, wk_c, wv_c, wo_c, ln2_g, ln2_b, w_router, b_router,
           w_gate_up, b_gate_up, w_down, b_down, ln3_g, ln3_b):
    B, S, D = x.shape
    depth = w_in.shape[0]
    x2d = x.reshape(B * S, D)
    mem2d = mem.reshape(-1, D)
    pos2d = positions.reshape(B * S, 1)
    per_layer = (w_in, sinks, w_attn_o, lam_re, lam_im, log_dt, b_re, b_im, c_re, c_im, d_skip, w_glu_a,
                 w_glu_b, w_out, ln1_g, ln1_b, wq_c, wk_c, wv_c, wo_c, ln2_g, ln2_b, w_router, b_router,
                 w_gate_up, b_gate_up, w_down, b_down, ln3_g, ln3_b)
    for l in range(depth):
        x2d = _layer(x2d, mem2d, pos2d, B, S, depth, *(w[l] for w in per_layer))
    return x2d.reshape(B, S, D)
```

```python
import functools

import jax
import jax.numpy as jnp
from jax import lax
from jax.experimental import pallas as pl
from jax.experimental.pallas import tpu as pltpu

N_Q_HEADS = 16
N_KV_HEADS = 2
HEAD_DIM = 64
WINDOW = 128
ROPE_THETA = 10000.0
SSM_GROUP = 16
SSM_GROUPS = 32
SSM_STATE = 64
N_CROSS_HEADS = 4
N_EXPERTS = 32
TOP_K = 4
SWIGLU_ALPHA = 1.702
SWIGLU_LIMIT = 7.0
MOE_BLOCK = 512
ROUTE_TILE = 256
LN_EPS = 1e-5

LANES = 128
VMEM_LIMIT_BYTES = 56 * 1024 * 1024

NEG_BIG = -1e30
BF16 = jnp.bfloat16
F32 = jnp.float32
HI = lax.Precision.HIGHEST


def _cparams(n_axes=1):
    return pltpu.CompilerParams(dimension_semantics=("arbitrary",) * n_axes,
                                vmem_limit_bytes=VMEM_LIMIT_BYTES)


def _full_spec(shape):
    n = len(shape)
    return pl.BlockSpec(shape, lambda *_: (0,) * n)


def _dot(a, b):
    return jnp.dot(a, b, preferred_element_type=F32)


def _dot_nt(a, b):
    return lax.dot_general(a, b, (((1,), (1,)), ((), ())), preferred_element_type=F32)


def _layer_norm(y, g, b):
    mu = jnp.mean(y, axis=-1, keepdims=True)
    d = y - mu
    var = jnp.mean(d * d, axis=-1, keepdims=True)
    return d * lax.rsqrt(var + LN_EPS) * g + b


def _rope(t, cos, sin_signed, first_half):
    half = HEAD_DIM // 2
    partner = jnp.where(first_half, pltpu.roll(t, LANES - half, axis=1), pltpu.roll(t, half, axis=1))
    return t * cos + partner * sin_signed


def _inproj_kernel(x_ref, pos_ref, invf_ref, wq_ref, wk_ref, wv_ref, wut_ref, wga_ref, wgs_ref,
                   q2_ref, k_ref, v_ref, ut_ref, ga_ref, gs_ref):
    tm = x_ref.shape[0]
    xb = x_ref[...].astype(BF16)
    ang = pos_ref[...].astype(F32) * invf_ref[...]
    cos = jnp.cos(ang)
    sin = jnp.sin(ang)
    first_half = (lax.broadcasted_iota(jnp.int32, (tm, LANES), 1) % HEAD_DIM) < (HEAD_DIM // 2)
    first_half_w = (lax.broadcasted_iota(jnp.int32, (WINDOW, LANES), 1) % HEAD_DIM) < (HEAD_DIM // 2)
    sin_signed = jnp.where(first_half, -sin, sin)

    q = _dot(xb, wq_ref[...])
    n_rep = q.shape[1] // LANES
    scale = HEAD_DIM ** -0.5
    for j in range(tm // WINDOW):
        rows = slice(j * WINDOW, (j + 1) * WINDOW)
        for r in range(n_rep):
            t = _rope(q[rows, r * LANES:(r + 1) * LANES], cos[rows], sin_signed[rows], first_half_w)
            base = (j * n_rep + r) * WINDOW
            q2_ref[base:base + WINDOW, :] = (t * scale).astype(BF16)
    k_ref[...] = _rope(_dot(xb, wk_ref[...]), cos, sin_signed, first_half).astype(BF16)
    v_ref[...] = _dot(xb, wv_ref[...]).astype(BF16)
    for j in range(tm // LANES):
        ut_ref[j] = _dot_nt(wut_ref[...], xb[j * LANES:(j + 1) * LANES, :])
    ga_ref[...] = jax.nn.sigmoid(_dot(xb, wga_ref[...])).astype(BF16)
    gs_ref[...] = jax.nn.sigmoid(_dot(xb, wgs_ref[...])).astype(BF16)


def _inproj(x2d, pos2d, invf, wq, wk, wv, wut, wga, wgs, tm):
    T, D = x2d.shape
    n_rep = wq.shape[1] // LANES
    ssm_w = wut.shape[0]
    row = lambda i: (i, 0)
    out_shape = (
        jax.ShapeDtypeStruct((T * n_rep, LANES), BF16),
        jax.ShapeDtypeStruct((T, LANES), BF16),
        jax.ShapeDtypeStruct((T, LANES), BF16),
        jax.ShapeDtypeStruct((T // LANES, ssm_w, LANES), F32),
        jax.ShapeDtypeStruct((T, D), BF16),
        jax.ShapeDtypeStruct((T, D), BF16),
    )
    in_specs = [pl.BlockSpec((tm, D), row), pl.BlockSpec((tm, 1), row), _full_spec(invf.shape),
                _full_spec(wq.shape), _full_spec(wk.shape), _full_spec(wv.shape),
                _full_spec(wut.shape), _full_spec(wga.shape), _full_spec(wgs.shape)]
    out_specs = (pl.BlockSpec((tm * n_rep, LANES), row), pl.BlockSpec((tm, LANES), row),
                 pl.BlockSpec((tm, LANES), row),
                 pl.BlockSpec((tm // LANES, ssm_w, LANES), lambda i: (i, 0, 0)),
                 pl.BlockSpec((tm, D), row), pl.BlockSpec((tm, D), row))
    return pl.pallas_call(_inproj_kernel, name="inproj", grid=(T // tm,), in_specs=in_specs, out_specs=out_specs,
                          out_shape=out_shape, compiler_params=_cparams())(
        x2d, pos2d, invf, wq, wk, wv, wut, wga, wgs)


def _swa_kernel(q2_ref, kc_ref, kp_ref, vc_ref, vp_ref, fill_ref, wo_ref, ga_ref, out_ref, cat_ref):
    i = pl.program_id(1)
    tq = kc_ref.shape[0]
    n_sub = tq // WINDOW
    rep = N_Q_HEADS // N_KV_HEADS
    rows_all = rep * WINDOW
    kfull = jnp.concatenate([kp_ref[...], kc_ref[...]], axis=0)
    vfull = jnp.concatenate([vp_ref[...], vc_ref[...]], axis=0)
    lane = lax.broadcasted_iota(jnp.int32, (2 * WINDOW, LANES), 1)
    qi = lax.broadcasted_iota(jnp.int32, (rows_all, 2 * WINDOW), 0) % WINDOW
    ci = lax.broadcasted_iota(jnp.int32, (rows_all, 2 * WINDOW), 1)
    local = (ci > qi) & (ci <= qi + WINDOW)
    out_lane = lax.broadcasted_iota(jnp.int32, (rows_all, LANES), 1)
    band_row = lax.broadcasted_iota(jnp.int32, (2 * WINDOW, LANES), 0)
    for j in range(n_sub):
        qs = q2_ref[j * rows_all:(j + 1) * rows_all, :]
        kb = kfull[j * WINDOW:(j + 2) * WINDOW, :]
        vb = vfull[j * WINDOW:(j + 2) * WINDOW, :]
        vb = jnp.where(band_row == 0, jnp.zeros_like(vb), vb)
        mask = local
        if j == 0:
            mask = mask & ((ci >= WINDOW) | (i > 0))
        o = None
        for g in range(N_KV_HEADS):
            in_group = (lane >= g * HEAD_DIM) & (lane < (g + 1) * HEAD_DIM)
            kg = jnp.where(in_group, kb, jnp.zeros_like(kb))
            s = _dot_nt(qs, kg)
            s = jnp.where(mask, s, fill_ref[g])
            m = jnp.max(s, axis=-1, keepdims=True)
            p = jnp.exp(s - m)
            denom = jnp.sum(p, axis=-1, keepdims=True)
            og = _dot(p.astype(BF16), vb) * (1.0 / denom)
            o = og if o is None else jnp.where(out_lane < g * HEAD_DIM, o, og)
        ob = o.astype(BF16)
        for r in range(rep):
            cat_ref[j * WINDOW:(j + 1) * WINDOW, r * LANES:(r + 1) * LANES] = ob[r * WINDOW:(r + 1) * WINDOW, :]
    attn = _dot(cat_ref[...], wo_ref[...])
    out_ref[...] = (attn * ga_ref[...].astype(F32)).astype(BF16)


def _swa(q2, k, v, sink_cols, wo, ga, B, S, tq):
    T, D = ga.shape
    rep = N_Q_HEADS // N_KV_HEADS
    n_i = S // tq
    n_sub = tq // WINDOW
    in_specs = [
        pl.BlockSpec((tq * rep, LANES), lambda b, i: (b * n_i + i, 0)),
        pl.BlockSpec((tq, LANES), lambda b, i: (b * n_i + i, 0)),
        pl.BlockSpec((WINDOW, LANES), lambda b, i: (b * (S // WINDOW) + jnp.maximum(i * n_sub - 1, 0), 0)),
        pl.BlockSpec((tq, LANES), lambda b, i: (b * n_i + i, 0)),
        pl.BlockSpec((WINDOW, LANES), lambda b, i: (b * (S // WINDOW) + jnp.maximum(i * n_sub - 1, 0), 0)),
        _full_spec(sink_cols.shape), _full_spec(wo.shape),
        pl.BlockSpec((tq, D), lambda b, i: (b * n_i + i, 0)),
    ]
    return pl.pallas_call(
        _swa_kernel, name="swa", grid=(B, n_i), in_specs=in_specs,
        out_specs=pl.BlockSpec((tq, D), lambda b, i: (b * n_i + i, 0)),
        out_shape=jax.ShapeDtypeStruct((T, D), BF16),
        scratch_shapes=[pltpu.VMEM((tq, D), BF16)],
        compiler_params=_cparams(2))(q2, k, k, v, v, sink_cols, wo, ga)


def _ssm_prep_kernel(lam_row_ref, lam_col_ref, dt_ref, bt_a_ref, bt_b_ref, c_re_ref, c_im_ref,
                     ct_a_ref, ct_b_ref, dsk_ref, toep_ref, wst_ref, wout_ref, apow_ref):
    L = LANES
    P = SSM_STATE
    dt = jnp.exp(dt_ref[0])
    lr2 = lam_row_ref[0, 0:1, :]
    li2 = lam_row_ref[0, 1:2, :]
    mag = jnp.exp(lr2 * dt)
    ar = mag * jnp.cos(li2 * dt)
    ai = mag * jnp.sin(li2 * dt)
    den = lr2 * lr2 + li2 * li2
    f_re = ((ar - 1.0) * lr2 + ai * li2) / den
    f_im = (ai * lr2 - (ar - 1.0) * li2) / den
    bt_a = bt_a_ref[0]
    bt_b = bt_b_ref[0]
    bb_a = f_re * bt_a + f_im * bt_b
    bb_b = f_re * bt_b - f_im * bt_a

    tau_rev = (L - 1 - lax.broadcasted_iota(jnp.int32, (L, 2 * P), 0)).astype(F32)
    g_mag = jnp.exp(lr2 * dt * tau_rev)
    g_re = g_mag * jnp.cos(li2 * dt * tau_rev)
    g_im = g_mag * jnp.sin(li2 * dt * tau_rev)
    for i in range(SSM_GROUP):
        wst_ref[0, i * L:(i + 1) * L, :] = (g_re * bb_a[i:i + 1, :] + g_im * bb_b[i:i + 1, :]).astype(BF16)

    lane2 = lax.broadcasted_iota(jnp.int32, (1, 2 * P), 1)
    for kk in range(4):
        n = float(L * (1 << kk))
        pm = jnp.exp(lr2 * dt * n)
        p_re = pm * jnp.cos(li2 * dt * n)
        p_im = pm * jnp.sin(li2 * dt * n)
        apow_ref[0, 2 * kk:2 * kk + 1, :] = p_re
        apow_ref[0, 2 * kk + 1:2 * kk + 2, :] = jnp.where(lane2 < P, -p_im, p_im)

    lrc = lam_col_ref[0, :, 0:1]
    lic = lam_col_ref[0, :, 1:2]
    tau = lax.broadcasted_iota(jnp.int32, (2 * P, L), 1).astype(F32)
    e0_mag = jnp.exp(lrc * dt * tau)
    e0_re = e0_mag * jnp.cos(lic * dt * tau)
    e0_im = e0_mag * jnp.sin(lic * dt * tau)
    e1_mag = jnp.exp(lrc * dt * (tau + 1.0))
    e1_re = e1_mag * jnp.cos(lic * dt * (tau + 1.0))
    e1_im = e1_mag * jnp.sin(lic * dt * (tau + 1.0))

    ct_a = ct_a_ref[0]
    ct_b = ct_b_ref[0]
    for o in range(SSM_GROUP):
        wout_ref[0, :, o * L:(o + 1) * L] = (ct_a[:, o:o + 1] * e1_re + ct_b[:, o:o + 1] * e1_im).astype(BF16)

    c_re = c_re_ref[0]
    c_im = c_im_ref[0]
    bb_re = bb_a[:, :P]
    bb_im = bb_a[:, P:]
    m_re = (c_re[:, None, :] * bb_re[None, :, :] - c_im[:, None, :] * bb_im[None, :, :])
    m_im = (c_re[:, None, :] * bb_im[None, :, :] + c_im[:, None, :] * bb_re[None, :, :])
    m_re = m_re.reshape(SSM_GROUP * SSM_GROUP, P)
    m_im = m_im.reshape(SSM_GROUP * SSM_GROUP, P)
    kt = (jnp.dot(m_re, e0_re[:P, :], precision=HI, preferred_element_type=F32)
          - jnp.dot(m_im, e0_im[:P, :], precision=HI, preferred_element_type=F32))
    rowi = lax.broadcasted_iota(jnp.int32, (SSM_GROUP * SSM_GROUP, L), 0)
    coli = lax.broadcasted_iota(jnp.int32, (SSM_GROUP * SSM_GROUP, L), 1)
    dsk = dsk_ref[0]
    kt = kt + jnp.where((coli == 0) & ((rowi // SSM_GROUP) == (rowi % SSM_GROUP)), dsk, 0.0)

    cc = lax.broadcasted_iota(jnp.int32, (L, L), 0)
    cp = lax.broadcasted_iota(jnp.int32, (L, L), 1)
    causal = cp >= cc
    for o in range(SSM_GROUP):
        for i in range(SSM_GROUP):
            kv = kt[o * SSM_GROUP + i:o * SSM_GROUP + i + 1, :]
            blk = pltpu.roll(jnp.broadcast_to(kv, (L, L)), 0, axis=1, stride=1, stride_axis=0)
            blk = jnp.where(causal, blk, 0.0)
            toep_ref[0, i * L:(i + 1) * L, o * L:(o + 1) * L] = blk.astype(BF16)


def _ssm_prep(lam_row, lam_col, log_dt, bt_a, bt_b, c_re, c_im, ct_a, ct_b, dsk):
    G = SSM_GROUPS
    KW = SSM_GROUP * LANES
    g3 = lambda g: (g, 0, 0)
    ins = [lam_row, lam_col, log_dt, bt_a, bt_b, c_re, c_im, ct_a, ct_b, dsk]
    in_specs = [pl.BlockSpec((1,) + a.shape[1:], g3) for a in ins]
    out_shape = (jax.ShapeDtypeStruct((G, KW, KW), BF16),
                 jax.ShapeDtypeStruct((G, KW, 2 * SSM_STATE), BF16),
                 jax.ShapeDtypeStruct((G, 2 * SSM_STATE, KW), BF16),
                 jax.ShapeDtypeStruct((G, 8, 2 * SSM_STATE), F32))
    out_specs = tuple(pl.BlockSpec((1,) + s.shape[1:], g3) for s in out_shape)
    return pl.pallas_call(_ssm_prep_kernel, name="ssm_prep", grid=(G,), in_specs=in_specs, out_specs=out_specs,
                          out_shape=out_shape, compiler_params=_cparams())(*ins)


def _ssm_scan_kernel(ut_ref, toep_ref, wst_ref, wout_ref, apow_ref, yt_ref, *, n_chunks):
    nb = ut_ref.shape[0]
    P = SSM_STATE
    ut2 = ut_ref.reshape(nb * SSM_GROUP, LANES)
    lhs = jnp.concatenate([ut2[pl.ds(i, nb, stride=SSM_GROUP), :] for i in range(SSM_GROUP)],
                          axis=1).astype(BF16)
    y = _dot(lhs, toep_ref[0])
    st = _dot(lhs, wst_ref[0])
    srow = lax.broadcasted_iota(jnp.int32, (nb, 2 * P), 0) % n_chunks
    kk = 0
    while (1 << kk) < n_chunks:
        d = 1 << kk
        pa = apow_ref[0, 2 * kk:2 * kk + 1, :]
        pb = apow_ref[0, 2 * kk + 1:2 * kk + 2, :]
        prev = pltpu.roll(st, d, axis=0)
        prev = prev * pa + pltpu.roll(prev, P, axis=1) * pb
        st = st + jnp.where(srow >= d, prev, 0.0)
        kk += 1
    h0 = jnp.where(srow >= 1, pltpu.roll(st, 1, axis=0), 0.0)
    y = y + _dot(h0.astype(BF16), wout_ref[0])
    for o in range(SSM_GROUP):
        yt_ref[o] = y[:, o * LANES:(o + 1) * LANES]


def _ssm_scan(ut, toep, wst, wout, apow, n_chunks):
    nb, ssm_w, _ = ut.shape
    G = SSM_GROUPS
    g3 = lambda g: (g, 0, 0)
    in_specs = [pl.BlockSpec((nb, SSM_GROUP, LANES), lambda g: (0, g, 0)),
                pl.BlockSpec((1,) + toep.shape[1:], g3), pl.BlockSpec((1,) + wst.shape[1:], g3),
                pl.BlockSpec((1,) + wout.shape[1:], g3), pl.BlockSpec((1,) + apow.shape[1:], g3)]
    return pl.pallas_call(
        functools.partial(_ssm_scan_kernel, n_chunks=n_chunks), name="ssm_scan", grid=(G,), in_specs=in_specs,
        out_specs=pl.BlockSpec((SSM_GROUP, nb, LANES), g3),
        out_shape=jax.ShapeDtypeStruct((ssm_w, nb, LANES), F32),
        compiler_params=_cparams())(ut, toep, wst, wout, apow)


def _mix_kernel(yt_ref, att_ref, gs_ref, x_ref, wa_ref, wb_ref, wo_ref, g_ref, b_ref, out_ref, zt_ref,
                *, alpha):
    n_ch, n_blk = yt_ref.shape[0], yt_ref.shape[1]
    yt2 = yt_ref.reshape(n_ch * n_blk, LANES)
    for j in range(n_blk):
        y = yt2[pl.ds(j, n_ch, stride=n_blk), :]
        z = jax.nn.gelu(y, approximate=True)
        zt_ref[j * LANES:(j + 1) * LANES, :] = z.T.astype(BF16)
    z = zt_ref[...]
    ssm_out = _dot(z, wa_ref[...]) * jax.nn.sigmoid(_dot(z, wb_ref[...]))
    mixed = att_ref[...].astype(F32) + gs_ref[...].astype(F32) * ssm_out
    y = alpha * x_ref[...] + _dot(mixed.astype(BF16), wo_ref[...])
    out_ref[...] = _layer_norm(y, g_ref[...], b_ref[...])


def _mix(yt, att, gs, x2d, wa, wb, wo, g, b, alpha, tm):
    T, D = x2d.shape
    ssm_w = yt.shape[0]
    row = lambda i: (i, 0)
    in_specs = [pl.BlockSpec((ssm_w, tm // LANES, LANES), lambda i: (0, i, 0)),
                pl.BlockSpec((tm, D), row), pl.BlockSpec((tm, D), row), pl.BlockSpec((tm, D), row),
                _full_spec(wa.shape), _full_spec(wb.shape), _full_spec(wo.shape),
                _full_spec(g.shape), _full_spec(b.shape)]
    return pl.pallas_call(functools.partial(_mix_kernel, alpha=alpha), name="mix", grid=(T // tm,), in_specs=in_specs,
                          out_specs=pl.BlockSpec((tm, D), row),
                          out_shape=jax.ShapeDtypeStruct((T, D), F32),
                          scratch_shapes=[pltpu.VMEM((tm, ssm_w), BF16)],
                          compiler_params=_cparams())(yt, att, gs, x2d, wa, wb, wo, g, b)


def _memkv_kernel(m_ref, wk_ref, wv_ref, k_ref, v_ref):
    mb = m_ref[...].astype(BF16)
    k_ref[...] = _dot(mb, wk_ref[...]).astype(BF16)
    v_ref[...] = _dot(mb, wv_ref[...]).astype(BF16)


def _memkv(mem2d, wk, wv, tm):
    R, D = mem2d.shape
    row = lambda i: (i, 0)
    return pl.pallas_call(_memkv_kernel, name="memkv", grid=(R // tm,),
                          in_specs=[pl.BlockSpec((tm, D), row), _full_spec(wk.shape), _full_spec(wv.shape)],
                          out_specs=(pl.BlockSpec((tm, D), row), pl.BlockSpec((tm, D), row)),
                          out_shape=(jax.ShapeDtypeStruct((R, D), BF16), jax.ShapeDtypeStruct((R, D), BF16)),
                          compiler_params=_cparams())(mem2d, wk, wv)


def _cross_kernel(x_ref, k_ref, v_ref, wq_ref, wo_ref, g_ref, b_ref, wr_ref, br_ref, out_ref, logit_ref,
                  cat_ref, *, alpha):
    x = x_ref[...]
    D = x.shape[1]
    hd = D // N_CROSS_HEADS
    q = (_dot(x.astype(BF16), wq_ref[...]) * (hd ** -0.5)).astype(BF16)
    for h in range(N_CROSS_HEADS):
        cols = slice(h * hd, (h + 1) * hd)
        s = _dot_nt(q[:, cols], k_ref[:, cols])
        m = jnp.max(s, axis=-1, keepdims=True)
        p = jnp.exp(s - m)
        w = (p / jnp.sum(p, axis=-1, keepdims=True)).astype(BF16)
        cat_ref[:, cols] = _dot(w, v_ref[:, cols]).astype(BF16)
    y = alpha * x + _dot(cat_ref[...], wo_ref[...])
    x2 = _layer_norm(y, g_ref[...], b_ref[...])
    out_ref[...] = x2
    logit_ref[...] = _dot(x2.astype(BF16), wr_ref[...]) + br_ref[...]


def _cross(x1, kc, vc, wq, wo, g, b, wr, br, alpha, S, tm):
    T, D = x1.shape
    n_mem = kc.shape[0] // (T // S)
    row = lambda i: (i, 0)
    per_b = S // tm
    in_specs = [pl.BlockSpec((tm, D), row),
                pl.BlockSpec((n_mem, D), lambda i: (i // per_b, 0)),
                pl.BlockSpec((n_mem, D), lambda i: (i // per_b, 0)),
                _full_spec(wq.shape), _full_spec(wo.shape), _full_spec(g.shape), _full_spec(b.shape),
                _full_spec(wr.shape), _full_spec(br.shape)]
    return pl.pallas_call(functools.partial(_cross_kernel, alpha=alpha), name="cross", grid=(T // tm,), in_specs=in_specs,
                          out_specs=(pl.BlockSpec((tm, D), row), pl.BlockSpec((tm, LANES), row)),
                          out_shape=(jax.ShapeDtypeStruct((T, D), F32), jax.ShapeDtypeStruct((T, LANES), F32)),
                          scratch_shapes=[pltpu.VMEM((tm, D), BF16)],
                          compiler_params=_cparams())(x1, kc, vc, wq, wo, g, b, wr, br)


SEG_ALIGN = 8


def _route_kernel(logit_ref, info_ref, seg_ref, count_ref, carry_ref):
    tm = logit_ref.shape[0]

    @pl.when(pl.program_id(0) == 0)
    def _():
        carry_ref[...] = jnp.zeros_like(carry_ref)

    vals = logit_ref[...]
    lane = lax.broadcasted_iota(jnp.int32, vals.shape, 1).astype(F32)
    sels, tops = [], []
    for _ in range(TOP_K):
        m = jnp.max(vals, axis=-1, keepdims=True)
        idx = jnp.min(jnp.where(vals == m, lane, float(LANES)), axis=-1, keepdims=True)
        sel = lane == idx
        vals = jnp.where(sel, -jnp.inf, vals)
        sels.append((sel, idx))
        tops.append(m)
    es = [jnp.exp(t - tops[0]) for t in tops]
    denom = es[0]
    for e in es[1:]:
        denom = denom + e
    onehot = jnp.zeros(vals.shape, F32)
    for sel, _ in sels:
        onehot = onehot + jnp.where(sel, 1.0, 0.0)
    r = lax.broadcasted_iota(jnp.int32, (tm, tm), 0)
    c = lax.broadcasted_iota(jnp.int32, (tm, tm), 1)
    tri = jnp.where(c < r, 1.0, 0.0).astype(BF16)
    cum = _dot(tri, onehot.astype(BF16))
    n_seg = jnp.floor((jnp.sum(onehot, axis=0, keepdims=True) + (SEG_ALIGN - 1.0)) * (1.0 / SEG_ALIGN)) * SEG_ALIGN
    a = lax.broadcasted_iota(jnp.int32, (LANES, LANES), 0)
    b = lax.broadcasted_iota(jnp.int32, (LANES, LANES), 1)
    upper = jnp.where(a < b, 1.0, 0.0).astype(BF16)
    seg_off = _dot(jnp.broadcast_to(n_seg, (8, LANES)).astype(BF16), upper)[0:1, :]
    carry = carry_ref[...]
    info = jnp.zeros(vals.shape, F32)
    for k, (sel, idx) in enumerate(sels):
        local = jnp.sum(jnp.where(sel, cum + seg_off, 0.0), axis=-1, keepdims=True)
        info = jnp.where(lane == k, idx, info)
        info = jnp.where(lane == TOP_K + k, local, info)
        info = jnp.where(lane == 2 * TOP_K + k, es[k] / denom, info)
    info_ref[...] = info
    row8 = lax.broadcasted_iota(jnp.int32, (8, LANES), 0)
    seg_ref[0] = jnp.where(row8 == 0, n_seg, jnp.where(row8 == 1, seg_off, jnp.where(row8 == 2, carry, 0.0)))
    carry_ref[...] = carry + n_seg
    count_ref[...] = carry_ref[...]


def _route(logits, tm):
    T = logits.shape[0]
    row = lambda i: (i, 0)
    return pl.pallas_call(_route_kernel, name="route", grid=(T // tm,),
                          in_specs=[pl.BlockSpec((tm, LANES), row)],
                          out_specs=(pl.BlockSpec((tm, LANES), row), pl.BlockSpec((1, 8, LANES), lambda i: (i, 0, 0)),
                                     _full_spec((1, LANES))),
                          out_shape=(jax.ShapeDtypeStruct((T, LANES), F32),
                                     jax.ShapeDtypeStruct((T // tm, 8, LANES), F32),
                                     jax.ShapeDtypeStruct((1, LANES), F32)),
                          scratch_shapes=[pltpu.VMEM((1, LANES), F32)],
                          compiler_params=_cparams())(logits)


GU_TILE = 2 * LANES


SEG_SIZES = (256, 128, 64, 32, 16, 8)


def _segment_copies(seg_ref, local_ref, global_ref, sem, to_global):
    for e in range(N_EXPERTS):
        n = seg_ref[0, 0, e]
        off = seg_ref[0, 1, e]
        start = seg_ref[0, 2, e]
        for sz in SEG_SIZES:
            take = (n & sz) != 0

            @pl.when(take)
            def _(off=off, start=start, sz=sz):
                loc = local_ref.at[pl.ds(pl.multiple_of(off, SEG_ALIGN), sz)]
                glo = global_ref.at[pl.ds(pl.multiple_of(start, SEG_ALIGN), sz)]
                if to_global:
                    pltpu.make_async_copy(loc, glo, sem).start()
                else:
                    pltpu.make_async_copy(glo, loc, sem).start()

            step = jnp.where(take, sz, 0)
            off = off + step
            start = start + step


def _wait_rows(local_ref, global_ref, sem, n_rows):
    @pl.when(n_rows > 0)
    def _():
        n = pl.multiple_of(n_rows, SEG_ALIGN)
        pltpu.make_async_copy(global_ref.at[pl.ds(0, n)], local_ref.at[pl.ds(0, n)], sem).wait()


def _dispatch_kernel(ends_ref, seg_ref, x_ref, info_ref, xs_hbm, comp, tot, sem):
    i = pl.program_id(0)
    n_steps = pl.num_programs(0)
    slot = i % 2
    tm = x_ref.shape[0]
    cap = comp.shape[1]

    @pl.when(i == 0)
    def _():
        comp[1, 0:MOE_BLOCK, :] = jnp.zeros((MOE_BLOCK, comp.shape[2]), F32)
        for e in range(N_EXPERTS):
            end = ends_ref[e]
            prev = ends_ref[e - 1] if e > 0 else 0

            @pl.when(end > prev)
            def _(end=end):
                cp = pltpu.make_async_copy(comp.at[1, pl.ds(0, MOE_BLOCK)],
                                           xs_hbm.at[pl.ds(pl.multiple_of(end - MOE_BLOCK, MOE_BLOCK), MOE_BLOCK)],
                                           sem.at[1])
                cp.start()
                cp.wait()

        def tail(b):
            return pltpu.make_async_copy(comp.at[1, pl.ds(0, MOE_BLOCK)],
                                         xs_hbm.at[pl.ds(pl.multiple_of(b * MOE_BLOCK, MOE_BLOCK), MOE_BLOCK)],
                                         sem.at[0])
        first_unused = ends_ref[N_EXPERTS - 1] // MOE_BLOCK
        n_all = xs_hbm.shape[0] // MOE_BLOCK
        lax.fori_loop(first_unused, n_all, lambda b, c: (tail(b).start(), c)[1], 0)
        lax.fori_loop(first_unused, n_all, lambda b, c: (tail(b).wait(), c)[1], 0)
        tot[0] = 0
        tot[1] = 0

    _wait_rows(comp.at[slot], xs_hbm, sem.at[slot], tot[slot])
    pos = info_ref[...].T
    r = lax.broadcasted_iota(jnp.int32, (cap, tm), 0).astype(F32)
    hit = r == pos[TOP_K:TOP_K + 1, :]
    for k in range(1, TOP_K):
        hit = hit | (r == pos[TOP_K + k:TOP_K + k + 1, :])
    sel = jnp.where(hit, 1.0, 0.0).astype(BF16)
    comp[slot] = _dot(sel, x_ref[...].astype(BF16))
    _segment_copies(seg_ref, comp.at[slot], xs_hbm, sem.at[slot], to_global=True)
    tot[slot] = seg_ref[0, 3, 0]

    @pl.when(i == n_steps - 1)
    def _():
        _wait_rows(comp.at[slot], xs_hbm, sem.at[slot], tot[slot])
        _wait_rows(comp.at[1 - slot], xs_hbm, sem.at[1 - slot], tot[1 - slot])


def _dispatch(ends_pad, seg, x2, info, n_slots, tm):
    T, D = x2.shape
    row = lambda i, ends: (i, 0)
    grid_spec = pltpu.PrefetchScalarGridSpec(
        num_scalar_prefetch=1, grid=(T // tm,),
        in_specs=[pl.BlockSpec((1, 4, LANES), lambda i, ends: (i, 0, 0), memory_space=pltpu.SMEM),
                  pl.BlockSpec((tm, D), row), pl.BlockSpec((tm, LANES), row)],
        out_specs=pl.BlockSpec(memory_space=pl.ANY),
        scratch_shapes=[pltpu.VMEM((2, TOP_K * tm + N_EXPERTS * SEG_ALIGN, D), F32), pltpu.SMEM((2,), jnp.int32),
                        pltpu.SemaphoreType.DMA((2,))])
    return pl.pallas_call(_dispatch_kernel, name="dispatch", grid_spec=grid_spec,
                          out_shape=jax.ShapeDtypeStruct((n_slots, D), F32),
                          compiler_params=_cparams())(ends_pad, seg, x2, info)


def _moe_kernel(bexp_ref, nused_ref, x_ref, wgu_ref, bgu_ref, wd_ref, bd_ref, y_ref, wgu_s, wd_s):
    j = pl.program_id(0)
    n_used = nused_ref[0]
    n_tiles = wgu_s.shape[1] // GU_TILE

    @pl.when(j >= n_used)
    def _():
        y_ref[...] = jnp.zeros_like(y_ref)

    @pl.when(j < n_used)
    def _():
        @pl.when((j == 0) | (bexp_ref[j] != bexp_ref[jnp.maximum(j - 1, 0)]))
        def _():
            k = lax.broadcasted_iota(jnp.int32, (GU_TILE, GU_TILE), 0)
            n = lax.broadcasted_iota(jnp.int32, (GU_TILE, GU_TILE), 1)
            perm = jnp.where(k == jnp.where(n < LANES, 2 * n, 2 * (n - LANES) + 1), 1.0, 0.0).astype(BF16)
            for t in range(n_tiles):
                cols = slice(t * GU_TILE, (t + 1) * GU_TILE)
                wgu_s[:, cols] = _dot(wgu_ref[0, :, cols].astype(BF16), perm).astype(BF16)
            wd_s[...] = wd_ref[0].astype(BF16)

        xb = x_ref[...].astype(BF16)
        hs = []
        for t in range(n_tiles):
            cols = slice(t * GU_TILE, (t + 1) * GU_TILE)
            gu = _dot(xb, wgu_s[:, cols]) + bgu_ref[0, :, cols]
            gate = jnp.minimum(gu[:, :LANES], SWIGLU_LIMIT)
            lin = jnp.clip(gu[:, LANES:], -SWIGLU_LIMIT, SWIGLU_LIMIT)
            hs.append((gate * jax.nn.sigmoid(SWIGLU_ALPHA * gate) * (lin + 1.0)).astype(BF16))
        y_ref[...] = _dot(jnp.concatenate(hs, axis=1), wd_s[...]) + bd_ref[0]


def _moe(bexp, n_used, xs, wgu, bgu, wd, bd):
    n_blocks = bexp.shape[0]
    D = xs.shape[1]
    F2 = wgu.shape[2]
    F = wd.shape[1]
    wmap = lambda j, be, nu: (be[jnp.minimum(j, nu[0] - 1)], 0, 0)
    blk = lambda j, be, nu: (j, 0)
    used = lambda j, be, nu: (jnp.minimum(j, nu[0] - 1), 0)
    in_specs = [pl.BlockSpec((MOE_BLOCK, D), used),
                pl.BlockSpec((1, D, F2), wmap), pl.BlockSpec((1, 1, F2), wmap),
                pl.BlockSpec((1, F, D), wmap), pl.BlockSpec((1, 1, D), wmap)]
    grid_spec = pltpu.PrefetchScalarGridSpec(
        num_scalar_prefetch=2, grid=(n_blocks,), in_specs=in_specs,
        out_specs=pl.BlockSpec((MOE_BLOCK, D), blk),
        scratch_shapes=[pltpu.VMEM((D, F2), BF16), pltpu.VMEM((F, D), BF16)])
    return pl.pallas_call(_moe_kernel, name="moe", grid_spec=grid_spec,
                          out_shape=jax.ShapeDtypeStruct((n_blocks * MOE_BLOCK, D), F32),
                          compiler_params=_cparams())(bexp, n_used, xs, wgu, bgu, wd, bd)


def _combine_kernel(seg_ref, seg_nxt_ref, ys_hbm, info_ref, x_ref, g_ref, b_ref, out_ref, comp, sem, *, alpha):
    i = pl.program_id(0)
    n_steps = pl.num_programs(0)
    slot = i % 2
    tm = x_ref.shape[0]
    cap = comp.shape[1]

    @pl.when(i == 0)
    def _():
        comp[...] = jnp.zeros_like(comp)
        _segment_copies(seg_ref, comp.at[0], ys_hbm, sem.at[0], to_global=False)

    @pl.when(i + 1 < n_steps)
    def _():
        _segment_copies(seg_nxt_ref, comp.at[1 - slot], ys_hbm, sem.at[1 - slot], to_global=False)

    _wait_rows(comp.at[slot], ys_hbm, sem.at[slot], seg_ref[0, 3, 0])
    info = info_ref[...]
    lane = lax.broadcasted_iota(jnp.int32, (tm, cap), 1).astype(F32)
    wmat = jnp.zeros((tm, cap), F32)
    for k in range(TOP_K):
        wmat = jnp.where(lane == info[:, TOP_K + k:TOP_K + k + 1], info[:, 2 * TOP_K + k:2 * TOP_K + k + 1], wmat)
    ffn = _dot(wmat.astype(BF16), comp[slot].astype(BF16))
    out_ref[...] = _layer_norm(alpha * x_ref[...] + ffn, g_ref[...], b_ref[...])


def _combine(seg, ys, info, x2, g, b, alpha, tm):
    T, D = x2.shape
    n_t = T // tm
    row = lambda i: (i, 0)
    seg_spec = lambda m: pl.BlockSpec((1, 4, LANES), m, memory_space=pltpu.SMEM)
    in_specs = [seg_spec(lambda i: (i, 0, 0)), seg_spec(lambda i: (jnp.minimum(i + 1, n_t - 1), 0, 0)),
                pl.BlockSpec(memory_space=pl.ANY),
                pl.BlockSpec((tm, LANES), row), pl.BlockSpec((tm, D), row),
                _full_spec(g.shape), _full_spec(b.shape)]
    cap = TOP_K * tm + N_EXPERTS * SEG_ALIGN
    return pl.pallas_call(functools.partial(_combine_kernel, alpha=alpha), name="combine", grid=(n_t,), in_specs=in_specs,
                          out_specs=pl.BlockSpec((tm, D), row),
                          out_shape=jax.ShapeDtypeStruct((T, D), F32),
                          scratch_shapes=[pltpu.VMEM((2, cap, D), F32), pltpu.SemaphoreType.DMA((2,))],
                          compiler_params=_cparams())(seg, seg, ys, info, x2, g, b)


def _tile(n, pref):
    return pref if n % pref == 0 else n


def _layer(x2d, mem2d, pos2d, B, S, depth, w_in, sinks, w_attn_o, lam_re, lam_im, log_dt, b_re, b_im,
           c_re, c_im, d_skip, w_glu_a, w_glu_b, w_out, ln1_g, ln1_b, wq_c, wk_c, wv_c, wo_c, ln2_g,
           ln2_b, w_router, b_router, w_gate_up, b_gate_up, w_down, b_down, ln3_g, ln3_b):
    T, D = x2d.shape
    alpha = (2 * depth) ** 0.25
    rep = N_Q_HEADS // N_KV_HEADS
    q_w = N_Q_HEADS * HEAD_DIM
    kv_w = N_KV_HEADS * HEAD_DIM
    ssm_w = SSM_GROUP * SSM_GROUPS
    P = SSM_STATE

    o_k, o_v, o_s = q_w, q_w + kv_w, q_w + 2 * kv_w
    o_ga, o_gs = o_s + ssm_w, o_s + ssm_w + D
    wq = w_in[:, :o_k].reshape(D, N_KV_HEADS, rep, HEAD_DIM).transpose(0, 2, 1, 3).reshape(D, q_w).astype(BF16)
    wk = w_in[:, o_k:o_v].astype(BF16)
    wv = w_in[:, o_v:o_s].astype(BF16)
    wut = w_in[:, o_s:o_ga].T.astype(BF16)
    wga = w_in[:, o_ga:o_gs].astype(BF16)
    wgs = w_in[:, o_gs:].astype(BF16)
    wo_attn = w_attn_o.reshape(N_KV_HEADS, rep, HEAD_DIM, D).transpose(1, 0, 2, 3).reshape(q_w, D).astype(BF16)
    half = HEAD_DIM // 2
    inv_freq = jnp.power(ROPE_THETA, -jnp.arange(half, dtype=F32) / half)
    invf = jnp.tile(inv_freq, LANES // half)[None, :]
    sink_rows = jnp.repeat(sinks.astype(F32).reshape(N_KV_HEADS, rep), WINDOW, axis=1)
    sink_cols = jnp.full((N_KV_HEADS, rep * WINDOW, 2 * WINDOW), NEG_BIG, F32).at[:, :, 0].set(sink_rows)

    tm1 = _tile(T, 512)
    q2, k, v, ut, ga, gs = _inproj(x2d, pos2d, invf, wq, wk, wv, wut, wga, wgs, tm1)
    att = _swa(q2, k, v, sink_cols, wo_attn, ga, B, S, _tile(S, 512))

    lam_row = jnp.stack([jnp.concatenate([lam_re, lam_re], -1), jnp.concatenate([lam_im, lam_im], -1)], 1)
    lam_col = jnp.swapaxes(lam_row, 1, 2)
    bt_re = jnp.swapaxes(b_re, 1, 2)
    bt_im = jnp.swapaxes(b_im, 1, 2)
    bt_a = jnp.concatenate([bt_re, bt_im], -1)
    bt_b = jnp.concatenate([-bt_im, bt_re], -1)
    ct_re = jnp.swapaxes(c_re, 1, 2)
    ct_im = jnp.swapaxes(c_im, 1, 2)
    ct_a = jnp.concatenate([ct_re, -ct_im], 1)
    ct_b = jnp.concatenate([-ct_im, -ct_re], 1)
    dsk = jnp.repeat(d_skip.reshape(SSM_GROUPS, SSM_GROUP), SSM_GROUP, axis=1)[:, :, None]
    toep, wst, wout, apow = _ssm_prep(lam_row, lam_col, log_dt.reshape(SSM_GROUPS, 1, 1), bt_a, bt_b,
                                      c_re, c_im, ct_a, ct_b, dsk)
    yt = _ssm_scan(ut, toep, wst, wout, apow, S // LANES)

    x1 = _mix(yt, att, gs, x2d, w_glu_a.astype(BF16), w_glu_b.astype(BF16), w_out.astype(BF16),
              ln1_g[None, :], ln1_b[None, :], alpha, _tile(T, 1024))

    kc, vc = _memkv(mem2d, wk_c.astype(BF16), wv_c.astype(BF16), _tile(mem2d.shape[0], 512))
    wr = jnp.zeros((D, LANES), F32).at[:, :N_EXPERTS].set(w_router).astype(BF16)
    br = jnp.full((1, LANES), NEG_BIG, F32).at[0, :N_EXPERTS].set(b_router)
    x2, logits = _cross(x1, kc, vc, wq_c.astype(BF16), wo_c.astype(BF16), ln2_g[None, :], ln2_b[None, :],
                        wr, br, alpha, S, _tile(S, 512))

    tm_r = _tile(T, ROUTE_TILE)
    n_tiles = T // tm_r
    info, segf, counts = _route(logits, tm_r)
    total = counts[0].astype(jnp.int32)
    padded = (total + MOE_BLOCK - 1) // MOE_BLOCK * MOE_BLOCK
    ends_lane = jnp.cumsum(padded)
    start_lane = ends_lane - padded
    ends_pad = ends_lane[:N_EXPERTS]
    n_blocks = -(-(T * TOP_K + n_tiles * N_EXPERTS * (SEG_ALIGN - 1)) // MOE_BLOCK) + N_EXPERTS
    segi = segf.astype(jnp.int32)
    seg_n, seg_off, seg_start = segi[:, 0, :], segi[:, 1, :], segi[:, 2, :] + start_lane[None, :]
    seg_tot = jnp.broadcast_to(jnp.sum(seg_n, axis=1, keepdims=True), seg_n.shape)
    seg = jnp.stack([seg_n, seg_off, seg_start, seg_tot], axis=1)
    block_start = jnp.arange(n_blocks, dtype=jnp.int32) * MOE_BLOCK
    bexp = jnp.minimum(jnp.sum(block_start[:, None] >= ends_pad[None, :], axis=1), N_EXPERTS - 1).astype(jnp.int32)
    n_used = (ends_pad[-1] // MOE_BLOCK).astype(jnp.int32)[None]
    n_gu_tiles = b_gate_up.shape[1] // GU_TILE
    bgu = b_gate_up.reshape(N_EXPERTS, n_gu_tiles, LANES, 2).transpose(0, 1, 3, 2).reshape(N_EXPERTS, 1, -1)
    xs = _dispatch(ends_pad, seg, x2, info, n_blocks * MOE_BLOCK, tm_r)
    ys = _moe(bexp, n_used, xs, w_gate_up, bgu, w_down, b_down[:, None, :])
    return _combine(seg, ys, info, x2, ln3_g[None, :], ln3_b[None, :], alpha, tm_r)


def kernel(x, mem, positions, w_in, sinks, w_attn_o, lam_re, lam_im, log_dt, b_re, b_im, c_re, c_im, d_skip,
           w_glu_a, w_glu_b, w_out, ln1_g, ln1_b, wq_c, wk_c, wv_c, wo_c, ln2_g, ln2_b, w_router, b_router,
           w_gate_up, b_gate_up, w_down, b_down, ln3_g, ln3_b):
    B, S, D = x.shape
    depth = w_in.shape[0]
    x2d = x.reshape(B * S, D)
    mem2d = mem.reshape(-1, D)
    pos2d = positions.reshape(B * S, 1)
    per_layer = (w_in, sinks, w_attn_o, lam_re, lam_im, log_dt, b_re, b_im, c_re, c_im, d_skip, w_glu_a,
                 w_glu_b, w_out, ln1_g, ln1_b, wq_c, wk_c, wv_c, wo_c, ln2_g, ln2_b, w_router, b_router,
                 w_gate_up, b_gate_up, w_down, b_down, ln3_g, ln3_b)
    for l in range(depth):
        x2d = _layer(x2d, mem2d, pos2d, B, S, depth, *(w[l] for w in per_layer))
    return x2d.reshape(B, S, D)
```

```python
import functools

import jax
import jax.numpy as jnp
from jax import lax
from jax.experimental import pallas as pl
from jax.experimental.pallas import tpu as pltpu

N_Q_HEADS = 16
N_KV_HEADS = 2
HEAD_DIM = 64
WINDOW = 128
ROPE_THETA = 10000.0
SSM_GROUP = 16
SSM_GROUPS = 32
SSM_STATE = 64
N_CROSS_HEADS = 4
N_EXPERTS = 32
TOP_K = 4
SWIGLU_ALPHA = 1.702
SWIGLU_LIMIT = 7.0
MOE_BLOCK = 512
ROUTE_TILE = 256
LN_EPS = 1e-5

LANES = 128
VMEM_LIMIT_BYTES = 56 * 1024 * 1024

NEG_BIG = -1e30
BF16 = jnp.bfloat16
F32 = jnp.float32
HI = lax.Precision.HIGHEST


def _cparams(n_axes=1):
    return pltpu.CompilerParams(dimension_semantics=("arbitrary",) * n_axes,
                                vmem_limit_bytes=VMEM_LIMIT_BYTES)


def _full_spec(shape):
    n = len(shape)
    return pl.BlockSpec(shape, lambda *_: (0,) * n, pipeline_mode=pl.Buffered(1))


def _dot(a, b):
    return jnp.dot(a, b, preferred_element_type=F32)


def _dot_nt(a, b):
    return lax.dot_general(a, b, (((1,), (1,)), ((), ())), preferred_element_type=F32)


def _layer_norm(y, g, b):
    mu = jnp.mean(y, axis=-1, keepdims=True)
    d = y - mu
    var = jnp.mean(d * d, axis=-1, keepdims=True)
    return d * lax.rsqrt(var + LN_EPS) * g + b


def _rope(t, cos, sin_signed, first_half):
    half = HEAD_DIM // 2
    partner = jnp.where(first_half, pltpu.roll(t, LANES - half, axis=1), pltpu.roll(t, half, axis=1))
    return t * cos + partner * sin_signed


def _inproj_kernel(x_ref, pos_ref, invf_ref, wq_ref, wk_ref, wv_ref, wu_ref, wga_ref, wgs_ref,
                   q2_ref, k_ref, v_ref, ut_ref, ga_ref, gs_ref):
    tm = x_ref.shape[0]
    xb = x_ref[...].astype(BF16)
    ang = pos_ref[...].astype(F32) * invf_ref[...]
    cos = jnp.cos(ang)
    sin = jnp.sin(ang)
    first_half = (lax.broadcasted_iota(jnp.int32, (tm, LANES), 1) % HEAD_DIM) < (HEAD_DIM // 2)
    first_half_w = (lax.broadcasted_iota(jnp.int32, (WINDOW, LANES), 1) % HEAD_DIM) < (HEAD_DIM // 2)
    sin_signed = jnp.where(first_half, -sin, sin)

    q = _dot(xb, wq_ref[...])
    n_rep = q.shape[1] // LANES
    scale = HEAD_DIM ** -0.5
    for j in range(tm // WINDOW):
        rows = slice(j * WINDOW, (j + 1) * WINDOW)
        for r in range(n_rep):
            t = _rope(q[rows, r * LANES:(r + 1) * LANES], cos[rows], sin_signed[rows], first_half_w)
            base = (j * n_rep + r) * WINDOW
            q2_ref[base:base + WINDOW, :] = (t * scale).astype(BF16)
    k_ref[...] = _rope(_dot(xb, wk_ref[...]), cos, sin_signed, first_half).astype(BF16)
    v_ref[...] = _dot(xb, wv_ref[...]).astype(BF16)
    u = _dot(xb, wu_ref[...])
    for j in range(tm // LANES):
        ut_ref[j] = u[j * LANES:(j + 1) * LANES, :].T
    ga_ref[...] = jax.nn.sigmoid(_dot(xb, wga_ref[...])).astype(BF16)
    gs_ref[...] = jax.nn.sigmoid(_dot(xb, wgs_ref[...])).astype(BF16)


def _inproj(x2d, pos2d, invf, wq, wk, wv, wu, wga, wgs, tm):
    T, D = x2d.shape
    n_rep = wq.shape[1] // LANES
    ssm_w = wu.shape[1]
    row = lambda i: (i, 0)
    out_shape = (
        jax.ShapeDtypeStruct((T * n_rep, LANES), BF16),
        jax.ShapeDtypeStruct((T, LANES), BF16),
        jax.ShapeDtypeStruct((T, LANES), BF16),
        jax.ShapeDtypeStruct((T // LANES, ssm_w, LANES), F32),
        jax.ShapeDtypeStruct((T, D), BF16),
        jax.ShapeDtypeStruct((T, D), BF16),
    )
    in_specs = [pl.BlockSpec((tm, D), row), pl.BlockSpec((tm, 1), row), _full_spec(invf.shape),
                _full_spec(wq.shape), _full_spec(wk.shape), _full_spec(wv.shape),
                _full_spec(wu.shape), _full_spec(wga.shape), _full_spec(wgs.shape)]
    out_specs = (pl.BlockSpec((tm * n_rep, LANES), row), pl.BlockSpec((tm, LANES), row),
                 pl.BlockSpec((tm, LANES), row),
                 pl.BlockSpec((tm // LANES, ssm_w, LANES), lambda i: (i, 0, 0)),
                 pl.BlockSpec((tm, D), row), pl.BlockSpec((tm, D), row))
    return pl.pallas_call(_inproj_kernel, name="inproj", grid=(T // tm,), in_specs=in_specs, out_specs=out_specs,
                          out_shape=out_shape, compiler_params=_cparams())(
        x2d, pos2d, invf, wq, wk, wv, wu, wga, wgs)


def _swa_kernel(q2_ref, kc_ref, kp_ref, vc_ref, vp_ref, fill_ref, wo_ref, ga_ref, out_ref, cat_ref):
    i = pl.program_id(1)
    tq = kc_ref.shape[0]
    n_sub = tq // WINDOW
    rep = N_Q_HEADS // N_KV_HEADS
    rows_all = rep * WINDOW
    kfull = jnp.concatenate([kp_ref[...], kc_ref[...]], axis=0)
    vfull = jnp.concatenate([vp_ref[...], vc_ref[...]], axis=0)
    lane = lax.broadcasted_iota(jnp.int32, (2 * WINDOW, LANES), 1)
    qi = lax.broadcasted_iota(jnp.int32, (rows_all, 2 * WINDOW), 0) % WINDOW
    ci = lax.broadcasted_iota(jnp.int32, (rows_all, 2 * WINDOW), 1)
    local = (ci > qi) & (ci <= qi + WINDOW)
    out_lane = lax.broadcasted_iota(jnp.int32, (rows_all, LANES), 1)
    band_row = lax.broadcasted_iota(jnp.int32, (2 * WINDOW, LANES), 0)
    for j in range(n_sub):
        qs = q2_ref[j * rows_all:(j + 1) * rows_all, :]
        kb = kfull[j * WINDOW:(j + 2) * WINDOW, :]
        vb = vfull[j * WINDOW:(j + 2) * WINDOW, :]
        vb = jnp.where(band_row == 0, jnp.zeros_like(vb), vb)
        mask = local
        if j == 0:
            mask = mask & ((ci >= WINDOW) | (i > 0))
        o = None
        for g in range(N_KV_HEADS):
            in_group = (lane >= g * HEAD_DIM) & (lane < (g + 1) * HEAD_DIM)
            kg = jnp.where(in_group, kb, jnp.zeros_like(kb))
            s = _dot_nt(qs, kg)
            s = jnp.where(mask, s, fill_ref[g])
            m = jnp.max(s, axis=-1, keepdims=True)
            p = jnp.exp(s - m)
            denom = jnp.sum(p, axis=-1, keepdims=True)
            og = _dot(p.astype(BF16), vb) * (1.0 / denom)
            o = og if o is None else jnp.where(out_lane < g * HEAD_DIM, o, og)
        ob = o.astype(BF16)
        for r in range(rep):
            cat_ref[j * WINDOW:(j + 1) * WINDOW, r * LANES:(r + 1) * LANES] = ob[r * WINDOW:(r + 1) * WINDOW, :]
    attn = _dot(cat_ref[...], wo_ref[...])
    out_ref[...] = (attn * ga_ref[...].astype(F32)).astype(BF16)


def _swa(q2, k, v, sink_cols, wo, ga, B, S, tq):
    T, D = ga.shape
    rep = N_Q_HEADS // N_KV_HEADS
    n_i = S // tq
    n_sub = tq // WINDOW
    in_specs = [
        pl.BlockSpec((tq * rep, LANES), lambda b, i: (b * n_i + i, 0)),
        pl.BlockSpec((tq, LANES), lambda b, i: (b * n_i + i, 0)),
        pl.BlockSpec((WINDOW, LANES), lambda b, i: (b * (S // WINDOW) + jnp.maximum(i * n_sub - 1, 0), 0)),
        pl.BlockSpec((tq, LANES), lambda b, i: (b * n_i + i, 0)),
        pl.BlockSpec((WINDOW, LANES), lambda b, i: (b * (S // WINDOW) + jnp.maximum(i * n_sub - 1, 0), 0)),
        _full_spec(sink_cols.shape), _full_spec(wo.shape),
        pl.BlockSpec((tq, D), lambda b, i: (b * n_i + i, 0)),
    ]
    return pl.pallas_call(
        _swa_kernel, name="swa", grid=(B, n_i), in_specs=in_specs,
        out_specs=pl.BlockSpec((tq, D), lambda b, i: (b * n_i + i, 0)),
        out_shape=jax.ShapeDtypeStruct((T, D), BF16),
        scratch_shapes=[pltpu.VMEM((tq, D), BF16)],
        compiler_params=_cparams(2))(q2, k, k, v, v, sink_cols, wo, ga)


def _ssm_prep_kernel(lam_row_ref, lam_col_ref, dt_ref, bt_a_ref, bt_b_ref, c_re_ref, c_im_ref,
                     ct_a_ref, ct_b_ref, dsk_ref, toep_ref, wst_ref, wout_ref, apow_ref):
    L = LANES
    P = SSM_STATE
    dt = jnp.exp(dt_ref[0])
    lr2 = lam_row_ref[0, 0:1, :]
    li2 = lam_row_ref[0, 1:2, :]
    mag = jnp.exp(lr2 * dt)
    ar = mag * jnp.cos(li2 * dt)
    ai = mag * jnp.sin(li2 * dt)
    den = lr2 * lr2 + li2 * li2
    f_re = ((ar - 1.0) * lr2 + ai * li2) / den
    f_im = (ai * lr2 - (ar - 1.0) * li2) / den
    bt_a = bt_a_ref[0]
    bt_b = bt_b_ref[0]
    bb_a = f_re * bt_a + f_im * bt_b
    bb_b = f_re * bt_b - f_im * bt_a

    tau_rev = (L - 1 - lax.broadcasted_iota(jnp.int32, (L, 2 * P), 0)).astype(F32)
    g_mag = jnp.exp(lr2 * dt * tau_rev)
    g_re = g_mag * jnp.cos(li2 * dt * tau_rev)
    g_im = g_mag * jnp.sin(li2 * dt * tau_rev)
    for i in range(SSM_GROUP):
        wst_ref[0, i * L:(i + 1) * L, :] = (g_re * bb_a[i:i + 1, :] + g_im * bb_b[i:i + 1, :]).astype(BF16)

    lane2 = lax.broadcasted_iota(jnp.int32, (1, 2 * P), 1)
    for kk in range(4):
        n = float(L * (1 << kk))
        pm = jnp.exp(lr2 * dt * n)
        p_re = pm * jnp.cos(li2 * dt * n)
        p_im = pm * jnp.sin(li2 * dt * n)
        apow_ref[0, 2 * kk:2 * kk + 1, :] = p_re
        apow_ref[0, 2 * kk + 1:2 * kk + 2, :] = jnp.where(lane2 < P, -p_im, p_im)

    lrc = lam_col_ref[0, :, 0:1]
    lic = lam_col_ref[0, :, 1:2]
    tau = lax.broadcasted_iota(jnp.int32, (2 * P, L), 1).astype(F32)
    e0_mag = jnp.exp(lrc * dt * tau)
    e0_re = e0_mag * jnp.cos(lic * dt * tau)
    e0_im = e0_mag * jnp.sin(lic * dt * tau)
    e1_mag = jnp.exp(lrc * dt * (tau + 1.0))
    e1_re = e1_mag * jnp.cos(lic * dt * (tau + 1.0))
    e1_im = e1_mag * jnp.sin(lic * dt * (tau + 1.0))

    ct_a = ct_a_ref[0]
    ct_b = ct_b_ref[0]
    for o in range(SSM_GROUP):
        wout_ref[0, :, o * L:(o + 1) * L] = (ct_a[:, o:o + 1] * e1_re + ct_b[:, o:o + 1] * e1_im).astype(BF16)

    c_re = c_re_ref[0]
    c_im = c_im_ref[0]
    bb_re = bb_a[:, :P]
    bb_im = bb_a[:, P:]
    m_re = (c_re[:, None, :] * bb_re[None, :, :] - c_im[:, None, :] * bb_im[None, :, :])
    m_im = (c_re[:, None, :] * bb_im[None, :, :] + c_im[:, None, :] * bb_re[None, :, :])
    m_re = m_re.reshape(SSM_GROUP * SSM_GROUP, P)
    m_im = m_im.reshape(SSM_GROUP * SSM_GROUP, P)
    kt = (jnp.dot(m_re, e0_re[:P, :], precision=HI, preferred_element_type=F32)
          - jnp.dot(m_im, e0_im[:P, :], precision=HI, preferred_element_type=F32))
    rowi = lax.broadcasted_iota(jnp.int32, (SSM_GROUP * SSM_GROUP, L), 0)
    coli = lax.broadcasted_iota(jnp.int32, (SSM_GROUP * SSM_GROUP, L), 1)
    dsk = dsk_ref[0]
    kt = kt + jnp.where((coli == 0) & ((rowi // SSM_GROUP) == (rowi % SSM_GROUP)), dsk, 0.0)

    cc = lax.broadcasted_iota(jnp.int32, (L, L), 0)
    cp = lax.broadcasted_iota(jnp.int32, (L, L), 1)
    causal = cp >= cc
    for o in range(SSM_GROUP):
        for i in range(SSM_GROUP):
            kv = kt[o * SSM_GROUP + i:o * SSM_GROUP + i + 1, :]
            blk = pltpu.roll(jnp.broadcast_to(kv, (L, L)), 0, axis=1, stride=1, stride_axis=0)
            blk = jnp.where(causal, blk, 0.0)
            toep_ref[0, i * L:(i + 1) * L, o * L:(o + 1) * L] = blk.astype(BF16)


def _ssm_prep(lam_row, lam_col, log_dt, bt_a, bt_b, c_re, c_im, ct_a, ct_b, dsk):
    G = SSM_GROUPS
    KW = SSM_GROUP * LANES
    g3 = lambda g: (g, 0, 0)
    ins = [lam_row, lam_col, log_dt, bt_a, bt_b, c_re, c_im, ct_a, ct_b, dsk]
    in_specs = [pl.BlockSpec((1,) + a.shape[1:], g3) for a in ins]
    out_shape = (jax.ShapeDtypeStruct((G, KW, KW), BF16),
                 jax.ShapeDtypeStruct((G, KW, 2 * SSM_STATE), BF16),
                 jax.ShapeDtypeStruct((G, 2 * SSM_STATE, KW), BF16),
                 jax.ShapeDtypeStruct((G, 8, 2 * SSM_STATE), F32))
    out_specs = tuple(pl.BlockSpec((1,) + s.shape[1:], g3) for s in out_shape)
    return pl.pallas_call(_ssm_prep_kernel, name="ssm_prep", grid=(G,), in_specs=in_specs, out_specs=out_specs,
                          out_shape=out_shape, compiler_params=_cparams())(*ins)


def _ssm_scan_kernel(ut_ref, toep_ref, wst_ref, wout_ref, apow_ref, yt_ref, *, n_chunks):
    nb = ut_ref.shape[0]
    P = SSM_STATE
    ut2 = ut_ref.reshape(nb * SSM_GROUP, LANES)
    lhs = jnp.concatenate([ut2[pl.ds(i, nb, stride=SSM_GROUP), :] for i in range(SSM_GROUP)],
                          axis=1).astype(BF16)
    y = _dot(lhs, toep_ref[0])
    st = _dot(lhs, wst_ref[0])
    srow = lax.broadcasted_iota(jnp.int32, (nb, 2 * P), 0) % n_chunks
    kk = 0
    while (1 << kk) < n_chunks:
        d = 1 << kk
        pa = apow_ref[0, 2 * kk:2 * kk + 1, :]
        pb = apow_ref[0, 2 * kk + 1:2 * kk + 2, :]
        prev = pltpu.roll(st, d, axis=0)
        prev = prev * pa + pltpu.roll(prev, P, axis=1) * pb
        st = st + jnp.where(srow >= d, prev, 0.0)
        kk += 1
    h0 = jnp.where(srow >= 1, pltpu.roll(st, 1, axis=0), 0.0)
    y = y + _dot(h0.astype(BF16), wout_ref[0])
    for o in range(SSM_GROUP):
        yt_ref[o] = y[:, o * LANES:(o + 1) * LANES]


def _ssm_scan(ut, toep, wst, wout, apow, n_chunks):
    nb, ssm_w, _ = ut.shape
    G = SSM_GROUPS
    g3 = lambda g: (g, 0, 0)
    in_specs = [pl.BlockSpec((nb, SSM_GROUP, LANES), lambda g: (0, g, 0)),
                pl.BlockSpec((1,) + toep.shape[1:], g3), pl.BlockSpec((1,) + wst.shape[1:], g3),
                pl.BlockSpec((1,) + wout.shape[1:], g3), pl.BlockSpec((1,) + apow.shape[1:], g3)]
    return pl.pallas_call(
        functools.partial(_ssm_scan_kernel, n_chunks=n_chunks), name="ssm_scan", grid=(G,), in_specs=in_specs,
        out_specs=pl.BlockSpec((SSM_GROUP, nb, LANES), g3),
        out_shape=jax.ShapeDtypeStruct((ssm_w, nb, LANES), F32),
        compiler_params=_cparams())(ut, toep, wst, wout, apow)


def _mix_kernel(yt_ref, att_ref, gs_ref, x_ref, wa_ref, wb_ref, wo_ref, g_ref, b_ref, out_ref, zt_ref,
                *, alpha):
    n_ch, n_blk = yt_ref.shape[0], yt_ref.shape[1]
    yt2 = yt_ref.reshape(n_ch * n_blk, LANES)
    for j in range(n_blk):
        y = yt2[pl.ds(j, n_ch, stride=n_blk), :]
        z = jax.nn.gelu(y, approximate=True)
        zt_ref[j * LANES:(j + 1) * LANES, :] = z.T.astype(BF16)
    z = zt_ref[...]
    ssm_out = _dot(z, wa_ref[...]) * jax.nn.sigmoid(_dot(z, wb_ref[...]))
    mixed = att_ref[...].astype(F32) + gs_ref[...].astype(F32) * ssm_out
    y = alpha * x_ref[...] + _dot(mixed.astype(BF16), wo_ref[...])
    out_ref[...] = _layer_norm(y, g_ref[...], b_ref[...])


def _mix(yt, att, gs, x2d, wa, wb, wo, g, b, alpha, tm):
    T, D = x2d.shape
    ssm_w = yt.shape[0]
    row = lambda i: (i, 0)
    in_specs = [pl.BlockSpec((ssm_w, tm // LANES, LANES), lambda i: (0, i, 0)),
                pl.BlockSpec((tm, D), row), pl.BlockSpec((tm, D), row), pl.BlockSpec((tm, D), row),
                _full_spec(wa.shape), _full_spec(wb.shape), _full_spec(wo.shape),
                _full_spec(g.shape), _full_spec(b.shape)]
    return pl.pallas_call(functools.partial(_mix_kernel, alpha=alpha), name="mix", grid=(T // tm,), in_specs=in_specs,
                          out_specs=pl.BlockSpec((tm, D), row),
                          out_shape=jax.ShapeDtypeStruct((T, D), F32),
                          scratch_shapes=[pltpu.VMEM((tm, ssm_w), BF16)],
                          compiler_params=_cparams())(yt, att, gs, x2d, wa, wb, wo, g, b)


def _memkv_kernel(m_ref, wk_ref, wv_ref, k_ref, v_ref):
    mb = m_ref[...].astype(BF16)
    k_ref[...] = _dot(mb, wk_ref[...]).astype(BF16)
    v_ref[...] = _dot(mb, wv_ref[...]).astype(BF16)


def _memkv(mem2d, wk, wv, tm):
    R, D = mem2d.shape
    row = lambda i: (i, 0)
    return pl.pallas_call(_memkv_kernel, name="memkv", grid=(R // tm,),
                          in_specs=[pl.BlockSpec((tm, D), row), _full_spec(wk.shape), _full_spec(wv.shape)],
                          out_specs=(pl.BlockSpec((tm, D), row), pl.BlockSpec((tm, D), row)),
                          out_shape=(jax.ShapeDtypeStruct((R, D), BF16), jax.ShapeDtypeStruct((R, D), BF16)),
                          compiler_params=_cparams())(mem2d, wk, wv)


def _cross_kernel(x_ref, k_ref, v_ref, wq_ref, wo_ref, g_ref, b_ref, wr_ref, br_ref, out_ref, logit_ref,
                  cat_ref, *, alpha):
    x = x_ref[...]
    D = x.shape[1]
    hd = D // N_CROSS_HEADS
    q = (_dot(x.astype(BF16), wq_ref[...]) * (hd ** -0.5)).astype(BF16)
    for h in range(N_CROSS_HEADS):
        cols = slice(h * hd, (h + 1) * hd)
        s = _dot_nt(q[:, cols], k_ref[:, cols])
        m = jnp.max(s, axis=-1, keepdims=True)
        p = jnp.exp(s - m)
        w = (p / jnp.sum(p, axis=-1, keepdims=True)).astype(BF16)
        cat_ref[:, cols] = _dot(w, v_ref[:, cols]).astype(BF16)
    y = alpha * x + _dot(cat_ref[...], wo_ref[...])
    x2 = _layer_norm(y, g_ref[...], b_ref[...])
    out_ref[...] = x2
    logit_ref[...] = _dot(x2.astype(BF16), wr_ref[...]) + br_ref[...]


def _cross(x1, kc, vc, wq, wo, g, b, wr, br, alpha, S, tm):
    T, D = x1.shape
    n_mem = kc.shape[0] // (T // S)
    row = lambda i: (i, 0)
    per_b = S // tm
    in_specs = [pl.BlockSpec((tm, D), row),
                pl.BlockSpec((n_mem, D), lambda i: (i // per_b, 0)),
                pl.BlockSpec((n_mem, D), lambda i: (i // per_b, 0)),
                _full_spec(wq.shape), _full_spec(wo.shape), _full_spec(g.shape), _full_spec(b.shape),
                _full_spec(wr.shape), _full_spec(br.shape)]
    return pl.pallas_call(functools.partial(_cross_kernel, alpha=alpha), name="cross", grid=(T // tm,), in_specs=in_specs,
                          out_specs=(pl.BlockSpec((tm, D), row), pl.BlockSpec((tm, LANES), row)),
                          out_shape=(jax.ShapeDtypeStruct((T, D), F32), jax.ShapeDtypeStruct((T, LANES), F32)),
                          scratch_shapes=[pltpu.VMEM((tm, D), BF16)],
                          compiler_params=_cparams())(x1, kc, vc, wq, wo, g, b, wr, br)


SEG_ALIGN = 8


def _route_kernel(logit_ref, info_ref, seg_ref, count_ref, carry_ref):
    tm = logit_ref.shape[0]

    @pl.when(pl.program_id(0) == 0)
    def _():
        carry_ref[...] = jnp.zeros_like(carry_ref)

    vals = logit_ref[...]
    lane = lax.broadcasted_iota(jnp.int32, vals.shape, 1).astype(F32)
    sels, tops = [], []
    for _ in range(TOP_K):
        m = jnp.max(vals, axis=-1, keepdims=True)
        idx = jnp.min(jnp.where(vals == m, lane, float(LANES)), axis=-1, keepdims=True)
        sel = lane == idx
        vals = jnp.where(sel, -jnp.inf, vals)
        sels.append((sel, idx))
        tops.append(m)
    es = [jnp.exp(t - tops[0]) for t in tops]
    denom = es[0]
    for e in es[1:]:
        denom = denom + e
    onehot = jnp.zeros(vals.shape, F32)
    for sel, _ in sels:
        onehot = onehot + jnp.where(sel, 1.0, 0.0)
    r = lax.broadcasted_iota(jnp.int32, (tm, tm), 0)
    c = lax.broadcasted_iota(jnp.int32, (tm, tm), 1)
    tri = jnp.where(c < r, 1.0, 0.0).astype(BF16)
    cum = _dot(tri, onehot.astype(BF16))
    n_seg = jnp.floor((jnp.sum(onehot, axis=0, keepdims=True) + (SEG_ALIGN - 1.0)) * (1.0 / SEG_ALIGN)) * SEG_ALIGN
    a = lax.broadcasted_iota(jnp.int32, (LANES, LANES), 0)
    b = lax.broadcasted_iota(jnp.int32, (LANES, LANES), 1)
    upper = jnp.where(a < b, 1.0, 0.0).astype(BF16)
    seg_off = _dot(jnp.broadcast_to(n_seg, (8, LANES)).astype(BF16), upper)[0:1, :]
    carry = carry_ref[...]
    info = jnp.zeros(vals.shape, F32)
    for k, (sel, idx) in enumerate(sels):
        local = jnp.sum(jnp.where(sel, cum + seg_off, 0.0), axis=-1, keepdims=True)
        info = jnp.where(lane == k, idx, info)
        info = jnp.where(lane == TOP_K + k, local, info)
        info = jnp.where(lane == 2 * TOP_K + k, es[k] / denom, info)
    info_ref[...] = info
    row8 = lax.broadcasted_iota(jnp.int32, (8, LANES), 0)
    seg_ref[0] = jnp.where(row8 == 0, n_seg, jnp.where(row8 == 1, seg_off, jnp.where(row8 == 2, carry, 0.0)))
    carry_ref[...] = carry + n_seg
    count_ref[...] = carry_ref[...]


def _route(logits, tm):
    T = logits.shape[0]
    row = lambda i: (i, 0)
    return pl.pallas_call(_route_kernel, name="route", grid=(T // tm,),
                          in_specs=[pl.BlockSpec((tm, LANES), row)],
                          out_specs=(pl.BlockSpec((tm, LANES), row), pl.BlockSpec((1, 8, LANES), lambda i: (i, 0, 0)),
                                     pl.BlockSpec((1, LANES), lambda i: (0, 0))),
                          out_shape=(jax.ShapeDtypeStruct((T, LANES), F32),
                                     jax.ShapeDtypeStruct((T // tm, 8, LANES), F32),
                                     jax.ShapeDtypeStruct((1, LANES), F32)),
                          scratch_shapes=[pltpu.VMEM((1, LANES), F32)],
                          compiler_params=_cparams())(logits)


GU_TILE = 2 * LANES


SEG_SIZES = (256, 128, 64, 32, 16, 8)
SEG_LARGE = 64


def _segment_copies(seg_ref, local_ref, global_ref, sem, to_global):
    def pieces(n, off, start, sizes):
        for sz in sizes:
            take = (n & sz) != 0

            @pl.when(take)
            def _(off=off, start=start, sz=sz):
                loc = local_ref.at[pl.ds(pl.multiple_of(off, SEG_ALIGN), sz)]
                glo = global_ref.at[pl.ds(pl.multiple_of(start, SEG_ALIGN), sz)]
                if to_global:
                    pltpu.make_async_copy(loc, glo, sem).start()
                else:
                    pltpu.make_async_copy(glo, loc, sem).start()

            step = jnp.where(take, sz, 0)
            off = off + step
            start = start + step

    n_large = sum(1 for sz in SEG_SIZES if sz >= SEG_LARGE)
    for e in range(N_EXPERTS):
        n = seg_ref[0, 0, e]
        off = seg_ref[0, 1, e]
        start = seg_ref[0, 2, e]

        @pl.when(n >= SEG_LARGE)
        def _(n=n, off=off, start=start):
            pieces(n, off, start, SEG_SIZES[:n_large])

        large = n & ~(SEG_LARGE - 1)
        pieces(n, off + large, start + large, SEG_SIZES[n_large:])


def _wait_rows(local_ref, global_ref, sem, n_rows):
    @pl.when(n_rows > 0)
    def _():
        n = pl.multiple_of(n_rows, SEG_ALIGN)
        pltpu.make_async_copy(global_ref.at[pl.ds(0, n)], local_ref.at[pl.ds(0, n)], sem).wait()


def _dispatch_kernel(ends_ref, seg_ref, x_ref, info_ref, xs_hbm, comp, tot, sem):
    i = pl.program_id(0)
    n_steps = pl.num_programs(0)
    slot = i % 2
    tm = x_ref.shape[0]
    cap = comp.shape[1]

    @pl.when(i == 0)
    def _():
        comp[1, 0:MOE_BLOCK, :] = jnp.zeros((MOE_BLOCK, comp.shape[2]), F32)
        for e in range(N_EXPERTS):
            end = ends_ref[e]
            prev = ends_ref[e - 1] if e > 0 else 0

            @pl.when(end > prev)
            def _(end=end):
                cp = pltpu.make_async_copy(comp.at[1, pl.ds(0, MOE_BLOCK)],
                                           xs_hbm.at[pl.ds(pl.multiple_of(end - MOE_BLOCK, MOE_BLOCK), MOE_BLOCK)],
                                           sem.at[1])
                cp.start()
                cp.wait()

        def tail(b):
            return pltpu.make_async_copy(comp.at[1, pl.ds(0, MOE_BLOCK)],
                                         xs_hbm.at[pl.ds(pl.multiple_of(b * MOE_BLOCK, MOE_BLOCK), MOE_BLOCK)],
                                         sem.at[0])
        first_unused = ends_ref[N_EXPERTS - 1] // MOE_BLOCK
        n_all = xs_hbm.shape[0] // MOE_BLOCK
        lax.fori_loop(first_unused, n_all, lambda b, c: (tail(b).start(), c)[1], 0)
        lax.fori_loop(first_unused, n_all, lambda b, c: (tail(b).wait(), c)[1], 0)
        tot[0] = 0
        tot[1] = 0

    _wait_rows(comp.at[slot], xs_hbm, sem.at[slot], tot[slot])
    pos = info_ref[...].T
    r = lax.broadcasted_iota(jnp.int32, (cap, tm), 0).astype(F32)
    sel = jnp.zeros((cap, tm), F32)
    for k in range(TOP_K):
        sel = jnp.where(r == pos[TOP_K + k:TOP_K + k + 1, :], 1.0, sel)
    sel = sel.astype(BF16)
    comp[slot] = _dot(sel, x_ref[...].astype(BF16))
    _segment_copies(seg_ref, comp.at[slot], xs_hbm, sem.at[slot], to_global=True)
    tot[slot] = seg_ref[0, 3, 0]

    @pl.when(i == n_steps - 1)
    def _():
        _wait_rows(comp.at[slot], xs_hbm, sem.at[slot], tot[slot])
        _wait_rows(comp.at[1 - slot], xs_hbm, sem.at[1 - slot], tot[1 - slot])


def _dispatch(ends_pad, seg, x2, info, n_slots, tm):
    T, D = x2.shape
    row = lambda i, ends: (i, 0)
    grid_spec = pltpu.PrefetchScalarGridSpec(
        num_scalar_prefetch=1, grid=(T // tm,),
        in_specs=[pl.BlockSpec((1, 4, LANES), lambda i, ends: (i, 0, 0), memory_space=pltpu.SMEM),
                  pl.BlockSpec((tm, D), row), pl.BlockSpec((tm, LANES), row)],
        out_specs=pl.BlockSpec(memory_space=pl.ANY),
        scratch_shapes=[pltpu.VMEM((2, TOP_K * tm + N_EXPERTS * SEG_ALIGN, D), F32), pltpu.SMEM((2,), jnp.int32),
                        pltpu.SemaphoreType.DMA((2,))])
    return pl.pallas_call(_dispatch_kernel, name="dispatch", grid_spec=grid_spec,
                          out_shape=jax.ShapeDtypeStruct((n_slots, D), F32),
                          compiler_params=_cparams())(ends_pad, seg, x2, info)


def _moe_kernel(bexp_ref, nused_ref, x_ref, wgu_ref, bgu_ref, wd_ref, bd_ref, y_ref, wgu_s, wd_s):
    j = pl.program_id(0)
    n_used = nused_ref[0]
    n_tiles = wgu_s.shape[1] // GU_TILE

    @pl.when(j >= n_used)
    def _():
        y_ref[...] = jnp.zeros_like(y_ref)

    @pl.when(j < n_used)
    def _():
        @pl.when((j == 0) | (bexp_ref[j] != bexp_ref[jnp.maximum(j - 1, 0)]))
        def _():
            k = lax.broadcasted_iota(jnp.int32, (GU_TILE, GU_TILE), 0)
            n = lax.broadcasted_iota(jnp.int32, (GU_TILE, GU_TILE), 1)
            perm = jnp.where(k == jnp.where(n < LANES, 2 * n, 2 * (n - LANES) + 1), 1.0, 0.0).astype(BF16)
            for t in range(n_tiles):
                cols = slice(t * GU_TILE, (t + 1) * GU_TILE)
                wgu_s[:, cols] = _dot(wgu_ref[0, :, cols].astype(BF16), perm).astype(BF16)
            wd_s[...] = wd_ref[0].astype(BF16)

        xb = x_ref[...].astype(BF16)
        hs = []
        for t in range(n_tiles):
            cols = slice(t * GU_TILE, (t + 1) * GU_TILE)
            gu = _dot(xb, wgu_s[:, cols]) + bgu_ref[0, :, cols]
            gate = jnp.minimum(gu[:, :LANES], SWIGLU_LIMIT)
            lin = jnp.clip(gu[:, LANES:], -SWIGLU_LIMIT, SWIGLU_LIMIT)
            hs.append((gate * jax.nn.sigmoid(SWIGLU_ALPHA * gate) * (lin + 1.0)).astype(BF16))
        y_ref[...] = _dot(jnp.concatenate(hs, axis=1), wd_s[...]) + bd_ref[0]


def _moe(bexp, n_used, xs, wgu, bgu, wd, bd):
    n_blocks = bexp.shape[0]
    D = xs.shape[1]
    F2 = wgu.shape[2]
    F = wd.shape[1]
    wmap = lambda j, be, nu: (be[jnp.minimum(j, nu[0] - 1)], 0, 0)
    blk = lambda j, be, nu: (j, 0)
    used = lambda j, be, nu: (jnp.minimum(j, nu[0] - 1), 0)
    in_specs = [pl.BlockSpec((MOE_BLOCK, D), used),
                pl.BlockSpec((1, D, F2), wmap), pl.BlockSpec((1, 1, F2), wmap),
                pl.BlockSpec((1, F, D), wmap), pl.BlockSpec((1, 1, D), wmap)]
    grid_spec = pltpu.PrefetchScalarGridSpec(
        num_scalar_prefetch=2, grid=(n_blocks,), in_specs=in_specs,
        out_specs=pl.BlockSpec((MOE_BLOCK, D), blk),
        scratch_shapes=[pltpu.VMEM((D, F2), BF16), pltpu.VMEM((F, D), BF16)])
    return pl.pallas_call(_moe_kernel, name="moe", grid_spec=grid_spec,
                          out_shape=jax.ShapeDtypeStruct((n_blocks * MOE_BLOCK, D), F32),
                          compiler_params=_cparams())(bexp, n_used, xs, wgu, bgu, wd, bd)


def _combine_kernel(seg_ref, seg_nxt_ref, ys_hbm, info_ref, x_ref, g_ref, b_ref, out_ref, comp, sem, *, alpha):
    i = pl.program_id(0)
    n_steps = pl.num_programs(0)
    slot = i % 2
    tm = x_ref.shape[0]
    cap = comp.shape[1]

    @pl.when(i == 0)
    def _():
        comp[...] = jnp.zeros_like(comp)
        _segment_copies(seg_ref, comp.at[0], ys_hbm, sem.at[0], to_global=False)

    @pl.when(i + 1 < n_steps)
    def _():
        _segment_copies(seg_nxt_ref, comp.at[1 - slot], ys_hbm, sem.at[1 - slot], to_global=False)

    _wait_rows(comp.at[slot], ys_hbm, sem.at[slot], seg_ref[0, 3, 0])
    info = info_ref[...]
    lane = lax.broadcasted_iota(jnp.int32, (tm, cap), 1).astype(F32)
    wmat = jnp.zeros((tm, cap), F32)
    for k in range(TOP_K):
        wmat = jnp.where(lane == info[:, TOP_K + k:TOP_K + k + 1], info[:, 2 * TOP_K + k:2 * TOP_K + k + 1], wmat)
    ffn = _dot(wmat.astype(BF16), comp[slot].astype(BF16))
    out_ref[...] = _layer_norm(alpha * x_ref[...] + ffn, g_ref[...], b_ref[...])


def _combine(seg, ys, info, x2, g, b, alpha, tm):
    T, D = x2.shape
    n_t = T // tm
    row = lambda i: (i, 0)
    seg_spec = lambda m: pl.BlockSpec((1, 4, LANES), m, memory_space=pltpu.SMEM)
    in_specs = [seg_spec(lambda i: (i, 0, 0)), seg_spec(lambda i: (jnp.minimum(i + 1, n_t - 1), 0, 0)),
                pl.BlockSpec(memory_space=pl.ANY),
                pl.BlockSpec((tm, LANES), row), pl.BlockSpec((tm, D), row),
                _full_spec(g.shape), _full_spec(b.shape)]
    cap = TOP_K * tm + N_EXPERTS * SEG_ALIGN
    return pl.pallas_call(functools.partial(_combine_kernel, alpha=alpha), name="combine", grid=(n_t,), in_specs=in_specs,
                          out_specs=pl.BlockSpec((tm, D), row),
                          out_shape=jax.ShapeDtypeStruct((T, D), F32),
                          scratch_shapes=[pltpu.VMEM((2, cap, D), F32), pltpu.SemaphoreType.DMA((2,))],
                          compiler_params=_cparams())(seg, seg, ys, info, x2, g, b)


def _tile(n, pref):
    return pref if n % pref == 0 else n


def _layer(x2d, mem2d, pos2d, B, S, depth, w_in, sinks, w_attn_o, lam_re, lam_im, log_dt, b_re, b_im,
           c_re, c_im, d_skip, w_glu_a, w_glu_b, w_out, ln1_g, ln1_b, wq_c, wk_c, wv_c, wo_c, ln2_g,
           ln2_b, w_router, b_router, w_gate_up, b_gate_up, w_down, b_down, ln3_g, ln3_b):
    T, D = x2d.shape
    alpha = (2 * depth) ** 0.25
    rep = N_Q_HEADS // N_KV_HEADS
    q_w = N_Q_HEADS * HEAD_DIM
    kv_w = N_KV_HEADS * HEAD_DIM
    ssm_w = SSM_GROUP * SSM_GROUPS
    P = SSM_STATE

    o_k, o_v, o_s = q_w, q_w + kv_w, q_w + 2 * kv_w
    o_ga, o_gs = o_s + ssm_w, o_s + ssm_w + D
    wq = w_in[:, :o_k].reshape(D, N_KV_HEADS, rep, HEAD_DIM).transpose(0, 2, 1, 3).reshape(D, q_w).astype(BF16)
    wk = w_in[:, o_k:o_v].astype(BF16)
    wv = w_in[:, o_v:o_s].astype(BF16)
    wu = w_in[:, o_s:o_ga].astype(BF16)
    wga = w_in[:, o_ga:o_gs].astype(BF16)
    wgs = w_in[:, o_gs:].astype(BF16)
    wo_attn = w_attn_o.reshape(N_KV_HEADS, rep, HEAD_DIM, D).transpose(1, 0, 2, 3).reshape(q_w, D).astype(BF16)
    half = HEAD_DIM // 2
    inv_freq = jnp.power(ROPE_THETA, -jnp.arange(half, dtype=F32) / half)
    invf = jnp.tile(inv_freq, LANES // half)[None, :]
    sink_rows = jnp.repeat(sinks.astype(F32).reshape(N_KV_HEADS, rep), WINDOW, axis=1)
    sink_cols = jnp.full((N_KV_HEADS, rep * WINDOW, 2 * WINDOW), NEG_BIG, F32).at[:, :, 0].set(sink_rows)

    tm1 = _tile(T, 1024)
    q2, k, v, ut, ga, gs = _inproj(x2d, pos2d, invf, wq, wk, wv, wu, wga, wgs, tm1)
    att = _swa(q2, k, v, sink_cols, wo_attn, ga, B, S, _tile(S, 512))

    lam_row = jnp.stack([jnp.concatenate([lam_re, lam_re], -1), jnp.concatenate([lam_im, lam_im], -1)], 1)
    lam_col = jnp.swapaxes(lam_row, 1, 2)
    bt_re = jnp.swapaxes(b_re, 1, 2)
    bt_im = jnp.swapaxes(b_im, 1, 2)
    bt_a = jnp.concatenate([bt_re, bt_im], -1)
    bt_b = jnp.concatenate([-bt_im, bt_re], -1)
    ct_re = jnp.swapaxes(c_re, 1, 2)
    ct_im = jnp.swapaxes(c_im, 1, 2)
    ct_a = jnp.concatenate([ct_re, -ct_im], 1)
    ct_b = jnp.concatenate([-ct_im, -ct_re], 1)
    dsk = jnp.repeat(d_skip.reshape(SSM_GROUPS, SSM_GROUP), SSM_GROUP, axis=1)[:, :, None]
    toep, wst, wout, apow = _ssm_prep(lam_row, lam_col, log_dt.reshape(SSM_GROUPS, 1, 1), bt_a, bt_b,
                                      c_re, c_im, ct_a, ct_b, dsk)
    yt = _ssm_scan(ut, toep, wst, wout, apow, S // LANES)

    x1 = _mix(yt, att, gs, x2d, w_glu_a.astype(BF16), w_glu_b.astype(BF16), w_out.astype(BF16),
              ln1_g[None, :], ln1_b[None, :], alpha, _tile(T, 1024))

    kc, vc = _memkv(mem2d, wk_c.astype(BF16), wv_c.astype(BF16), _tile(mem2d.shape[0], 512))
    wr = jnp.zeros((D, LANES), F32).at[:, :N_EXPERTS].set(w_router).astype(BF16)
    br = jnp.full((1, LANES), NEG_BIG, F32).at[0, :N_EXPERTS].set(b_router)
    x2, logits = _cross(x1, kc, vc, wq_c.astype(BF16), wo_c.astype(BF16), ln2_g[None, :], ln2_b[None, :],
                        wr, br, alpha, S, _tile(S, 1024))

    tm_r = _tile(T, ROUTE_TILE)
    n_tiles = T // tm_r
    info, segf, counts = _route(logits, tm_r)
    total = counts[0].astype(jnp.int32)
    padded = (total + MOE_BLOCK - 1) // MOE_BLOCK * MOE_BLOCK
    ends_lane = jnp.cumsum(padded)
    start_lane = ends_lane - padded
    ends_pad = ends_lane[:N_EXPERTS]
    n_blocks = -(-(T * TOP_K + n_tiles * N_EXPERTS * (SEG_ALIGN - 1)) // MOE_BLOCK) + N_EXPERTS
    segi = segf.astype(jnp.int32)
    seg_n, seg_off, seg_start = segi[:, 0, :], segi[:, 1, :], segi[:, 2, :] + start_lane[None, :]
    seg_tot = jnp.broadcast_to(jnp.sum(seg_n, axis=1, keepdims=True), seg_n.shape)
    seg = jnp.stack([seg_n, seg_off, seg_start, seg_tot], axis=1)
    block_start = jnp.arange(n_blocks, dtype=jnp.int32) * MOE_BLOCK
    bexp = jnp.minimum(jnp.sum(block_start[:, None] >= ends_pad[None, :], axis=1), N_EXPERTS - 1).astype(jnp.int32)
    n_used = (ends_pad[-1] // MOE_BLOCK).astype(jnp.int32)[None]
    n_gu_tiles = b_gate_up.shape[1] // GU_TILE
    bgu = b_gate_up.reshape(N_EXPERTS, n_gu_tiles, LANES, 2).transpose(0, 1, 3, 2).reshape(N_EXPERTS, 1, -1)
    xs = _dispatch(ends_pad, seg, x2, info, n_blocks * MOE_BLOCK, tm_r)
    ys = _moe(bexp, n_used, xs, w_gate_up, bgu, w_down, b_down[:, None, :])
    return _combine(seg, ys, info, x2, ln3_g[None, :], ln3_b[None, :], alpha, tm_r)


def kernel(x, mem, positions, w_in, sinks, w_attn_o, lam_re, lam_im, log_dt, b_re, b_im, c_re, c_im, d_skip,
           w_glu_a, w_glu_b, w_out, ln1_g, ln1_b, wq_c, wk_c, wv_c, wo_c, ln2_g, ln2_b, w_router, b_router,
           w_gate_up, b_gate_up, w_down, b_down, ln3_g, ln3_b):
    B, S, D = x.shape
    depth = w_in.shape[0]
    x2d = x.reshape(B * S, D)
    mem2d = mem.reshape(-1, D)
    pos2d = positions.reshape(B * S, 1)
    per_layer = (w_in, sinks, w_attn_o, lam_re, lam_im, log_dt, b_re, b_im, c_re, c_im, d_skip, w_glu_a,
                 w_glu_b, w_out, ln1_g, ln1_b, wq_c, wk_c, wv_c, wo_c, ln2_g, ln2_b, w_router, b_router,
                 w_gate_up, b_gate_up, w_down, b_down, ln3_g, ln3_b)
    for l in range(depth):
        x2d = _layer(x2d, mem2d, pos2d, B, S, depth, *(w[l] for w in per_layer))
    return x2d.reshape(B, S, D)
```

```python
import functools

import jax
import jax.numpy as jnp
from jax import lax
from jax.experimental import pallas as pl
from jax.experimental.pallas import tpu as pltpu

N_Q_HEADS = 16
N_KV_HEADS = 2
HEAD_DIM = 64
WINDOW = 128
ROPE_THETA = 10000.0
SSM_GROUP = 16
SSM_GROUPS = 32
SSM_STATE = 64
N_CROSS_HEADS = 4
N_EXPERTS = 32
TOP_K = 4
SWIGLU_ALPHA = 1.702
SWIGLU_LIMIT = 7.0
MOE_BLOCK = 512
ROUTE_TILE = 256
LN_EPS = 1e-5

LANES = 128
VMEM_LIMIT_BYTES = 56 * 1024 * 1024

NEG_BIG = -1e30
BF16 = jnp.bfloat16
F32 = jnp.float32
HI = lax.Precision.HIGHEST


def _cparams(n_axes=1):
    return pltpu.CompilerParams(dimension_semantics=("arbitrary",) * n_axes,
                                vmem_limit_bytes=VMEM_LIMIT_BYTES)


def _full_spec(shape):
    n = len(shape)
    return pl.BlockSpec(shape, lambda *_: (0,) * n, pipeline_mode=pl.Buffered(1))


def _dot(a, b):
    return jnp.dot(a, b, preferred_element_type=F32)


def _dot_nt(a, b):
    return lax.dot_general(a, b, (((1,), (1,)), ((), ())), preferred_element_type=F32)


def _layer_norm(y, g, b):
    mu = jnp.mean(y, axis=-1, keepdims=True)
    d = y - mu
    var = jnp.mean(d * d, axis=-1, keepdims=True)
    return d * lax.rsqrt(var + LN_EPS) * g + b


def _rope(t, cos, sin_signed, first_half):
    half = HEAD_DIM // 2
    partner = jnp.where(first_half, pltpu.roll(t, LANES - half, axis=1), pltpu.roll(t, half, axis=1))
    return t * cos + partner * sin_signed


def _inproj_kernel(x_ref, pos_ref, invf_ref, wq_ref, wk_ref, wv_ref, wu_ref, wga_ref, wgs_ref,
                   q2_ref, k_ref, v_ref, ut_ref, ga_ref, gs_ref):
    tm = x_ref.shape[0]
    xb = x_ref[...].astype(BF16)
    ang = pos_ref[...].astype(F32) * invf_ref[...]
    cos = jnp.cos(ang)
    sin = jnp.sin(ang)
    first_half = (lax.broadcasted_iota(jnp.int32, (tm, LANES), 1) % HEAD_DIM) < (HEAD_DIM // 2)
    first_half_w = (lax.broadcasted_iota(jnp.int32, (WINDOW, LANES), 1) % HEAD_DIM) < (HEAD_DIM // 2)
    sin_signed = jnp.where(first_half, -sin, sin)

    q = _dot(xb, wq_ref[...])
    n_rep = q.shape[1] // LANES
    scale = HEAD_DIM ** -0.5
    for j in range(tm // WINDOW):
        rows = slice(j * WINDOW, (j + 1) * WINDOW)
        for r in range(n_rep):
            t = _rope(q[rows, r * LANES:(r + 1) * LANES], cos[rows], sin_signed[rows], first_half_w)
            base = (j * n_rep + r) * WINDOW
            q2_ref[base:base + WINDOW, :] = (t * scale).astype(BF16)
    k_ref[...] = _rope(_dot(xb, wk_ref[...]), cos, sin_signed, first_half).astype(BF16)
    v_ref[...] = _dot(xb, wv_ref[...]).astype(BF16)
    u = _dot(xb, wu_ref[...])
    for j in range(tm // LANES):
        ut_ref[j] = u[j * LANES:(j + 1) * LANES, :].T
    ga_ref[...] = jax.nn.sigmoid(_dot(xb, wga_ref[...])).astype(BF16)
    gs_ref[...] = jax.nn.sigmoid(_dot(xb, wgs_ref[...])).astype(BF16)


def _inproj(x2d, pos2d, invf, wq, wk, wv, wu, wga, wgs, tm):
    T, D = x2d.shape
    n_rep = wq.shape[1] // LANES
    ssm_w = wu.shape[1]
    row = lambda i: (i, 0)
    out_shape = (
        jax.ShapeDtypeStruct((T * n_rep, LANES), BF16),
        jax.ShapeDtypeStruct((T, LANES), BF16),
        jax.ShapeDtypeStruct((T, LANES), BF16),
        jax.ShapeDtypeStruct((T // LANES, ssm_w, LANES), F32),
        jax.ShapeDtypeStruct((T, D), BF16),
        jax.ShapeDtypeStruct((T, D), BF16),
    )
    in_specs = [pl.BlockSpec((tm, D), row), pl.BlockSpec((tm, 1), row), _full_spec(invf.shape),
                _full_spec(wq.shape), _full_spec(wk.shape), _full_spec(wv.shape),
                _full_spec(wu.shape), _full_spec(wga.shape), _full_spec(wgs.shape)]
    out_specs = (pl.BlockSpec((tm * n_rep, LANES), row), pl.BlockSpec((tm, LANES), row),
                 pl.BlockSpec((tm, LANES), row),
                 pl.BlockSpec((tm // LANES, ssm_w, LANES), lambda i: (i, 0, 0)),
                 pl.BlockSpec((tm, D), row), pl.BlockSpec((tm, D), row))
    return pl.pallas_call(_inproj_kernel, name="inproj", grid=(T // tm,), in_specs=in_specs, out_specs=out_specs,
                          out_shape=out_shape, compiler_params=_cparams())(
        x2d, pos2d, invf, wq, wk, wv, wu, wga, wgs)


def _swa_kernel(q2_ref, kc_ref, kp_ref, vc_ref, vp_ref, fill_ref, wo_ref, ga_ref, out_ref, cat_ref):
    i = pl.program_id(1)
    tq = kc_ref.shape[0]
    n_sub = tq // WINDOW
    rep = N_Q_HEADS // N_KV_HEADS
    rows_all = rep * WINDOW
    kfull = jnp.concatenate([kp_ref[...], kc_ref[...]], axis=0)
    vfull = jnp.concatenate([vp_ref[...], vc_ref[...]], axis=0)
    lane = lax.broadcasted_iota(jnp.int32, (2 * WINDOW, LANES), 1)
    qi = lax.broadcasted_iota(jnp.int32, (rows_all, 2 * WINDOW), 0) % WINDOW
    ci = lax.broadcasted_iota(jnp.int32, (rows_all, 2 * WINDOW), 1)
    local = (ci > qi) & (ci <= qi + WINDOW)
    out_lane = lax.broadcasted_iota(jnp.int32, (rows_all, LANES), 1)
    band_row = lax.broadcasted_iota(jnp.int32, (2 * WINDOW, LANES), 0)
    for j in range(n_sub):
        qs = q2_ref[j * rows_all:(j + 1) * rows_all, :]
        kb = kfull[j * WINDOW:(j + 2) * WINDOW, :]
        vb = vfull[j * WINDOW:(j + 2) * WINDOW, :]
        vb = jnp.where(band_row == 0, jnp.zeros_like(vb), vb)
        mask = local
        if j == 0:
            mask = mask & ((ci >= WINDOW) | (i > 0))
        o = None
        for g in range(N_KV_HEADS):
            in_group = (lane >= g * HEAD_DIM) & (lane < (g + 1) * HEAD_DIM)
            kg = jnp.where(in_group, kb, jnp.zeros_like(kb))
            s = _dot_nt(qs, kg)
            s = jnp.where(mask, s, fill_ref[g])
            m = jnp.max(s, axis=-1, keepdims=True)
            p = jnp.exp(s - m)
            denom = jnp.sum(p, axis=-1, keepdims=True)
            og = _dot(p.astype(BF16), vb) * (1.0 / denom)
            o = og if o is None else jnp.where(out_lane < g * HEAD_DIM, o, og)
        ob = o.astype(BF16)
        for r in range(rep):
            cat_ref[j * WINDOW:(j + 1) * WINDOW, r * LANES:(r + 1) * LANES] = ob[r * WINDOW:(r + 1) * WINDOW, :]
    attn = _dot(cat_ref[...], wo_ref[...])
    out_ref[...] = (attn * ga_ref[...].astype(F32)).astype(BF16)


def _swa(q2, k, v, sink_cols, wo, ga, B, S, tq):
    T, D = ga.shape
    rep = N_Q_HEADS // N_KV_HEADS
    n_i = S // tq
    n_sub = tq // WINDOW
    in_specs = [
        pl.BlockSpec((tq * rep, LANES), lambda b, i: (b * n_i + i, 0)),
        pl.BlockSpec((tq, LANES), lambda b, i: (b * n_i + i, 0)),
        pl.BlockSpec((WINDOW, LANES), lambda b, i: (b * (S // WINDOW) + jnp.maximum(i * n_sub - 1, 0), 0)),
        pl.BlockSpec((tq, LANES), lambda b, i: (b * n_i + i, 0)),
        pl.BlockSpec((WINDOW, LANES), lambda b, i: (b * (S // WINDOW) + jnp.maximum(i * n_sub - 1, 0), 0)),
        _full_spec(sink_cols.shape), _full_spec(wo.shape),
        pl.BlockSpec((tq, D), lambda b, i: (b * n_i + i, 0)),
    ]
    return pl.pallas_call(
        _swa_kernel, name="swa", grid=(B, n_i), in_specs=in_specs,
        out_specs=pl.BlockSpec((tq, D), lambda b, i: (b * n_i + i, 0)),
        out_shape=jax.ShapeDtypeStruct((T, D), BF16),
        scratch_shapes=[pltpu.VMEM((tq, D), BF16)],
        compiler_params=_cparams(2))(q2, k, k, v, v, sink_cols, wo, ga)


def _ssm_prep_kernel(lam_row_ref, lam_col_ref, dt_ref, bt_a_ref, bt_b_ref, c_re_ref, c_im_ref,
                     ct_a_ref, ct_b_ref, dsk_ref, toep_ref, wst_ref, wout_ref, apow_ref):
    L = LANES
    P = SSM_STATE
    dt = jnp.exp(dt_ref[0])
    lr2 = lam_row_ref[0, 0:1, :]
    li2 = lam_row_ref[0, 1:2, :]
    mag = jnp.exp(lr2 * dt)
    ar = mag * jnp.cos(li2 * dt)
    ai = mag * jnp.sin(li2 * dt)
    den = lr2 * lr2 + li2 * li2
    f_re = ((ar - 1.0) * lr2 + ai * li2) / den
    f_im = (ai * lr2 - (ar - 1.0) * li2) / den
    bt_a = bt_a_ref[0]
    bt_b = bt_b_ref[0]
    bb_a = f_re * bt_a + f_im * bt_b
    bb_b = f_re * bt_b - f_im * bt_a

    tau_rev = (L - 1 - lax.broadcasted_iota(jnp.int32, (L, 2 * P), 0)).astype(F32)
    g_mag = jnp.exp(lr2 * dt * tau_rev)
    g_re = g_mag * jnp.cos(li2 * dt * tau_rev)
    g_im = g_mag * jnp.sin(li2 * dt * tau_rev)
    for i in range(SSM_GROUP):
        wst_ref[0, i * L:(i + 1) * L, :] = (g_re * bb_a[i:i + 1, :] + g_im * bb_b[i:i + 1, :]).astype(BF16)

    lane2 = lax.broadcasted_iota(jnp.int32, (1, 2 * P), 1)
    for kk in range(4):
        n = float(L * (1 << kk))
        pm = jnp.exp(lr2 * dt * n)
        p_re = pm * jnp.cos(li2 * dt * n)
        p_im = pm * jnp.sin(li2 * dt * n)
        apow_ref[0, 2 * kk:2 * kk + 1, :] = p_re
        apow_ref[0, 2 * kk + 1:2 * kk + 2, :] = jnp.where(lane2 < P, -p_im, p_im)

    lrc = lam_col_ref[0, :, 0:1]
    lic = lam_col_ref[0, :, 1:2]
    tau = lax.broadcasted_iota(jnp.int32, (2 * P, L), 1).astype(F32)
    e0_mag = jnp.exp(lrc * dt * tau)
    e0_re = e0_mag * jnp.cos(lic * dt * tau)
    e0_im = e0_mag * jnp.sin(lic * dt * tau)
    e1_mag = jnp.exp(lrc * dt * (tau + 1.0))
    e1_re = e1_mag * jnp.cos(lic * dt * (tau + 1.0))
    e1_im = e1_mag * jnp.sin(lic * dt * (tau + 1.0))

    ct_a = ct_a_ref[0]
    ct_b = ct_b_ref[0]
    for o in range(SSM_GROUP):
        wout_ref[0, :, o * L:(o + 1) * L] = (ct_a[:, o:o + 1] * e1_re + ct_b[:, o:o + 1] * e1_im).astype(BF16)

    c_re = c_re_ref[0]
    c_im = c_im_ref[0]
    bb_re = bb_a[:, :P]
    bb_im = bb_a[:, P:]
    m_re = (c_re[:, None, :] * bb_re[None, :, :] - c_im[:, None, :] * bb_im[None, :, :])
    m_im = (c_re[:, None, :] * bb_im[None, :, :] + c_im[:, None, :] * bb_re[None, :, :])
    m_re = m_re.reshape(SSM_GROUP * SSM_GROUP, P)
    m_im = m_im.reshape(SSM_GROUP * SSM_GROUP, P)
    kt = (jnp.dot(m_re, e0_re[:P, :], precision=HI, preferred_element_type=F32)
          - jnp.dot(m_im, e0_im[:P, :], precision=HI, preferred_element_type=F32))
    rowi = lax.broadcasted_iota(jnp.int32, (SSM_GROUP * SSM_GROUP, L), 0)
    coli = lax.broadcasted_iota(jnp.int32, (SSM_GROUP * SSM_GROUP, L), 1)
    dsk = dsk_ref[0]
    kt = kt + jnp.where((coli == 0) & ((rowi // SSM_GROUP) == (rowi % SSM_GROUP)), dsk, 0.0)

    cc = lax.broadcasted_iota(jnp.int32, (L, L), 0)
    cp = lax.broadcasted_iota(jnp.int32, (L, L), 1)
    causal = cp >= cc
    for o in range(SSM_GROUP):
        for i in range(SSM_GROUP):
            kv = kt[o * SSM_GROUP + i:o * SSM_GROUP + i + 1, :]
            blk = pltpu.roll(jnp.broadcast_to(kv, (L, L)), 0, axis=1, stride=1, stride_axis=0)
            blk = jnp.where(causal, blk, 0.0)
            toep_ref[0, i * L:(i + 1) * L, o * L:(o + 1) * L] = blk.astype(BF16)


def _ssm_scan_kernel(ut_ref, toep_ref, wst_ref, wout_ref, apow_ref, yt_ref, *, n_chunks):
    nb = ut_ref.shape[0]
    P = SSM_STATE
    ut2 = ut_ref.reshape(nb * SSM_GROUP, LANES)
    lhs = jnp.concatenate([ut2[pl.ds(i, nb, stride=SSM_GROUP), :] for i in range(SSM_GROUP)],
                          axis=1).astype(BF16)
    y = _dot(lhs, toep_ref[0])
    st = _dot(lhs, wst_ref[0])
    srow = lax.broadcasted_iota(jnp.int32, (nb, 2 * P), 0) % n_chunks
    kk = 0
    while (1 << kk) < n_chunks:
        d = 1 << kk
        pa = apow_ref[0, 2 * kk:2 * kk + 1, :]
        pb = apow_ref[0, 2 * kk + 1:2 * kk + 2, :]
        prev = pltpu.roll(st, d, axis=0)
        prev = prev * pa + pltpu.roll(prev, P, axis=1) * pb
        st = st + jnp.where(srow >= d, prev, 0.0)
        kk += 1
    h0 = jnp.where(srow >= 1, pltpu.roll(st, 1, axis=0), 0.0)
    y = y + _dot(h0.astype(BF16), wout_ref[0])
    for o in range(SSM_GROUP):
        yt_ref[o] = y[:, o * LANES:(o + 1) * LANES]


def _ssm_kernel(*refs, n_chunks):
    params, (ut_ref, yt_ref), ops = refs[:10], refs[10:12], refs[12:]
    sets = (ops[0::2], ops[1::2])
    g = pl.program_id(0)

    @pl.when(g == 0)
    def _():
        _ssm_prep_kernel(*params, *sets[0])

    for p in range(2):
        @pl.when((g > 0) & (g % 2 == p))
        def _(p=p):
            _ssm_prep_kernel(*params, *sets[p])
            _ssm_scan_kernel(ut_ref, *sets[1 - p], yt_ref, n_chunks=n_chunks)


def _ssm(ut, lam_row, lam_col, log_dt, bt_a, bt_b, c_re, c_im, ct_a, ct_b, dsk, n_chunks):
    nb, ssm_w, _ = ut.shape
    G = SSM_GROUPS
    KW = SSM_GROUP * LANES
    params = [lam_row, lam_col, log_dt, bt_a, bt_b, c_re, c_im, ct_a, ct_b, dsk]
    build = lambda g: (jnp.minimum(g, G - 1), 0, 0)
    in_specs = [pl.BlockSpec((1,) + a.shape[1:], build) for a in params]
    in_specs.append(pl.BlockSpec((nb, SSM_GROUP, LANES), lambda g: (0, jnp.maximum(g - 1, 0), 0)))
    op_shapes = [pltpu.VMEM((1, KW, KW), BF16), pltpu.VMEM((1, KW, 2 * SSM_STATE), BF16),
                 pltpu.VMEM((1, 2 * SSM_STATE, KW), BF16), pltpu.VMEM((1, 8, 2 * SSM_STATE), F32)]
    return pl.pallas_call(
        functools.partial(_ssm_kernel, n_chunks=n_chunks), name="ssm", grid=(G + 1,), in_specs=in_specs,
        out_specs=pl.BlockSpec((SSM_GROUP, nb, LANES), lambda g: (jnp.maximum(g - 1, 0), 0, 0)),
        out_shape=jax.ShapeDtypeStruct((ssm_w, nb, LANES), F32),
        scratch_shapes=[s for s in op_shapes for _ in range(2)],
        compiler_params=_cparams())(*params, ut)


def _mix_kernel(yt_ref, att_ref, gs_ref, x_ref, wa_ref, wb_ref, wo_ref, g_ref, b_ref, out_ref, zt_ref,
                *, alpha):
    n_ch, n_blk = yt_ref.shape[0], yt_ref.shape[1]
    yt2 = yt_ref.reshape(n_ch * n_blk, LANES)
    for j in range(n_blk):
        y = yt2[pl.ds(j, n_ch, stride=n_blk), :]
        z = jax.nn.gelu(y, approximate=True)
        zt_ref[j * LANES:(j + 1) * LANES, :] = z.T.astype(BF16)
    z = zt_ref[...]
    ssm_out = _dot(z, wa_ref[...]) * jax.nn.sigmoid(_dot(z, wb_ref[...]))
    mixed = att_ref[...].astype(F32) + gs_ref[...].astype(F32) * ssm_out
    y = alpha * x_ref[...] + _dot(mixed.astype(BF16), wo_ref[...])
    out_ref[...] = _layer_norm(y, g_ref[...], b_ref[...])


def _mix(yt, att, gs, x2d, wa, wb, wo, g, b, alpha, tm):
    T, D = x2d.shape
    ssm_w = yt.shape[0]
    row = lambda i: (i, 0)
    in_specs = [pl.BlockSpec((ssm_w, tm // LANES, LANES), lambda i: (0, i, 0)),
                pl.BlockSpec((tm, D), row), pl.BlockSpec((tm, D), row), pl.BlockSpec((tm, D), row),
                _full_spec(wa.shape), _full_spec(wb.shape), _full_spec(wo.shape),
                _full_spec(g.shape), _full_spec(b.shape)]
    return pl.pallas_call(functools.partial(_mix_kernel, alpha=alpha), name="mix", grid=(T // tm,), in_specs=in_specs,
                          out_specs=pl.BlockSpec((tm, D), row),
                          out_shape=jax.ShapeDtypeStruct((T, D), F32),
                          scratch_shapes=[pltpu.VMEM((tm, ssm_w), BF16)],
                          compiler_params=_cparams())(yt, att, gs, x2d, wa, wb, wo, g, b)


def _memkv_kernel(m_ref, wk_ref, wv_ref, k_ref, v_ref):
    mb = m_ref[...].astype(BF16)
    k_ref[...] = _dot(mb, wk_ref[...]).astype(BF16)
    v_ref[...] = _dot(mb, wv_ref[...]).astype(BF16)


def _memkv(mem2d, wk, wv, tm):
    R, D = mem2d.shape
    row = lambda i: (i, 0)
    return pl.pallas_call(_memkv_kernel, name="memkv", grid=(R // tm,),
                          in_specs=[pl.BlockSpec((tm, D), row), _full_spec(wk.shape), _full_spec(wv.shape)],
                          out_specs=(pl.BlockSpec((tm, D), row), pl.BlockSpec((tm, D), row)),
                          out_shape=(jax.ShapeDtypeStruct((R, D), BF16), jax.ShapeDtypeStruct((R, D), BF16)),
                          compiler_params=_cparams())(mem2d, wk, wv)


def _cross_kernel(x_ref, k_ref, v_ref, wq_ref, wo_ref, g_ref, b_ref, wr_ref, br_ref, out_ref, logit_ref,
                  cat_ref, *, alpha):
    x = x_ref[...]
    D = x.shape[1]
    hd = D // N_CROSS_HEADS
    q = (_dot(x.astype(BF16), wq_ref[...]) * (hd ** -0.5)).astype(BF16)
    for h in range(N_CROSS_HEADS):
        cols = slice(h * hd, (h + 1) * hd)
        s = _dot_nt(q[:, cols], k_ref[:, cols])
        m = jnp.max(s, axis=-1, keepdims=True)
        p = jnp.exp(s - m)
        w = (p / jnp.sum(p, axis=-1, keepdims=True)).astype(BF16)
        cat_ref[:, cols] = _dot(w, v_ref[:, cols]).astype(BF16)
    y = alpha * x + _dot(cat_ref[...], wo_ref[...])
    x2 = _layer_norm(y, g_ref[...], b_ref[...])
    out_ref[...] = x2
    logit_ref[...] = _dot(x2.astype(BF16), wr_ref[...]) + br_ref[...]


def _cross(x1, kc, vc, wq, wo, g, b, wr, br, alpha, S, tm):
    T, D = x1.shape
    n_mem = kc.shape[0] // (T // S)
    row = lambda i: (i, 0)
    per_b = S // tm
    in_specs = [pl.BlockSpec((tm, D), row),
                pl.BlockSpec((n_mem, D), lambda i: (i // per_b, 0)),
                pl.BlockSpec((n_mem, D), lambda i: (i // per_b, 0)),
                _full_spec(wq.shape), _full_spec(wo.shape), _full_spec(g.shape), _full_spec(b.shape),
                _full_spec(wr.shape), _full_spec(br.shape)]
    return pl.pallas_call(functools.partial(_cross_kernel, alpha=alpha), name="cross", grid=(T // tm,), in_specs=in_specs,
                          out_specs=(pl.BlockSpec((tm, D), row), pl.BlockSpec((tm, LANES), row)),
                          out_shape=(jax.ShapeDtypeStruct((T, D), F32), jax.ShapeDtypeStruct((T, LANES), F32)),
                          scratch_shapes=[pltpu.VMEM((tm, D), BF16)],
                          compiler_params=_cparams())(x1, kc, vc, wq, wo, g, b, wr, br)


SEG_ALIGN = 8


def _route_kernel(logit_ref, info_ref, seg_ref, count_ref, carry_ref):
    tm = logit_ref.shape[0]

    @pl.when(pl.program_id(0) == 0)
    def _():
        carry_ref[...] = jnp.zeros_like(carry_ref)

    vals = logit_ref[...]
    lane = lax.broadcasted_iota(jnp.int32, vals.shape, 1).astype(F32)
    sels, tops = [], []
    for _ in range(TOP_K):
        m = jnp.max(vals, axis=-1, keepdims=True)
        idx = jnp.min(jnp.where(vals == m, lane, float(LANES)), axis=-1, keepdims=True)
        sel = lane == idx
        vals = jnp.where(sel, -jnp.inf, vals)
        sels.append((sel, idx))
        tops.append(m)
    es = [jnp.exp(t - tops[0]) for t in tops]
    denom = es[0]
    for e in es[1:]:
        denom = denom + e
    onehot = jnp.zeros(vals.shape, F32)
    for sel, _ in sels:
        onehot = onehot + jnp.where(sel, 1.0, 0.0)
    r = lax.broadcasted_iota(jnp.int32, (tm, tm), 0)
    c = lax.broadcasted_iota(jnp.int32, (tm, tm), 1)
    tri = jnp.where(c < r, 1.0, 0.0).astype(BF16)
    cum = _dot(tri, onehot.astype(BF16))
    n_seg = jnp.floor((jnp.sum(onehot, axis=0, keepdims=True) + (SEG_ALIGN - 1.0)) * (1.0 / SEG_ALIGN)) * SEG_ALIGN
    a = lax.broadcasted_iota(jnp.int32, (LANES, LANES), 0)
    b = lax.broadcasted_iota(jnp.int32, (LANES, LANES), 1)
    upper = jnp.where(a < b, 1.0, 0.0).astype(BF16)
    seg_off = _dot(jnp.broadcast_to(n_seg, (8, LANES)).astype(BF16), upper)[0:1, :]
    carry = carry_ref[...]
    info = jnp.zeros(vals.shape, F32)
    for k, (sel, idx) in enumerate(sels):
        local = jnp.sum(jnp.where(sel, cum + seg_off, 0.0), axis=-1, keepdims=True)
        info = jnp.where(lane == k, idx, info)
        info = jnp.where(lane == TOP_K + k, local, info)
        info = jnp.where(lane == 2 * TOP_K + k, es[k] / denom, info)
    info_ref[...] = info
    row8 = lax.broadcasted_iota(jnp.int32, (8, LANES), 0)
    seg_ref[0] = jnp.where(row8 == 0, n_seg, jnp.where(row8 == 1, seg_off, jnp.where(row8 == 2, carry, 0.0)))
    carry_ref[...] = carry + n_seg
    count_ref[...] = carry_ref[...]


def _route(logits, tm):
    T = logits.shape[0]
    row = lambda i: (i, 0)
    return pl.pallas_call(_route_kernel, name="route", grid=(T // tm,),
                          in_specs=[pl.BlockSpec((tm, LANES), row)],
                          out_specs=(pl.BlockSpec((tm, LANES), row), pl.BlockSpec((1, 8, LANES), lambda i: (i, 0, 0)),
                                     pl.BlockSpec((1, LANES), lambda i: (0, 0))),
                          out_shape=(jax.ShapeDtypeStruct((T, LANES), F32),
                                     jax.ShapeDtypeStruct((T // tm, 8, LANES), F32),
                                     jax.ShapeDtypeStruct((1, LANES), F32)),
                          scratch_shapes=[pltpu.VMEM((1, LANES), F32)],
                          compiler_params=_cparams())(logits)


GU_TILE = 2 * LANES


SEG_SIZES = (256, 128, 64, 32, 16, 8)
SEG_LARGE = 64


def _segment_copies(seg_ref, local_ref, global_ref, sem, to_global):
    def pieces(n, off, start, sizes, prio):
        for sz in sizes:
            take = (n & sz) != 0

            @pl.when(take)
            def _(off=off, start=start, sz=sz):
                loc = local_ref.at[pl.ds(pl.multiple_of(off, SEG_ALIGN), sz)]
                glo = global_ref.at[pl.ds(pl.multiple_of(start, SEG_ALIGN), sz)]
                if to_global:
                    pltpu.make_async_copy(loc, glo, sem).start(priority=prio)
                else:
                    pltpu.make_async_copy(glo, loc, sem).start(priority=prio)

            step = jnp.where(take, sz, 0)
            off = off + step
            start = start + step

    n_large = sum(1 for sz in SEG_SIZES if sz >= SEG_LARGE)
    for e in range(N_EXPERTS):
        n = seg_ref[0, 0, e]
        off = seg_ref[0, 1, e]
        start = seg_ref[0, 2, e]

        @pl.when(n >= SEG_LARGE)
        def _(n=n, off=off, start=start, e=e):
            pieces(n, off, start, SEG_SIZES[:n_large], e % 2)

        large = n & ~(SEG_LARGE - 1)
        pieces(n, off + large, start + large, SEG_SIZES[n_large:], e % 2)


def _wait_rows(local_ref, global_ref, sem, n_rows):
    @pl.when(n_rows > 0)
    def _():
        n = pl.multiple_of(n_rows, SEG_ALIGN)
        pltpu.make_async_copy(global_ref.at[pl.ds(0, n)], local_ref.at[pl.ds(0, n)], sem).wait()


def _dispatch_kernel(ends_ref, seg_ref, x_ref, info_ref, xs_hbm, comp, tot, sem):
    i = pl.program_id(0)
    n_steps = pl.num_programs(0)
    slot = i % 2
    tm = x_ref.shape[0]
    cap = comp.shape[1]

    @pl.when(i == 0)
    def _():
        comp[1, 0:MOE_BLOCK, :] = jnp.zeros((MOE_BLOCK, comp.shape[2]), F32)
        for e in range(N_EXPERTS):
            end = ends_ref[e]
            prev = ends_ref[e - 1] if e > 0 else 0

            @pl.when(end > prev)
            def _(end=end):
                cp = pltpu.make_async_copy(comp.at[1, pl.ds(0, MOE_BLOCK)],
                                           xs_hbm.at[pl.ds(pl.multiple_of(end - MOE_BLOCK, MOE_BLOCK), MOE_BLOCK)],
                                           sem.at[1])
                cp.start()
                cp.wait()

        def tail(b):
            return pltpu.make_async_copy(comp.at[1, pl.ds(0, MOE_BLOCK)],
                                         xs_hbm.at[pl.ds(pl.multiple_of(b * MOE_BLOCK, MOE_BLOCK), MOE_BLOCK)],
                                         sem.at[0])
        first_unused = ends_ref[N_EXPERTS - 1] // MOE_BLOCK
        n_all = xs_hbm.shape[0] // MOE_BLOCK
        lax.fori_loop(first_unused, n_all, lambda b, c: (tail(b).start(), c)[1], 0)
        lax.fori_loop(first_unused, n_all, lambda b, c: (tail(b).wait(), c)[1], 0)
        tot[0] = 0
        tot[1] = 0

    _wait_rows(comp.at[slot], xs_hbm, sem.at[slot], tot[slot])
    pos = info_ref[...].T
    r = lax.broadcasted_iota(jnp.int32, (cap, tm), 0).astype(F32)
    sel = jnp.zeros((cap, tm), F32)
    for k in range(TOP_K):
        sel = jnp.where(r == pos[TOP_K + k:TOP_K + k + 1, :], 1.0, sel)
    sel = sel.astype(BF16)
    comp[slot] = _dot(sel, x_ref[...].astype(BF16))
    _segment_copies(seg_ref, comp.at[slot], xs_hbm, sem.at[slot], to_global=True)
    tot[slot] = seg_ref[0, 3, 0]

    @pl.when(i == n_steps - 1)
    def _():
        _wait_rows(comp.at[slot], xs_hbm, sem.at[slot], tot[slot])
        _wait_rows(comp.at[1 - slot], xs_hbm, sem.at[1 - slot], tot[1 - slot])


def _dispatch(ends_pad, seg, x2, info, n_slots, tm):
    T, D = x2.shape
    row = lambda i, ends: (i, 0)
    grid_spec = pltpu.PrefetchScalarGridSpec(
        num_scalar_prefetch=1, grid=(T // tm,),
        in_specs=[pl.BlockSpec((1, 4, LANES), lambda i, ends: (i, 0, 0), memory_space=pltpu.SMEM),
                  pl.BlockSpec((tm, D), row), pl.BlockSpec((tm, LANES), row)],
        out_specs=pl.BlockSpec(memory_space=pl.ANY),
        scratch_shapes=[pltpu.VMEM((2, TOP_K * tm + N_EXPERTS * SEG_ALIGN, D), F32), pltpu.SMEM((2,), jnp.int32),
                        pltpu.SemaphoreType.DMA((2,))])
    return pl.pallas_call(_dispatch_kernel, name="dispatch", grid_spec=grid_spec,
                          out_shape=jax.ShapeDtypeStruct((n_slots, D), F32),
                          compiler_params=_cparams())(ends_pad, seg, x2, info)


def _moe_kernel(bexp_ref, nused_ref, x_ref, wgu_ref, bgu_ref, wd_ref, bd_ref, y_ref, wgu_s, wd_s):
    j = pl.program_id(0)
    n_used = nused_ref[0]
    n_tiles = wgu_s.shape[1] // GU_TILE

    @pl.when(j >= n_used)
    def _():
        y_ref[...] = jnp.zeros_like(y_ref)

    @pl.when(j < n_used)
    def _():
        @pl.when((j == 0) | (bexp_ref[j] != bexp_ref[jnp.maximum(j - 1, 0)]))
        def _():
            k = lax.broadcasted_iota(jnp.int32, (GU_TILE, GU_TILE), 0)
            n = lax.broadcasted_iota(jnp.int32, (GU_TILE, GU_TILE), 1)
            perm = jnp.where(k == jnp.where(n < LANES, 2 * n, 2 * (n - LANES) + 1), 1.0, 0.0).astype(BF16)
            for t in range(n_tiles):
                cols = slice(t * GU_TILE, (t + 1) * GU_TILE)
                wgu_s[:, cols] = _dot(wgu_ref[0, :, cols].astype(BF16), perm).astype(BF16)
            wd_s[...] = wd_ref[0].astype(BF16)

        xb = x_ref[...].astype(BF16)
        hs = []
        for t in range(n_tiles):
            cols = slice(t * GU_TILE, (t + 1) * GU_TILE)
            gu = _dot(xb, wgu_s[:, cols]) + bgu_ref[0, :, cols]
            gate = jnp.minimum(gu[:, :LANES], SWIGLU_LIMIT)
            lin = jnp.clip(gu[:, LANES:], -SWIGLU_LIMIT, SWIGLU_LIMIT)
            hs.append((gate * jax.nn.sigmoid(SWIGLU_ALPHA * gate) * (lin + 1.0)).astype(BF16))
        y_ref[...] = _dot(jnp.concatenate(hs, axis=1), wd_s[...]) + bd_ref[0]


def _moe(bexp, n_used, xs, wgu, bgu, wd, bd):
    n_blocks = bexp.shape[0]
    D = xs.shape[1]
    F2 = wgu.shape[2]
    F = wd.shape[1]
    wmap = lambda j, be, nu: (be[jnp.minimum(j, nu[0] - 1)], 0, 0)
    blk = lambda j, be, nu: (j, 0)
    used = lambda j, be, nu: (jnp.minimum(j, nu[0] - 1), 0)
    in_specs = [pl.BlockSpec((MOE_BLOCK, D), used),
                pl.BlockSpec((1, D, F2), wmap), pl.BlockSpec((1, 1, F2), wmap),
                pl.BlockSpec((1, F, D), wmap), pl.BlockSpec((1, 1, D), wmap)]
    grid_spec = pltpu.PrefetchScalarGridSpec(
        num_scalar_prefetch=2, grid=(n_blocks,), in_specs=in_specs,
        out_specs=pl.BlockSpec((MOE_BLOCK, D), blk),
        scratch_shapes=[pltpu.VMEM((D, F2), BF16), pltpu.VMEM((F, D), BF16)])
    return pl.pallas_call(_moe_kernel, name="moe", grid_spec=grid_spec,
                          out_shape=jax.ShapeDtypeStruct((n_blocks * MOE_BLOCK, D), F32),
                          compiler_params=_cparams())(bexp, n_used, xs, wgu, bgu, wd, bd)


def _combine_kernel(seg_ref, seg_nxt_ref, ys_hbm, info_ref, x_ref, g_ref, b_ref, out_ref, comp, sem, *, alpha):
    i = pl.program_id(0)
    n_steps = pl.num_programs(0)
    slot = i % 2
    tm = x_ref.shape[0]
    cap = comp.shape[1]

    @pl.when(i == 0)
    def _():
        comp[...] = jnp.zeros_like(comp)
        _segment_copies(seg_ref, comp.at[0], ys_hbm, sem.at[0], to_global=False)

    @pl.when(i + 1 < n_steps)
    def _():
        _segment_copies(seg_nxt_ref, comp.at[1 - slot], ys_hbm, sem.at[1 - slot], to_global=False)

    _wait_rows(comp.at[slot], ys_hbm, sem.at[slot], seg_ref[0, 3, 0])
    info = info_ref[...]
    lane = lax.broadcasted_iota(jnp.int32, (tm, cap), 1).astype(F32)
    wmat = jnp.zeros((tm, cap), F32)
    for k in range(TOP_K):
        wmat = jnp.where(lane == info[:, TOP_K + k:TOP_K + k + 1], info[:, 2 * TOP_K + k:2 * TOP_K + k + 1], wmat)
    ffn = _dot(wmat.astype(BF16), comp[slot].astype(BF16))
    out_ref[...] = _layer_norm(alpha * x_ref[...] + ffn, g_ref[...], b_ref[...])


def _combine(seg, ys, info, x2, g, b, alpha, tm):
    T, D = x2.shape
    n_t = T // tm
    row = lambda i: (i, 0)
    seg_spec = lambda m: pl.BlockSpec((1, 4, LANES), m, memory_space=pltpu.SMEM)
    in_specs = [seg_spec(lambda i: (i, 0, 0)), seg_spec(lambda i: (jnp.minimum(i + 1, n_t - 1), 0, 0)),
                pl.BlockSpec(memory_space=pl.ANY),
                pl.BlockSpec((tm, LANES), row), pl.BlockSpec((tm, D), row),
                _full_spec(g.shape), _full_spec(b.shape)]
    cap = TOP_K * tm + N_EXPERTS * SEG_ALIGN
    return pl.pallas_call(functools.partial(_combine_kernel, alpha=alpha), name="combine", grid=(n_t,), in_specs=in_specs,
                          out_specs=pl.BlockSpec((tm, D), row),
                          out_shape=jax.ShapeDtypeStruct((T, D), F32),
                          scratch_shapes=[pltpu.VMEM((2, cap, D), F32), pltpu.SemaphoreType.DMA((2,))],
                          compiler_params=_cparams())(seg, seg, ys, info, x2, g, b)


def _tile(n, pref):
    return pref if n % pref == 0 else n


def _layer(x2d, mem2d, pos2d, B, S, depth, w_in, sinks, w_attn_o, lam_re, lam_im, log_dt, b_re, b_im,
           c_re, c_im, d_skip, w_glu_a, w_glu_b, w_out, ln1_g, ln1_b, wq_c, wk_c, wv_c, wo_c, ln2_g,
           ln2_b, w_router, b_router, w_gate_up, b_gate_up, w_down, b_down, ln3_g, ln3_b):
    T, D = x2d.shape
    alpha = (2 * depth) ** 0.25
    rep = N_Q_HEADS // N_KV_HEADS
    q_w = N_Q_HEADS * HEAD_DIM
    kv_w = N_KV_HEADS * HEAD_DIM
    ssm_w = SSM_GROUP * SSM_GROUPS
    P = SSM_STATE

    o_k, o_v, o_s = q_w, q_w + kv_w, q_w + 2 * kv_w
    o_ga, o_gs = o_s + ssm_w, o_s + ssm_w + D
    wq = w_in[:, :o_k].reshape(D, N_KV_HEADS, rep, HEAD_DIM).transpose(0, 2, 1, 3).reshape(D, q_w).astype(BF16)
    wk = w_in[:, o_k:o_v].astype(BF16)
    wv = w_in[:, o_v:o_s].astype(BF16)
    wu = w_in[:, o_s:o_ga].astype(BF16)
    wga = w_in[:, o_ga:o_gs].astype(BF16)
    wgs = w_in[:, o_gs:].astype(BF16)
    wo_attn = w_attn_o.reshape(N_KV_HEADS, rep, HEAD_DIM, D).transpose(1, 0, 2, 3).reshape(q_w, D).astype(BF16)
    half = HEAD_DIM // 2
    inv_freq = jnp.power(ROPE_THETA, -jnp.arange(half, dtype=F32) / half)
    invf = jnp.tile(inv_freq, LANES // half)[None, :]
    sink_rows = jnp.repeat(sinks.astype(F32).reshape(N_KV_HEADS, rep), WINDOW, axis=1)
    sink_cols = jnp.full((N_KV_HEADS, rep * WINDOW, 2 * WINDOW), NEG_BIG, F32).at[:, :, 0].set(sink_rows)

    tm1 = _tile(T, 1024)
    q2, k, v, ut, ga, gs = _inproj(x2d, pos2d, invf, wq, wk, wv, wu, wga, wgs, tm1)
    att = _swa(q2, k, v, sink_cols, wo_attn, ga, B, S, _tile(S, 512))

    lam_row = jnp.stack([jnp.concatenate([lam_re, lam_re], -1), jnp.concatenate([lam_im, lam_im], -1)], 1)
    lam_col = jnp.swapaxes(lam_row, 1, 2)
    bt_re = jnp.swapaxes(b_re, 1, 2)
    bt_im = jnp.swapaxes(b_im, 1, 2)
    bt_a = jnp.concatenate([bt_re, bt_im], -1)
    bt_b = jnp.concatenate([-bt_im, bt_re], -1)
    ct_re = jnp.swapaxes(c_re, 1, 2)
    ct_im = jnp.swapaxes(c_im, 1, 2)
    ct_a = jnp.concatenate([ct_re, -ct_im], 1)
    ct_b = jnp.concatenate([-ct_im, -ct_re], 1)
    dsk = jnp.repeat(d_skip.reshape(SSM_GROUPS, SSM_GROUP), SSM_GROUP, axis=1)[:, :, None]
    yt = _ssm(ut, lam_row, lam_col, log_dt.reshape(SSM_GROUPS, 1, 1), bt_a, bt_b, c_re, c_im, ct_a, ct_b, dsk,
              S // LANES)

    x1 = _mix(yt, att, gs, x2d, w_glu_a.astype(BF16), w_glu_b.astype(BF16), w_out.astype(BF16),
              ln1_g[None, :], ln1_b[None, :], alpha, _tile(T, 1024))

    kc, vc = _memkv(mem2d, wk_c.astype(BF16), wv_c.astype(BF16), _tile(mem2d.shape[0], 512))
    wr = jnp.zeros((D, LANES), F32).at[:, :N_EXPERTS].set(w_router).astype(BF16)
    br = jnp.full((1, LANES), NEG_BIG, F32).at[0, :N_EXPERTS].set(b_router)
    x2, logits = _cross(x1, kc, vc, wq_c.astype(BF16), wo_c.astype(BF16), ln2_g[None, :], ln2_b[None, :],
                        wr, br, alpha, S, _tile(S, 1024))

    tm_r = _tile(T, ROUTE_TILE)
    n_tiles = T // tm_r
    info, segf, counts = _route(logits, tm_r)
    total = counts[0].astype(jnp.int32)
    padded = (total + MOE_BLOCK - 1) // MOE_BLOCK * MOE_BLOCK
    ends_lane = jnp.cumsum(padded)
    start_lane = ends_lane - padded
    ends_pad = ends_lane[:N_EXPERTS]
    n_blocks = -(-(T * TOP_K + n_tiles * N_EXPERTS * (SEG_ALIGN - 1)) // MOE_BLOCK) + N_EXPERTS
    segi = segf.astype(jnp.int32)
    seg_n, seg_off, seg_start = segi[:, 0, :], segi[:, 1, :], segi[:, 2, :] + start_lane[None, :]
    seg_tot = jnp.broadcast_to(jnp.sum(seg_n, axis=1, keepdims=True), seg_n.shape)
    seg = jnp.stack([seg_n, seg_off, seg_start, seg_tot], axis=1)
    block_start = jnp.arange(n_blocks, dtype=jnp.int32) * MOE_BLOCK
    bexp = jnp.minimum(jnp.sum(block_start[:, None] >= ends_pad[None, :], axis=1), N_EXPERTS - 1).astype(jnp.int32)
    n_used = (ends_pad[-1] // MOE_BLOCK).astype(jnp.int32)[None]
    n_gu_tiles = b_gate_up.shape[1] // GU_TILE
    bgu = b_gate_up.reshape(N_EXPERTS, n_gu_tiles, LANES, 2).transpose(0, 1, 3, 2).reshape(N_EXPERTS, 1, -1)
    xs = _dispatch(ends_pad, seg, x2, info, n_blocks * MOE_BLOCK, tm_r)
    ys = _moe(bexp, n_used, xs, w_gate_up, bgu, w_down, b_down[:, None, :])
    return _combine(seg, ys, info, x2, ln3_g[None, :], ln3_b[None, :], alpha, tm_r)


def kernel(x, mem, positions, w_in, sinks, w_attn_o, lam_re, lam_im, log_dt, b_re, b_im, c_re, c_im, d_skip,
           w_glu_a, w_glu_b, w_out, ln1_g, ln1_b, wq_c, wk_c, wv_c, wo_c, ln2_g, ln2_b, w_router, b_router,
           w_gate_up, b_gate_up, w_down, b_down, ln3_g, ln3_b):
    B, S, D = x.shape
    depth = w_in.shape[0]
    x2d = x.reshape(B * S, D)
    mem2d = mem.reshape(-1, D)
    pos2d = positions.reshape(B * S, 1)
    per_layer = (w_in, sinks, w_attn_o, lam_re, lam_im, log_dt, b_re, b_im, c_re, c_im, d_skip, w_glu_a,
                 w_glu_b, w_out, ln1_g, ln1_b, wq_c, wk_c, wv_c, wo_c, ln2_g, ln2_b, w_router, b_router,
                 w_gate_up, b_gate_up, w_down, b_down, ln3_g, ln3_b)
    for l in range(depth):
        x2d = _layer(x2d, mem2d, pos2d, B, S, depth, *(w[l] for w in per_layer))
    return x2d.reshape(B, S, D)
```

```python
import functools

import jax
import jax.numpy as jnp
from jax import lax
from jax.experimental import pallas as pl
from jax.experimental.pallas import tpu as pltpu

N_Q_HEADS = 16
N_KV_HEADS = 2
HEAD_DIM = 64
WINDOW = 128
ROPE_THETA = 10000.0
SSM_GROUP = 16
SSM_GROUPS = 32
SSM_STATE = 64
N_CROSS_HEADS = 4
N_EXPERTS = 32
TOP_K = 4
SWIGLU_ALPHA = 1.702
SWIGLU_LIMIT = 7.0
MOE_BLOCK = 512
ROUTE_TILE = 256
LN_EPS = 1e-5

LANES = 128
VMEM_LIMIT_BYTES = 56 * 1024 * 1024

NEG_BIG = -1e30
BF16 = jnp.bfloat16
F32 = jnp.float32
HI = lax.Precision.HIGHEST


def _cparams(n_axes=1):
    return pltpu.CompilerParams(dimension_semantics=("arbitrary",) * n_axes,
                                vmem_limit_bytes=VMEM_LIMIT_BYTES)


def _full_spec(shape):
    n = len(shape)
    return pl.BlockSpec(shape, lambda *_: (0,) * n, pipeline_mode=pl.Buffered(1))


def _dot(a, b):
    return jnp.dot(a, b, preferred_element_type=F32)


def _dot_nt(a, b):
    return lax.dot_general(a, b, (((1,), (1,)), ((), ())), preferred_element_type=F32)


def _layer_norm(y, g, b):
    mu = jnp.mean(y, axis=-1, keepdims=True)
    d = y - mu
    var = jnp.mean(d * d, axis=-1, keepdims=True)
    return d * lax.rsqrt(var + LN_EPS) * g + b


def _rope(t, cos, sin_signed, first_half):
    half = HEAD_DIM // 2
    partner = jnp.where(first_half, pltpu.roll(t, LANES - half, axis=1), pltpu.roll(t, half, axis=1))
    return t * cos + partner * sin_signed


def _inproj_kernel(x_ref, pos_ref, invf_ref, wq_ref, wk_ref, wv_ref, wu_ref, wga_ref, wgs_ref,
                   q2_ref, k_ref, v_ref, ut_ref, ga_ref, gs_ref):
    tm = x_ref.shape[0]
    xb = x_ref[...].astype(BF16)
    ang = pos_ref[...].astype(F32) * invf_ref[...]
    cos = jnp.cos(ang)
    sin = jnp.sin(ang)
    first_half = (lax.broadcasted_iota(jnp.int32, (tm, LANES), 1) % HEAD_DIM) < (HEAD_DIM // 2)
    first_half_w = (lax.broadcasted_iota(jnp.int32, (WINDOW, LANES), 1) % HEAD_DIM) < (HEAD_DIM // 2)
    sin_signed = jnp.where(first_half, -sin, sin)

    q = _dot(xb, wq_ref[...])
    n_rep = q.shape[1] // LANES
    scale = HEAD_DIM ** -0.5
    for j in range(tm // WINDOW):
        rows = slice(j * WINDOW, (j + 1) * WINDOW)
        for r in range(n_rep):
            t = _rope(q[rows, r * LANES:(r + 1) * LANES], cos[rows], sin_signed[rows], first_half_w)
            base = (j * n_rep + r) * WINDOW
            q2_ref[base:base + WINDOW, :] = (t * scale).astype(BF16)
    k_ref[...] = _rope(_dot(xb, wk_ref[...]), cos, sin_signed, first_half).astype(BF16)
    v_ref[...] = _dot(xb, wv_ref[...]).astype(BF16)
    u = _dot(xb, wu_ref[...])
    for j in range(tm // LANES):
        ut_ref[j] = u[j * LANES:(j + 1) * LANES, :].T
    ga_ref[...] = jax.nn.sigmoid(_dot(xb, wga_ref[...])).astype(BF16)
    gs_ref[...] = jax.nn.sigmoid(_dot(xb, wgs_ref[...])).astype(BF16)


def _inproj(x2d, pos2d, invf, wq, wk, wv, wu, wga, wgs, tm):
    T, D = x2d.shape
    n_rep = wq.shape[1] // LANES
    ssm_w = wu.shape[1]
    row = lambda i: (i, 0)
    out_shape = (
        jax.ShapeDtypeStruct((T * n_rep, LANES), BF16),
        jax.ShapeDtypeStruct((T, LANES), BF16),
        jax.ShapeDtypeStruct((T, LANES), BF16),
        jax.ShapeDtypeStruct((T // LANES, ssm_w, LANES), F32),
        jax.ShapeDtypeStruct((T, D), BF16),
        jax.ShapeDtypeStruct((T, D), BF16),
    )
    in_specs = [pl.BlockSpec((tm, D), row), pl.BlockSpec((tm, 1), row), _full_spec(invf.shape),
                _full_spec(wq.shape), _full_spec(wk.shape), _full_spec(wv.shape),
                _full_spec(wu.shape), _full_spec(wga.shape), _full_spec(wgs.shape)]
    out_specs = (pl.BlockSpec((tm * n_rep, LANES), row), pl.BlockSpec((tm, LANES), row),
                 pl.BlockSpec((tm, LANES), row),
                 pl.BlockSpec((tm // LANES, ssm_w, LANES), lambda i: (i, 0, 0)),
                 pl.BlockSpec((tm, D), row), pl.BlockSpec((tm, D), row))
    return pl.pallas_call(_inproj_kernel, name="inproj", grid=(T // tm,), in_specs=in_specs, out_specs=out_specs,
                          out_shape=out_shape, compiler_params=_cparams())(
        x2d, pos2d, invf, wq, wk, wv, wu, wga, wgs)


def _swa_kernel(q2_ref, kc_ref, kp_ref, vc_ref, vp_ref, fill_ref, wo_ref, ga_ref, out_ref, cat_ref):
    i = pl.program_id(1)
    tq = kc_ref.shape[0]
    n_sub = tq // WINDOW
    rep = N_Q_HEADS // N_KV_HEADS
    rows_all = rep * WINDOW
    kfull = jnp.concatenate([kp_ref[...], kc_ref[...]], axis=0)
    vfull = jnp.concatenate([vp_ref[...], vc_ref[...]], axis=0)
    lane = lax.broadcasted_iota(jnp.int32, (2 * WINDOW, LANES), 1)
    qi = lax.broadcasted_iota(jnp.int32, (rows_all, 2 * WINDOW), 0) % WINDOW
    ci = lax.broadcasted_iota(jnp.int32, (rows_all, 2 * WINDOW), 1)
    local = (ci > qi) & (ci <= qi + WINDOW)
    out_lane = lax.broadcasted_iota(jnp.int32, (rows_all, LANES), 1)
    band_row = lax.broadcasted_iota(jnp.int32, (2 * WINDOW, LANES), 0)
    for j in range(n_sub):
        qs = q2_ref[j * rows_all:(j + 1) * rows_all, :]
        kb = kfull[j * WINDOW:(j + 2) * WINDOW, :]
        vb = vfull[j * WINDOW:(j + 2) * WINDOW, :]
        vb = jnp.where(band_row == 0, jnp.zeros_like(vb), vb)
        mask = local
        if j == 0:
            mask = mask & ((ci >= WINDOW) | (i > 0))
        o = None
        for g in range(N_KV_HEADS):
            in_group = (lane >= g * HEAD_DIM) & (lane < (g + 1) * HEAD_DIM)
            kg = jnp.where(in_group, kb, jnp.zeros_like(kb))
            s = _dot_nt(qs, kg)
            s = jnp.where(mask, s, fill_ref[g])
            m = jnp.max(s, axis=-1, keepdims=True)
            p = jnp.exp(s - m)
            denom = jnp.sum(p, axis=-1, keepdims=True)
            og = _dot(p.astype(BF16), vb) * (1.0 / denom)
            o = og if o is None else jnp.where(out_lane < g * HEAD_DIM, o, og)
        ob = o.astype(BF16)
        for r in range(rep):
            cat_ref[j * WINDOW:(j + 1) * WINDOW, r * LANES:(r + 1) * LANES] = ob[r * WINDOW:(r + 1) * WINDOW, :]
    attn = _dot(cat_ref[...], wo_ref[...])
    out_ref[...] = (attn * ga_ref[...].astype(F32)).astype(BF16)


def _swa(q2, k, v, sink_cols, wo, ga, B, S, tq):
    T, D = ga.shape
    rep = N_Q_HEADS // N_KV_HEADS
    n_i = S // tq
    n_sub = tq // WINDOW
    in_specs = [
        pl.BlockSpec((tq * rep, LANES), lambda b, i: (b * n_i + i, 0)),
        pl.BlockSpec((tq, LANES), lambda b, i: (b * n_i + i, 0)),
        pl.BlockSpec((WINDOW, LANES), lambda b, i: (b * (S // WINDOW) + jnp.maximum(i * n_sub - 1, 0), 0)),
        pl.BlockSpec((tq, LANES), lambda b, i: (b * n_i + i, 0)),
        pl.BlockSpec((WINDOW, LANES), lambda b, i: (b * (S // WINDOW) + jnp.maximum(i * n_sub - 1, 0), 0)),
        _full_spec(sink_cols.shape), _full_spec(wo.shape),
        pl.BlockSpec((tq, D), lambda b, i: (b * n_i + i, 0)),
    ]
    return pl.pallas_call(
        _swa_kernel, name="swa", grid=(B, n_i), in_specs=in_specs,
        out_specs=pl.BlockSpec((tq, D), lambda b, i: (b * n_i + i, 0)),
        out_shape=jax.ShapeDtypeStruct((T, D), BF16),
        scratch_shapes=[pltpu.VMEM((tq, D), BF16)],
        compiler_params=_cparams(2))(q2, k, k, v, v, sink_cols, wo, ga)


def _ssm_prep_kernel(lam_row_ref, lam_col_ref, dt_ref, bt_a_ref, bt_b_ref, c_re_ref, c_im_ref,
                     ct_a_ref, ct_b_ref, dsk_ref, toep_ref, wst_ref, wout_ref, apow_ref):
    L = LANES
    P = SSM_STATE
    dt = jnp.exp(dt_ref[0])
    lr2 = lam_row_ref[0, 0:1, :]
    li2 = lam_row_ref[0, 1:2, :]
    mag = jnp.exp(lr2 * dt)
    ar = mag * jnp.cos(li2 * dt)
    ai = mag * jnp.sin(li2 * dt)
    den = lr2 * lr2 + li2 * li2
    f_re = ((ar - 1.0) * lr2 + ai * li2) / den
    f_im = (ai * lr2 - (ar - 1.0) * li2) / den
    bt_a = bt_a_ref[0]
    bt_b = bt_b_ref[0]
    bb_a = f_re * bt_a + f_im * bt_b
    bb_b = f_re * bt_b - f_im * bt_a

    tau_rev = (L - 1 - lax.broadcasted_iota(jnp.int32, (L, 2 * P), 0)).astype(F32)
    g_mag = jnp.exp(lr2 * dt * tau_rev)
    g_re = g_mag * jnp.cos(li2 * dt * tau_rev)
    g_im = g_mag * jnp.sin(li2 * dt * tau_rev)
    for i in range(SSM_GROUP):
        wst_ref[0, i * L:(i + 1) * L, :] = (g_re * bb_a[i:i + 1, :] + g_im * bb_b[i:i + 1, :]).astype(BF16)

    lane2 = lax.broadcasted_iota(jnp.int32, (1, 2 * P), 1)
    for kk in range(4):
        n = float(L * (1 << kk))
        pm = jnp.exp(lr2 * dt * n)
        p_re = pm * jnp.cos(li2 * dt * n)
        p_im = pm * jnp.sin(li2 * dt * n)
        apow_ref[0, 2 * kk:2 * kk + 1, :] = p_re
        apow_ref[0, 2 * kk + 1:2 * kk + 2, :] = jnp.where(lane2 < P, -p_im, p_im)

    lrc = lam_col_ref[0, :, 0:1]
    lic = lam_col_ref[0, :, 1:2]
    tau = lax.broadcasted_iota(jnp.int32, (2 * P, L), 1).astype(F32)
    e0_mag = jnp.exp(lrc * dt * tau)
    e0_re = e0_mag * jnp.cos(lic * dt * tau)
    e0_im = e0_mag * jnp.sin(lic * dt * tau)
    e1_mag = jnp.exp(lrc * dt * (tau + 1.0))
    e1_re = e1_mag * jnp.cos(lic * dt * (tau + 1.0))
    e1_im = e1_mag * jnp.sin(lic * dt * (tau + 1.0))

    ct_a = ct_a_ref[0]
    ct_b = ct_b_ref[0]
    for o in range(SSM_GROUP):
        wout_ref[0, :, o * L:(o + 1) * L] = (ct_a[:, o:o + 1] * e1_re + ct_b[:, o:o + 1] * e1_im).astype(BF16)

    c_re = c_re_ref[0]
    c_im = c_im_ref[0]
    bb_re = bb_a[:, :P]
    bb_im = bb_a[:, P:]
    m_re = (c_re[:, None, :] * bb_re[None, :, :] - c_im[:, None, :] * bb_im[None, :, :])
    m_im = (c_re[:, None, :] * bb_im[None, :, :] + c_im[:, None, :] * bb_re[None, :, :])
    m_re = m_re.reshape(SSM_GROUP * SSM_GROUP, P)
    m_im = m_im.reshape(SSM_GROUP * SSM_GROUP, P)
    kt = (jnp.dot(m_re, e0_re[:P, :], precision=HI, preferred_element_type=F32)
          - jnp.dot(m_im, e0_im[:P, :], precision=HI, preferred_element_type=F32))
    rowi = lax.broadcasted_iota(jnp.int32, (SSM_GROUP * SSM_GROUP, L), 0)
    coli = lax.broadcasted_iota(jnp.int32, (SSM_GROUP * SSM_GROUP, L), 1)
    dsk = dsk_ref[0]
    kt = kt + jnp.where((coli == 0) & ((rowi // SSM_GROUP) == (rowi % SSM_GROUP)), dsk, 0.0)

    cc = lax.broadcasted_iota(jnp.int32, (L, L), 0)
    cp = lax.broadcasted_iota(jnp.int32, (L, L), 1)
    causal = cp >= cc
    for o in range(SSM_GROUP):
        for i in range(SSM_GROUP):
            kv = kt[o * SSM_GROUP + i:o * SSM_GROUP + i + 1, :]
            blk = pltpu.roll(jnp.broadcast_to(kv, (L, L)), 0, axis=1, stride=1, stride_axis=0)
            blk = jnp.where(causal, blk, 0.0)
            toep_ref[0, i * L:(i + 1) * L, o * L:(o + 1) * L] = blk.astype(BF16)


def _ssm_scan_kernel(ut_ref, toep_ref, wst_ref, wout_ref, apow_ref, yt_ref, *, n_chunks):
    nb = ut_ref.shape[0]
    P = SSM_STATE
    ut2 = ut_ref.reshape(nb * SSM_GROUP, LANES)
    lhs = jnp.concatenate([ut2[pl.ds(i, nb, stride=SSM_GROUP), :] for i in range(SSM_GROUP)],
                          axis=1).astype(BF16)
    y = _dot(lhs, toep_ref[0])
    st = _dot(lhs, wst_ref[0])
    srow = lax.broadcasted_iota(jnp.int32, (nb, 2 * P), 0) % n_chunks
    kk = 0
    while (1 << kk) < n_chunks:
        d = 1 << kk
        pa = apow_ref[0, 2 * kk:2 * kk + 1, :]
        pb = apow_ref[0, 2 * kk + 1:2 * kk + 2, :]
        prev = pltpu.roll(st, d, axis=0)
        prev = prev * pa + pltpu.roll(prev, P, axis=1) * pb
        st = st + jnp.where(srow >= d, prev, 0.0)
        kk += 1
    h0 = jnp.where(srow >= 1, pltpu.roll(st, 1, axis=0), 0.0)
    y = y + _dot(h0.astype(BF16), wout_ref[0])
    for o in range(SSM_GROUP):
        yt_ref[o] = y[:, o * LANES:(o + 1) * LANES]


def _ssm_kernel(*refs, n_chunks):
    params, (ut_ref, yt_ref), ops = refs[:10], refs[10:12], refs[12:]
    sets = (ops[0::2], ops[1::2])
    g = pl.program_id(0)

    @pl.when(g == 0)
    def _():
        _ssm_prep_kernel(*params, *sets[0])

    for p in range(2):
        @pl.when((g > 0) & (g % 2 == p))
        def _(p=p):
            _ssm_prep_kernel(*params, *sets[p])
            _ssm_scan_kernel(ut_ref, *sets[1 - p], yt_ref, n_chunks=n_chunks)


def _ssm(ut, lam_row, lam_col, log_dt, bt_a, bt_b, c_re, c_im, ct_a, ct_b, dsk, n_chunks):
    nb, ssm_w, _ = ut.shape
    G = SSM_GROUPS
    KW = SSM_GROUP * LANES
    params = [lam_row, lam_col, log_dt, bt_a, bt_b, c_re, c_im, ct_a, ct_b, dsk]
    build = lambda g: (jnp.minimum(g, G - 1), 0, 0)
    in_specs = [pl.BlockSpec((1,) + a.shape[1:], build) for a in params]
    in_specs.append(pl.BlockSpec((nb, SSM_GROUP, LANES), lambda g: (0, jnp.maximum(g - 1, 0), 0)))
    op_shapes = [pltpu.VMEM((1, KW, KW), BF16), pltpu.VMEM((1, KW, 2 * SSM_STATE), BF16),
                 pltpu.VMEM((1, 2 * SSM_STATE, KW), BF16), pltpu.VMEM((1, 8, 2 * SSM_STATE), F32)]
    return pl.pallas_call(
        functools.partial(_ssm_kernel, n_chunks=n_chunks), name="ssm", grid=(G + 1,), in_specs=in_specs,
        out_specs=pl.BlockSpec((SSM_GROUP, nb, LANES), lambda g: (jnp.maximum(g - 1, 0), 0, 0)),
        out_shape=jax.ShapeDtypeStruct((ssm_w, nb, LANES), F32),
        scratch_shapes=[s for s in op_shapes for _ in range(2)],
        compiler_params=_cparams())(*params, ut)


def _mix_kernel(yt_ref, att_ref, gs_ref, x_ref, wa_ref, wb_ref, wo_ref, g_ref, b_ref, out_ref, zt_ref,
                *, alpha):
    n_ch, n_blk = yt_ref.shape[0], yt_ref.shape[1]
    yt2 = yt_ref.reshape(n_ch * n_blk, LANES)
    for j in range(n_blk):
        y = yt2[pl.ds(j, n_ch, stride=n_blk), :]
        z = jax.nn.gelu(y, approximate=True)
        zt_ref[j * LANES:(j + 1) * LANES, :] = z.T.astype(BF16)
    z = zt_ref[...]
    ssm_out = _dot(z, wa_ref[...]) * jax.nn.sigmoid(_dot(z, wb_ref[...]))
    mixed = att_ref[...].astype(F32) + gs_ref[...].astype(F32) * ssm_out
    y = alpha * x_ref[...] + _dot(mixed.astype(BF16), wo_ref[...])
    out_ref[...] = _layer_norm(y, g_ref[...], b_ref[...])


def _mix(yt, att, gs, x2d, wa, wb, wo, g, b, alpha, tm):
    T, D = x2d.shape
    ssm_w = yt.shape[0]
    row = lambda i: (i, 0)
    in_specs = [pl.BlockSpec((ssm_w, tm // LANES, LANES), lambda i: (0, i, 0)),
                pl.BlockSpec((tm, D), row), pl.BlockSpec((tm, D), row), pl.BlockSpec((tm, D), row),
                _full_spec(wa.shape), _full_spec(wb.shape), _full_spec(wo.shape),
                _full_spec(g.shape), _full_spec(b.shape)]
    return pl.pallas_call(functools.partial(_mix_kernel, alpha=alpha), name="mix", grid=(T // tm,), in_specs=in_specs,
                          out_specs=pl.BlockSpec((tm, D), row),
                          out_shape=jax.ShapeDtypeStruct((T, D), F32),
                          scratch_shapes=[pltpu.VMEM((tm, ssm_w), BF16)],
                          compiler_params=_cparams())(yt, att, gs, x2d, wa, wb, wo, g, b)


def _memkv_kernel(m_ref, wk_ref, wv_ref, k_ref, v_ref):
    mb = m_ref[...].astype(BF16)
    k_ref[...] = _dot(mb, wk_ref[...]).astype(BF16)
    v_ref[...] = _dot(mb, wv_ref[...]).astype(BF16)


def _memkv(mem2d, wk, wv, tm):
    R, D = mem2d.shape
    row = lambda i: (i, 0)
    return pl.pallas_call(_memkv_kernel, name="memkv", grid=(R // tm,),
                          in_specs=[pl.BlockSpec((tm, D), row), _full_spec(wk.shape), _full_spec(wv.shape)],
                          out_specs=(pl.BlockSpec((tm, D), row), pl.BlockSpec((tm, D), row)),
                          out_shape=(jax.ShapeDtypeStruct((R, D), BF16), jax.ShapeDtypeStruct((R, D), BF16)),
                          compiler_params=_cparams())(mem2d, wk, wv)


def _cross_kernel(x_ref, k_ref, v_ref, wq_ref, wo_ref, g_ref, b_ref, wr_ref, br_ref, out_ref, logit_ref,
                  cat_ref, *, alpha):
    x = x_ref[...]
    D = x.shape[1]
    hd = D // N_CROSS_HEADS
    q = (_dot(x.astype(BF16), wq_ref[...]) * (hd ** -0.5)).astype(BF16)
    for h in range(N_CROSS_HEADS):
        cols = slice(h * hd, (h + 1) * hd)
        s = _dot_nt(q[:, cols], k_ref[:, cols])
        m = jnp.max(s, axis=-1, keepdims=True)
        p = jnp.exp(s - m)
        w = (p / jnp.sum(p, axis=-1, keepdims=True)).astype(BF16)
        cat_ref[:, cols] = _dot(w, v_ref[:, cols]).astype(BF16)
    y = alpha * x + _dot(cat_ref[...], wo_ref[...])
    x2 = _layer_norm(y, g_ref[...], b_ref[...])
    out_ref[...] = x2
    logit_ref[...] = _dot_nt(wr_ref[...], x2.astype(BF16)) + br_ref[...]


def _cross(x1, kc, vc, wq, wo, g, b, wr, br, alpha, S, tm):
    T, D = x1.shape
    n_mem = kc.shape[0] // (T // S)
    row = lambda i: (i, 0)
    per_b = S // tm
    in_specs = [pl.BlockSpec((tm, D), row),
                pl.BlockSpec((n_mem, D), lambda i: (i // per_b, 0)),
                pl.BlockSpec((n_mem, D), lambda i: (i // per_b, 0)),
                _full_spec(wq.shape), _full_spec(wo.shape), _full_spec(g.shape), _full_spec(b.shape),
                _full_spec(wr.shape), _full_spec(br.shape)]
    return pl.pallas_call(functools.partial(_cross_kernel, alpha=alpha), name="cross", grid=(T // tm,), in_specs=in_specs,
                          out_specs=(pl.BlockSpec((tm, D), row), pl.BlockSpec((LANES, tm), lambda i: (0, i))),
                          out_shape=(jax.ShapeDtypeStruct((T, D), F32), jax.ShapeDtypeStruct((LANES, T), F32)),
                          scratch_shapes=[pltpu.VMEM((tm, D), BF16)],
                          compiler_params=_cparams())(x1, kc, vc, wq, wo, g, b, wr, br)


SEG_ALIGN = 8


def _route_kernel(lt_ref, info_ref, seg_ref, count_ref, carry_ref):
    tm = lt_ref.shape[1]
    E = N_EXPERTS

    @pl.when(pl.program_id(0) == 0)
    def _():
        carry_ref[...] = jnp.zeros_like(carry_ref)

    l = lt_ref[0:E, :]
    erow = lax.broadcasted_iota(jnp.int32, (E, tm), 0)
    rank = jnp.zeros((E, tm), F32)
    for e2 in range(E):
        other = l[e2:e2 + 1, :]
        tie = jnp.where(erow > e2, 1.0, 0.0)
        rank = rank + jnp.where(other > l, 1.0, jnp.where(other == l, tie, 0.0))
    chosen = rank < float(TOP_K)
    onehot = jnp.where(chosen, 1.0, 0.0)
    p = jnp.where(chosen, jnp.exp(l - jnp.max(l, axis=0, keepdims=True)), 0.0)
    gate = p / jnp.sum(p, axis=0, keepdims=True)
    a = lax.broadcasted_iota(jnp.int32, (tm, tm), 0)
    b = lax.broadcasted_iota(jnp.int32, (tm, tm), 1)
    cum = _dot(onehot.astype(BF16), jnp.where(a < b, 1.0, 0.0).astype(BF16))
    n_seg = jnp.floor((jnp.sum(onehot, axis=1, keepdims=True) + (SEG_ALIGN - 1.0)) * (1.0 / SEG_ALIGN)) * SEG_ALIGN
    n_seg = jnp.broadcast_to(n_seg, (E, LANES))
    erow_l = lax.broadcasted_iota(jnp.int32, (E, LANES), 0)
    seg_off = jnp.zeros((E, LANES), F32)
    for e2 in range(E - 1):
        seg_off = seg_off + jnp.where(erow_l > e2, n_seg[e2:e2 + 1, :], 0.0)
    pos = cum + seg_off[:, 0:1]
    rows = [jnp.sum(jnp.where(rank == float(k), pos, 0.0), axis=0, keepdims=True) for k in range(TOP_K)]
    rows += [jnp.sum(jnp.where(rank == float(k), gate, 0.0), axis=0, keepdims=True) for k in range(TOP_K)]
    rows.append(jnp.zeros((info_ref.shape[0] - 2 * TOP_K, tm), F32))
    info_ref[...] = jnp.concatenate(rows, axis=0)
    carry = carry_ref[...]
    lane = lax.broadcasted_iota(jnp.int32, (E, LANES), 1)
    seg_ref[0] = jnp.where(lane == 0, n_seg, jnp.where(lane == 1, seg_off, jnp.where(lane == 2, carry, 0.0)))
    carry_ref[...] = carry + n_seg
    count_ref[...] = carry_ref[...]


def _route(logits_t, tm):
    T = logits_t.shape[1]
    E = N_EXPERTS
    col = lambda i: (0, i)
    return pl.pallas_call(_route_kernel, name="route", grid=(T // tm,),
                          in_specs=[pl.BlockSpec((LANES, tm), col)],
                          out_specs=(pl.BlockSpec((LANES, tm), col), pl.BlockSpec((1, E, LANES), lambda i: (i, 0, 0)),
                                     pl.BlockSpec((E, LANES), lambda i: (0, 0))),
                          out_shape=(jax.ShapeDtypeStruct((LANES, T), F32),
                                     jax.ShapeDtypeStruct((T // tm, E, LANES), F32),
                                     jax.ShapeDtypeStruct((E, LANES), F32)),
                          scratch_shapes=[pltpu.VMEM((E, LANES), F32)],
                          compiler_params=_cparams())(logits_t)


GU_TILE = 2 * LANES


SEG_SIZES = (256, 128, 64, 32, 16, 8)
SEG_LARGE = 64


def _segment_copies(seg_ref, local_ref, global_ref, sem, to_global):
    def pieces(n, off, start, sizes, prio):
        for sz in sizes:
            take = (n & sz) != 0

            @pl.when(take)
            def _(off=off, start=start, sz=sz):
                loc = local_ref.at[pl.ds(pl.multiple_of(off, SEG_ALIGN), sz)]
                glo = global_ref.at[pl.ds(pl.multiple_of(start, SEG_ALIGN), sz)]
                if to_global:
                    pltpu.make_async_copy(loc, glo, sem).start(priority=prio)
                else:
                    pltpu.make_async_copy(glo, loc, sem).start(priority=prio)

            step = jnp.where(take, sz, 0)
            off = off + step
            start = start + step

    n_large = sum(1 for sz in SEG_SIZES if sz >= SEG_LARGE)
    for e in range(N_EXPERTS):
        n = seg_ref[0, 0, e]
        off = seg_ref[0, 1, e]
        start = seg_ref[0, 2, e]

        @pl.when(n >= SEG_LARGE)
        def _(n=n, off=off, start=start, e=e):
            pieces(n, off, start, SEG_SIZES[:n_large], e % 2)

        large = n & ~(SEG_LARGE - 1)
        pieces(n, off + large, start + large, SEG_SIZES[n_large:], e % 2)


def _wait_rows(local_ref, global_ref, sem, n_rows):
    @pl.when(n_rows > 0)
    def _():
        n = pl.multiple_of(n_rows, SEG_ALIGN)
        pltpu.make_async_copy(global_ref.at[pl.ds(0, n)], local_ref.at[pl.ds(0, n)], sem).wait()


def _dispatch_kernel(ends_ref, seg_ref, x_ref, info_ref, xs_hbm, comp, tot, sem):
    i = pl.program_id(0)
    n_steps = pl.num_programs(0)
    slot = i % 2
    tm = x_ref.shape[0]
    cap = comp.shape[1]

    @pl.when(i == 0)
    def _():
        comp[1, 0:MOE_BLOCK, :] = jnp.zeros((MOE_BLOCK, comp.shape[2]), F32)
        for e in range(N_EXPERTS):
            end = ends_ref[e]
            prev = ends_ref[e - 1] if e > 0 else 0

            @pl.when(end > prev)
            def _(end=end):
                cp = pltpu.make_async_copy(comp.at[1, pl.ds(0, MOE_BLOCK)],
                                           xs_hbm.at[pl.ds(pl.multiple_of(end - MOE_BLOCK, MOE_BLOCK), MOE_BLOCK)],
                                           sem.at[1])
                cp.start()
                cp.wait()

        def tail(b):
            return pltpu.make_async_copy(comp.at[1, pl.ds(0, MOE_BLOCK)],
                                         xs_hbm.at[pl.ds(pl.multiple_of(b * MOE_BLOCK, MOE_BLOCK), MOE_BLOCK)],
                                         sem.at[0])
        first_unused = ends_ref[N_EXPERTS - 1] // MOE_BLOCK
        n_all = xs_hbm.shape[0] // MOE_BLOCK
        lax.fori_loop(first_unused, n_all, lambda b, c: (tail(b).start(), c)[1], 0)
        lax.fori_loop(first_unused, n_all, lambda b, c: (tail(b).wait(), c)[1], 0)
        tot[0] = 0
        tot[1] = 0

    _wait_rows(comp.at[slot], xs_hbm, sem.at[slot], tot[slot])
    r = lax.broadcasted_iota(jnp.int32, (cap, tm), 0).astype(F32)
    sel = jnp.zeros((cap, tm), F32)
    for k in range(TOP_K):
        sel = jnp.where(r == info_ref[k:k + 1, :], 1.0, sel)
    sel = sel.astype(BF16)
    comp[slot] = _dot(sel, x_ref[...].astype(BF16))
    _segment_copies(seg_ref, comp.at[slot], xs_hbm, sem.at[slot], to_global=True)
    tot[slot] = seg_ref[0, 3, 0]

    @pl.when(i == n_steps - 1)
    def _():
        _wait_rows(comp.at[slot], xs_hbm, sem.at[slot], tot[slot])
        _wait_rows(comp.at[1 - slot], xs_hbm, sem.at[1 - slot], tot[1 - slot])


def _dispatch(ends_pad, seg, x2, info, n_slots, tm):
    T, D = x2.shape
    row = lambda i, ends: (i, 0)
    grid_spec = pltpu.PrefetchScalarGridSpec(
        num_scalar_prefetch=1, grid=(T // tm,),
        in_specs=[pl.BlockSpec((1, 4, LANES), lambda i, ends: (i, 0, 0), memory_space=pltpu.SMEM),
                  pl.BlockSpec((tm, D), row), pl.BlockSpec((LANES, tm), lambda i, ends: (0, i))],
        out_specs=pl.BlockSpec(memory_space=pl.ANY),
        scratch_shapes=[pltpu.VMEM((2, TOP_K * tm + N_EXPERTS * SEG_ALIGN, D), F32), pltpu.SMEM((2,), jnp.int32),
                        pltpu.SemaphoreType.DMA((2,))])
    return pl.pallas_call(_dispatch_kernel, name="dispatch", grid_spec=grid_spec,
                          out_shape=jax.ShapeDtypeStruct((n_slots, D), F32),
                          compiler_params=_cparams())(ends_pad, seg, x2, info)


def _moe_kernel(bexp_ref, nused_ref, x_ref, wgu_ref, bgu_ref, wd_ref, bd_ref, y_ref, wgu_s, wd_s):
    j = pl.program_id(0)
    n_used = nused_ref[0]
    n_tiles = wgu_s.shape[1] // GU_TILE

    @pl.when(j >= n_used)
    def _():
        y_ref[...] = jnp.zeros_like(y_ref)

    @pl.when(j < n_used)
    def _():
        @pl.when((j == 0) | (bexp_ref[j] != bexp_ref[jnp.maximum(j - 1, 0)]))
        def _():
            k = lax.broadcasted_iota(jnp.int32, (GU_TILE, GU_TILE), 0)
            n = lax.broadcasted_iota(jnp.int32, (GU_TILE, GU_TILE), 1)
            perm = jnp.where(k == jnp.where(n < LANES, 2 * n, 2 * (n - LANES) + 1), 1.0, 0.0).astype(BF16)
            for t in range(n_tiles):
                cols = slice(t * GU_TILE, (t + 1) * GU_TILE)
                wgu_s[:, cols] = _dot(wgu_ref[0, :, cols].astype(BF16), perm).astype(BF16)
            wd_s[...] = wd_ref[0].astype(BF16)

        xb = x_ref[...].astype(BF16)
        hs = []
        for t in range(n_tiles):
            cols = slice(t * GU_TILE, (t + 1) * GU_TILE)
            gu = _dot(xb, wgu_s[:, cols]) + bgu_ref[0, :, cols]
            gate = jnp.minimum(gu[:, :LANES], SWIGLU_LIMIT)
            lin = jnp.clip(gu[:, LANES:], -SWIGLU_LIMIT, SWIGLU_LIMIT)
            hs.append((gate * jax.nn.sigmoid(SWIGLU_ALPHA * gate) * (lin + 1.0)).astype(BF16))
        y_ref[...] = _dot(jnp.concatenate(hs, axis=1), wd_s[...]) + bd_ref[0]


def _moe(bexp, n_used, xs, wgu, bgu, wd, bd):
    n_blocks = bexp.shape[0]
    D = xs.shape[1]
    F2 = wgu.shape[2]
    F = wd.shape[1]
    wmap = lambda j, be, nu: (be[jnp.minimum(j, nu[0] - 1)], 0, 0)
    blk = lambda j, be, nu: (j, 0)
    used = lambda j, be, nu: (jnp.minimum(j, nu[0] - 1), 0)
    in_specs = [pl.BlockSpec((MOE_BLOCK, D), used),
                pl.BlockSpec((1, D, F2), wmap), pl.BlockSpec((1, 1, F2), wmap),
                pl.BlockSpec((1, F, D), wmap), pl.BlockSpec((1, 1, D), wmap)]
    grid_spec = pltpu.PrefetchScalarGridSpec(
        num_scalar_prefetch=2, grid=(n_blocks,), in_specs=in_specs,
        out_specs=pl.BlockSpec((MOE_BLOCK, D), blk),
        scratch_shapes=[pltpu.VMEM((D, F2), BF16), pltpu.VMEM((F, D), BF16)])
    return pl.pallas_call(_moe_kernel, name="moe", grid_spec=grid_spec,
                          out_shape=jax.ShapeDtypeStruct((n_blocks * MOE_BLOCK, D), F32),
                          compiler_params=_cparams())(bexp, n_used, xs, wgu, bgu, wd, bd)


def _combine_kernel(seg_ref, seg_nxt_ref, ys_hbm, info_ref, x_ref, g_ref, b_ref, out_ref, comp, sem, *, alpha):
    i = pl.program_id(0)
    n_steps = pl.num_programs(0)
    slot = i % 2
    tm = x_ref.shape[0]
    cap = comp.shape[1]

    @pl.when(i == 0)
    def _():
        comp[...] = jnp.zeros_like(comp)
        _segment_copies(seg_ref, comp.at[0], ys_hbm, sem.at[0], to_global=False)

    @pl.when(i + 1 < n_steps)
    def _():
        _segment_copies(seg_nxt_ref, comp.at[1 - slot], ys_hbm, sem.at[1 - slot], to_global=False)

    _wait_rows(comp.at[slot], ys_hbm, sem.at[slot], seg_ref[0, 3, 0])
    info = info_ref[...].T
    lane = lax.broadcasted_iota(jnp.int32, (tm, cap), 1).astype(F32)
    wmat = jnp.zeros((tm, cap), F32)
    for k in range(TOP_K):
        wmat = jnp.where(lane == info[:, k:k + 1], info[:, TOP_K + k:TOP_K + k + 1], wmat)
    ffn = _dot(wmat.astype(BF16), comp[slot].astype(BF16))
    out_ref[...] = _layer_norm(alpha * x_ref[...] + ffn, g_ref[...], b_ref[...])


def _combine(seg, ys, info, x2, g, b, alpha, tm):
    T, D = x2.shape
    n_t = T // tm
    row = lambda i: (i, 0)
    seg_spec = lambda m: pl.BlockSpec((1, 4, LANES), m, memory_space=pltpu.SMEM)
    in_specs = [seg_spec(lambda i: (i, 0, 0)), seg_spec(lambda i: (jnp.minimum(i + 1, n_t - 1), 0, 0)),
                pl.BlockSpec(memory_space=pl.ANY),
                pl.BlockSpec((LANES, tm), lambda i: (0, i)), pl.BlockSpec((tm, D), row),
                _full_spec(g.shape), _full_spec(b.shape)]
    cap = TOP_K * tm + N_EXPERTS * SEG_ALIGN
    return pl.pallas_call(functools.partial(_combine_kernel, alpha=alpha), name="combine", grid=(n_t,), in_specs=in_specs,
                          out_specs=pl.BlockSpec((tm, D), row),
                          out_shape=jax.ShapeDtypeStruct((T, D), F32),
                          scratch_shapes=[pltpu.VMEM((2, cap, D), F32), pltpu.SemaphoreType.DMA((2,))],
                          compiler_params=_cparams())(seg, seg, ys, info, x2, g, b)


def _tile(n, pref):
    return pref if n % pref == 0 else n


def _layer(x2d, mem2d, pos2d, B, S, depth, w_in, sinks, w_attn_o, lam_re, lam_im, log_dt, b_re, b_im,
           c_re, c_im, d_skip, w_glu_a, w_glu_b, w_out, ln1_g, ln1_b, wq_c, wk_c, wv_c, wo_c, ln2_g,
           ln2_b, w_router, b_router, w_gate_up, b_gate_up, w_down, b_down, ln3_g, ln3_b):
    T, D = x2d.shape
    alpha = (2 * depth) ** 0.25
    rep = N_Q_HEADS // N_KV_HEADS
    q_w = N_Q_HEADS * HEAD_DIM
    kv_w = N_KV_HEADS * HEAD_DIM
    ssm_w = SSM_GROUP * SSM_GROUPS
    P = SSM_STATE

    o_k, o_v, o_s = q_w, q_w + kv_w, q_w + 2 * kv_w
    o_ga, o_gs = o_s + ssm_w, o_s + ssm_w + D
    wq = w_in[:, :o_k].reshape(D, N_KV_HEADS, rep, HEAD_DIM).transpose(0, 2, 1, 3).reshape(D, q_w).astype(BF16)
    wk = w_in[:, o_k:o_v].astype(BF16)
    wv = w_in[:, o_v:o_s].astype(BF16)
    wu = w_in[:, o_s:o_ga].astype(BF16)
    wga = w_in[:, o_ga:o_gs].astype(BF16)
    wgs = w_in[:, o_gs:].astype(BF16)
    wo_attn = w_attn_o.reshape(N_KV_HEADS, rep, HEAD_DIM, D).transpose(1, 0, 2, 3).reshape(q_w, D).astype(BF16)
    half = HEAD_DIM // 2
    inv_freq = jnp.power(ROPE_THETA, -jnp.arange(half, dtype=F32) / half)
    invf = jnp.tile(inv_freq, LANES // half)[None, :]
    sink_rows = jnp.repeat(sinks.astype(F32).reshape(N_KV_HEADS, rep), WINDOW, axis=1)
    sink_cols = jnp.full((N_KV_HEADS, rep * WINDOW, 2 * WINDOW), NEG_BIG, F32).at[:, :, 0].set(sink_rows)

    tm1 = _tile(T, 1024)
    q2, k, v, ut, ga, gs = _inproj(x2d, pos2d, invf, wq, wk, wv, wu, wga, wgs, tm1)
    att = _swa(q2, k, v, sink_cols, wo_attn, ga, B, S, _tile(S, 512))

    lam_row = jnp.stack([jnp.concatenate([lam_re, lam_re], -1), jnp.concatenate([lam_im, lam_im], -1)], 1)
    lam_col = jnp.swapaxes(lam_row, 1, 2)
    bt_re = jnp.swapaxes(b_re, 1, 2)
    bt_im = jnp.swapaxes(b_im, 1, 2)
    bt_a = jnp.concatenate([bt_re, bt_im], -1)
    bt_b = jnp.concatenate([-bt_im, bt_re], -1)
    ct_re = jnp.swapaxes(c_re, 1, 2)
    ct_im = jnp.swapaxes(c_im, 1, 2)
    ct_a = jnp.concatenate([ct_re, -ct_im], 1)
    ct_b = jnp.concatenate([-ct_im, -ct_re], 1)
    dsk = jnp.repeat(d_skip.reshape(SSM_GROUPS, SSM_GROUP), SSM_GROUP, axis=1)[:, :, None]
    yt = _ssm(ut, lam_row, lam_col, log_dt.reshape(SSM_GROUPS, 1, 1), bt_a, bt_b, c_re, c_im, ct_a, ct_b, dsk,
              S // LANES)

    x1 = _mix(yt, att, gs, x2d, w_glu_a.astype(BF16), w_glu_b.astype(BF16), w_out.astype(BF16),
              ln1_g[None, :], ln1_b[None, :], alpha, _tile(T, 1024))

    kc, vc = _memkv(mem2d, wk_c.astype(BF16), wv_c.astype(BF16), _tile(mem2d.shape[0], 512))
    wr = jnp.zeros((LANES, D), F32).at[:N_EXPERTS, :].set(w_router.T).astype(BF16)
    br = jnp.full((LANES, 1), NEG_BIG, F32).at[:N_EXPERTS, 0].set(b_router)
    x2, logits = _cross(x1, kc, vc, wq_c.astype(BF16), wo_c.astype(BF16), ln2_g[None, :], ln2_b[None, :],
                        wr, br, alpha, S, _tile(S, 1024))

    tm_r = _tile(T, ROUTE_TILE)
    n_tiles = T // tm_r
    info, segf, counts = _route(logits, tm_r)
    total = counts[:, 0].astype(jnp.int32)
    padded = (total + MOE_BLOCK - 1) // MOE_BLOCK * MOE_BLOCK
    ends_pad = jnp.cumsum(padded)
    start_pad = ends_pad - padded
    n_blocks = -(-(T * TOP_K + n_tiles * N_EXPERTS * (SEG_ALIGN - 1)) // MOE_BLOCK) + N_EXPERTS
    segi = segf.astype(jnp.int32)
    seg_n, seg_off, seg_start = segi[:, :, 0], segi[:, :, 1], segi[:, :, 2] + start_pad[None, :]
    seg_tot = jnp.broadcast_to(jnp.sum(seg_n, axis=1, keepdims=True), seg_n.shape)
    seg = jnp.stack([seg_n, seg_off, seg_start, seg_tot], axis=1)
    seg = jnp.pad(seg, ((0, 0), (0, 0), (0, LANES - N_EXPERTS)))
    block_start = jnp.arange(n_blocks, dtype=jnp.int32) * MOE_BLOCK
    bexp = jnp.minimum(jnp.sum(block_start[:, None] >= ends_pad[None, :], axis=1), N_EXPERTS - 1).astype(jnp.int32)
    n_used = (ends_pad[-1] // MOE_BLOCK).astype(jnp.int32)[None]
    n_gu_tiles = b_gate_up.shape[1] // GU_TILE
    bgu = b_gate_up.reshape(N_EXPERTS, n_gu_tiles, LANES, 2).transpose(0, 1, 3, 2).reshape(N_EXPERTS, 1, -1)
    xs = _dispatch(ends_pad, seg, x2, info, n_blocks * MOE_BLOCK, tm_r)
    ys = _moe(bexp, n_used, xs, w_gate_up, bgu, w_down, b_down[:, None, :])
    return _combine(seg, ys, info, x2, ln3_g[None, :], ln3_b[None, :], alpha, tm_r)


def kernel(x, mem, positions, w_in, sinks, w_attn_o, lam_re, lam_im, log_dt, b_re, b_im, c_re, c_im, d_skip,
           w_glu_a, w_glu_b, w_out, ln1_g, ln1_b, wq_c, wk_c, wv_c, wo_c, ln2_g, ln2_b, w_router, b_router,
           w_gate_up, b_gate_up, w_down, b_down, ln3_g, ln3_b):
    B, S, D = x.shape
    depth = w_in.shape[0]
    x2d = x.reshape(B * S, D)
    mem2d = mem.reshape(-1, D)
    pos2d = positions.reshape(B * S, 1)
    per_layer = (w_in, sinks, w_attn_o, lam_re, lam_im, log_dt, b_re, b_im, c_re, c_im, d_skip, w_glu_a,
                 w_glu_b, w_out, ln1_g, ln1_b, wq_c, wk_c, wv_c, wo_c, ln2_g, ln2_b, w_router, b_router,
                 w_gate_up, b_gate_up, w_down, b_down, ln3_g, ln3_b)
    for l in range(depth):
        x2d = _layer(x2d, mem2d, pos2d, B, S, depth, *(w[l] for w in per_layer))
    return x2d.reshape(B, S, D)
```

```python
import functools

import jax
import jax.numpy as jnp
from jax import lax
from jax.experimental import pallas as pl
from jax.experimental.pallas import tpu as pltpu

N_Q_HEADS = 16
N_KV_HEADS = 2
HEAD_DIM = 64
WINDOW = 128
ROPE_THETA = 10000.0
SSM_GROUP = 16
SSM_GROUPS = 32
SSM_STATE = 64
N_CROSS_HEADS = 4
N_EXPERTS = 32
TOP_K = 4
SWIGLU_ALPHA = 1.702
SWIGLU_LIMIT = 7.0
MOE_BLOCK = 512
ROUTE_TILE = 512
LN_EPS = 1e-5

LANES = 128
VMEM_LIMIT_BYTES = 56 * 1024 * 1024

NEG_BIG = -1e30
BF16 = jnp.bfloat16
F32 = jnp.float32
HI = lax.Precision.HIGHEST


def _cparams(n_axes=1):
    return pltpu.CompilerParams(dimension_semantics=("arbitrary",) * n_axes,
                                vmem_limit_bytes=VMEM_LIMIT_BYTES)


def _full_spec(shape):
    n = len(shape)
    return pl.BlockSpec(shape, lambda *_: (0,) * n, pipeline_mode=pl.Buffered(1))


def _dot(a, b):
    return jnp.dot(a, b, preferred_element_type=F32)


def _dot_nt(a, b):
    return lax.dot_general(a, b, (((1,), (1,)), ((), ())), preferred_element_type=F32)


def _layer_norm(y, g, b):
    mu = jnp.mean(y, axis=-1, keepdims=True)
    d = y - mu
    var = jnp.mean(d * d, axis=-1, keepdims=True)
    return d * lax.rsqrt(var + LN_EPS) * g + b


def _rope(t, cos, sin_signed, first_half):
    half = HEAD_DIM // 2
    partner = jnp.where(first_half, pltpu.roll(t, LANES - half, axis=1), pltpu.roll(t, half, axis=1))
    return t * cos + partner * sin_signed


def _inproj_kernel(x_ref, pos_ref, invf_ref, wq_ref, wk_ref, wv_ref, wu_ref, wga_ref, wgs_ref,
                   q2_ref, k_ref, v_ref, ut_ref, ga_ref, gs_ref):
    tm = x_ref.shape[0]
    xb = x_ref[...].astype(BF16)
    ang = pos_ref[...].astype(F32) * invf_ref[...]
    cos = jnp.cos(ang)
    sin = jnp.sin(ang)
    first_half = (lax.broadcasted_iota(jnp.int32, (tm, LANES), 1) % HEAD_DIM) < (HEAD_DIM // 2)
    first_half_w = (lax.broadcasted_iota(jnp.int32, (WINDOW, LANES), 1) % HEAD_DIM) < (HEAD_DIM // 2)
    sin_signed = jnp.where(first_half, -sin, sin)

    q = _dot(xb, wq_ref[...])
    n_rep = q.shape[1] // LANES
    scale = HEAD_DIM ** -0.5
    for j in range(tm // WINDOW):
        rows = slice(j * WINDOW, (j + 1) * WINDOW)
        for r in range(n_rep):
            t = _rope(q[rows, r * LANES:(r + 1) * LANES], cos[rows], sin_signed[rows], first_half_w)
            base = (j * n_rep + r) * WINDOW
            q2_ref[base:base + WINDOW, :] = (t * scale).astype(BF16)
    k_ref[...] = _rope(_dot(xb, wk_ref[...]), cos, sin_signed, first_half).astype(BF16)
    v_ref[...] = _dot(xb, wv_ref[...]).astype(BF16)
    u = _dot(xb, wu_ref[...])
    for j in range(tm // LANES):
        ut_ref[j] = u[j * LANES:(j + 1) * LANES, :].T
    ga_ref[...] = jax.nn.sigmoid(_dot(xb, wga_ref[...])).astype(BF16)
    gs_ref[...] = jax.nn.sigmoid(_dot(xb, wgs_ref[...])).astype(BF16)


def _inproj(x2d, pos2d, invf, wq, wk, wv, wu, wga, wgs, tm):
    T, D = x2d.shape
    n_rep = wq.shape[1] // LANES
    ssm_w = wu.shape[1]
    row = lambda i: (i, 0)
    out_shape = (
        jax.ShapeDtypeStruct((T * n_rep, LANES), BF16),
        jax.ShapeDtypeStruct((T, LANES), BF16),
        jax.ShapeDtypeStruct((T, LANES), BF16),
        jax.ShapeDtypeStruct((T // LANES, ssm_w, LANES), F32),
        jax.ShapeDtypeStruct((T, D), BF16),
        jax.ShapeDtypeStruct((T, D), BF16),
    )
    in_specs = [pl.BlockSpec((tm, D), row), pl.BlockSpec((tm, 1), row), _full_spec(invf.shape),
                _full_spec(wq.shape), _full_spec(wk.shape), _full_spec(wv.shape),
                _full_spec(wu.shape), _full_spec(wga.shape), _full_spec(wgs.shape)]
    out_specs = (pl.BlockSpec((tm * n_rep, LANES), row), pl.BlockSpec((tm, LANES), row),
                 pl.BlockSpec((tm, LANES), row),
                 pl.BlockSpec((tm // LANES, ssm_w, LANES), lambda i: (i, 0, 0)),
                 pl.BlockSpec((tm, D), row), pl.BlockSpec((tm, D), row))
    return pl.pallas_call(_inproj_kernel, name="inproj", grid=(T // tm,), in_specs=in_specs, out_specs=out_specs,
                          out_shape=out_shape, compiler_params=_cparams())(
        x2d, pos2d, invf, wq, wk, wv, wu, wga, wgs)


def _swa_kernel(q2_ref, kc_ref, kp_ref, vc_ref, vp_ref, fill_ref, wo_ref, ga_ref, out_ref, cat_ref):
    i = pl.program_id(1)
    tq = kc_ref.shape[0]
    n_sub = tq // WINDOW
    rep = N_Q_HEADS // N_KV_HEADS
    rows_all = rep * WINDOW
    kfull = jnp.concatenate([kp_ref[...], kc_ref[...]], axis=0)
    vfull = jnp.concatenate([vp_ref[...], vc_ref[...]], axis=0)
    lane = lax.broadcasted_iota(jnp.int32, (2 * WINDOW, LANES), 1)
    qi = lax.broadcasted_iota(jnp.int32, (rows_all, 2 * WINDOW), 0) % WINDOW
    ci = lax.broadcasted_iota(jnp.int32, (rows_all, 2 * WINDOW), 1)
    local = (ci > qi) & (ci <= qi + WINDOW)
    out_lane = lax.broadcasted_iota(jnp.int32, (rows_all, LANES), 1)
    band_row = lax.broadcasted_iota(jnp.int32, (2 * WINDOW, LANES), 0)
    for j in range(n_sub):
        qs = q2_ref[j * rows_all:(j + 1) * rows_all, :]
        kb = kfull[j * WINDOW:(j + 2) * WINDOW, :]
        vb = vfull[j * WINDOW:(j + 2) * WINDOW, :]
        vb = jnp.where(band_row == 0, jnp.zeros_like(vb), vb)
        mask = local
        if j == 0:
            mask = mask & ((ci >= WINDOW) | (i > 0))
        o = None
        for g in range(N_KV_HEADS):
            in_group = (lane >= g * HEAD_DIM) & (lane < (g + 1) * HEAD_DIM)
            kg = jnp.where(in_group, kb, jnp.zeros_like(kb))
            s = _dot_nt(qs, kg)
            s = jnp.where(mask, s, fill_ref[g])
            m = jnp.max(s, axis=-1, keepdims=True)
            p = jnp.exp(s - m)
            denom = jnp.sum(p, axis=-1, keepdims=True)
            og = _dot(p.astype(BF16), vb) * (1.0 / denom)
            o = og if o is None else jnp.where(out_lane < g * HEAD_DIM, o, og)
        ob = o.astype(BF16)
        for r in range(rep):
            cat_ref[j * WINDOW:(j + 1) * WINDOW, r * LANES:(r + 1) * LANES] = ob[r * WINDOW:(r + 1) * WINDOW, :]
    attn = _dot(cat_ref[...], wo_ref[...])
    out_ref[...] = (attn * ga_ref[...].astype(F32)).astype(BF16)


def _swa(q2, k, v, sink_cols, wo, ga, B, S, tq):
    T, D = ga.shape
    rep = N_Q_HEADS // N_KV_HEADS
    n_i = S // tq
    n_sub = tq // WINDOW
    in_specs = [
        pl.BlockSpec((tq * rep, LANES), lambda b, i: (b * n_i + i, 0)),
        pl.BlockSpec((tq, LANES), lambda b, i: (b * n_i + i, 0)),
        pl.BlockSpec((WINDOW, LANES), lambda b, i: (b * (S // WINDOW) + jnp.maximum(i * n_sub - 1, 0), 0)),
        pl.BlockSpec((tq, LANES), lambda b, i: (b * n_i + i, 0)),
        pl.BlockSpec((WINDOW, LANES), lambda b, i: (b * (S // WINDOW) + jnp.maximum(i * n_sub - 1, 0), 0)),
        _full_spec(sink_cols.shape), _full_spec(wo.shape),
        pl.BlockSpec((tq, D), lambda b, i: (b * n_i + i, 0)),
    ]
    return pl.pallas_call(
        _swa_kernel, name="swa", grid=(B, n_i), in_specs=in_specs,
        out_specs=pl.BlockSpec((tq, D), lambda b, i: (b * n_i + i, 0)),
        out_shape=jax.ShapeDtypeStruct((T, D), BF16),
        scratch_shapes=[pltpu.VMEM((tq, D), BF16)],
        compiler_params=_cparams(2))(q2, k, k, v, v, sink_cols, wo, ga)


def _ssm_prep_kernel(lam_row_ref, lam_col_ref, dt_ref, bt_a_ref, bt_b_ref, c_re_ref, c_im_ref,
                     ct_a_ref, ct_b_ref, dsk_ref, toep_ref, wst_ref, wout_ref, apow_ref):
    L = LANES
    P = SSM_STATE
    dt = jnp.exp(dt_ref[0])
    lr2 = lam_row_ref[0, 0:1, :]
    li2 = lam_row_ref[0, 1:2, :]
    mag = jnp.exp(lr2 * dt)
    ar = mag * jnp.cos(li2 * dt)
    ai = mag * jnp.sin(li2 * dt)
    den = lr2 * lr2 + li2 * li2
    f_re = ((ar - 1.0) * lr2 + ai * li2) / den
    f_im = (ai * lr2 - (ar - 1.0) * li2) / den
    bt_a = bt_a_ref[0]
    bt_b = bt_b_ref[0]
    bb_a = f_re * bt_a + f_im * bt_b
    bb_b = f_re * bt_b - f_im * bt_a

    tau_rev = (L - 1 - lax.broadcasted_iota(jnp.int32, (L, 2 * P), 0)).astype(F32)
    g_mag = jnp.exp(lr2 * dt * tau_rev)
    g_re = g_mag * jnp.cos(li2 * dt * tau_rev)
    g_im = g_mag * jnp.sin(li2 * dt * tau_rev)
    for i in range(SSM_GROUP):
        wst_ref[0, i * L:(i + 1) * L, :] = (g_re * bb_a[i:i + 1, :] + g_im * bb_b[i:i + 1, :]).astype(BF16)

    lane2 = lax.broadcasted_iota(jnp.int32, (1, 2 * P), 1)
    for kk in range(4):
        n = float(L * (1 << kk))
        pm = jnp.exp(lr2 * dt * n)
        p_re = pm * jnp.cos(li2 * dt * n)
        p_im = pm * jnp.sin(li2 * dt * n)
        apow_ref[0, 2 * kk:2 * kk + 1, :] = p_re
        apow_ref[0, 2 * kk + 1:2 * kk + 2, :] = jnp.where(lane2 < P, -p_im, p_im)

    lrc = lam_col_ref[0, :, 0:1]
    lic = lam_col_ref[0, :, 1:2]
    tau = lax.broadcasted_iota(jnp.int32, (2 * P, L), 1).astype(F32)
    e0_mag = jnp.exp(lrc * dt * tau)
    e0_re = e0_mag * jnp.cos(lic * dt * tau)
    e0_im = e0_mag * jnp.sin(lic * dt * tau)
    e1_mag = jnp.exp(lrc * dt * (tau + 1.0))
    e1_re = e1_mag * jnp.cos(lic * dt * (tau + 1.0))
    e1_im = e1_mag * jnp.sin(lic * dt * (tau + 1.0))

    ct_a = ct_a_ref[0]
    ct_b = ct_b_ref[0]
    for o in range(SSM_GROUP):
        wout_ref[0, :, o * L:(o + 1) * L] = (ct_a[:, o:o + 1] * e1_re + ct_b[:, o:o + 1] * e1_im).astype(BF16)

    c_re = c_re_ref[0]
    c_im = c_im_ref[0]
    bb_re = bb_a[:, :P]
    bb_im = bb_a[:, P:]
    m_re = (c_re[:, None, :] * bb_re[None, :, :] - c_im[:, None, :] * bb_im[None, :, :])
    m_im = (c_re[:, None, :] * bb_im[None, :, :] + c_im[:, None, :] * bb_re[None, :, :])
    m_re = m_re.reshape(SSM_GROUP * SSM_GROUP, P)
    m_im = m_im.reshape(SSM_GROUP * SSM_GROUP, P)
    kt = (jnp.dot(m_re, e0_re[:P, :], precision=HI, preferred_element_type=F32)
          - jnp.dot(m_im, e0_im[:P, :], precision=HI, preferred_element_type=F32))
    rowi = lax.broadcasted_iota(jnp.int32, (SSM_GROUP * SSM_GROUP, L), 0)
    coli = lax.broadcasted_iota(jnp.int32, (SSM_GROUP * SSM_GROUP, L), 1)
    dsk = dsk_ref[0]
    kt = kt + jnp.where((coli == 0) & ((rowi // SSM_GROUP) == (rowi % SSM_GROUP)), dsk, 0.0)

    cc = lax.broadcasted_iota(jnp.int32, (L, L), 0)
    cp = lax.broadcasted_iota(jnp.int32, (L, L), 1)
    causal = cp >= cc
    for o in range(SSM_GROUP):
        for i in range(SSM_GROUP):
            kv = kt[o * SSM_GROUP + i:o * SSM_GROUP + i + 1, :]
            blk = pltpu.roll(jnp.broadcast_to(kv, (L, L)), 0, axis=1, stride=1, stride_axis=0)
            blk = jnp.where(causal, blk, 0.0)
            toep_ref[0, i * L:(i + 1) * L, o * L:(o + 1) * L] = blk.astype(BF16)


def _ssm_scan_kernel(ut_ref, toep_ref, wst_ref, wout_ref, apow_ref, yt_ref, *, n_chunks):
    nb = ut_ref.shape[0]
    P = SSM_STATE
    ut2 = ut_ref.reshape(nb * SSM_GROUP, LANES)
    lhs = jnp.concatenate([ut2[pl.ds(i, nb, stride=SSM_GROUP), :] for i in range(SSM_GROUP)],
                          axis=1).astype(BF16)
    y = _dot(lhs, toep_ref[0])
    st = _dot(lhs, wst_ref[0])
    srow = lax.broadcasted_iota(jnp.int32, (nb, 2 * P), 0) % n_chunks
    kk = 0
    while (1 << kk) < n_chunks:
        d = 1 << kk
        pa = apow_ref[0, 2 * kk:2 * kk + 1, :]
        pb = apow_ref[0, 2 * kk + 1:2 * kk + 2, :]
        prev = pltpu.roll(st, d, axis=0)
        prev = prev * pa + pltpu.roll(prev, P, axis=1) * pb
        st = st + jnp.where(srow >= d, prev, 0.0)
        kk += 1
    h0 = jnp.where(srow >= 1, pltpu.roll(st, 1, axis=0), 0.0)
    y = y + _dot(h0.astype(BF16), wout_ref[0])
    for o in range(SSM_GROUP):
        yt_ref[o] = y[:, o * LANES:(o + 1) * LANES]


def _ssm_kernel(*refs, n_chunks):
    params, (ut_ref, yt_ref), ops = refs[:10], refs[10:12], refs[12:]
    sets = (ops[0::2], ops[1::2])
    g = pl.program_id(0)

    @pl.when(g == 0)
    def _():
        _ssm_prep_kernel(*params, *sets[0])

    for p in range(2):
        @pl.when((g > 0) & (g % 2 == p))
        def _(p=p):
            _ssm_prep_kernel(*params, *sets[p])
            _ssm_scan_kernel(ut_ref, *sets[1 - p], yt_ref, n_chunks=n_chunks)


def _ssm(ut, lam_row, lam_col, log_dt, bt_a, bt_b, c_re, c_im, ct_a, ct_b, dsk, n_chunks):
    nb, ssm_w, _ = ut.shape
    G = SSM_GROUPS
    KW = SSM_GROUP * LANES
    params = [lam_row, lam_col, log_dt, bt_a, bt_b, c_re, c_im, ct_a, ct_b, dsk]
    build = lambda g: (jnp.minimum(g, G - 1), 0, 0)
    in_specs = [pl.BlockSpec((1,) + a.shape[1:], build) for a in params]
    in_specs.append(pl.BlockSpec((nb, SSM_GROUP, LANES), lambda g: (0, jnp.maximum(g - 1, 0), 0)))
    op_shapes = [pltpu.VMEM((1, KW, KW), BF16), pltpu.VMEM((1, KW, 2 * SSM_STATE), BF16),
                 pltpu.VMEM((1, 2 * SSM_STATE, KW), BF16), pltpu.VMEM((1, 8, 2 * SSM_STATE), F32)]
    return pl.pallas_call(
        functools.partial(_ssm_kernel, n_chunks=n_chunks), name="ssm", grid=(G + 1,), in_specs=in_specs,
        out_specs=pl.BlockSpec((SSM_GROUP, nb, LANES), lambda g: (jnp.maximum(g - 1, 0), 0, 0)),
        out_shape=jax.ShapeDtypeStruct((ssm_w, nb, LANES), F32),
        scratch_shapes=[s for s in op_shapes for _ in range(2)],
        compiler_params=_cparams())(*params, ut)


def _mix_kernel(yt_ref, att_ref, gs_ref, x_ref, wa_ref, wb_ref, wo_ref, g_ref, b_ref, out_ref, zt_ref,
                *, alpha):
    n_ch, n_blk = yt_ref.shape[0], yt_ref.shape[1]
    yt2 = yt_ref.reshape(n_ch * n_blk, LANES)
    for j in range(n_blk):
        y = yt2[pl.ds(j, n_ch, stride=n_blk), :]
        z = jax.nn.gelu(y, approximate=True)
        zt_ref[j * LANES:(j + 1) * LANES, :] = z.T.astype(BF16)
    z = zt_ref[...]
    ssm_out = _dot(z, wa_ref[...]) * jax.nn.sigmoid(_dot(z, wb_ref[...]))
    mixed = att_ref[...].astype(F32) + gs_ref[...].astype(F32) * ssm_out
    y = alpha * x_ref[...] + _dot(mixed.astype(BF16), wo_ref[...])
    out_ref[...] = _layer_norm(y, g_ref[...], b_ref[...])


def _mix(yt, att, gs, x2d, wa, wb, wo, g, b, alpha, tm):
    T, D = x2d.shape
    ssm_w = yt.shape[0]
    row = lambda i: (i, 0)
    in_specs = [pl.BlockSpec((ssm_w, tm // LANES, LANES), lambda i: (0, i, 0)),
                pl.BlockSpec((tm, D), row), pl.BlockSpec((tm, D), row), pl.BlockSpec((tm, D), row),
                _full_spec(wa.shape), _full_spec(wb.shape), _full_spec(wo.shape),
                _full_spec(g.shape), _full_spec(b.shape)]
    return pl.pallas_call(functools.partial(_mix_kernel, alpha=alpha), name="mix", grid=(T // tm,), in_specs=in_specs,
                          out_specs=pl.BlockSpec((tm, D), row),
                          out_shape=jax.ShapeDtypeStruct((T, D), F32),
                          scratch_shapes=[pltpu.VMEM((tm, ssm_w), BF16)],
                          compiler_params=_cparams())(yt, att, gs, x2d, wa, wb, wo, g, b)


def _memkv_kernel(m_ref, wk_ref, wv_ref, k_ref, v_ref):
    mb = m_ref[...].astype(BF16)
    k_ref[...] = _dot(mb, wk_ref[...]).astype(BF16)
    v_ref[...] = _dot(mb, wv_ref[...]).astype(BF16)


def _memkv(mem2d, wk, wv, tm):
    R, D = mem2d.shape
    row = lambda i: (i, 0)
    return pl.pallas_call(_memkv_kernel, name="memkv", grid=(R // tm,),
                          in_specs=[pl.BlockSpec((tm, D), row), _full_spec(wk.shape), _full_spec(wv.shape)],
                          out_specs=(pl.BlockSpec((tm, D), row), pl.BlockSpec((tm, D), row)),
                          out_shape=(jax.ShapeDtypeStruct((R, D), BF16), jax.ShapeDtypeStruct((R, D), BF16)),
                          compiler_params=_cparams())(mem2d, wk, wv)


def _cross_kernel(x_ref, k_ref, v_ref, wq_ref, wo_ref, g_ref, b_ref, wr_ref, br_ref, out_ref, logit_ref,
                  cat_ref, *, alpha):
    x = x_ref[...]
    D = x.shape[1]
    hd = D // N_CROSS_HEADS
    q = (_dot(x.astype(BF16), wq_ref[...]) * (hd ** -0.5)).astype(BF16)
    for h in range(N_CROSS_HEADS):
        cols = slice(h * hd, (h + 1) * hd)
        s = _dot_nt(q[:, cols], k_ref[:, cols])
        m = jnp.max(s, axis=-1, keepdims=True)
        p = jnp.exp(s - m)
        w = (p / jnp.sum(p, axis=-1, keepdims=True)).astype(BF16)
        cat_ref[:, cols] = _dot(w, v_ref[:, cols]).astype(BF16)
    y = alpha * x + _dot(cat_ref[...], wo_ref[...])
    x2 = _layer_norm(y, g_ref[...], b_ref[...])
    out_ref[...] = x2
    logit_ref[...] = _dot_nt(wr_ref[...], x2.astype(BF16)) + br_ref[...]


def _cross(x1, kc, vc, wq, wo, g, b, wr, br, alpha, S, tm):
    T, D = x1.shape
    n_mem = kc.shape[0] // (T // S)
    row = lambda i: (i, 0)
    per_b = S // tm
    in_specs = [pl.BlockSpec((tm, D), row),
                pl.BlockSpec((n_mem, D), lambda i: (i // per_b, 0)),
                pl.BlockSpec((n_mem, D), lambda i: (i // per_b, 0)),
                _full_spec(wq.shape), _full_spec(wo.shape), _full_spec(g.shape), _full_spec(b.shape),
                _full_spec(wr.shape), _full_spec(br.shape)]
    return pl.pallas_call(functools.partial(_cross_kernel, alpha=alpha), name="cross", grid=(T // tm,), in_specs=in_specs,
                          out_specs=(pl.BlockSpec((tm, D), row), pl.BlockSpec((LANES, tm), lambda i: (0, i))),
                          out_shape=(jax.ShapeDtypeStruct((T, D), F32), jax.ShapeDtypeStruct((LANES, T), F32)),
                          scratch_shapes=[pltpu.VMEM((tm, D), BF16)],
                          compiler_params=_cparams())(x1, kc, vc, wq, wo, g, b, wr, br)


SEG_ALIGN = 8


def _route_kernel(lt_ref, info_ref, seg_ref, count_ref, carry_ref):
    tm = lt_ref.shape[1]
    E = N_EXPERTS

    @pl.when(pl.program_id(0) == 0)
    def _():
        carry_ref[...] = jnp.zeros_like(carry_ref)

    l = lt_ref[0:E, :]
    erow = lax.broadcasted_iota(jnp.int32, (E, tm), 0)
    rank = jnp.zeros((E, tm), F32)
    for e2 in range(E):
        other = l[e2:e2 + 1, :]
        tie = jnp.where(erow > e2, 1.0, 0.0)
        rank = rank + jnp.where(other > l, 1.0, jnp.where(other == l, tie, 0.0))
    chosen = rank < float(TOP_K)
    onehot = jnp.where(chosen, 1.0, 0.0)
    p = jnp.where(chosen, jnp.exp(l - jnp.max(l, axis=0, keepdims=True)), 0.0)
    gate = p / jnp.sum(p, axis=0, keepdims=True)
    a = lax.broadcasted_iota(jnp.int32, (tm, tm), 0)
    b = lax.broadcasted_iota(jnp.int32, (tm, tm), 1)
    cum = _dot(onehot.astype(BF16), jnp.where(a < b, 1.0, 0.0).astype(BF16))
    n_seg = jnp.floor((jnp.sum(onehot, axis=1, keepdims=True) + (SEG_ALIGN - 1.0)) * (1.0 / SEG_ALIGN)) * SEG_ALIGN
    n_seg = jnp.broadcast_to(n_seg, (E, LANES))
    erow_l = lax.broadcasted_iota(jnp.int32, (E, LANES), 0)
    seg_off = jnp.zeros((E, LANES), F32)
    for e2 in range(E - 1):
        seg_off = seg_off + jnp.where(erow_l > e2, n_seg[e2:e2 + 1, :], 0.0)
    pos = cum + seg_off[:, 0:1]
    rows = [jnp.sum(jnp.where(rank == float(k), pos, 0.0), axis=0, keepdims=True) for k in range(TOP_K)]
    rows += [jnp.sum(jnp.where(rank == float(k), gate, 0.0), axis=0, keepdims=True) for k in range(TOP_K)]
    rows.append(jnp.zeros((info_ref.shape[0] - 2 * TOP_K, tm), F32))
    info_ref[...] = jnp.concatenate(rows, axis=0)
    carry = carry_ref[...]
    lane = lax.broadcasted_iota(jnp.int32, (E, LANES), 1)
    seg_ref[0] = jnp.where(lane == 0, n_seg, jnp.where(lane == 1, seg_off, jnp.where(lane == 2, carry, 0.0)))
    carry_ref[...] = carry + n_seg
    count_ref[...] = carry_ref[...]


def _route(logits_t, tm):
    T = logits_t.shape[1]
    E = N_EXPERTS
    col = lambda i: (0, i)
    return pl.pallas_call(_route_kernel, name="route", grid=(T // tm,),
                          in_specs=[pl.BlockSpec((LANES, tm), col)],
                          out_specs=(pl.BlockSpec((LANES, tm), col), pl.BlockSpec((1, E, LANES), lambda i: (i, 0, 0)),
                                     pl.BlockSpec((E, LANES), lambda i: (0, 0))),
                          out_shape=(jax.ShapeDtypeStruct((LANES, T), F32),
                                     jax.ShapeDtypeStruct((T // tm, E, LANES), F32),
                                     jax.ShapeDtypeStruct((E, LANES), F32)),
                          scratch_shapes=[pltpu.VMEM((E, LANES), F32)],
                          compiler_params=_cparams())(logits_t)


GU_TILE = 2 * LANES


SEG_SIZES = tuple(ROUTE_TILE >> s for s in range(ROUTE_TILE.bit_length()) if (ROUTE_TILE >> s) >= SEG_ALIGN)
SEG_LARGE = 128


def _segment_copies(seg_ref, local_ref, global_ref, sem, to_global):
    def pieces(n, off, start, sizes, prio):
        for sz in sizes:
            take = (n & sz) != 0

            @pl.when(take)
            def _(off=off, start=start, sz=sz):
                loc = local_ref.at[pl.ds(pl.multiple_of(off, SEG_ALIGN), sz)]
                glo = global_ref.at[pl.ds(pl.multiple_of(start, SEG_ALIGN), sz)]
                if to_global:
                    pltpu.make_async_copy(loc, glo, sem).start(priority=prio)
                else:
                    pltpu.make_async_copy(glo, loc, sem).start(priority=prio)

            step = jnp.where(take, sz, 0)
            off = off + step
            start = start + step

    n_large = sum(1 for sz in SEG_SIZES if sz >= SEG_LARGE)
    for e in range(N_EXPERTS):
        n = seg_ref[0, 0, e]
        off = seg_ref[0, 1, e]
        start = seg_ref[0, 2, e]

        @pl.when(n >= SEG_LARGE)
        def _(n=n, off=off, start=start, e=e):
            pieces(n, off, start, SEG_SIZES[:n_large], e % 2)

        large = n & ~(SEG_LARGE - 1)
        pieces(n, off + large, start + large, SEG_SIZES[n_large:], e % 2)


def _wait_rows(local_ref, global_ref, sem, n_rows):
    @pl.when(n_rows > 0)
    def _():
        n = pl.multiple_of(n_rows, SEG_ALIGN)
        pltpu.make_async_copy(global_ref.at[pl.ds(0, n)], local_ref.at[pl.ds(0, n)], sem).wait()


def _dispatch_kernel(ends_ref, seg_ref, x_ref, info_ref, xs_hbm, comp, tot, sem):
    i = pl.program_id(0)
    n_steps = pl.num_programs(0)
    slot = i % 2
    tm = x_ref.shape[0]
    cap = comp.shape[1]

    @pl.when(i == 0)
    def _():
        comp[1, 0:MOE_BLOCK, :] = jnp.zeros((MOE_BLOCK, comp.shape[2]), F32)
        for e in range(N_EXPERTS):
            end = ends_ref[e]
            prev = ends_ref[e - 1] if e > 0 else 0

            @pl.when(end > prev)
            def _(end=end):
                cp = pltpu.make_async_copy(comp.at[1, pl.ds(0, MOE_BLOCK)],
                                           xs_hbm.at[pl.ds(pl.multiple_of(end - MOE_BLOCK, MOE_BLOCK), MOE_BLOCK)],
                                           sem.at[1])
                cp.start()
                cp.wait()

        def tail(b):
            return pltpu.make_async_copy(comp.at[1, pl.ds(0, MOE_BLOCK)],
                                         xs_hbm.at[pl.ds(pl.multiple_of(b * MOE_BLOCK, MOE_BLOCK), MOE_BLOCK)],
                                         sem.at[0])
        first_unused = ends_ref[N_EXPERTS - 1] // MOE_BLOCK
        n_all = xs_hbm.shape[0] // MOE_BLOCK
        lax.fori_loop(first_unused, n_all, lambda b, c: (tail(b).start(), c)[1], 0)
        lax.fori_loop(first_unused, n_all, lambda b, c: (tail(b).wait(), c)[1], 0)
        tot[0] = 0
        tot[1] = 0

    _wait_rows(comp.at[slot], xs_hbm, sem.at[slot], tot[slot])
    r = lax.broadcasted_iota(jnp.int32, (cap, tm), 0).astype(F32)
    sel = jnp.zeros((cap, tm), F32)
    for k in range(TOP_K):
        sel = jnp.where(r == info_ref[k:k + 1, :], 1.0, sel)
    sel = sel.astype(BF16)
    comp[slot] = _dot(sel, x_ref[...].astype(BF16))
    _segment_copies(seg_ref, comp.at[slot], xs_hbm, sem.at[slot], to_global=True)
    tot[slot] = seg_ref[0, 3, 0]

    @pl.when(i == n_steps - 1)
    def _():
        _wait_rows(comp.at[slot], xs_hbm, sem.at[slot], tot[slot])
        _wait_rows(comp.at[1 - slot], xs_hbm, sem.at[1 - slot], tot[1 - slot])


def _dispatch(ends_pad, seg, x2, info, n_slots, tm):
    T, D = x2.shape
    row = lambda i, ends: (i, 0)
    grid_spec = pltpu.PrefetchScalarGridSpec(
        num_scalar_prefetch=1, grid=(T // tm,),
        in_specs=[pl.BlockSpec((1, 4, LANES), lambda i, ends: (i, 0, 0), memory_space=pltpu.SMEM),
                  pl.BlockSpec((tm, D), row), pl.BlockSpec((LANES, tm), lambda i, ends: (0, i))],
        out_specs=pl.BlockSpec(memory_space=pl.ANY),
        scratch_shapes=[pltpu.VMEM((2, TOP_K * tm + N_EXPERTS * SEG_ALIGN, D), F32), pltpu.SMEM((2,), jnp.int32),
                        pltpu.SemaphoreType.DMA((2,))])
    return pl.pallas_call(_dispatch_kernel, name="dispatch", grid_spec=grid_spec,
                          out_shape=jax.ShapeDtypeStruct((n_slots, D), F32),
                          compiler_params=_cparams())(ends_pad, seg, x2, info)


def _moe_kernel(bexp_ref, nused_ref, x_ref, wgu_ref, bgu_ref, wd_ref, bd_ref, y_ref, wgu_s, wd_s):
    j = pl.program_id(0)
    n_used = nused_ref[0]
    n_tiles = wgu_s.shape[1] // GU_TILE

    @pl.when(j >= n_used)
    def _():
        y_ref[...] = jnp.zeros_like(y_ref)

    @pl.when(j < n_used)
    def _():
        @pl.when((j == 0) | (bexp_ref[j] != bexp_ref[jnp.maximum(j - 1, 0)]))
        def _():
            k = lax.broadcasted_iota(jnp.int32, (GU_TILE, GU_TILE), 0)
            n = lax.broadcasted_iota(jnp.int32, (GU_TILE, GU_TILE), 1)
            perm = jnp.where(k == jnp.where(n < LANES, 2 * n, 2 * (n - LANES) + 1), 1.0, 0.0).astype(BF16)
            for t in range(n_tiles):
                cols = slice(t * GU_TILE, (t + 1) * GU_TILE)
                wgu_s[:, cols] = _dot(wgu_ref[0, :, cols].astype(BF16), perm).astype(BF16)
            wd_s[...] = wd_ref[0].astype(BF16)

        xb = x_ref[...].astype(BF16)
        hs = []
        for t in range(n_tiles):
            cols = slice(t * GU_TILE, (t + 1) * GU_TILE)
            gu = _dot(xb, wgu_s[:, cols]) + bgu_ref[0, :, cols]
            gate = jnp.minimum(gu[:, :LANES], SWIGLU_LIMIT)
            lin = jnp.clip(gu[:, LANES:], -SWIGLU_LIMIT, SWIGLU_LIMIT)
            hs.append((gate * jax.nn.sigmoid(SWIGLU_ALPHA * gate) * (lin + 1.0)).astype(BF16))
        y_ref[...] = _dot(jnp.concatenate(hs, axis=1), wd_s[...]) + bd_ref[0]


def _moe(bexp, n_used, xs, wgu, bgu, wd, bd):
    n_blocks = bexp.shape[0]
    D = xs.shape[1]
    F2 = wgu.shape[2]
    F = wd.shape[1]
    wmap = lambda j, be, nu: (be[jnp.minimum(j, nu[0] - 1)], 0, 0)
    blk = lambda j, be, nu: (j, 0)
    used = lambda j, be, nu: (jnp.minimum(j, nu[0] - 1), 0)
    in_specs = [pl.BlockSpec((MOE_BLOCK, D), used),
                pl.BlockSpec((1, D, F2), wmap), pl.BlockSpec((1, 1, F2), wmap),
                pl.BlockSpec((1, F, D), wmap), pl.BlockSpec((1, 1, D), wmap)]
    grid_spec = pltpu.PrefetchScalarGridSpec(
        num_scalar_prefetch=2, grid=(n_blocks,), in_specs=in_specs,
        out_specs=pl.BlockSpec((MOE_BLOCK, D), blk),
        scratch_shapes=[pltpu.VMEM((D, F2), BF16), pltpu.VMEM((F, D), BF16)])
    return pl.pallas_call(_moe_kernel, name="moe", grid_spec=grid_spec,
                          out_shape=jax.ShapeDtypeStruct((n_blocks * MOE_BLOCK, D), F32),
                          compiler_params=_cparams())(bexp, n_used, xs, wgu, bgu, wd, bd)


def _combine_kernel(seg_ref, seg_nxt_ref, ys_hbm, info_ref, x_ref, g_ref, b_ref, out_ref, comp, sem, *, alpha):
    i = pl.program_id(0)
    n_steps = pl.num_programs(0)
    slot = i % 2
    tm = x_ref.shape[0]
    cap = comp.shape[1]

    @pl.when(i == 0)
    def _():
        comp[...] = jnp.zeros_like(comp)
        _segment_copies(seg_ref, comp.at[0], ys_hbm, sem.at[0], to_global=False)

    @pl.when(i + 1 < n_steps)
    def _():
        _segment_copies(seg_nxt_ref, comp.at[1 - slot], ys_hbm, sem.at[1 - slot], to_global=False)

    _wait_rows(comp.at[slot], ys_hbm, sem.at[slot], seg_ref[0, 3, 0])
    info = info_ref[...].T
    lane = lax.broadcasted_iota(jnp.int32, (tm, cap), 1).astype(F32)
    wmat = jnp.zeros((tm, cap), F32)
    for k in range(TOP_K):
        wmat = jnp.where(lane == info[:, k:k + 1], info[:, TOP_K + k:TOP_K + k + 1], wmat)
    ffn = _dot(wmat.astype(BF16), comp[slot].astype(BF16))
    out_ref[...] = _layer_norm(alpha * x_ref[...] + ffn, g_ref[...], b_ref[...])


def _combine(seg, ys, info, x2, g, b, alpha, tm):
    T, D = x2.shape
    n_t = T // tm
    row = lambda i: (i, 0)
    seg_spec = lambda m: pl.BlockSpec((1, 4, LANES), m, memory_space=pltpu.SMEM)
    in_specs = [seg_spec(lambda i: (i, 0, 0)), seg_spec(lambda i: (jnp.minimum(i + 1, n_t - 1), 0, 0)),
                pl.BlockSpec(memory_space=pl.ANY),
                pl.BlockSpec((LANES, tm), lambda i: (0, i)), pl.BlockSpec((tm, D), row),
                _full_spec(g.shape), _full_spec(b.shape)]
    cap = TOP_K * tm + N_EXPERTS * SEG_ALIGN
    return pl.pallas_call(functools.partial(_combine_kernel, alpha=alpha), name="combine", grid=(n_t,), in_specs=in_specs,
                          out_specs=pl.BlockSpec((tm, D), row),
                          out_shape=jax.ShapeDtypeStruct((T, D), F32),
                          scratch_shapes=[pltpu.VMEM((2, cap, D), F32), pltpu.SemaphoreType.DMA((2,))],
                          compiler_params=_cparams())(seg, seg, ys, info, x2, g, b)


def _tile(n, pref):
    return pref if n % pref == 0 else n


def _layer(x2d, mem2d, pos2d, B, S, depth, w_in, sinks, w_attn_o, lam_re, lam_im, log_dt, b_re, b_im,
           c_re, c_im, d_skip, w_glu_a, w_glu_b, w_out, ln1_g, ln1_b, wq_c, wk_c, wv_c, wo_c, ln2_g,
           ln2_b, w_router, b_router, w_gate_up, b_gate_up, w_down, b_down, ln3_g, ln3_b):
    T, D = x2d.shape
    alpha = (2 * depth) ** 0.25
    rep = N_Q_HEADS // N_KV_HEADS
    q_w = N_Q_HEADS * HEAD_DIM
    kv_w = N_KV_HEADS * HEAD_DIM
    ssm_w = SSM_GROUP * SSM_GROUPS
    P = SSM_STATE

    o_k, o_v, o_s = q_w, q_w + kv_w, q_w + 2 * kv_w
    o_ga, o_gs = o_s + ssm_w, o_s + ssm_w + D
    wq = w_in[:, :o_k].reshape(D, N_KV_HEADS, rep, HEAD_DIM).transpose(0, 2, 1, 3).reshape(D, q_w).astype(BF16)
    wk = w_in[:, o_k:o_v].astype(BF16)
    wv = w_in[:, o_v:o_s].astype(BF16)
    wu = w_in[:, o_s:o_ga].astype(BF16)
    wga = w_in[:, o_ga:o_gs].astype(BF16)
    wgs = w_in[:, o_gs:].astype(BF16)
    wo_attn = w_attn_o.reshape(N_KV_HEADS, rep, HEAD_DIM, D).transpose(1, 0, 2, 3).reshape(q_w, D).astype(BF16)
    half = HEAD_DIM // 2
    inv_freq = jnp.power(ROPE_THETA, -jnp.arange(half, dtype=F32) / half)
    invf = jnp.tile(inv_freq, LANES // half)[None, :]
    sink_rows = jnp.repeat(sinks.astype(F32).reshape(N_KV_HEADS, rep), WINDOW, axis=1)
    sink_cols = jnp.full((N_KV_HEADS, rep * WINDOW, 2 * WINDOW), NEG_BIG, F32).at[:, :, 0].set(sink_rows)

    tm1 = _tile(T, 1024)
    q2, k, v, ut, ga, gs = _inproj(x2d, pos2d, invf, wq, wk, wv, wu, wga, wgs, tm1)
    att = _swa(q2, k, v, sink_cols, wo_attn, ga, B, S, _tile(S, 512))

    lam_row = jnp.stack([jnp.concatenate([lam_re, lam_re], -1), jnp.concatenate([lam_im, lam_im], -1)], 1)
    lam_col = jnp.swapaxes(lam_row, 1, 2)
    bt_re = jnp.swapaxes(b_re, 1, 2)
    bt_im = jnp.swapaxes(b_im, 1, 2)
    bt_a = jnp.concatenate([bt_re, bt_im], -1)
    bt_b = jnp.concatenate([-bt_im, bt_re], -1)
    ct_re = jnp.swapaxes(c_re, 1, 2)
    ct_im = jnp.swapaxes(c_im, 1, 2)
    ct_a = jnp.concatenate([ct_re, -ct_im], 1)
    ct_b = jnp.concatenate([-ct_im, -ct_re], 1)
    dsk = jnp.repeat(d_skip.reshape(SSM_GROUPS, SSM_GROUP), SSM_GROUP, axis=1)[:, :, None]
    yt = _ssm(ut, lam_row, lam_col, log_dt.reshape(SSM_GROUPS, 1, 1), bt_a, bt_b, c_re, c_im, ct_a, ct_b, dsk,
              S // LANES)

    x1 = _mix(yt, att, gs, x2d, w_glu_a.astype(BF16), w_glu_b.astype(BF16), w_out.astype(BF16),
              ln1_g[None, :], ln1_b[None, :], alpha, _tile(T, 1024))

    kc, vc = _memkv(mem2d, wk_c.astype(BF16), wv_c.astype(BF16), _tile(mem2d.shape[0], 512))
    wr = jnp.zeros((LANES, D), F32).at[:N_EXPERTS, :].set(w_router.T).astype(BF16)
    br = jnp.full((LANES, 1), NEG_BIG, F32).at[:N_EXPERTS, 0].set(b_router)
    x2, logits = _cross(x1, kc, vc, wq_c.astype(BF16), wo_c.astype(BF16), ln2_g[None, :], ln2_b[None, :],
                        wr, br, alpha, S, _tile(S, 1024))

    tm_r = _tile(T, ROUTE_TILE)
    n_tiles = T // tm_r
    info, segf, counts = _route(logits, tm_r)
    total = counts[:, 0].astype(jnp.int32)
    padded = (total + MOE_BLOCK - 1) // MOE_BLOCK * MOE_BLOCK
    ends_pad = jnp.cumsum(padded)
    start_pad = ends_pad - padded
    n_blocks = -(-(T * TOP_K + n_tiles * N_EXPERTS * (SEG_ALIGN - 1)) // MOE_BLOCK) + N_EXPERTS
    segi = segf.astype(jnp.int32)
    seg_n, seg_off, seg_start = segi[:, :, 0], segi[:, :, 1], segi[:, :, 2] + start_pad[None, :]
    seg_tot = jnp.broadcast_to(jnp.sum(seg_n, axis=1, keepdims=True), seg_n.shape)
    seg = jnp.stack([seg_n, seg_off, seg_start, seg_tot], axis=1)
    seg = jnp.pad(seg, ((0, 0), (0, 0), (0, LANES - N_EXPERTS)))
    block_start = jnp.arange(n_blocks, dtype=jnp.int32) * MOE_BLOCK
    bexp = jnp.minimum(jnp.sum(block_start[:, None] >= ends_pad[None, :], axis=1), N_EXPERTS - 1).astype(jnp.int32)
    n_used = (ends_pad[-1] // MOE_BLOCK).astype(jnp.int32)[None]
    n_gu_tiles = b_gate_up.shape[1] // GU_TILE
    bgu = b_gate_up.reshape(N_EXPERTS, n_gu_tiles, LANES, 2).transpose(0, 1, 3, 2).reshape(N_EXPERTS, 1, -1)
    xs = _dispatch(ends_pad, seg, x2, info, n_blocks * MOE_BLOCK, tm_r)
    ys = _moe(bexp, n_used, xs, w_gate_up, bgu, w_down, b_down[:, None, :])
    return _combine(seg, ys, info, x2, ln3_g[None, :], ln3_b[None, :], alpha, tm_r)


def kernel(x, mem, positions, w_in, sinks, w_attn_o, lam_re, lam_im, log_dt, b_re, b_im, c_re, c_im, d_skip,
           w_glu_a, w_glu_b, w_out, ln1_g, ln1_b, wq_c, wk_c, wv_c, wo_c, ln2_g, ln2_b, w_router, b_router,
           w_gate_up, b_gate_up, w_down, b_down, ln3_g, ln3_b):
    B, S, D = x.shape
    depth = w_in.shape[0]
    x2d = x.reshape(B * S, D)
    mem2d = mem.reshape(-1, D)
    pos2d = positions.reshape(B * S, 1)
    per_layer = (w_in, sinks, w_attn_o, lam_re, lam_im, log_dt, b_re, b_im, c_re, c_im, d_skip, w_glu_a,
                 w_glu_b, w_out, ln1_g, ln1_b, wq_c, wk_c, wv_c, wo_c, ln2_g, ln2_b, w_router, b_router,
                 w_gate_up, b_gate_up, w_down, b_down, ln3_g, ln3_b)
    for l in range(depth):
        x2d = _layer(x2d, mem2d, pos2d, B, S, depth, *(w[l] for w in per_layer))
    return x2d.reshape(B, S, D)
```

```python
import functools

import jax
import jax.numpy as jnp
from jax import lax
from jax.experimental import pallas as pl
from jax.experimental.pallas import tpu as pltpu

N_Q_HEADS = 16
N_KV_HEADS = 2
HEAD_DIM = 64
WINDOW = 128
ROPE_THETA = 10000.0
SSM_GROUP = 16
SSM_GROUPS = 32
SSM_STATE = 64
N_CROSS_HEADS = 4
N_EXPERTS = 32
TOP_K = 4
SWIGLU_ALPHA = 1.702
SWIGLU_LIMIT = 7.0
MOE_BLOCK = 512
ROUTE_TILE = 512
LN_EPS = 1e-5

LANES = 128
VMEM_LIMIT_BYTES = 56 * 1024 * 1024

NEG_BIG = -1e30
BF16 = jnp.bfloat16
F32 = jnp.float32
HI = lax.Precision.HIGHEST


def _cparams(n_axes=1):
    return pltpu.CompilerParams(dimension_semantics=("arbitrary",) * n_axes,
                                vmem_limit_bytes=VMEM_LIMIT_BYTES)


def _full_spec(shape):
    n = len(shape)
    return pl.BlockSpec(shape, lambda *_: (0,) * n, pipeline_mode=pl.Buffered(1))


def _dot(a, b):
    return jnp.dot(a, b, preferred_element_type=F32)


def _dot_nt(a, b):
    return lax.dot_general(a, b, (((1,), (1,)), ((), ())), preferred_element_type=F32)


def _layer_norm(y, g, b):
    mu = jnp.mean(y, axis=-1, keepdims=True)
    d = y - mu
    var = jnp.mean(d * d, axis=-1, keepdims=True)
    return d * lax.rsqrt(var + LN_EPS) * g + b


def _rope(t, cos, sin_signed, first_half):
    half = HEAD_DIM // 2
    partner = jnp.where(first_half, pltpu.roll(t, LANES - half, axis=1), pltpu.roll(t, half, axis=1))
    return t * cos + partner * sin_signed


def _inproj_kernel(x_ref, pos_ref, invf_ref, wq_ref, wk_ref, wv_ref, wu_ref, wga_ref, wgs_ref,
                   q2_ref, k_ref, v_ref, ut_ref, ga_ref, gs_ref):
    tm = x_ref.shape[0]
    xb = x_ref[...].astype(BF16)
    ang = pos_ref[...].astype(F32) * invf_ref[...]
    cos = jnp.cos(ang)
    sin = jnp.sin(ang)
    first_half = (lax.broadcasted_iota(jnp.int32, (tm, LANES), 1) % HEAD_DIM) < (HEAD_DIM // 2)
    first_half_w = (lax.broadcasted_iota(jnp.int32, (WINDOW, LANES), 1) % HEAD_DIM) < (HEAD_DIM // 2)
    sin_signed = jnp.where(first_half, -sin, sin)

    q = _dot(xb, wq_ref[...])
    n_rep = q.shape[1] // LANES
    scale = HEAD_DIM ** -0.5
    for j in range(tm // WINDOW):
        rows = slice(j * WINDOW, (j + 1) * WINDOW)
        for r in range(n_rep):
            t = _rope(q[rows, r * LANES:(r + 1) * LANES], cos[rows], sin_signed[rows], first_half_w)
            base = (j * n_rep + r) * WINDOW
            q2_ref[base:base + WINDOW, :] = (t * scale).astype(BF16)
    k_ref[...] = _rope(_dot(xb, wk_ref[...]), cos, sin_signed, first_half).astype(BF16)
    v_ref[...] = _dot(xb, wv_ref[...]).astype(BF16)
    u = _dot(xb, wu_ref[...])
    for j in range(tm // LANES):
        ut_ref[j] = u[j * LANES:(j + 1) * LANES, :].T
    ga_ref[...] = jax.nn.sigmoid(_dot(xb, wga_ref[...])).astype(BF16)
    gs_ref[...] = jax.nn.sigmoid(_dot(xb, wgs_ref[...])).astype(BF16)


def _inproj(x2d, pos2d, invf, wq, wk, wv, wu, wga, wgs, tm):
    T, D = x2d.shape
    n_rep = wq.shape[1] // LANES
    ssm_w = wu.shape[1]
    row = lambda i: (i, 0)
    out_shape = (
        jax.ShapeDtypeStruct((T * n_rep, LANES), BF16),
        jax.ShapeDtypeStruct((T, LANES), BF16),
        jax.ShapeDtypeStruct((T, LANES), BF16),
        jax.ShapeDtypeStruct((T // LANES, ssm_w, LANES), F32),
        jax.ShapeDtypeStruct((T, D), BF16),
        jax.ShapeDtypeStruct((T, D), BF16),
    )
    in_specs = [pl.BlockSpec((tm, D), row), pl.BlockSpec((tm, 1), row), _full_spec(invf.shape),
                _full_spec(wq.shape), _full_spec(wk.shape), _full_spec(wv.shape),
                _full_spec(wu.shape), _full_spec(wga.shape), _full_spec(wgs.shape)]
    out_specs = (pl.BlockSpec((tm * n_rep, LANES), row), pl.BlockSpec((tm, LANES), row),
                 pl.BlockSpec((tm, LANES), row),
                 pl.BlockSpec((tm // LANES, ssm_w, LANES), lambda i: (i, 0, 0)),
                 pl.BlockSpec((tm, D), row), pl.BlockSpec((tm, D), row))
    return pl.pallas_call(_inproj_kernel, name="inproj", grid=(T // tm,), in_specs=in_specs, out_specs=out_specs,
                          out_shape=out_shape, compiler_params=_cparams())(
        x2d, pos2d, invf, wq, wk, wv, wu, wga, wgs)


def _swa_kernel(q2_ref, kc_ref, kp_ref, vc_ref, vp_ref, fill_ref, wo_ref, ga_ref, out_ref, cat_ref):
    i = pl.program_id(1)
    tq = kc_ref.shape[0]
    n_sub = tq // WINDOW
    rep = N_Q_HEADS // N_KV_HEADS
    rows_all = rep * WINDOW
    kfull = jnp.concatenate([kp_ref[...], kc_ref[...]], axis=0)
    vfull = jnp.concatenate([vp_ref[...], vc_ref[...]], axis=0)
    lane = lax.broadcasted_iota(jnp.int32, (2 * WINDOW, LANES), 1)
    qi = lax.broadcasted_iota(jnp.int32, (rows_all, 2 * WINDOW), 0) % WINDOW
    ci = lax.broadcasted_iota(jnp.int32, (rows_all, 2 * WINDOW), 1)
    local = (ci > qi) & (ci <= qi + WINDOW)
    out_lane = lax.broadcasted_iota(jnp.int32, (rows_all, LANES), 1)
    band_row = lax.broadcasted_iota(jnp.int32, (2 * WINDOW, LANES), 0)
    for j in range(n_sub):
        qs = q2_ref[j * rows_all:(j + 1) * rows_all, :]
        kb = kfull[j * WINDOW:(j + 2) * WINDOW, :]
        vb = vfull[j * WINDOW:(j + 2) * WINDOW, :]
        vb = jnp.where(band_row == 0, jnp.zeros_like(vb), vb)
        mask = local
        if j == 0:
            mask = mask & ((ci >= WINDOW) | (i > 0))
        o = None
        for g in range(N_KV_HEADS):
            in_group = (lane >= g * HEAD_DIM) & (lane < (g + 1) * HEAD_DIM)
            kg = jnp.where(in_group, kb, jnp.zeros_like(kb))
            s = _dot_nt(qs, kg)
            s = jnp.where(mask, s, fill_ref[g])
            m = jnp.max(s, axis=-1, keepdims=True)
            p = jnp.exp(s - m)
            denom = jnp.sum(p, axis=-1, keepdims=True)
            og = _dot(p.astype(BF16), vb) * (1.0 / denom)
            o = og if o is None else jnp.where(out_lane < g * HEAD_DIM, o, og)
        ob = o.astype(BF16)
        for r in range(rep):
            cat_ref[j * WINDOW:(j + 1) * WINDOW, r * LANES:(r + 1) * LANES] = ob[r * WINDOW:(r + 1) * WINDOW, :]
    attn = _dot(cat_ref[...], wo_ref[...])
    out_ref[...] = (attn * ga_ref[...].astype(F32)).astype(BF16)


def _swa(q2, k, v, sink_cols, wo, ga, B, S, tq):
    T, D = ga.shape
    rep = N_Q_HEADS // N_KV_HEADS
    n_i = S // tq
    n_sub = tq // WINDOW
    in_specs = [
        pl.BlockSpec((tq * rep, LANES), lambda b, i: (b * n_i + i, 0)),
        pl.BlockSpec((tq, LANES), lambda b, i: (b * n_i + i, 0)),
        pl.BlockSpec((WINDOW, LANES), lambda b, i: (b * (S // WINDOW) + jnp.maximum(i * n_sub - 1, 0), 0)),
        pl.BlockSpec((tq, LANES), lambda b, i: (b * n_i + i, 0)),
        pl.BlockSpec((WINDOW, LANES), lambda b, i: (b * (S // WINDOW) + jnp.maximum(i * n_sub - 1, 0), 0)),
        _full_spec(sink_cols.shape), _full_spec(wo.shape),
        pl.BlockSpec((tq, D), lambda b, i: (b * n_i + i, 0)),
    ]
    return pl.pallas_call(
        _swa_kernel, name="swa", grid=(B, n_i), in_specs=in_specs,
        out_specs=pl.BlockSpec((tq, D), lambda b, i: (b * n_i + i, 0)),
        out_shape=jax.ShapeDtypeStruct((T, D), BF16),
        scratch_shapes=[pltpu.VMEM((tq, D), BF16)],
        compiler_params=_cparams(2))(q2, k, k, v, v, sink_cols, wo, ga)


def _ssm_prep_kernel(lam_row_ref, lam_col_ref, dt_ref, bt_a_ref, bt_b_ref, c_re_ref, c_im_ref,
                     ct_a_ref, ct_b_ref, dsk_ref, toep_ref, wst_ref, wout_ref, apow_ref):
    L = LANES
    P = SSM_STATE
    dt = jnp.exp(dt_ref[0])
    lr2 = lam_row_ref[0, 0:1, :]
    li2 = lam_row_ref[0, 1:2, :]
    mag = jnp.exp(lr2 * dt)
    ar = mag * jnp.cos(li2 * dt)
    ai = mag * jnp.sin(li2 * dt)
    den = lr2 * lr2 + li2 * li2
    f_re = ((ar - 1.0) * lr2 + ai * li2) / den
    f_im = (ai * lr2 - (ar - 1.0) * li2) / den
    bt_a = bt_a_ref[0]
    bt_b = bt_b_ref[0]
    bb_a = f_re * bt_a + f_im * bt_b
    bb_b = f_re * bt_b - f_im * bt_a

    tau_rev = (L - 1 - lax.broadcasted_iota(jnp.int32, (L, 2 * P), 0)).astype(F32)
    g_mag = jnp.exp(lr2 * dt * tau_rev)
    g_re = g_mag * jnp.cos(li2 * dt * tau_rev)
    g_im = g_mag * jnp.sin(li2 * dt * tau_rev)
    for i in range(SSM_GROUP):
        wst_ref[0, i * L:(i + 1) * L, :] = (g_re * bb_a[i:i + 1, :] + g_im * bb_b[i:i + 1, :]).astype(BF16)

    lane2 = lax.broadcasted_iota(jnp.int32, (1, 2 * P), 1)
    for kk in range(4):
        n = float(L * (1 << kk))
        pm = jnp.exp(lr2 * dt * n)
        p_re = pm * jnp.cos(li2 * dt * n)
        p_im = pm * jnp.sin(li2 * dt * n)
        apow_ref[0, 2 * kk:2 * kk + 1, :] = p_re
        apow_ref[0, 2 * kk + 1:2 * kk + 2, :] = jnp.where(lane2 < P, -p_im, p_im)

    lrc = lam_col_ref[0, :, 0:1]
    lic = lam_col_ref[0, :, 1:2]
    tau = lax.broadcasted_iota(jnp.int32, (2 * P, L), 1).astype(F32)
    e0_mag = jnp.exp(lrc * dt * tau)
    e0_re = e0_mag * jnp.cos(lic * dt * tau)
    e0_im = e0_mag * jnp.sin(lic * dt * tau)
    e1_mag = jnp.exp(lrc * dt * (tau + 1.0))
    e1_re = e1_mag * jnp.cos(lic * dt * (tau + 1.0))
    e1_im = e1_mag * jnp.sin(lic * dt * (tau + 1.0))

    ct_a = ct_a_ref[0]
    ct_b = ct_b_ref[0]
    for o in range(SSM_GROUP):
        wout_ref[0, :, o * L:(o + 1) * L] = (ct_a[:, o:o + 1] * e1_re + ct_b[:, o:o + 1] * e1_im).astype(BF16)

    c_re = c_re_ref[0]
    c_im = c_im_ref[0]
    bb_re = bb_a[:, :P]
    bb_im = bb_a[:, P:]
    m_re = (c_re[:, None, :] * bb_re[None, :, :] - c_im[:, None, :] * bb_im[None, :, :])
    m_im = (c_re[:, None, :] * bb_im[None, :, :] + c_im[:, None, :] * bb_re[None, :, :])
    m_re = m_re.reshape(SSM_GROUP * SSM_GROUP, P)
    m_im = m_im.reshape(SSM_GROUP * SSM_GROUP, P)
    kt = (jnp.dot(m_re, e0_re[:P, :], precision=HI, preferred_element_type=F32)
          - jnp.dot(m_im, e0_im[:P, :], precision=HI, preferred_element_type=F32))
    rowi = lax.broadcasted_iota(jnp.int32, (SSM_GROUP * SSM_GROUP, L), 0)
    coli = lax.broadcasted_iota(jnp.int32, (SSM_GROUP * SSM_GROUP, L), 1)
    dsk = dsk_ref[0]
    kt = kt + jnp.where((coli == 0) & ((rowi // SSM_GROUP) == (rowi % SSM_GROUP)), dsk, 0.0)

    cc = lax.broadcasted_iota(jnp.int32, (L, L), 0)
    cp = lax.broadcasted_iota(jnp.int32, (L, L), 1)
    causal = cp >= cc
    for o in range(SSM_GROUP):
        for i in range(SSM_GROUP):
            kv = kt[o * SSM_GROUP + i:o * SSM_GROUP + i + 1, :]
            blk = pltpu.roll(jnp.broadcast_to(kv, (L, L)), 0, axis=1, stride=1, stride_axis=0)
            blk = jnp.where(causal, blk, 0.0)
            toep_ref[0, i * L:(i + 1) * L, o * L:(o + 1) * L] = blk.astype(BF16)


def _ssm_scan_kernel(ut_ref, toep_ref, wst_ref, wout_ref, apow_ref, yt_ref, *, n_chunks):
    nb = ut_ref.shape[0]
    P = SSM_STATE
    ut2 = ut_ref.reshape(nb * SSM_GROUP, LANES)
    lhs = jnp.concatenate([ut2[pl.ds(i, nb, stride=SSM_GROUP), :] for i in range(SSM_GROUP)],
                          axis=1).astype(BF16)
    y = _dot(lhs, toep_ref[0])
    st = _dot(lhs, wst_ref[0])
    srow = lax.broadcasted_iota(jnp.int32, (nb, 2 * P), 0) % n_chunks
    kk = 0
    while (1 << kk) < n_chunks:
        d = 1 << kk
        pa = apow_ref[0, 2 * kk:2 * kk + 1, :]
        pb = apow_ref[0, 2 * kk + 1:2 * kk + 2, :]
        prev = pltpu.roll(st, d, axis=0)
        prev = prev * pa + pltpu.roll(prev, P, axis=1) * pb
        st = st + jnp.where(srow >= d, prev, 0.0)
        kk += 1
    h0 = jnp.where(srow >= 1, pltpu.roll(st, 1, axis=0), 0.0)
    y = y + _dot(h0.astype(BF16), wout_ref[0])
    for o in range(SSM_GROUP):
        yt_ref[o] = y[:, o * LANES:(o + 1) * LANES]


def _ssm_kernel(*refs, n_chunks):
    params, (ut_ref, yt_ref), ops = refs[:10], refs[10:12], refs[12:]
    sets = (ops[0::2], ops[1::2])
    g = pl.program_id(0)

    @pl.when(g == 0)
    def _():
        _ssm_prep_kernel(*params, *sets[0])

    for p in range(2):
        @pl.when((g > 0) & (g % 2 == p))
        def _(p=p):
            _ssm_prep_kernel(*params, *sets[p])
            _ssm_scan_kernel(ut_ref, *sets[1 - p], yt_ref, n_chunks=n_chunks)


def _ssm(ut, lam_row, lam_col, log_dt, bt_a, bt_b, c_re, c_im, ct_a, ct_b, dsk, n_chunks):
    nb, ssm_w, _ = ut.shape
    G = SSM_GROUPS
    KW = SSM_GROUP * LANES
    params = [lam_row, lam_col, log_dt, bt_a, bt_b, c_re, c_im, ct_a, ct_b, dsk]
    build = lambda g: (jnp.minimum(g, G - 1), 0, 0)
    in_specs = [pl.BlockSpec((1,) + a.shape[1:], build) for a in params]
    in_specs.append(pl.BlockSpec((nb, SSM_GROUP, LANES), lambda g: (0, jnp.maximum(g - 1, 0), 0)))
    op_shapes = [pltpu.VMEM((1, KW, KW), BF16), pltpu.VMEM((1, KW, 2 * SSM_STATE), BF16),
                 pltpu.VMEM((1, 2 * SSM_STATE, KW), BF16), pltpu.VMEM((1, 8, 2 * SSM_STATE), F32)]
    return pl.pallas_call(
        functools.partial(_ssm_kernel, n_chunks=n_chunks), name="ssm", grid=(G + 1,), in_specs=in_specs,
        out_specs=pl.BlockSpec((SSM_GROUP, nb, LANES), lambda g: (jnp.maximum(g - 1, 0), 0, 0)),
        out_shape=jax.ShapeDtypeStruct((ssm_w, nb, LANES), F32),
        scratch_shapes=[s for s in op_shapes for _ in range(2)],
        compiler_params=_cparams())(*params, ut)


def _mix_kernel(yt_ref, att_ref, gs_ref, x_ref, wa_ref, wb_ref, wo_ref, g_ref, b_ref, out_ref, zt_ref,
                *, alpha):
    n_ch, n_blk = yt_ref.shape[0], yt_ref.shape[1]
    yt2 = yt_ref.reshape(n_ch * n_blk, LANES)
    for j in range(n_blk):
        y = yt2[pl.ds(j, n_ch, stride=n_blk), :]
        z = jax.nn.gelu(y, approximate=True)
        zt_ref[j * LANES:(j + 1) * LANES, :] = z.T.astype(BF16)
    z = zt_ref[...]
    ssm_out = _dot(z, wa_ref[...]) * jax.nn.sigmoid(_dot(z, wb_ref[...]))
    mixed = att_ref[...].astype(F32) + gs_ref[...].astype(F32) * ssm_out
    y = alpha * x_ref[...] + _dot(mixed.astype(BF16), wo_ref[...])
    out_ref[...] = _layer_norm(y, g_ref[...], b_ref[...])


def _mix(yt, att, gs, x2d, wa, wb, wo, g, b, alpha, tm):
    T, D = x2d.shape
    ssm_w = yt.shape[0]
    row = lambda i: (i, 0)
    in_specs = [pl.BlockSpec((ssm_w, tm // LANES, LANES), lambda i: (0, i, 0)),
                pl.BlockSpec((tm, D), row), pl.BlockSpec((tm, D), row), pl.BlockSpec((tm, D), row),
                _full_spec(wa.shape), _full_spec(wb.shape), _full_spec(wo.shape),
                _full_spec(g.shape), _full_spec(b.shape)]
    return pl.pallas_call(functools.partial(_mix_kernel, alpha=alpha), name="mix", grid=(T // tm,), in_specs=in_specs,
                          out_specs=pl.BlockSpec((tm, D), row),
                          out_shape=jax.ShapeDtypeStruct((T, D), F32),
                          scratch_shapes=[pltpu.VMEM((tm, ssm_w), BF16)],
                          compiler_params=_cparams())(yt, att, gs, x2d, wa, wb, wo, g, b)


def _memkv_kernel(m_ref, wk_ref, wv_ref, k_ref, v_ref):
    mb = m_ref[...].astype(BF16)
    k_ref[...] = _dot(mb, wk_ref[...]).astype(BF16)
    v_ref[...] = _dot(mb, wv_ref[...]).astype(BF16)


def _memkv(mem2d, wk, wv, tm):
    R, D = mem2d.shape
    row = lambda i: (i, 0)
    return pl.pallas_call(_memkv_kernel, name="memkv", grid=(R // tm,),
                          in_specs=[pl.BlockSpec((tm, D), row), _full_spec(wk.shape), _full_spec(wv.shape)],
                          out_specs=(pl.BlockSpec((tm, D), row), pl.BlockSpec((tm, D), row)),
                          out_shape=(jax.ShapeDtypeStruct((R, D), BF16), jax.ShapeDtypeStruct((R, D), BF16)),
                          compiler_params=_cparams())(mem2d, wk, wv)


def _cross_kernel(x_ref, k_ref, v_ref, wq_ref, wo_ref, g_ref, b_ref, wr_ref, br_ref, out_ref, logit_ref,
                  cat_ref, *, alpha):
    x = x_ref[...]
    D = x.shape[1]
    hd = D // N_CROSS_HEADS
    q = (_dot(x.astype(BF16), wq_ref[...]) * (hd ** -0.5)).astype(BF16)
    for h in range(N_CROSS_HEADS):
        cols = slice(h * hd, (h + 1) * hd)
        s = _dot_nt(q[:, cols], k_ref[:, cols])
        m = jnp.max(s, axis=-1, keepdims=True)
        p = jnp.exp(s - m)
        w = (p / jnp.sum(p, axis=-1, keepdims=True)).astype(BF16)
        cat_ref[:, cols] = _dot(w, v_ref[:, cols]).astype(BF16)
    y = alpha * x + _dot(cat_ref[...], wo_ref[...])
    x2 = _layer_norm(y, g_ref[...], b_ref[...])
    out_ref[...] = x2
    logit_ref[...] = _dot_nt(wr_ref[...], x2.astype(BF16)) + br_ref[...]


def _cross(x1, kc, vc, wq, wo, g, b, wr, br, alpha, S, tm):
    T, D = x1.shape
    n_mem = kc.shape[0] // (T // S)
    row = lambda i: (i, 0)
    per_b = S // tm
    in_specs = [pl.BlockSpec((tm, D), row),
                pl.BlockSpec((n_mem, D), lambda i: (i // per_b, 0)),
                pl.BlockSpec((n_mem, D), lambda i: (i // per_b, 0)),
                _full_spec(wq.shape), _full_spec(wo.shape), _full_spec(g.shape), _full_spec(b.shape),
                _full_spec(wr.shape), _full_spec(br.shape)]
    return pl.pallas_call(functools.partial(_cross_kernel, alpha=alpha), name="cross", grid=(T // tm,), in_specs=in_specs,
                          out_specs=(pl.BlockSpec((tm, D), row), pl.BlockSpec((LANES, tm), lambda i: (0, i))),
                          out_shape=(jax.ShapeDtypeStruct((T, D), F32), jax.ShapeDtypeStruct((LANES, T), F32)),
                          scratch_shapes=[pltpu.VMEM((tm, D), BF16)],
                          compiler_params=_cparams())(x1, kc, vc, wq, wo, g, b, wr, br)


SEG_ALIGN = 8


def _route_kernel(lt_ref, info_ref, seg_ref, count_ref, carry_ref):
    tm = lt_ref.shape[1]
    E = N_EXPERTS

    @pl.when(pl.program_id(0) == 0)
    def _():
        carry_ref[...] = jnp.zeros_like(carry_ref)

    l = lt_ref[0:E, :]
    erow = lax.broadcasted_iota(jnp.int32, (E, tm), 0)
    rank = jnp.zeros((E, tm), F32)
    for e2 in range(E):
        other = l[e2:e2 + 1, :]
        tie = jnp.where(erow > e2, 1.0, 0.0)
        rank = rank + jnp.where(other > l, 1.0, jnp.where(other == l, tie, 0.0))
    chosen = rank < float(TOP_K)
    onehot = jnp.where(chosen, 1.0, 0.0)
    p = jnp.where(chosen, jnp.exp(l - jnp.max(l, axis=0, keepdims=True)), 0.0)
    gate = p / jnp.sum(p, axis=0, keepdims=True)
    a = lax.broadcasted_iota(jnp.int32, (tm, tm), 0)
    b = lax.broadcasted_iota(jnp.int32, (tm, tm), 1)
    cum = _dot(onehot.astype(BF16), jnp.where(a < b, 1.0, 0.0).astype(BF16))
    n_seg = jnp.floor((jnp.sum(onehot, axis=1, keepdims=True) + (SEG_ALIGN - 1.0)) * (1.0 / SEG_ALIGN)) * SEG_ALIGN
    n_seg = jnp.broadcast_to(n_seg, (E, LANES))
    erow_l = lax.broadcasted_iota(jnp.int32, (E, LANES), 0)
    seg_off = jnp.zeros((E, LANES), F32)
    for e2 in range(E - 1):
        seg_off = seg_off + jnp.where(erow_l > e2, n_seg[e2:e2 + 1, :], 0.0)
    pos = cum + seg_off[:, 0:1]
    rows = [jnp.sum(jnp.where(rank == float(k), pos, 0.0), axis=0, keepdims=True) for k in range(TOP_K)]
    rows += [jnp.sum(jnp.where(rank == float(k), gate, 0.0), axis=0, keepdims=True) for k in range(TOP_K)]
    rows.append(jnp.zeros((info_ref.shape[0] - 2 * TOP_K, tm), F32))
    info_ref[...] = jnp.concatenate(rows, axis=0)
    carry = carry_ref[...]
    lane = lax.broadcasted_iota(jnp.int32, (E, LANES), 1)
    seg_ref[0] = jnp.where(lane == 0, n_seg, jnp.where(lane == 1, seg_off, jnp.where(lane == 2, carry, 0.0)))
    carry_ref[...] = carry + n_seg
    count_ref[...] = carry_ref[...]


def _route(logits_t, tm):
    T = logits_t.shape[1]
    E = N_EXPERTS
    col = lambda i: (0, i)
    return pl.pallas_call(_route_kernel, name="route", grid=(T // tm,),
                          in_specs=[pl.BlockSpec((LANES, tm), col)],
                          out_specs=(pl.BlockSpec((LANES, tm), col), pl.BlockSpec((1, E, LANES), lambda i: (i, 0, 0)),
                                     pl.BlockSpec((E, LANES), lambda i: (0, 0))),
                          out_shape=(jax.ShapeDtypeStruct((LANES, T), F32),
                                     jax.ShapeDtypeStruct((T // tm, E, LANES), F32),
                                     jax.ShapeDtypeStruct((E, LANES), F32)),
                          scratch_shapes=[pltpu.VMEM((E, LANES), F32)],
                          compiler_params=_cparams())(logits_t)


GU_TILE = 2 * LANES


SEG_SIZES = tuple(ROUTE_TILE >> s for s in range(ROUTE_TILE.bit_length()) if (ROUTE_TILE >> s) >= SEG_ALIGN)
SEG_LARGE = 128


def _segment_copies(seg_ref, local_ref, global_ref, sem, to_global):
    def pieces(n, off, start, sizes, prio):
        for sz in sizes:
            take = (n & sz) != 0

            @pl.when(take)
            def _(off=off, start=start, sz=sz):
                loc = local_ref.at[pl.ds(pl.multiple_of(off, SEG_ALIGN), sz)]
                glo = global_ref.at[pl.ds(pl.multiple_of(start, SEG_ALIGN), sz)]
                if to_global:
                    pltpu.make_async_copy(loc, glo, sem).start(priority=prio)
                else:
                    pltpu.make_async_copy(glo, loc, sem).start(priority=prio)

            step = jnp.where(take, sz, 0)
            off = off + step
            start = start + step

    n_large = sum(1 for sz in SEG_SIZES if sz >= SEG_LARGE)
    for e in range(N_EXPERTS):
        n = seg_ref[0, 0, e]
        off = seg_ref[0, 1, e]
        start = seg_ref[0, 2, e]

        @pl.when(n >= SEG_LARGE)
        def _(n=n, off=off, start=start, e=e):
            pieces(n, off, start, SEG_SIZES[:n_large], e % 2)

        large = n & ~(SEG_LARGE - 1)
        pieces(n, off + large, start + large, SEG_SIZES[n_large:], e % 2)


def _wait_rows(local_ref, global_ref, sem, n_rows):
    @pl.when(n_rows > 0)
    def _():
        n = pl.multiple_of(n_rows, SEG_ALIGN)
        pltpu.make_async_copy(global_ref.at[pl.ds(0, n)], local_ref.at[pl.ds(0, n)], sem).wait()


def _dispatch_kernel(ends_ref, seg_ref, x_ref, info_ref, xs_hbm, comp, tot, sem):
    i = pl.program_id(0)
    n_steps = pl.num_programs(0)
    slot = i % 2
    tm = x_ref.shape[0]
    cap = comp.shape[1]

    @pl.when(i == 0)
    def _():
        comp[1, 0:MOE_BLOCK, :] = jnp.zeros((MOE_BLOCK, comp.shape[2]), F32)
        for e in range(N_EXPERTS):
            end = ends_ref[e]
            prev = ends_ref[e - 1] if e > 0 else 0

            @pl.when(end > prev)
            def _(end=end):
                cp = pltpu.make_async_copy(comp.at[1, pl.ds(0, MOE_BLOCK)],
                                           xs_hbm.at[pl.ds(pl.multiple_of(end - MOE_BLOCK, MOE_BLOCK), MOE_BLOCK)],
                                           sem.at[1])
                cp.start()
                cp.wait()

        def tail(b):
            return pltpu.make_async_copy(comp.at[1, pl.ds(0, MOE_BLOCK)],
                                         xs_hbm.at[pl.ds(pl.multiple_of(b * MOE_BLOCK, MOE_BLOCK), MOE_BLOCK)],
                                         sem.at[0])
        first_unused = ends_ref[N_EXPERTS - 1] // MOE_BLOCK
        n_all = xs_hbm.shape[0] // MOE_BLOCK
        lax.fori_loop(first_unused, n_all, lambda b, c: (tail(b).start(), c)[1], 0)
        lax.fori_loop(first_unused, n_all, lambda b, c: (tail(b).wait(), c)[1], 0)
        tot[0] = 0
        tot[1] = 0

    _wait_rows(comp.at[slot], xs_hbm, sem.at[slot], tot[slot])
    r = lax.broadcasted_iota(jnp.int32, (cap, tm), 0).astype(F32)
    sel = jnp.zeros((cap, tm), F32)
    for k in range(TOP_K):
        sel = jnp.where(r == info_ref[k:k + 1, :], 1.0, sel)
    sel = sel.astype(BF16)
    comp[slot] = _dot(sel, x_ref[...].astype(BF16))
    _segment_copies(seg_ref, comp.at[slot], xs_hbm, sem.at[slot], to_global=True)
    tot[slot] = seg_ref[0, 3, 0]

    @pl.when(i == n_steps - 1)
    def _():
        _wait_rows(comp.at[slot], xs_hbm, sem.at[slot], tot[slot])
        _wait_rows(comp.at[1 - slot], xs_hbm, sem.at[1 - slot], tot[1 - slot])


def _dispatch(ends_pad, seg, x2, info, n_slots, tm):
    T, D = x2.shape
    row = lambda i, ends: (i, 0)
    grid_spec = pltpu.PrefetchScalarGridSpec(
        num_scalar_prefetch=1, grid=(T // tm,),
        in_specs=[pl.BlockSpec((1, 4, LANES), lambda i, ends: (i, 0, 0), memory_space=pltpu.SMEM),
                  pl.BlockSpec((tm, D), row), pl.BlockSpec((LANES, tm), lambda i, ends: (0, i))],
        out_specs=pl.BlockSpec(memory_space=pl.ANY),
        scratch_shapes=[pltpu.VMEM((2, TOP_K * tm + N_EXPERTS * SEG_ALIGN, D), F32), pltpu.SMEM((2,), jnp.int32),
                        pltpu.SemaphoreType.DMA((2,))])
    return pl.pallas_call(_dispatch_kernel, name="dispatch", grid_spec=grid_spec,
                          out_shape=jax.ShapeDtypeStruct((n_slots, D), F32),
                          compiler_params=_cparams())(ends_pad, seg, x2, info)


def _moe_kernel(bexp_ref, nused_ref, x_ref, wgu_ref, bgu_ref, wd_ref, bd_ref, y_ref, wgu_s, wd_s):
    j = pl.program_id(0)
    n_used = nused_ref[0]
    n_tiles = wgu_s.shape[1] // GU_TILE

    @pl.when(j >= n_used)
    def _():
        y_ref[...] = jnp.zeros_like(y_ref)

    @pl.when(j < n_used)
    def _():
        @pl.when((j == 0) | (bexp_ref[j] != bexp_ref[jnp.maximum(j - 1, 0)]))
        def _():
            k = lax.broadcasted_iota(jnp.int32, (GU_TILE, GU_TILE), 0)
            n = lax.broadcasted_iota(jnp.int32, (GU_TILE, GU_TILE), 1)
            perm = jnp.where(k == jnp.where(n < LANES, 2 * n, 2 * (n - LANES) + 1), 1.0, 0.0).astype(BF16)
            for t in range(n_tiles):
                cols = slice(t * GU_TILE, (t + 1) * GU_TILE)
                wgu_s[:, cols] = _dot(wgu_ref[0, :, cols].astype(BF16), perm).astype(BF16)
            wd_s[...] = wd_ref[0].astype(BF16)

        xb = x_ref[...].astype(BF16)
        hs = []
        for t in range(n_tiles):
            cols = slice(t * GU_TILE, (t + 1) * GU_TILE)
            gu = _dot(xb, wgu_s[:, cols]) + bgu_ref[0, :, cols]
            gate = jnp.minimum(gu[:, :LANES], SWIGLU_LIMIT)
            lin = jnp.clip(gu[:, LANES:], -SWIGLU_LIMIT, SWIGLU_LIMIT)
            hs.append((gate * jax.nn.sigmoid(SWIGLU_ALPHA * gate) * (lin + 1.0)).astype(BF16))
        y_ref[...] = _dot(jnp.concatenate(hs, axis=1), wd_s[...]) + bd_ref[0]


def _moe(bexp, n_used, xs, wgu, bgu, wd, bd):
    n_blocks = bexp.shape[0]
    D = xs.shape[1]
    F2 = wgu.shape[2]
    F = wd.shape[1]
    wmap = lambda j, be, nu: (be[jnp.minimum(j, nu[0] - 1)], 0, 0)
    blk = lambda j, be, nu: (j, 0)
    used = lambda j, be, nu: (jnp.minimum(j, nu[0] - 1), 0)
    in_specs = [pl.BlockSpec((MOE_BLOCK, D), used),
                pl.BlockSpec((1, D, F2), wmap), pl.BlockSpec((1, 1, F2), wmap),
                pl.BlockSpec((1, F, D), wmap), pl.BlockSpec((1, 1, D), wmap)]
    grid_spec = pltpu.PrefetchScalarGridSpec(
        num_scalar_prefetch=2, grid=(n_blocks,), in_specs=in_specs,
        out_specs=pl.BlockSpec((MOE_BLOCK, D), blk),
        scratch_shapes=[pltpu.VMEM((D, F2), BF16), pltpu.VMEM((F, D), BF16)])
    return pl.pallas_call(_moe_kernel, name="moe", grid_spec=grid_spec,
                          out_shape=jax.ShapeDtypeStruct((n_blocks * MOE_BLOCK, D), F32),
                          compiler_params=_cparams())(bexp, n_used, xs, wgu, bgu, wd, bd)


def _combine_kernel(seg_ref, seg_nxt_ref, ys_hbm, info_ref, x_ref, g_ref, b_ref, out_ref, comp, sem, *, alpha):
    i = pl.program_id(0)
    n_steps = pl.num_programs(0)
    slot = i % 2
    tm = x_ref.shape[0]
    cap = comp.shape[1]

    @pl.when(i == 0)
    def _():
        comp[...] = jnp.zeros_like(comp)
        _segment_copies(seg_ref, comp.at[0], ys_hbm, sem.at[0], to_global=False)

    @pl.when(i + 1 < n_steps)
    def _():
        _segment_copies(seg_nxt_ref, comp.at[1 - slot], ys_hbm, sem.at[1 - slot], to_global=False)

    _wait_rows(comp.at[slot], ys_hbm, sem.at[slot], seg_ref[0, 3, 0])
    info = info_ref[...].T
    lane = lax.broadcasted_iota(jnp.int32, (tm, cap), 1).astype(F32)
    wmat = jnp.zeros((tm, cap), F32)
    for k in range(TOP_K):
        wmat = jnp.where(lane == info[:, k:k + 1], info[:, TOP_K + k:TOP_K + k + 1], wmat)
    ffn = _dot(wmat.astype(BF16), comp[slot].astype(BF16))
    out_ref[...] = _layer_norm(alpha * x_ref[...] + ffn, g_ref[...], b_ref[...])


def _combine(seg, ys, info, x2, g, b, alpha, tm):
    T, D = x2.shape
    n_t = T // tm
    row = lambda i: (i, 0)
    seg_spec = lambda m: pl.BlockSpec((1, 4, LANES), m, memory_space=pltpu.SMEM)
    in_specs = [seg_spec(lambda i: (i, 0, 0)), seg_spec(lambda i: (jnp.minimum(i + 1, n_t - 1), 0, 0)),
                pl.BlockSpec(memory_space=pl.ANY),
                pl.BlockSpec((LANES, tm), lambda i: (0, i)), pl.BlockSpec((tm, D), row),
                _full_spec(g.shape), _full_spec(b.shape)]
    cap = TOP_K * tm + N_EXPERTS * SEG_ALIGN
    return pl.pallas_call(functools.partial(_combine_kernel, alpha=alpha), name="combine", grid=(n_t,), in_specs=in_specs,
                          out_specs=pl.BlockSpec((tm, D), row),
                          out_shape=jax.ShapeDtypeStruct((T, D), F32),
                          scratch_shapes=[pltpu.VMEM((2, cap, D), F32), pltpu.SemaphoreType.DMA((2,))],
                          compiler_params=_cparams())(seg, seg, ys, info, x2, g, b)


def _tile(n, pref):
    return pref if n % pref == 0 else n


def _layer(x2d, mem2d, pos2d, B, S, depth, w_in, sinks, w_attn_o, lam_re, lam_im, log_dt, b_re, b_im,
           c_re, c_im, d_skip, w_glu_a, w_glu_b, w_out, ln1_g, ln1_b, wq_c, wk_c, wv_c, wo_c, ln2_g,
           ln2_b, w_router, b_router, w_gate_up, b_gate_up, w_down, b_down, ln3_g, ln3_b):
    T, D = x2d.shape
    alpha = (2 * depth) ** 0.25
    rep = N_Q_HEADS // N_KV_HEADS
    q_w = N_Q_HEADS * HEAD_DIM
    kv_w = N_KV_HEADS * HEAD_DIM
    ssm_w = SSM_GROUP * SSM_GROUPS
    P = SSM_STATE

    o_k, o_v, o_s = q_w, q_w + kv_w, q_w + 2 * kv_w
    o_ga, o_gs = o_s + ssm_w, o_s + ssm_w + D
    wq = w_in[:, :o_k].reshape(D, N_KV_HEADS, rep, HEAD_DIM).transpose(0, 2, 1, 3).reshape(D, q_w).astype(BF16)
    wk = w_in[:, o_k:o_v].astype(BF16)
    wv = w_in[:, o_v:o_s].astype(BF16)
    wu = w_in[:, o_s:o_ga].astype(BF16)
    wga = w_in[:, o_ga:o_gs].astype(BF16)
    wgs = w_in[:, o_gs:].astype(BF16)
    wo_attn = w_attn_o.reshape(N_KV_HEADS, rep, HEAD_DIM, D).transpose(1, 0, 2, 3).reshape(q_w, D).astype(BF16)
    half = HEAD_DIM // 2
    inv_freq = jnp.power(ROPE_THETA, -jnp.arange(half, dtype=F32) / half)
    invf = jnp.tile(inv_freq, LANES // half)[None, :]
    sink_rows = jnp.repeat(sinks.astype(F32).reshape(N_KV_HEADS, rep), WINDOW, axis=1)
    sink_cols = jnp.full((N_KV_HEADS, rep * WINDOW, 2 * WINDOW), NEG_BIG, F32).at[:, :, 0].set(sink_rows)

    tm1 = _tile(T, 1024)
    q2, k, v, ut, ga, gs = _inproj(x2d, pos2d, invf, wq, wk, wv, wu, wga, wgs, tm1)
    att = _swa(q2, k, v, sink_cols, wo_attn, ga, B, S, _tile(S, 1024))

    lam_row = jnp.stack([jnp.concatenate([lam_re, lam_re], -1), jnp.concatenate([lam_im, lam_im], -1)], 1)
    lam_col = jnp.swapaxes(lam_row, 1, 2)
    bt_re = jnp.swapaxes(b_re, 1, 2)
    bt_im = jnp.swapaxes(b_im, 1, 2)
    bt_a = jnp.concatenate([bt_re, bt_im], -1)
    bt_b = jnp.concatenate([-bt_im, bt_re], -1)
    ct_re = jnp.swapaxes(c_re, 1, 2)
    ct_im = jnp.swapaxes(c_im, 1, 2)
    ct_a = jnp.concatenate([ct_re, -ct_im], 1)
    ct_b = jnp.concatenate([-ct_im, -ct_re], 1)
    dsk = jnp.repeat(d_skip.reshape(SSM_GROUPS, SSM_GROUP), SSM_GROUP, axis=1)[:, :, None]
    yt = _ssm(ut, lam_row, lam_col, log_dt.reshape(SSM_GROUPS, 1, 1), bt_a, bt_b, c_re, c_im, ct_a, ct_b, dsk,
              S // LANES)

    x1 = _mix(yt, att, gs, x2d, w_glu_a.astype(BF16), w_glu_b.astype(BF16), w_out.astype(BF16),
              ln1_g[None, :], ln1_b[None, :], alpha, _tile(T, 1024))

    kc, vc = _memkv(mem2d, wk_c.astype(BF16), wv_c.astype(BF16), _tile(mem2d.shape[0], 512))
    wr = jnp.zeros((LANES, D), F32).at[:N_EXPERTS, :].set(w_router.T).astype(BF16)
    br = jnp.full((LANES, 1), NEG_BIG, F32).at[:N_EXPERTS, 0].set(b_router)
    x2, logits = _cross(x1, kc, vc, wq_c.astype(BF16), wo_c.astype(BF16), ln2_g[None, :], ln2_b[None, :],
                        wr, br, alpha, S, _tile(S, 1024))

    tm_r = _tile(T, ROUTE_TILE)
    n_tiles = T // tm_r
    info, segf, counts = _route(logits, tm_r)
    total = counts[:, 0].astype(jnp.int32)
    padded = (total + MOE_BLOCK - 1) // MOE_BLOCK * MOE_BLOCK
    ends_pad = jnp.cumsum(padded)
    start_pad = ends_pad - padded
    n_blocks = -(-(T * TOP_K + n_tiles * N_EXPERTS * (SEG_ALIGN - 1)) // MOE_BLOCK) + N_EXPERTS
    segi = segf.astype(jnp.int32)
    seg_n, seg_off, seg_start = segi[:, :, 0], segi[:, :, 1], segi[:, :, 2] + start_pad[None, :]
    seg_tot = jnp.broadcast_to(jnp.sum(seg_n, axis=1, keepdims=True), seg_n.shape)
    seg = jnp.stack([seg_n, seg_off, seg_start, seg_tot], axis=1)
    seg = jnp.pad(seg, ((0, 0), (0, 0), (0, LANES - N_EXPERTS)))
    block_start = jnp.arange(n_blocks, dtype=jnp.int32) * MOE_BLOCK
    bexp = jnp.minimum(jnp.sum(block_start[:, None] >= ends_pad[None, :], axis=1), N_EXPERTS - 1).astype(jnp.int32)
    n_used = (ends_pad[-1] // MOE_BLOCK).astype(jnp.int32)[None]
    n_gu_tiles = b_gate_up.shape[1] // GU_TILE
    bgu = b_gate_up.reshape(N_EXPERTS, n_gu_tiles, LANES, 2).transpose(0, 1, 3, 2).reshape(N_EXPERTS, 1, -1)
    xs = _dispatch(ends_pad, seg, x2, info, n_blocks * MOE_BLOCK, tm_r)
    ys = _moe(bexp, n_used, xs, w_gate_up, bgu, w_down, b_down[:, None, :])
    return _combine(seg, ys, info, x2, ln3_g[None, :], ln3_b[None, :], alpha, tm_r)


def kernel(x, mem, positions, w_in, sinks, w_attn_o, lam_re, lam_im, log_dt, b_re, b_im, c_re, c_im, d_skip,
           w_glu_a, w_glu_b, w_out, ln1_g, ln1_b, wq_c, wk_c, wv_c, wo_c, ln2_g, ln2_b, w_router, b_router,
           w_gate_up, b_gate_up, w_down, b_down, ln3_g, ln3_b):
    B, S, D = x.shape
    depth = w_in.shape[0]
    x2d = x.reshape(B * S, D)
    mem2d = mem.reshape(-1, D)
    pos2d = positions.reshape(B * S, 1)
    per_layer = (w_in, sinks, w_attn_o, lam_re, lam_im, log_dt, b_re, b_im, c_re, c_im, d_skip, w_glu_a,
                 w_glu_b, w_out, ln1_g, ln1_b, wq_c, wk_c, wv_c, wo_c, ln2_g, ln2_b, w_router, b_router,
                 w_gate_up, b_gate_up, w_down, b_down, ln3_g, ln3_b)
    for l in range(depth):
        x2d = _layer(x2d, mem2d, pos2d, B, S, depth, *(w[l] for w in per_layer))
    return x2d.reshape(B, S, D)
```

```python
import functools

import jax
import jax.numpy as jnp
from jax import lax
from jax.experimental import pallas as pl
from jax.experimental.pallas import tpu as pltpu

N_Q_HEADS = 16
N_KV_HEADS = 2
HEAD_DIM = 64
WINDOW = 128
ROPE_THETA = 10000.0
SSM_GROUP = 16
SSM_GROUPS = 32
SSM_STATE = 64
N_CROSS_HEADS = 4
N_EXPERTS = 32
TOP_K = 4
SWIGLU_ALPHA = 1.702
SWIGLU_LIMIT = 7.0
MOE_BLOCK = 512
ROUTE_TILE = 512
ROW_TILE = 1024
MEM_TILE = 512
LN_EPS = 1e-5

LANES = 128
VMEM_LIMIT_BYTES = 56 * 1024 * 1024

NEG_BIG = -1e30
BF16 = jnp.bfloat16
F32 = jnp.float32
HI = lax.Precision.HIGHEST


def _cparams(n_axes=1):
    return pltpu.CompilerParams(dimension_semantics=("arbitrary",) * n_axes,
                                vmem_limit_bytes=VMEM_LIMIT_BYTES)


def _full_spec(shape):
    n = len(shape)
    return pl.BlockSpec(shape, lambda *_: (0,) * n, pipeline_mode=pl.Buffered(1))


def _dot(a, b):
    return jnp.dot(a, b, preferred_element_type=F32)


def _dot_nt(a, b):
    return lax.dot_general(a, b, (((1,), (1,)), ((), ())), preferred_element_type=F32)


def _layer_norm(y, g, b):
    mu = jnp.mean(y, axis=-1, keepdims=True)
    d = y - mu
    var = jnp.mean(d * d, axis=-1, keepdims=True)
    return d * lax.rsqrt(var + LN_EPS) * g + b


def _rope(t, cos, sin_signed, first_half):
    half = HEAD_DIM // 2
    partner = jnp.where(first_half, pltpu.roll(t, LANES - half, axis=1), pltpu.roll(t, half, axis=1))
    return t * cos + partner * sin_signed


def _inproj_kernel(x_ref, pos_ref, invf_ref, wq_ref, wk_ref, wv_ref, wu_ref, wga_ref, wgs_ref,
                   q2_ref, k_ref, v_ref, ut_ref, ga_ref, gs_ref):
    tm = x_ref.shape[0]
    xb = x_ref[...].astype(BF16)
    ang = pos_ref[...].astype(F32) * invf_ref[...]
    cos = jnp.cos(ang)
    sin = jnp.sin(ang)
    first_half = (lax.broadcasted_iota(jnp.int32, (tm, LANES), 1) % HEAD_DIM) < (HEAD_DIM // 2)
    first_half_w = (lax.broadcasted_iota(jnp.int32, (WINDOW, LANES), 1) % HEAD_DIM) < (HEAD_DIM // 2)
    sin_signed = jnp.where(first_half, -sin, sin)

    q = _dot(xb, wq_ref[...])
    n_rep = q.shape[1] // LANES
    scale = HEAD_DIM ** -0.5
    for j in range(tm // WINDOW):
        rows = slice(j * WINDOW, (j + 1) * WINDOW)
        for r in range(n_rep):
            t = _rope(q[rows, r * LANES:(r + 1) * LANES], cos[rows], sin_signed[rows], first_half_w)
            base = (j * n_rep + r) * WINDOW
            q2_ref[base:base + WINDOW, :] = (t * scale).astype(BF16)
    k_ref[...] = _rope(_dot(xb, wk_ref[...]), cos, sin_signed, first_half).astype(BF16)
    v_ref[...] = _dot(xb, wv_ref[...]).astype(BF16)
    u = _dot(xb, wu_ref[...])
    for j in range(tm // LANES):
        ut_ref[j] = u[j * LANES:(j + 1) * LANES, :].T
    ga_ref[...] = jax.nn.sigmoid(_dot(xb, wga_ref[...])).astype(BF16)
    gs_ref[...] = jax.nn.sigmoid(_dot(xb, wgs_ref[...])).astype(BF16)


def _inproj(x2d, pos2d, invf, wq, wk, wv, wu, wga, wgs, tm):
    T, D = x2d.shape
    n_rep = wq.shape[1] // LANES
    ssm_w = wu.shape[1]
    row = lambda i: (i, 0)
    out_shape = (
        jax.ShapeDtypeStruct((T * n_rep, LANES), BF16),
        jax.ShapeDtypeStruct((T, LANES), BF16),
        jax.ShapeDtypeStruct((T, LANES), BF16),
        jax.ShapeDtypeStruct((T // LANES, ssm_w, LANES), F32),
        jax.ShapeDtypeStruct((T, D), BF16),
        jax.ShapeDtypeStruct((T, D), BF16),
    )
    in_specs = [pl.BlockSpec((tm, D), row), pl.BlockSpec((tm, 1), row), _full_spec(invf.shape),
                _full_spec(wq.shape), _full_spec(wk.shape), _full_spec(wv.shape),
                _full_spec(wu.shape), _full_spec(wga.shape), _full_spec(wgs.shape)]
    out_specs = (pl.BlockSpec((tm * n_rep, LANES), row), pl.BlockSpec((tm, LANES), row),
                 pl.BlockSpec((tm, LANES), row),
                 pl.BlockSpec((tm // LANES, ssm_w, LANES), lambda i: (i, 0, 0)),
                 pl.BlockSpec((tm, D), row), pl.BlockSpec((tm, D), row))
    return pl.pallas_call(_inproj_kernel, name="inproj", grid=(T // tm,), in_specs=in_specs, out_specs=out_specs,
                          out_shape=out_shape, compiler_params=_cparams())(
        x2d, pos2d, invf, wq, wk, wv, wu, wga, wgs)


def _swa_kernel(q2_ref, kc_ref, kp_ref, vc_ref, vp_ref, fill_ref, wo_ref, ga_ref, out_ref, cat_ref):
    i = pl.program_id(1)
    tq = kc_ref.shape[0]
    n_sub = tq // WINDOW
    rep = N_Q_HEADS // N_KV_HEADS
    rows_all = rep * WINDOW
    kfull = jnp.concatenate([kp_ref[...], kc_ref[...]], axis=0)
    vfull = jnp.concatenate([vp_ref[...], vc_ref[...]], axis=0)
    lane = lax.broadcasted_iota(jnp.int32, (2 * WINDOW, LANES), 1)
    qi = lax.broadcasted_iota(jnp.int32, (rows_all, 2 * WINDOW), 0) % WINDOW
    ci = lax.broadcasted_iota(jnp.int32, (rows_all, 2 * WINDOW), 1)
    local = (ci > qi) & (ci <= qi + WINDOW)
    out_lane = lax.broadcasted_iota(jnp.int32, (rows_all, LANES), 1)
    band_row = lax.broadcasted_iota(jnp.int32, (2 * WINDOW, LANES), 0)
    for j in range(n_sub):
        qs = q2_ref[j * rows_all:(j + 1) * rows_all, :]
        kb = kfull[j * WINDOW:(j + 2) * WINDOW, :]
        vb = vfull[j * WINDOW:(j + 2) * WINDOW, :]
        vb = jnp.where(band_row == 0, jnp.zeros_like(vb), vb)
        mask = local
        if j == 0:
            mask = mask & ((ci >= WINDOW) | (i > 0))
        o = None
        for g in range(N_KV_HEADS):
            in_group = (lane >= g * HEAD_DIM) & (lane < (g + 1) * HEAD_DIM)
            kg = jnp.where(in_group, kb, jnp.zeros_like(kb))
            s = _dot_nt(qs, kg)
            s = jnp.where(mask, s, fill_ref[g])
            m = jnp.max(s, axis=-1, keepdims=True)
            p = jnp.exp(s - m)
            denom = jnp.sum(p, axis=-1, keepdims=True)
            og = _dot(p.astype(BF16), vb) * (1.0 / denom)
            o = og if o is None else jnp.where(out_lane < g * HEAD_DIM, o, og)
        ob = o.astype(BF16)
        for r in range(rep):
            cat_ref[j * WINDOW:(j + 1) * WINDOW, r * LANES:(r + 1) * LANES] = ob[r * WINDOW:(r + 1) * WINDOW, :]
    attn = _dot(cat_ref[...], wo_ref[...])
    out_ref[...] = (attn * ga_ref[...].astype(F32)).astype(BF16)


def _swa(q2, k, v, sink_cols, wo, ga, B, S, tq):
    T, D = ga.shape
    rep = N_Q_HEADS // N_KV_HEADS
    n_i = S // tq
    n_sub = tq // WINDOW
    in_specs = [
        pl.BlockSpec((tq * rep, LANES), lambda b, i: (b * n_i + i, 0)),
        pl.BlockSpec((tq, LANES), lambda b, i: (b * n_i + i, 0)),
        pl.BlockSpec((WINDOW, LANES), lambda b, i: (b * (S // WINDOW) + jnp.maximum(i * n_sub - 1, 0), 0)),
        pl.BlockSpec((tq, LANES), lambda b, i: (b * n_i + i, 0)),
        pl.BlockSpec((WINDOW, LANES), lambda b, i: (b * (S // WINDOW) + jnp.maximum(i * n_sub - 1, 0), 0)),
        _full_spec(sink_cols.shape), _full_spec(wo.shape),
        pl.BlockSpec((tq, D), lambda b, i: (b * n_i + i, 0)),
    ]
    return pl.pallas_call(
        _swa_kernel, name="swa", grid=(B, n_i), in_specs=in_specs,
        out_specs=pl.BlockSpec((tq, D), lambda b, i: (b * n_i + i, 0)),
        out_shape=jax.ShapeDtypeStruct((T, D), BF16),
        scratch_shapes=[pltpu.VMEM((tq, D), BF16)],
        compiler_params=_cparams(2))(q2, k, k, v, v, sink_cols, wo, ga)


def _ssm_prep_kernel(lam_row_ref, lam_col_ref, dt_ref, bt_a_ref, bt_b_ref, c_re_ref, c_im_ref,
                     ct_a_ref, ct_b_ref, dsk_ref, toep_ref, wst_ref, wout_ref, apow_ref):
    L = LANES
    P = SSM_STATE
    dt = jnp.exp(dt_ref[0])
    lr2 = lam_row_ref[0, 0:1, :]
    li2 = lam_row_ref[0, 1:2, :]
    mag = jnp.exp(lr2 * dt)
    ar = mag * jnp.cos(li2 * dt)
    ai = mag * jnp.sin(li2 * dt)
    den = lr2 * lr2 + li2 * li2
    f_re = ((ar - 1.0) * lr2 + ai * li2) / den
    f_im = (ai * lr2 - (ar - 1.0) * li2) / den
    bt_a = bt_a_ref[0]
    bt_b = bt_b_ref[0]
    bb_a = f_re * bt_a + f_im * bt_b
    bb_b = f_re * bt_b - f_im * bt_a

    tau_rev = (L - 1 - lax.broadcasted_iota(jnp.int32, (L, 2 * P), 0)).astype(F32)
    g_mag = jnp.exp(lr2 * dt * tau_rev)
    g_re = g_mag * jnp.cos(li2 * dt * tau_rev)
    g_im = g_mag * jnp.sin(li2 * dt * tau_rev)
    for i in range(SSM_GROUP):
        wst_ref[0, i * L:(i + 1) * L, :] = (g_re * bb_a[i:i + 1, :] + g_im * bb_b[i:i + 1, :]).astype(BF16)

    lane2 = lax.broadcasted_iota(jnp.int32, (1, 2 * P), 1)
    for kk in range(4):
        n = float(L * (1 << kk))
        pm = jnp.exp(lr2 * dt * n)
        p_re = pm * jnp.cos(li2 * dt * n)
        p_im = pm * jnp.sin(li2 * dt * n)
        apow_ref[0, 2 * kk:2 * kk + 1, :] = p_re
        apow_ref[0, 2 * kk + 1:2 * kk + 2, :] = jnp.where(lane2 < P, -p_im, p_im)

    lrc = lam_col_ref[0, :, 0:1]
    lic = lam_col_ref[0, :, 1:2]
    tau = lax.broadcasted_iota(jnp.int32, (2 * P, L), 1).astype(F32)
    e0_mag = jnp.exp(lrc * dt * tau)
    e0_re = e0_mag * jnp.cos(lic * dt * tau)
    e0_im = e0_mag * jnp.sin(lic * dt * tau)
    e1_mag = jnp.exp(lrc * dt * (tau + 1.0))
    e1_re = e1_mag * jnp.cos(lic * dt * (tau + 1.0))
    e1_im = e1_mag * jnp.sin(lic * dt * (tau + 1.0))

    ct_a = ct_a_ref[0]
    ct_b = ct_b_ref[0]
    for o in range(SSM_GROUP):
        wout_ref[0, :, o * L:(o + 1) * L] = (ct_a[:, o:o + 1] * e1_re + ct_b[:, o:o + 1] * e1_im).astype(BF16)

    c_re = c_re_ref[0]
    c_im = c_im_ref[0]
    bb_re = bb_a[:, :P]
    bb_im = bb_a[:, P:]
    m_re = (c_re[:, None, :] * bb_re[None, :, :] - c_im[:, None, :] * bb_im[None, :, :])
    m_im = (c_re[:, None, :] * bb_im[None, :, :] + c_im[:, None, :] * bb_re[None, :, :])
    m_re = m_re.reshape(SSM_GROUP * SSM_GROUP, P)
    m_im = m_im.reshape(SSM_GROUP * SSM_GROUP, P)
    kt = (jnp.dot(m_re, e0_re[:P, :], precision=HI, preferred_element_type=F32)
          - jnp.dot(m_im, e0_im[:P, :], precision=HI, preferred_element_type=F32))
    rowi = lax.broadcasted_iota(jnp.int32, (SSM_GROUP * SSM_GROUP, L), 0)
    coli = lax.broadcasted_iota(jnp.int32, (SSM_GROUP * SSM_GROUP, L), 1)
    dsk = dsk_ref[0]
    kt = kt + jnp.where((coli == 0) & ((rowi // SSM_GROUP) == (rowi % SSM_GROUP)), dsk, 0.0)

    cc = lax.broadcasted_iota(jnp.int32, (L, L), 0)
    cp = lax.broadcasted_iota(jnp.int32, (L, L), 1)
    causal = cp >= cc
    for o in range(SSM_GROUP):
        for i in range(SSM_GROUP):
            kv = kt[o * SSM_GROUP + i:o * SSM_GROUP + i + 1, :]
            blk = pltpu.roll(jnp.broadcast_to(kv, (L, L)), 0, axis=1, stride=1, stride_axis=0)
            blk = jnp.where(causal, blk, 0.0)
            toep_ref[0, i * L:(i + 1) * L, o * L:(o + 1) * L] = blk.astype(BF16)


def _ssm_scan_kernel(ut_ref, toep_ref, wst_ref, wout_ref, apow_ref, yt_ref, *, n_chunks):
    nb = ut_ref.shape[0]
    P = SSM_STATE
    ut2 = ut_ref.reshape(nb * SSM_GROUP, LANES)
    lhs = jnp.concatenate([ut2[pl.ds(i, nb, stride=SSM_GROUP), :] for i in range(SSM_GROUP)],
                          axis=1).astype(BF16)
    y = _dot(lhs, toep_ref[0])
    st = _dot(lhs, wst_ref[0])
    srow = lax.broadcasted_iota(jnp.int32, (nb, 2 * P), 0) % n_chunks
    kk = 0
    while (1 << kk) < n_chunks:
        d = 1 << kk
        pa = apow_ref[0, 2 * kk:2 * kk + 1, :]
        pb = apow_ref[0, 2 * kk + 1:2 * kk + 2, :]
        prev = pltpu.roll(st, d, axis=0)
        prev = prev * pa + pltpu.roll(prev, P, axis=1) * pb
        st = st + jnp.where(srow >= d, prev, 0.0)
        kk += 1
    h0 = jnp.where(srow >= 1, pltpu.roll(st, 1, axis=0), 0.0)
    y = y + _dot(h0.astype(BF16), wout_ref[0])
    for o in range(SSM_GROUP):
        yt_ref[o] = y[:, o * LANES:(o + 1) * LANES]


def _ssm_kernel(*refs, n_chunks):
    params, (ut_ref, yt_ref), ops = refs[:10], refs[10:12], refs[12:]
    sets = (ops[0::2], ops[1::2])
    g = pl.program_id(0)

    @pl.when(g == 0)
    def _():
        _ssm_prep_kernel(*params, *sets[0])

    for p in range(2):
        @pl.when((g > 0) & (g % 2 == p))
        def _(p=p):
            _ssm_prep_kernel(*params, *sets[p])
            _ssm_scan_kernel(ut_ref, *sets[1 - p], yt_ref, n_chunks=n_chunks)


def _ssm(ut, lam_row, lam_col, log_dt, bt_a, bt_b, c_re, c_im, ct_a, ct_b, dsk, n_chunks):
    nb, ssm_w, _ = ut.shape
    G = SSM_GROUPS
    KW = SSM_GROUP * LANES
    params = [lam_row, lam_col, log_dt, bt_a, bt_b, c_re, c_im, ct_a, ct_b, dsk]
    build = lambda g: (jnp.minimum(g, G - 1), 0, 0)
    in_specs = [pl.BlockSpec((1,) + a.shape[1:], build) for a in params]
    in_specs.append(pl.BlockSpec((nb, SSM_GROUP, LANES), lambda g: (0, jnp.maximum(g - 1, 0), 0)))
    op_shapes = [pltpu.VMEM((1, KW, KW), BF16), pltpu.VMEM((1, KW, 2 * SSM_STATE), BF16),
                 pltpu.VMEM((1, 2 * SSM_STATE, KW), BF16), pltpu.VMEM((1, 8, 2 * SSM_STATE), F32)]
    return pl.pallas_call(
        functools.partial(_ssm_kernel, n_chunks=n_chunks), name="ssm", grid=(G + 1,), in_specs=in_specs,
        out_specs=pl.BlockSpec((SSM_GROUP, nb, LANES), lambda g: (jnp.maximum(g - 1, 0), 0, 0)),
        out_shape=jax.ShapeDtypeStruct((ssm_w, nb, LANES), F32),
        scratch_shapes=[s for s in op_shapes for _ in range(2)],
        compiler_params=_cparams())(*params, ut)


def _mix_kernel(yt_ref, att_ref, gs_ref, x_ref, wa_ref, wb_ref, wo_ref, g_ref, b_ref, out_ref, zt_ref,
                *, alpha):
    n_ch, n_blk = yt_ref.shape[0], yt_ref.shape[1]
    yt2 = yt_ref.reshape(n_ch * n_blk, LANES)
    for j in range(n_blk):
        y = yt2[pl.ds(j, n_ch, stride=n_blk), :]
        z = jax.nn.gelu(y, approximate=True)
        zt_ref[j * LANES:(j + 1) * LANES, :] = z.T.astype(BF16)
    z = zt_ref[...]
    ssm_out = _dot(z, wa_ref[...]) * jax.nn.sigmoid(_dot(z, wb_ref[...]))
    mixed = att_ref[...].astype(F32) + gs_ref[...].astype(F32) * ssm_out
    y = alpha * x_ref[...] + _dot(mixed.astype(BF16), wo_ref[...])
    out_ref[...] = _layer_norm(y, g_ref[...], b_ref[...])


def _mix(yt, att, gs, x2d, wa, wb, wo, g, b, alpha, tm):
    T, D = x2d.shape
    ssm_w = yt.shape[0]
    row = lambda i: (i, 0)
    in_specs = [pl.BlockSpec((ssm_w, tm // LANES, LANES), lambda i: (0, i, 0)),
                pl.BlockSpec((tm, D), row), pl.BlockSpec((tm, D), row), pl.BlockSpec((tm, D), row),
                _full_spec(wa.shape), _full_spec(wb.shape), _full_spec(wo.shape),
                _full_spec(g.shape), _full_spec(b.shape)]
    return pl.pallas_call(functools.partial(_mix_kernel, alpha=alpha), name="mix", grid=(T // tm,), in_specs=in_specs,
                          out_specs=pl.BlockSpec((tm, D), row),
                          out_shape=jax.ShapeDtypeStruct((T, D), F32),
                          scratch_shapes=[pltpu.VMEM((tm, ssm_w), BF16)],
                          compiler_params=_cparams())(yt, att, gs, x2d, wa, wb, wo, g, b)


def _memkv_kernel(m_ref, wk_ref, wv_ref, k_ref, v_ref):
    mb = m_ref[...].astype(BF16)
    k_ref[...] = _dot(mb, wk_ref[...]).astype(BF16)
    v_ref[...] = _dot(mb, wv_ref[...]).astype(BF16)


def _memkv(mem2d, wk, wv, tm):
    R, D = mem2d.shape
    row = lambda i: (i, 0)
    return pl.pallas_call(_memkv_kernel, name="memkv", grid=(R // tm,),
                          in_specs=[pl.BlockSpec((tm, D), row), _full_spec(wk.shape), _full_spec(wv.shape)],
                          out_specs=(pl.BlockSpec((tm, D), row), pl.BlockSpec((tm, D), row)),
                          out_shape=(jax.ShapeDtypeStruct((R, D), BF16), jax.ShapeDtypeStruct((R, D), BF16)),
                          compiler_params=_cparams())(mem2d, wk, wv)


def _cross_kernel(x_ref, k_ref, v_ref, wq_ref, wo_ref, g_ref, b_ref, wr_ref, br_ref, out_ref, logit_ref,
                  cat_ref, *, alpha):
    x = x_ref[...]
    D = x.shape[1]
    hd = D // N_CROSS_HEADS
    q = (_dot(x.astype(BF16), wq_ref[...]) * (hd ** -0.5)).astype(BF16)
    for h in range(N_CROSS_HEADS):
        cols = slice(h * hd, (h + 1) * hd)
        s = _dot_nt(q[:, cols], k_ref[:, cols])
        m = jnp.max(s, axis=-1, keepdims=True)
        p = jnp.exp(s - m)
        w = (p / jnp.sum(p, axis=-1, keepdims=True)).astype(BF16)
        cat_ref[:, cols] = _dot(w, v_ref[:, cols]).astype(BF16)
    y = alpha * x + _dot(cat_ref[...], wo_ref[...])
    x2 = _layer_norm(y, g_ref[...], b_ref[...])
    out_ref[...] = x2
    logit_ref[...] = _dot_nt(wr_ref[...], x2.astype(BF16)) + br_ref[...]


def _cross(x1, kc, vc, wq, wo, g, b, wr, br, alpha, S, tm):
    T, D = x1.shape
    n_mem = kc.shape[0] // (T // S)
    row = lambda i: (i, 0)
    per_b = S // tm
    in_specs = [pl.BlockSpec((tm, D), row),
                pl.BlockSpec((n_mem, D), lambda i: (i // per_b, 0)),
                pl.BlockSpec((n_mem, D), lambda i: (i // per_b, 0)),
                _full_spec(wq.shape), _full_spec(wo.shape), _full_spec(g.shape), _full_spec(b.shape),
                _full_spec(wr.shape), _full_spec(br.shape)]
    return pl.pallas_call(functools.partial(_cross_kernel, alpha=alpha), name="cross", grid=(T // tm,), in_specs=in_specs,
                          out_specs=(pl.BlockSpec((tm, D), row), pl.BlockSpec((LANES, tm), lambda i: (0, i))),
                          out_shape=(jax.ShapeDtypeStruct((T, D), F32), jax.ShapeDtypeStruct((LANES, T), F32)),
                          scratch_shapes=[pltpu.VMEM((tm, D), BF16)],
                          compiler_params=_cparams())(x1, kc, vc, wq, wo, g, b, wr, br)


SEG_ALIGN = 8


def _route_kernel(lt_ref, info_ref, seg_ref, count_ref, carry_ref):
    tm = lt_ref.shape[1]
    E = N_EXPERTS

    @pl.when(pl.program_id(0) == 0)
    def _():
        carry_ref[...] = jnp.zeros_like(carry_ref)

    l = lt_ref[0:E, :]
    erow = lax.broadcasted_iota(jnp.int32, (E, tm), 0)
    rank = jnp.zeros((E, tm), F32)
    for e2 in range(E):
        other = l[e2:e2 + 1, :]
        tie = jnp.where(erow > e2, 1.0, 0.0)
        rank = rank + jnp.where(other > l, 1.0, jnp.where(other == l, tie, 0.0))
    chosen = rank < float(TOP_K)
    onehot = jnp.where(chosen, 1.0, 0.0)
    p = jnp.where(chosen, jnp.exp(l - jnp.max(l, axis=0, keepdims=True)), 0.0)
    gate = p / jnp.sum(p, axis=0, keepdims=True)
    a = lax.broadcasted_iota(jnp.int32, (tm, tm), 0)
    b = lax.broadcasted_iota(jnp.int32, (tm, tm), 1)
    cum = _dot(onehot.astype(BF16), jnp.where(a < b, 1.0, 0.0).astype(BF16))
    n_seg = jnp.floor((jnp.sum(onehot, axis=1, keepdims=True) + (SEG_ALIGN - 1.0)) * (1.0 / SEG_ALIGN)) * SEG_ALIGN
    n_seg = jnp.broadcast_to(n_seg, (E, LANES))
    erow_l = lax.broadcasted_iota(jnp.int32, (E, LANES), 0)
    seg_off = jnp.zeros((E, LANES), F32)
    for e2 in range(E - 1):
        seg_off = seg_off + jnp.where(erow_l > e2, n_seg[e2:e2 + 1, :], 0.0)
    pos = cum + seg_off[:, 0:1]
    rows = [jnp.sum(jnp.where(rank == float(k), pos, 0.0), axis=0, keepdims=True) for k in range(TOP_K)]
    rows += [jnp.sum(jnp.where(rank == float(k), gate, 0.0), axis=0, keepdims=True) for k in range(TOP_K)]
    rows.append(jnp.zeros((info_ref.shape[0] - 2 * TOP_K, tm), F32))
    info_ref[...] = jnp.concatenate(rows, axis=0)
    carry = carry_ref[...]
    lane = lax.broadcasted_iota(jnp.int32, (E, LANES), 1)
    seg_ref[0] = jnp.where(lane == 0, n_seg, jnp.where(lane == 1, seg_off, jnp.where(lane == 2, carry, 0.0)))
    carry_ref[...] = carry + n_seg
    count_ref[...] = carry_ref[...]


def _route(logits_t, tm):
    T = logits_t.shape[1]
    E = N_EXPERTS
    col = lambda i: (0, i)
    return pl.pallas_call(_route_kernel, name="route", grid=(T // tm,),
                          in_specs=[pl.BlockSpec((LANES, tm), col)],
                          out_specs=(pl.BlockSpec((LANES, tm), col), pl.BlockSpec((1, E, LANES), lambda i: (i, 0, 0)),
                                     pl.BlockSpec((E, LANES), lambda i: (0, 0))),
                          out_shape=(jax.ShapeDtypeStruct((LANES, T), F32),
                                     jax.ShapeDtypeStruct((T // tm, E, LANES), F32),
                                     jax.ShapeDtypeStruct((E, LANES), F32)),
                          scratch_shapes=[pltpu.VMEM((E, LANES), F32)],
                          compiler_params=_cparams())(logits_t)


GU_TILE = 2 * LANES


SEG_SIZES = tuple(ROUTE_TILE >> s for s in range(ROUTE_TILE.bit_length()) if (ROUTE_TILE >> s) >= SEG_ALIGN)
SEG_LARGE = 128


def _segment_copies(seg_ref, local_ref, global_ref, sem, to_global):
    def pieces(n, off, start, sizes, prio):
        for sz in sizes:
            take = (n & sz) != 0

            @pl.when(take)
            def _(off=off, start=start, sz=sz):
                loc = local_ref.at[pl.ds(pl.multiple_of(off, SEG_ALIGN), sz)]
                glo = global_ref.at[pl.ds(pl.multiple_of(start, SEG_ALIGN), sz)]
                if to_global:
                    pltpu.make_async_copy(loc, glo, sem).start(priority=prio)
                else:
                    pltpu.make_async_copy(glo, loc, sem).start(priority=prio)

            step = jnp.where(take, sz, 0)
            off = off + step
            start = start + step

    n_large = sum(1 for sz in SEG_SIZES if sz >= SEG_LARGE)
    for e in range(N_EXPERTS):
        n = seg_ref[0, 0, e]
        off = seg_ref[0, 1, e]
        start = seg_ref[0, 2, e]

        @pl.when(n >= SEG_LARGE)
        def _(n=n, off=off, start=start, e=e):
            pieces(n, off, start, SEG_SIZES[:n_large], e % 2)

        large = n & ~(SEG_LARGE - 1)
        pieces(n, off + large, start + large, SEG_SIZES[n_large:], e % 2)


def _wait_rows(local_ref, global_ref, sem, n_rows):
    @pl.when(n_rows > 0)
    def _():
        n = pl.multiple_of(n_rows, SEG_ALIGN)
        pltpu.make_async_copy(global_ref.at[pl.ds(0, n)], local_ref.at[pl.ds(0, n)], sem).wait()


def _dispatch_kernel(ends_ref, seg_ref, x_ref, info_ref, xs_hbm, comp, tot, sem):
    i = pl.program_id(0)
    n_steps = pl.num_programs(0)
    slot = i % 2
    tm = x_ref.shape[0]
    cap = comp.shape[1]

    @pl.when(i == 0)
    def _():
        comp[1, 0:MOE_BLOCK, :] = jnp.zeros((MOE_BLOCK, comp.shape[2]), F32)

        def zero_block(b, sem_ref):
            return pltpu.make_async_copy(comp.at[1, pl.ds(0, MOE_BLOCK)],
                                         xs_hbm.at[pl.ds(pl.multiple_of(b * MOE_BLOCK, MOE_BLOCK), MOE_BLOCK)],
                                         sem_ref)

        first_unused = ends_ref[N_EXPERTS - 1] // MOE_BLOCK
        n_all = xs_hbm.shape[0] // MOE_BLOCK
        for act in ("start", "wait"):
            for e in range(N_EXPERTS):
                end = ends_ref[e]
                prev = ends_ref[e - 1] if e > 0 else 0

                @pl.when(end > prev)
                def _(end=end, act=act):
                    getattr(zero_block(end // MOE_BLOCK - 1, sem.at[1]), act)()

            lax.fori_loop(first_unused, n_all, lambda b, c, act=act: (getattr(zero_block(b, sem.at[0]), act)(), c)[1], 0)
        tot[0] = 0
        tot[1] = 0

    _wait_rows(comp.at[slot], xs_hbm, sem.at[slot], tot[slot])
    r = lax.broadcasted_iota(jnp.int32, (cap, tm), 0).astype(F32)
    sel = jnp.zeros((cap, tm), F32)
    for k in range(TOP_K):
        sel = jnp.where(r == info_ref[k:k + 1, :], 1.0, sel)
    sel = sel.astype(BF16)
    comp[slot] = _dot(sel, x_ref[...].astype(BF16))
    _segment_copies(seg_ref, comp.at[slot], xs_hbm, sem.at[slot], to_global=True)
    tot[slot] = seg_ref[0, 3, 0]

    @pl.when(i == n_steps - 1)
    def _():
        _wait_rows(comp.at[slot], xs_hbm, sem.at[slot], tot[slot])
        _wait_rows(comp.at[1 - slot], xs_hbm, sem.at[1 - slot], tot[1 - slot])


def _dispatch(ends_pad, seg, x2, info, n_slots, tm):
    T, D = x2.shape
    row = lambda i, ends: (i, 0)
    grid_spec = pltpu.PrefetchScalarGridSpec(
        num_scalar_prefetch=1, grid=(T // tm,),
        in_specs=[pl.BlockSpec((1, 4, LANES), lambda i, ends: (i, 0, 0), memory_space=pltpu.SMEM),
                  pl.BlockSpec((tm, D), row), pl.BlockSpec((LANES, tm), lambda i, ends: (0, i))],
        out_specs=pl.BlockSpec(memory_space=pl.ANY),
        scratch_shapes=[pltpu.VMEM((2, TOP_K * tm + N_EXPERTS * SEG_ALIGN, D), F32), pltpu.SMEM((2,), jnp.int32),
                        pltpu.SemaphoreType.DMA((2,))])
    return pl.pallas_call(_dispatch_kernel, name="dispatch", grid_spec=grid_spec,
                          out_shape=jax.ShapeDtypeStruct((n_slots, D), F32),
                          compiler_params=_cparams())(ends_pad, seg, x2, info)


def _moe_kernel(bexp_ref, nused_ref, x_ref, wgu_ref, bgu_ref, wd_ref, bd_ref, y_ref, wgu_s, wd_s):
    j = pl.program_id(0)
    n_used = nused_ref[0]
    n_tiles = wgu_s.shape[1] // GU_TILE

    @pl.when(j >= n_used)
    def _():
        y_ref[...] = jnp.zeros_like(y_ref)

    @pl.when(j < n_used)
    def _():
        @pl.when((j == 0) | (bexp_ref[j] != bexp_ref[jnp.maximum(j - 1, 0)]))
        def _():
            k = lax.broadcasted_iota(jnp.int32, (GU_TILE, GU_TILE), 0)
            n = lax.broadcasted_iota(jnp.int32, (GU_TILE, GU_TILE), 1)
            perm = jnp.where(k == jnp.where(n < LANES, 2 * n, 2 * (n - LANES) + 1), 1.0, 0.0).astype(BF16)
            for t in range(n_tiles):
                cols = slice(t * GU_TILE, (t + 1) * GU_TILE)
                wgu_s[:, cols] = _dot(wgu_ref[0, :, cols].astype(BF16), perm).astype(BF16)
            wd_s[...] = wd_ref[0].astype(BF16)

        xb = x_ref[...].astype(BF16)
        hs = []
        for t in range(n_tiles):
            cols = slice(t * GU_TILE, (t + 1) * GU_TILE)
            gu = _dot(xb, wgu_s[:, cols]) + bgu_ref[0, :, cols]
            gate = jnp.minimum(gu[:, :LANES], SWIGLU_LIMIT)
            lin = jnp.clip(gu[:, LANES:], -SWIGLU_LIMIT, SWIGLU_LIMIT)
            hs.append((gate * jax.nn.sigmoid(SWIGLU_ALPHA * gate) * (lin + 1.0)).astype(BF16))
        y_ref[...] = _dot(jnp.concatenate(hs, axis=1), wd_s[...]) + bd_ref[0]


def _moe(bexp, n_used, xs, wgu, bgu, wd, bd):
    n_blocks = bexp.shape[0]
    D = xs.shape[1]
    F2 = wgu.shape[2]
    F = wd.shape[1]
    wmap = lambda j, be, nu: (be[jnp.minimum(j, nu[0] - 1)], 0, 0)
    blk = lambda j, be, nu: (j, 0)
    used = lambda j, be, nu: (jnp.minimum(j, nu[0] - 1), 0)
    in_specs = [pl.BlockSpec((MOE_BLOCK, D), used),
                pl.BlockSpec((1, D, F2), wmap), pl.BlockSpec((1, 1, F2), wmap),
                pl.BlockSpec((1, F, D), wmap), pl.BlockSpec((1, 1, D), wmap)]
    grid_spec = pltpu.PrefetchScalarGridSpec(
        num_scalar_prefetch=2, grid=(n_blocks,), in_specs=in_specs,
        out_specs=pl.BlockSpec((MOE_BLOCK, D), blk),
        scratch_shapes=[pltpu.VMEM((D, F2), BF16), pltpu.VMEM((F, D), BF16)])
    return pl.pallas_call(_moe_kernel, name="moe", grid_spec=grid_spec,
                          out_shape=jax.ShapeDtypeStruct((n_blocks * MOE_BLOCK, D), F32),
                          compiler_params=_cparams())(bexp, n_used, xs, wgu, bgu, wd, bd)


def _combine_kernel(seg_ref, seg_nxt_ref, ys_hbm, info_ref, x_ref, g_ref, b_ref, out_ref, comp, sem, *, alpha):
    i = pl.program_id(0)
    n_steps = pl.num_programs(0)
    slot = i % 2
    tm = x_ref.shape[0]
    cap = comp.shape[1]

    @pl.when(i == 0)
    def _():
        comp[...] = jnp.zeros_like(comp)
        _segment_copies(seg_ref, comp.at[0], ys_hbm, sem.at[0], to_global=False)

    @pl.when(i + 1 < n_steps)
    def _():
        _segment_copies(seg_nxt_ref, comp.at[1 - slot], ys_hbm, sem.at[1 - slot], to_global=False)

    _wait_rows(comp.at[slot], ys_hbm, sem.at[slot], seg_ref[0, 3, 0])
    info = info_ref[...].T
    lane = lax.broadcasted_iota(jnp.int32, (tm, cap), 1).astype(F32)
    wmat = jnp.zeros((tm, cap), F32)
    for k in range(TOP_K):
        wmat = jnp.where(lane == info[:, k:k + 1], info[:, TOP_K + k:TOP_K + k + 1], wmat)
    ffn = _dot(wmat.astype(BF16), comp[slot].astype(BF16))
    out_ref[...] = _layer_norm(alpha * x_ref[...] + ffn, g_ref[...], b_ref[...])


def _combine(seg, ys, info, x2, g, b, alpha, tm):
    T, D = x2.shape
    n_t = T // tm
    row = lambda i: (i, 0)
    seg_spec = lambda m: pl.BlockSpec((1, 4, LANES), m, memory_space=pltpu.SMEM)
    in_specs = [seg_spec(lambda i: (i, 0, 0)), seg_spec(lambda i: (jnp.minimum(i + 1, n_t - 1), 0, 0)),
                pl.BlockSpec(memory_space=pl.ANY),
                pl.BlockSpec((LANES, tm), lambda i: (0, i)), pl.BlockSpec((tm, D), row),
                _full_spec(g.shape), _full_spec(b.shape)]
    cap = TOP_K * tm + N_EXPERTS * SEG_ALIGN
    return pl.pallas_call(functools.partial(_combine_kernel, alpha=alpha), name="combine", grid=(n_t,), in_specs=in_specs,
                          out_specs=pl.BlockSpec((tm, D), row),
                          out_shape=jax.ShapeDtypeStruct((T, D), F32),
                          scratch_shapes=[pltpu.VMEM((2, cap, D), F32), pltpu.SemaphoreType.DMA((2,))],
                          compiler_params=_cparams())(seg, seg, ys, info, x2, g, b)


def _tile(n, pref):
    return pref if n % pref == 0 else n


def _layer(x2d, mem2d, pos2d, B, S, depth, w_in, sinks, w_attn_o, lam_re, lam_im, log_dt, b_re, b_im,
           c_re, c_im, d_skip, w_glu_a, w_glu_b, w_out, ln1_g, ln1_b, wq_c, wk_c, wv_c, wo_c, ln2_g,
           ln2_b, w_router, b_router, w_gate_up, b_gate_up, w_down, b_down, ln3_g, ln3_b):
    T, D = x2d.shape
    alpha = (2 * depth) ** 0.25
    rep = N_Q_HEADS // N_KV_HEADS
    q_w = N_Q_HEADS * HEAD_DIM
    kv_w = N_KV_HEADS * HEAD_DIM
    ssm_w = SSM_GROUP * SSM_GROUPS

    o_k, o_v, o_s = q_w, q_w + kv_w, q_w + 2 * kv_w
    o_ga, o_gs = o_s + ssm_w, o_s + ssm_w + D
    wq = w_in[:, :o_k].reshape(D, N_KV_HEADS, rep, HEAD_DIM).transpose(0, 2, 1, 3).reshape(D, q_w).astype(BF16)
    wk = w_in[:, o_k:o_v].astype(BF16)
    wv = w_in[:, o_v:o_s].astype(BF16)
    wu = w_in[:, o_s:o_ga].astype(BF16)
    wga = w_in[:, o_ga:o_gs].astype(BF16)
    wgs = w_in[:, o_gs:].astype(BF16)
    wo_attn = w_attn_o.reshape(N_KV_HEADS, rep, HEAD_DIM, D).transpose(1, 0, 2, 3).reshape(q_w, D).astype(BF16)
    half = HEAD_DIM // 2
    inv_freq = jnp.power(ROPE_THETA, -jnp.arange(half, dtype=F32) / half)
    invf = jnp.tile(inv_freq, LANES // half)[None, :]
    sink_rows = jnp.repeat(sinks.astype(F32).reshape(N_KV_HEADS, rep), WINDOW, axis=1)
    sink_cols = jnp.full((N_KV_HEADS, rep * WINDOW, 2 * WINDOW), NEG_BIG, F32).at[:, :, 0].set(sink_rows)

    q2, k, v, ut, ga, gs = _inproj(x2d, pos2d, invf, wq, wk, wv, wu, wga, wgs, _tile(T, ROW_TILE))
    att = _swa(q2, k, v, sink_cols, wo_attn, ga, B, S, _tile(S, ROW_TILE))

    lam_row = jnp.stack([jnp.concatenate([lam_re, lam_re], -1), jnp.concatenate([lam_im, lam_im], -1)], 1)
    lam_col = jnp.swapaxes(lam_row, 1, 2)
    bt_re = jnp.swapaxes(b_re, 1, 2)
    bt_im = jnp.swapaxes(b_im, 1, 2)
    bt_a = jnp.concatenate([bt_re, bt_im], -1)
    bt_b = jnp.concatenate([-bt_im, bt_re], -1)
    ct_re = jnp.swapaxes(c_re, 1, 2)
    ct_im = jnp.swapaxes(c_im, 1, 2)
    ct_a = jnp.concatenate([ct_re, -ct_im], 1)
    ct_b = jnp.concatenate([-ct_im, -ct_re], 1)
    dsk = jnp.repeat(d_skip.reshape(SSM_GROUPS, SSM_GROUP), SSM_GROUP, axis=1)[:, :, None]
    yt = _ssm(ut, lam_row, lam_col, log_dt.reshape(SSM_GROUPS, 1, 1), bt_a, bt_b, c_re, c_im, ct_a, ct_b, dsk,
              S // LANES)

    x1 = _mix(yt, att, gs, x2d, w_glu_a.astype(BF16), w_glu_b.astype(BF16), w_out.astype(BF16),
              ln1_g[None, :], ln1_b[None, :], alpha, _tile(T, ROW_TILE))

    kc, vc = _memkv(mem2d, wk_c.astype(BF16), wv_c.astype(BF16), _tile(mem2d.shape[0], MEM_TILE))
    wr = jnp.zeros((LANES, D), F32).at[:N_EXPERTS, :].set(w_router.T).astype(BF16)
    br = jnp.full((LANES, 1), NEG_BIG, F32).at[:N_EXPERTS, 0].set(b_router)
    x2, logits = _cross(x1, kc, vc, wq_c.astype(BF16), wo_c.astype(BF16), ln2_g[None, :], ln2_b[None, :],
                        wr, br, alpha, S, _tile(S, ROW_TILE))

    tm_r = _tile(T, ROUTE_TILE)
    n_tiles = T // tm_r
    info, segf, counts = _route(logits, tm_r)
    total = counts[:, 0].astype(jnp.int32)
    padded = (total + MOE_BLOCK - 1) // MOE_BLOCK * MOE_BLOCK
    ends_pad = jnp.cumsum(padded)
    start_pad = ends_pad - padded
    n_blocks = -(-(T * TOP_K + n_tiles * N_EXPERTS * (SEG_ALIGN - 1)) // MOE_BLOCK) + N_EXPERTS
    segi = segf.astype(jnp.int32)
    seg_n, seg_off, seg_start = segi[:, :, 0], segi[:, :, 1], segi[:, :, 2] + start_pad[None, :]
    seg_tot = jnp.broadcast_to(jnp.sum(seg_n, axis=1, keepdims=True), seg_n.shape)
    seg = jnp.stack([seg_n, seg_off, seg_start, seg_tot], axis=1)
    seg = jnp.pad(seg, ((0, 0), (0, 0), (0, LANES - N_EXPERTS)))
    block_start = jnp.arange(n_blocks, dtype=jnp.int32) * MOE_BLOCK
    bexp = jnp.minimum(jnp.sum(block_start[:, None] >= ends_pad[None, :], axis=1), N_EXPERTS - 1).astype(jnp.int32)
    n_used = (ends_pad[-1] // MOE_BLOCK).astype(jnp.int32)[None]
    n_gu_tiles = b_gate_up.shape[1] // GU_TILE
    bgu = b_gate_up.reshape(N_EXPERTS, n_gu_tiles, LANES, 2).transpose(0, 1, 3, 2).reshape(N_EXPERTS, 1, -1)
    xs = _dispatch(ends_pad, seg, x2, info, n_blocks * MOE_BLOCK, tm_r)
    ys = _moe(bexp, n_used, xs, w_gate_up, bgu, w_down, b_down[:, None, :])
    return _combine(seg, ys, info, x2, ln3_g[None, :], ln3_b[None, :], alpha, tm_r)


def kernel(x, mem, positions, w_in, sinks, w_attn_o, lam_re, lam_im, log_dt, b_re, b_im, c_re, c_im, d_skip,
           w_glu_a, w_glu_b, w_out, ln1_g, ln1_b, wq_c, wk_c, wv_c, wo_c, ln2_g, ln2_b, w_router, b_router,
           w_gate_up, b_gate_up, w_down, b_down, ln3_g, ln3_b):
    B, S, D = x.shape
    depth = w_in.shape[0]
    x2d = x.reshape(B * S, D)
    mem2d = mem.reshape(-1, D)
    pos2d = positions.reshape(B * S, 1)
    per_layer = (w_in, sinks, w_attn_o, lam_re, lam_im, log_dt, b_re, b_im, c_re, c_im, d_skip, w_glu_a,
                 w_glu_b, w_out, ln1_g, ln1_b, wq_c, wk_c, wv_c, wo_c, ln2_g, ln2_b, w_router, b_router,
                 w_gate_up, b_gate_up, w_down, b_down, ln3_g, ln3_b)
    for l in range(depth):
        x2d = _layer(x2d, mem2d, pos2d, B, S, depth, *(w[l] for w in per_layer))
    return x2d.reshape(B, S, D)
```

```python
import functools

import jax
import jax.numpy as jnp
from jax import lax
from jax.experimental import pallas as pl
from jax.experimental.pallas import tpu as pltpu

N_Q_HEADS = 16
N_KV_HEADS = 2
HEAD_DIM = 64
WINDOW = 128
ROPE_THETA = 10000.0
SSM_GROUP = 16
SSM_GROUPS = 32
SSM_STATE = 64
N_CROSS_HEADS = 4
N_EXPERTS = 32
TOP_K = 4
SWIGLU_ALPHA = 1.702
SWIGLU_LIMIT = 7.0
MOE_BLOCK = 512
ROUTE_TILE = 512
ROW_TILE = 1024
MEM_TILE = 512
LN_EPS = 1e-5

LANES = 128
VMEM_LIMIT_BYTES = 56 * 1024 * 1024

NEG_BIG = -1e30
BF16 = jnp.bfloat16
F32 = jnp.float32
HI = lax.Precision.HIGHEST


def _cparams(n_axes=1):
    return pltpu.CompilerParams(dimension_semantics=("arbitrary",) * n_axes,
                                vmem_limit_bytes=VMEM_LIMIT_BYTES)


def _full_spec(shape):
    n = len(shape)
    return pl.BlockSpec(shape, lambda *_: (0,) * n, pipeline_mode=pl.Buffered(1))


def _dot(a, b):
    return jnp.dot(a, b, preferred_element_type=F32)


def _dot_nt(a, b):
    return lax.dot_general(a, b, (((1,), (1,)), ((), ())), preferred_element_type=F32)


def _layer_norm(y, g, b):
    mu = jnp.mean(y, axis=-1, keepdims=True)
    d = y - mu
    var = jnp.mean(d * d, axis=-1, keepdims=True)
    return d * lax.rsqrt(var + LN_EPS) * g + b


def _rope(t, cos, sin_signed, first_half):
    half = HEAD_DIM // 2
    partner = jnp.where(first_half, pltpu.roll(t, LANES - half, axis=1), pltpu.roll(t, half, axis=1))
    return t * cos + partner * sin_signed


def _inproj_kernel(x_ref, pos_ref, invf_ref, wq_ref, wk_ref, wv_ref, wu_ref, wga_ref, wgs_ref,
                   q2_ref, k_ref, v_ref, ut_ref, ga_ref, gs_ref):
    tm = x_ref.shape[0]
    xb = x_ref[...].astype(BF16)
    ang = pos_ref[...].astype(F32) * invf_ref[...]
    cos = jnp.cos(ang)
    sin = jnp.sin(ang)
    first_half = (lax.broadcasted_iota(jnp.int32, (tm, LANES), 1) % HEAD_DIM) < (HEAD_DIM // 2)
    first_half_w = (lax.broadcasted_iota(jnp.int32, (WINDOW, LANES), 1) % HEAD_DIM) < (HEAD_DIM // 2)
    sin_signed = jnp.where(first_half, -sin, sin)

    q = _dot(xb, wq_ref[...])
    n_rep = q.shape[1] // LANES
    scale = HEAD_DIM ** -0.5
    for j in range(tm // WINDOW):
        rows = slice(j * WINDOW, (j + 1) * WINDOW)
        for r in range(n_rep):
            t = _rope(q[rows, r * LANES:(r + 1) * LANES], cos[rows], sin_signed[rows], first_half_w)
            base = (j * n_rep + r) * WINDOW
            q2_ref[base:base + WINDOW, :] = (t * scale).astype(BF16)
    k_ref[...] = _rope(_dot(xb, wk_ref[...]), cos, sin_signed, first_half).astype(BF16)
    v_ref[...] = _dot(xb, wv_ref[...]).astype(BF16)
    u = _dot(xb, wu_ref[...])
    for j in range(tm // LANES):
        ut_ref[j] = u[j * LANES:(j + 1) * LANES, :].T
    ga_ref[...] = jax.nn.sigmoid(_dot(xb, wga_ref[...])).astype(BF16)
    gs_ref[...] = jax.nn.sigmoid(_dot(xb, wgs_ref[...])).astype(BF16)


def _inproj(x2d, pos2d, invf, wq, wk, wv, wu, wga, wgs, tm):
    T, D = x2d.shape
    n_rep = wq.shape[1] // LANES
    ssm_w = wu.shape[1]
    row = lambda i: (i, 0)
    out_shape = (
        jax.ShapeDtypeStruct((T * n_rep, LANES), BF16),
        jax.ShapeDtypeStruct((T, LANES), BF16),
        jax.ShapeDtypeStruct((T, LANES), BF16),
        jax.ShapeDtypeStruct((T // LANES, ssm_w, LANES), F32),
        jax.ShapeDtypeStruct((T, D), BF16),
        jax.ShapeDtypeStruct((T, D), BF16),
    )
    in_specs = [pl.BlockSpec((tm, D), row), pl.BlockSpec((tm, 1), row), _full_spec(invf.shape),
                _full_spec(wq.shape), _full_spec(wk.shape), _full_spec(wv.shape),
                _full_spec(wu.shape), _full_spec(wga.shape), _full_spec(wgs.shape)]
    out_specs = (pl.BlockSpec((tm * n_rep, LANES), row), pl.BlockSpec((tm, LANES), row),
                 pl.BlockSpec((tm, LANES), row),
                 pl.BlockSpec((tm // LANES, ssm_w, LANES), lambda i: (i, 0, 0)),
                 pl.BlockSpec((tm, D), row), pl.BlockSpec((tm, D), row))
    return pl.pallas_call(_inproj_kernel, name="inproj", grid=(T // tm,), in_specs=in_specs, out_specs=out_specs,
                          out_shape=out_shape, compiler_params=_cparams())(
        x2d, pos2d, invf, wq, wk, wv, wu, wga, wgs)


def _swa_kernel(q2_ref, kc_ref, kp_ref, vc_ref, vp_ref, fill_ref, wo_ref, ga_ref, out_ref, cat_ref):
    i = pl.program_id(1)
    tq = kc_ref.shape[0]
    n_sub = tq // WINDOW
    rep = N_Q_HEADS // N_KV_HEADS
    rows_all = rep * WINDOW
    kfull = jnp.concatenate([kp_ref[...], kc_ref[...]], axis=0)
    vfull = jnp.concatenate([vp_ref[...], vc_ref[...]], axis=0)
    lane = lax.broadcasted_iota(jnp.int32, (2 * WINDOW, LANES), 1)
    qi = lax.broadcasted_iota(jnp.int32, (rows_all, 2 * WINDOW), 0) % WINDOW
    ci = lax.broadcasted_iota(jnp.int32, (rows_all, 2 * WINDOW), 1)
    local = (ci > qi) & (ci <= qi + WINDOW)
    out_lane = lax.broadcasted_iota(jnp.int32, (rows_all, LANES), 1)
    band_row = lax.broadcasted_iota(jnp.int32, (2 * WINDOW, LANES), 0)
    for j in range(n_sub):
        qs = q2_ref[j * rows_all:(j + 1) * rows_all, :]
        kb = kfull[j * WINDOW:(j + 2) * WINDOW, :]
        vb = vfull[j * WINDOW:(j + 2) * WINDOW, :]
        vb = jnp.where(band_row == 0, jnp.zeros_like(vb), vb)
        mask = local
        if j == 0:
            mask = mask & ((ci >= WINDOW) | (i > 0))
        o = None
        for g in range(N_KV_HEADS):
            in_group = (lane >= g * HEAD_DIM) & (lane < (g + 1) * HEAD_DIM)
            kg = jnp.where(in_group, kb, jnp.zeros_like(kb))
            s = _dot_nt(qs, kg)
            s = jnp.where(mask, s, fill_ref[g])
            m = jnp.max(s, axis=-1, keepdims=True)
            p = jnp.exp(s - m)
            denom = jnp.sum(p, axis=-1, keepdims=True)
            og = _dot(p.astype(BF16), vb) * (1.0 / denom)
            o = og if o is None else jnp.where(out_lane < g * HEAD_DIM, o, og)
        ob = o.astype(BF16)
        for r in range(rep):
            cat_ref[j * WINDOW:(j + 1) * WINDOW, r * LANES:(r + 1) * LANES] = ob[r * WINDOW:(r + 1) * WINDOW, :]
    attn = _dot(cat_ref[...], wo_ref[...])
    out_ref[...] = (attn * ga_ref[...].astype(F32)).astype(BF16)


def _swa(q2, k, v, sink_cols, wo, ga, B, S, tq):
    T, D = ga.shape
    rep = N_Q_HEADS // N_KV_HEADS
    n_i = S // tq
    n_sub = tq // WINDOW
    in_specs = [
        pl.BlockSpec((tq * rep, LANES), lambda b, i: (b * n_i + i, 0)),
        pl.BlockSpec((tq, LANES), lambda b, i: (b * n_i + i, 0)),
        pl.BlockSpec((WINDOW, LANES), lambda b, i: (b * (S // WINDOW) + jnp.maximum(i * n_sub - 1, 0), 0)),
        pl.BlockSpec((tq, LANES), lambda b, i: (b * n_i + i, 0)),
        pl.BlockSpec((WINDOW, LANES), lambda b, i: (b * (S // WINDOW) + jnp.maximum(i * n_sub - 1, 0), 0)),
        _full_spec(sink_cols.shape), _full_spec(wo.shape),
        pl.BlockSpec((tq, D), lambda b, i: (b * n_i + i, 0)),
    ]
    return pl.pallas_call(
        _swa_kernel, name="swa", grid=(B, n_i), in_specs=in_specs,
        out_specs=pl.BlockSpec((tq, D), lambda b, i: (b * n_i + i, 0)),
        out_shape=jax.ShapeDtypeStruct((T, D), BF16),
        scratch_shapes=[pltpu.VMEM((tq, D), BF16)],
        compiler_params=_cparams(2))(q2, k, k, v, v, sink_cols, wo, ga)


def _ssm_prep_kernel(lam_row_ref, lam_col_ref, dt_ref, bt_a_ref, bt_b_ref, c_re_ref, c_im_ref,
                     ct_a_ref, ct_b_ref, dsk_ref, toep_ref, wst_ref, wout_ref, apow_ref):
    L = LANES
    P = SSM_STATE
    dt = jnp.exp(dt_ref[0])
    lr2 = lam_row_ref[0, 0:1, :]
    li2 = lam_row_ref[0, 1:2, :]
    mag = jnp.exp(lr2 * dt)
    ar = mag * jnp.cos(li2 * dt)
    ai = mag * jnp.sin(li2 * dt)
    den = lr2 * lr2 + li2 * li2
    f_re = ((ar - 1.0) * lr2 + ai * li2) / den
    f_im = (ai * lr2 - (ar - 1.0) * li2) / den
    bt_a = bt_a_ref[0]
    bt_b = bt_b_ref[0]
    bb_a = f_re * bt_a + f_im * bt_b
    bb_b = f_re * bt_b - f_im * bt_a

    tau_rev = (L - 1 - lax.broadcasted_iota(jnp.int32, (L, 2 * P), 0)).astype(F32)
    g_mag = jnp.exp(lr2 * dt * tau_rev)
    g_re = g_mag * jnp.cos(li2 * dt * tau_rev)
    g_im = g_mag * jnp.sin(li2 * dt * tau_rev)
    for i in range(SSM_GROUP):
        wst_ref[0, i * L:(i + 1) * L, :] = (g_re * bb_a[i:i + 1, :] + g_im * bb_b[i:i + 1, :]).astype(BF16)

    lane2 = lax.broadcasted_iota(jnp.int32, (1, 2 * P), 1)
    for kk in range(4):
        n = float(L * (1 << kk))
        pm = jnp.exp(lr2 * dt * n)
        p_re = pm * jnp.cos(li2 * dt * n)
        p_im = pm * jnp.sin(li2 * dt * n)
        apow_ref[0, 2 * kk:2 * kk + 1, :] = p_re
        apow_ref[0, 2 * kk + 1:2 * kk + 2, :] = jnp.where(lane2 < P, -p_im, p_im)

    lrc = lam_col_ref[0, :, 0:1]
    lic = lam_col_ref[0, :, 1:2]
    tau = lax.broadcasted_iota(jnp.int32, (2 * P, L), 1).astype(F32)
    e0_mag = jnp.exp(lrc * dt * tau)
    e0_re = e0_mag * jnp.cos(lic * dt * tau)
    e0_im = e0_mag * jnp.sin(lic * dt * tau)
    e1_mag = jnp.exp(lrc * dt * (tau + 1.0))
    e1_re = e1_mag * jnp.cos(lic * dt * (tau + 1.0))
    e1_im = e1_mag * jnp.sin(lic * dt * (tau + 1.0))

    ct_a = ct_a_ref[0]
    ct_b = ct_b_ref[0]
    for o in range(SSM_GROUP):
        wout_ref[0, :, o * L:(o + 1) * L] = (ct_a[:, o:o + 1] * e1_re + ct_b[:, o:o + 1] * e1_im).astype(BF16)

    c_re = c_re_ref[0]
    c_im = c_im_ref[0]
    bb_re = bb_a[:, :P]
    bb_im = bb_a[:, P:]
    m_re = (c_re[:, None, :] * bb_re[None, :, :] - c_im[:, None, :] * bb_im[None, :, :])
    m_im = (c_re[:, None, :] * bb_im[None, :, :] + c_im[:, None, :] * bb_re[None, :, :])
    m_re = m_re.reshape(SSM_GROUP * SSM_GROUP, P)
    m_im = m_im.reshape(SSM_GROUP * SSM_GROUP, P)
    kt = (jnp.dot(m_re, e0_re[:P, :], precision=HI, preferred_element_type=F32)
          - jnp.dot(m_im, e0_im[:P, :], precision=HI, preferred_element_type=F32))
    rowi = lax.broadcasted_iota(jnp.int32, (SSM_GROUP * SSM_GROUP, L), 0)
    coli = lax.broadcasted_iota(jnp.int32, (SSM_GROUP * SSM_GROUP, L), 1)
    dsk = dsk_ref[0]
    kt = kt + jnp.where((coli == 0) & ((rowi // SSM_GROUP) == (rowi % SSM_GROUP)), dsk, 0.0)

    cc = lax.broadcasted_iota(jnp.int32, (L, L), 0)
    cp = lax.broadcasted_iota(jnp.int32, (L, L), 1)
    causal = cp >= cc
    for o in range(SSM_GROUP):
        for i in range(SSM_GROUP):
            kv = kt[o * SSM_GROUP + i:o * SSM_GROUP + i + 1, :]
            blk = pltpu.roll(jnp.broadcast_to(kv, (L, L)), 0, axis=1, stride=1, stride_axis=0)
            blk = jnp.where(causal, blk, 0.0)
            toep_ref[0, i * L:(i + 1) * L, o * L:(o + 1) * L] = blk.astype(BF16)


def _ssm_scan_kernel(ut_ref, toep_ref, wst_ref, wout_ref, apow_ref, yt_ref, *, n_chunks):
    nb = ut_ref.shape[0]
    P = SSM_STATE
    ut2 = ut_ref.reshape(nb * SSM_GROUP, LANES)
    lhs = jnp.concatenate([ut2[pl.ds(i, nb, stride=SSM_GROUP), :] for i in range(SSM_GROUP)],
                          axis=1).astype(BF16)
    y = _dot(lhs, toep_ref[0])
    st = _dot(lhs, wst_ref[0])
    srow = lax.broadcasted_iota(jnp.int32, (nb, 2 * P), 0) % n_chunks
    kk = 0
    while (1 << kk) < n_chunks:
        d = 1 << kk
        pa = apow_ref[0, 2 * kk:2 * kk + 1, :]
        pb = apow_ref[0, 2 * kk + 1:2 * kk + 2, :]
        prev = pltpu.roll(st, d, axis=0)
        prev = prev * pa + pltpu.roll(prev, P, axis=1) * pb
        st = st + jnp.where(srow >= d, prev, 0.0)
        kk += 1
    h0 = jnp.where(srow >= 1, pltpu.roll(st, 1, axis=0), 0.0)
    y = y + _dot(h0.astype(BF16), wout_ref[0])
    for o in range(SSM_GROUP):
        yt_ref[o] = y[:, o * LANES:(o + 1) * LANES]


def _ssm_kernel(*refs, n_chunks):
    params, (ut_ref, yt_ref), ops = refs[:10], refs[10:12], refs[12:]
    sets = (ops[0::2], ops[1::2])
    g = pl.program_id(0)

    @pl.when(g == 0)
    def _():
        _ssm_prep_kernel(*params, *sets[0])

    for p in range(2):
        @pl.when((g > 0) & (g % 2 == p))
        def _(p=p):
            _ssm_prep_kernel(*params, *sets[p])
            _ssm_scan_kernel(ut_ref, *sets[1 - p], yt_ref, n_chunks=n_chunks)


def _ssm(ut, lam_row, lam_col, log_dt, bt_a, bt_b, c_re, c_im, ct_a, ct_b, dsk, n_chunks):
    nb, ssm_w, _ = ut.shape
    G = SSM_GROUPS
    KW = SSM_GROUP * LANES
    params = [lam_row, lam_col, log_dt, bt_a, bt_b, c_re, c_im, ct_a, ct_b, dsk]
    build = lambda g: (jnp.minimum(g, G - 1), 0, 0)
    in_specs = [pl.BlockSpec((1,) + a.shape[1:], build) for a in params]
    in_specs.append(pl.BlockSpec((nb, SSM_GROUP, LANES), lambda g: (0, jnp.maximum(g - 1, 0), 0)))
    op_shapes = [pltpu.VMEM((1, KW, KW), BF16), pltpu.VMEM((1, KW, 2 * SSM_STATE), BF16),
                 pltpu.VMEM((1, 2 * SSM_STATE, KW), BF16), pltpu.VMEM((1, 8, 2 * SSM_STATE), F32)]
    return pl.pallas_call(
        functools.partial(_ssm_kernel, n_chunks=n_chunks), name="ssm", grid=(G + 1,), in_specs=in_specs,
        out_specs=pl.BlockSpec((SSM_GROUP, nb, LANES), lambda g: (jnp.maximum(g - 1, 0), 0, 0)),
        out_shape=jax.ShapeDtypeStruct((ssm_w, nb, LANES), F32),
        scratch_shapes=[s for s in op_shapes for _ in range(2)],
        compiler_params=_cparams())(*params, ut)


def _mix_kernel(yt_ref, att_ref, gs_ref, x_ref, wa_ref, wb_ref, wo_ref, g_ref, b_ref, out_ref, zt_ref,
                *, alpha):
    n_ch, n_blk = yt_ref.shape[0], yt_ref.shape[1]
    yt2 = yt_ref.reshape(n_ch * n_blk, LANES)
    for j in range(n_blk):
        y = yt2[pl.ds(j, n_ch, stride=n_blk), :]
        z = jax.nn.gelu(y, approximate=True)
        zt_ref[j * LANES:(j + 1) * LANES, :] = z.T.astype(BF16)
    z = zt_ref[...]
    ssm_out = _dot(z, wa_ref[...]) * jax.nn.sigmoid(_dot(z, wb_ref[...]))
    mixed = att_ref[...].astype(F32) + gs_ref[...].astype(F32) * ssm_out
    y = alpha * x_ref[...] + _dot(mixed.astype(BF16), wo_ref[...])
    out_ref[...] = _layer_norm(y, g_ref[...], b_ref[...])


def _mix(yt, att, gs, x2d, wa, wb, wo, g, b, alpha, tm):
    T, D = x2d.shape
    ssm_w = yt.shape[0]
    row = lambda i: (i, 0)
    in_specs = [pl.BlockSpec((ssm_w, tm // LANES, LANES), lambda i: (0, i, 0)),
                pl.BlockSpec((tm, D), row), pl.BlockSpec((tm, D), row), pl.BlockSpec((tm, D), row),
                _full_spec(wa.shape), _full_spec(wb.shape), _full_spec(wo.shape),
                _full_spec(g.shape), _full_spec(b.shape)]
    return pl.pallas_call(functools.partial(_mix_kernel, alpha=alpha), name="mix", grid=(T // tm,), in_specs=in_specs,
                          out_specs=pl.BlockSpec((tm, D), row),
                          out_shape=jax.ShapeDtypeStruct((T, D), F32),
                          scratch_shapes=[pltpu.VMEM((tm, ssm_w), BF16)],
                          compiler_params=_cparams())(yt, att, gs, x2d, wa, wb, wo, g, b)


def _memkv_kernel(m_ref, wk_ref, wv_ref, k_ref, v_ref):
    mb = m_ref[...].astype(BF16)
    k_ref[...] = _dot(mb, wk_ref[...]).astype(BF16)
    v_ref[...] = _dot(mb, wv_ref[...]).astype(BF16)


def _memkv(mem2d, wk, wv, tm):
    R, D = mem2d.shape
    row = lambda i: (i, 0)
    return pl.pallas_call(_memkv_kernel, name="memkv", grid=(R // tm,),
                          in_specs=[pl.BlockSpec((tm, D), row), _full_spec(wk.shape), _full_spec(wv.shape)],
                          out_specs=(pl.BlockSpec((tm, D), row), pl.BlockSpec((tm, D), row)),
                          out_shape=(jax.ShapeDtypeStruct((R, D), BF16), jax.ShapeDtypeStruct((R, D), BF16)),
                          compiler_params=_cparams())(mem2d, wk, wv)


def _cross_kernel(x_ref, k_ref, v_ref, wq_ref, wo_ref, g_ref, b_ref, wr_ref, br_ref, out_ref, logit_ref,
                  cat_ref, *, alpha):
    x = x_ref[...]
    D = x.shape[1]
    hd = D // N_CROSS_HEADS
    q = (_dot(x.astype(BF16), wq_ref[...]) * (hd ** -0.5)).astype(BF16)
    for h in range(N_CROSS_HEADS):
        cols = slice(h * hd, (h + 1) * hd)
        s = _dot_nt(q[:, cols], k_ref[:, cols])
        m = jnp.max(s, axis=-1, keepdims=True)
        p = jnp.exp(s - m)
        w = (p / jnp.sum(p, axis=-1, keepdims=True)).astype(BF16)
        cat_ref[:, cols] = _dot(w, v_ref[:, cols]).astype(BF16)
    y = alpha * x + _dot(cat_ref[...], wo_ref[...])
    x2 = _layer_norm(y, g_ref[...], b_ref[...])
    out_ref[...] = x2
    logit_ref[...] = _dot_nt(wr_ref[...], x2.astype(BF16)) + br_ref[...]


def _cross(x1, kc, vc, wq, wo, g, b, wr, br, alpha, S, tm):
    T, D = x1.shape
    n_mem = kc.shape[0] // (T // S)
    row = lambda i: (i, 0)
    per_b = S // tm
    in_specs = [pl.BlockSpec((tm, D), row),
                pl.BlockSpec((n_mem, D), lambda i: (i // per_b, 0)),
                pl.BlockSpec((n_mem, D), lambda i: (i // per_b, 0)),
                _full_spec(wq.shape), _full_spec(wo.shape), _full_spec(g.shape), _full_spec(b.shape),
                _full_spec(wr.shape), _full_spec(br.shape)]
    return pl.pallas_call(functools.partial(_cross_kernel, alpha=alpha), name="cross", grid=(T // tm,), in_specs=in_specs,
                          out_specs=(pl.BlockSpec((tm, D), row), pl.BlockSpec((LANES, tm), lambda i: (0, i))),
                          out_shape=(jax.ShapeDtypeStruct((T, D), F32), jax.ShapeDtypeStruct((LANES, T), F32)),
                          scratch_shapes=[pltpu.VMEM((tm, D), BF16)],
                          compiler_params=_cparams())(x1, kc, vc, wq, wo, g, b, wr, br)


SEG_ALIGN = 8


def _route_kernel(lt_ref, info_ref, seg_ref, count_ref, carry_ref):
    tm = lt_ref.shape[1]
    E = N_EXPERTS

    @pl.when(pl.program_id(0) == 0)
    def _():
        carry_ref[...] = jnp.zeros_like(carry_ref)

    l = lt_ref[0:E, :]
    erow = lax.broadcasted_iota(jnp.int32, (E, tm), 0)
    rank = jnp.zeros((E, tm), F32)
    for e2 in range(E):
        other = l[e2:e2 + 1, :]
        tie = jnp.where(erow > e2, 1.0, 0.0)
        rank = rank + jnp.where(other > l, 1.0, jnp.where(other == l, tie, 0.0))
    chosen = rank < float(TOP_K)
    onehot = jnp.where(chosen, 1.0, 0.0)
    p = jnp.where(chosen, jnp.exp(l - jnp.max(l, axis=0, keepdims=True)), 0.0)
    gate = p / jnp.sum(p, axis=0, keepdims=True)
    a = lax.broadcasted_iota(jnp.int32, (tm, tm), 0)
    b = lax.broadcasted_iota(jnp.int32, (tm, tm), 1)
    cum = _dot(onehot.astype(BF16), jnp.where(a < b, 1.0, 0.0).astype(BF16))
    n_seg = jnp.floor((jnp.sum(onehot, axis=1, keepdims=True) + (SEG_ALIGN - 1.0)) * (1.0 / SEG_ALIGN)) * SEG_ALIGN
    n_seg = jnp.broadcast_to(n_seg, (E, LANES))
    erow_l = lax.broadcasted_iota(jnp.int32, (E, LANES), 0)
    seg_off = jnp.zeros((E, LANES), F32)
    for e2 in range(E - 1):
        seg_off = seg_off + jnp.where(erow_l > e2, n_seg[e2:e2 + 1, :], 0.0)
    pos = cum + seg_off[:, 0:1]
    rows = [jnp.sum(jnp.where(rank == float(k), pos, 0.0), axis=0, keepdims=True) for k in range(TOP_K)]
    rows += [jnp.sum(jnp.where(rank == float(k), gate, 0.0), axis=0, keepdims=True) for k in range(TOP_K)]
    rows.append(jnp.zeros((info_ref.shape[0] - 2 * TOP_K, tm), F32))
    info_ref[...] = jnp.concatenate(rows, axis=0)
    carry = carry_ref[...]
    lane = lax.broadcasted_iota(jnp.int32, (E, LANES), 1)
    seg_ref[0] = jnp.where(lane == 0, n_seg, jnp.where(lane == 1, seg_off, jnp.where(lane == 2, carry, 0.0)))
    carry_ref[...] = carry + n_seg
    count_ref[...] = carry_ref[...]


def _route(logits_t, tm):
    T = logits_t.shape[1]
    E = N_EXPERTS
    col = lambda i: (0, i)
    return pl.pallas_call(_route_kernel, name="route", grid=(T // tm,),
                          in_specs=[pl.BlockSpec((LANES, tm), col)],
                          out_specs=(pl.BlockSpec((LANES, tm), col), pl.BlockSpec((1, E, LANES), lambda i: (i, 0, 0)),
                                     pl.BlockSpec((E, LANES), lambda i: (0, 0))),
                          out_shape=(jax.ShapeDtypeStruct((LANES, T), F32),
                                     jax.ShapeDtypeStruct((T // tm, E, LANES), F32),
                                     jax.ShapeDtypeStruct((E, LANES), F32)),
                          scratch_shapes=[pltpu.VMEM((E, LANES), F32)],
                          compiler_params=_cparams())(logits_t)


GU_TILE = 2 * LANES


SEG_SIZES = tuple(ROUTE_TILE >> s for s in range(ROUTE_TILE.bit_length()) if (ROUTE_TILE >> s) >= SEG_ALIGN)
SEG_LARGE = 128


def _segment_copies(seg_ref, local_ref, global_ref, sem, to_global):
    def pieces(n, off, start, sizes, prio):
        for sz in sizes:
            take = (n & sz) != 0

            @pl.when(take)
            def _(off=off, start=start, sz=sz):
                loc = local_ref.at[pl.ds(pl.multiple_of(off, SEG_ALIGN), sz)]
                glo = global_ref.at[pl.ds(pl.multiple_of(start, SEG_ALIGN), sz)]
                if to_global:
                    pltpu.make_async_copy(loc, glo, sem).start(priority=prio)
                else:
                    pltpu.make_async_copy(glo, loc, sem).start(priority=prio)

            step = jnp.where(take, sz, 0)
            off = off + step
            start = start + step

    n_large = sum(1 for sz in SEG_SIZES if sz >= SEG_LARGE)
    for e in range(N_EXPERTS):
        n = seg_ref[0, 0, e]
        off = seg_ref[0, 1, e]
        start = seg_ref[0, 2, e]

        @pl.when(n >= SEG_LARGE)
        def _(n=n, off=off, start=start, e=e):
            pieces(n, off, start, SEG_SIZES[:n_large], e % 2)

        large = n & ~(SEG_LARGE - 1)
        pieces(n, off + large, start + large, SEG_SIZES[n_large:], e % 2)


def _wait_rows(local_ref, global_ref, sem, n_rows):
    @pl.when(n_rows > 0)
    def _():
        n = pl.multiple_of(n_rows, SEG_ALIGN)
        pltpu.make_async_copy(global_ref.at[pl.ds(0, n)], local_ref.at[pl.ds(0, n)], sem).wait()


def _dispatch_kernel(ends_ref, seg_ref, x_ref, info_ref, xs_hbm, comp, tot, sem):
    i = pl.program_id(0)
    n_steps = pl.num_programs(0)
    slot = i % 2
    tm = x_ref.shape[0]
    cap = comp.shape[1]

    @pl.when(i == 0)
    def _():
        comp[1, 0:MOE_BLOCK, :] = jnp.zeros((MOE_BLOCK, comp.shape[2]), F32)

        def zero_block(b, sem_ref):
            return pltpu.make_async_copy(comp.at[1, pl.ds(0, MOE_BLOCK)],
                                         xs_hbm.at[pl.ds(pl.multiple_of(b * MOE_BLOCK, MOE_BLOCK), MOE_BLOCK)],
                                         sem_ref)

        first_unused = ends_ref[N_EXPERTS - 1] // MOE_BLOCK
        n_all = xs_hbm.shape[0] // MOE_BLOCK
        for act in ("start", "wait"):
            for e in range(N_EXPERTS):
                end = ends_ref[e]
                prev = ends_ref[e - 1] if e > 0 else 0

                @pl.when(end > prev)
                def _(end=end, act=act):
                    getattr(zero_block(end // MOE_BLOCK - 1, sem.at[1]), act)()

            lax.fori_loop(first_unused, n_all, lambda b, c, act=act: (getattr(zero_block(b, sem.at[0]), act)(), c)[1], 0)
        tot[0] = 0
        tot[1] = 0

    _wait_rows(comp.at[slot], xs_hbm, sem.at[slot], tot[slot])
    r = lax.broadcasted_iota(jnp.int32, (cap, tm), 0).astype(F32)
    sel = jnp.zeros((cap, tm), F32)
    for k in range(TOP_K):
        sel = jnp.where(r == info_ref[k:k + 1, :], 1.0, sel)
    sel = sel.astype(BF16)
    comp[slot] = _dot(sel, x_ref[...].astype(BF16))
    _segment_copies(seg_ref, comp.at[slot], xs_hbm, sem.at[slot], to_global=True)
    tot[slot] = seg_ref[0, 3, 0]

    @pl.when(i == n_steps - 1)
    def _():
        _wait_rows(comp.at[slot], xs_hbm, sem.at[slot], tot[slot])
        _wait_rows(comp.at[1 - slot], xs_hbm, sem.at[1 - slot], tot[1 - slot])


def _dispatch(ends_pad, seg, x2, info, n_slots, tm):
    T, D = x2.shape
    row = lambda i, ends: (i, 0)
    grid_spec = pltpu.PrefetchScalarGridSpec(
        num_scalar_prefetch=1, grid=(T // tm,),
        in_specs=[pl.BlockSpec((1, 4, LANES), lambda i, ends: (i, 0, 0), memory_space=pltpu.SMEM),
                  pl.BlockSpec((tm, D), row), pl.BlockSpec((LANES, tm), lambda i, ends: (0, i))],
        out_specs=pl.BlockSpec(memory_space=pl.ANY),
        scratch_shapes=[pltpu.VMEM((2, TOP_K * tm + N_EXPERTS * SEG_ALIGN, D), F32), pltpu.SMEM((2,), jnp.int32),
                        pltpu.SemaphoreType.DMA((2,))])
    return pl.pallas_call(_dispatch_kernel, name="dispatch", grid_spec=grid_spec,
                          out_shape=jax.ShapeDtypeStruct((n_slots, D), F32),
                          compiler_params=_cparams())(ends_pad, seg, x2, info)


def _moe_kernel(bexp_ref, nused_ref, nxt_ref, wslot_ref, x_ref, wgu_hbm, bgu_ref, wd_hbm, bd_ref, y_ref,
                wgu_f, wd_f, wgu_s, wd_s, wsem):
    j = pl.program_id(0)
    n_used = nused_ref[0]
    n_tiles = wgu_s.shape[1] // GU_TILE

    def weight_copies(e, s):
        return (pltpu.make_async_copy(wgu_hbm.at[e], wgu_f.at[s], wsem.at[s]),
                pltpu.make_async_copy(wd_hbm.at[e], wd_f.at[s], wsem.at[s]))

    @pl.when(j >= n_used)
    def _():
        y_ref[...] = jnp.zeros_like(y_ref)

    @pl.when(j < n_used)
    def _():
        @pl.when((j == 0) | (bexp_ref[j] != bexp_ref[jnp.maximum(j - 1, 0)]))
        def _():
            e, s = bexp_ref[j], wslot_ref[j]

            @pl.when(j == 0)
            def _():
                for cp in weight_copies(e, s):
                    cp.start()

            for cp in weight_copies(e, s):
                cp.wait()
            k = lax.broadcasted_iota(jnp.int32, (GU_TILE, GU_TILE), 0)
            n = lax.broadcasted_iota(jnp.int32, (GU_TILE, GU_TILE), 1)
            perm = jnp.where(k == jnp.where(n < LANES, 2 * n, 2 * (n - LANES) + 1), 1.0, 0.0).astype(BF16)
            for t in range(n_tiles):
                cols = slice(t * GU_TILE, (t + 1) * GU_TILE)
                wgu_s[:, cols] = _dot(wgu_f[s, :, cols].astype(BF16), perm).astype(BF16)
            wd_s[...] = wd_f[s].astype(BF16)
            nxt = nxt_ref[j]

            @pl.when(nxt >= 0)
            def _():
                for cp in weight_copies(nxt, 1 - s):
                    cp.start()

        xb = x_ref[...].astype(BF16)
        hs = []
        for t in range(n_tiles):
            cols = slice(t * GU_TILE, (t + 1) * GU_TILE)
            gu = _dot(xb, wgu_s[:, cols]) + bgu_ref[0, :, cols]
            gate = jnp.minimum(gu[:, :LANES], SWIGLU_LIMIT)
            lin = jnp.clip(gu[:, LANES:], -SWIGLU_LIMIT, SWIGLU_LIMIT)
            hs.append((gate * jax.nn.sigmoid(SWIGLU_ALPHA * gate) * (lin + 1.0)).astype(BF16))
        y_ref[...] = _dot(jnp.concatenate(hs, axis=1), wd_s[...]) + bd_ref[0]


def _moe(bexp, n_used, nxt, wslot, xs, wgu, bgu, wd, bd):
    n_blocks = bexp.shape[0]
    D = xs.shape[1]
    F2 = wgu.shape[2]
    F = wd.shape[1]
    wmap = lambda j, be, nu, nx, ws: (be[jnp.minimum(j, nu[0] - 1)], 0, 0)
    blk = lambda j, be, nu, nx, ws: (j, 0)
    used = lambda j, be, nu, nx, ws: (jnp.minimum(j, nu[0] - 1), 0)
    in_specs = [pl.BlockSpec((MOE_BLOCK, D), used),
                pl.BlockSpec(memory_space=pl.ANY), pl.BlockSpec((1, 1, F2), wmap),
                pl.BlockSpec(memory_space=pl.ANY), pl.BlockSpec((1, 1, D), wmap)]
    grid_spec = pltpu.PrefetchScalarGridSpec(
        num_scalar_prefetch=4, grid=(n_blocks,), in_specs=in_specs,
        out_specs=pl.BlockSpec((MOE_BLOCK, D), blk),
        scratch_shapes=[pltpu.VMEM((2, D, F2), F32), pltpu.VMEM((2, F, D), F32),
                        pltpu.VMEM((D, F2), BF16), pltpu.VMEM((F, D), BF16), pltpu.SemaphoreType.DMA((2,))])
    return pl.pallas_call(_moe_kernel, name="moe", grid_spec=grid_spec,
                          out_shape=jax.ShapeDtypeStruct((n_blocks * MOE_BLOCK, D), F32),
                          compiler_params=_cparams())(bexp, n_used, nxt, wslot, xs, wgu, bgu, wd, bd)


def _combine_kernel(seg_ref, seg_nxt_ref, ys_hbm, info_ref, x_ref, g_ref, b_ref, out_ref, comp, sem, *, alpha):
    i = pl.program_id(0)
    n_steps = pl.num_programs(0)
    slot = i % 2
    tm = x_ref.shape[0]
    cap = comp.shape[1]

    @pl.when(i == 0)
    def _():
        comp[...] = jnp.zeros_like(comp)
        _segment_copies(seg_ref, comp.at[0], ys_hbm, sem.at[0], to_global=False)

    @pl.when(i + 1 < n_steps)
    def _():
        _segment_copies(seg_nxt_ref, comp.at[1 - slot], ys_hbm, sem.at[1 - slot], to_global=False)

    _wait_rows(comp.at[slot], ys_hbm, sem.at[slot], seg_ref[0, 3, 0])
    info = info_ref[...].T
    lane = lax.broadcasted_iota(jnp.int32, (tm, cap), 1).astype(F32)
    wmat = jnp.zeros((tm, cap), F32)
    for k in range(TOP_K):
        wmat = jnp.where(lane == info[:, k:k + 1], info[:, TOP_K + k:TOP_K + k + 1], wmat)
    ffn = _dot(wmat.astype(BF16), comp[slot].astype(BF16))
    out_ref[...] = _layer_norm(alpha * x_ref[...] + ffn, g_ref[...], b_ref[...])


def _combine(seg, ys, info, x2, g, b, alpha, tm):
    T, D = x2.shape
    n_t = T // tm
    row = lambda i: (i, 0)
    seg_spec = lambda m: pl.BlockSpec((1, 4, LANES), m, memory_space=pltpu.SMEM)
    in_specs = [seg_spec(lambda i: (i, 0, 0)), seg_spec(lambda i: (jnp.minimum(i + 1, n_t - 1), 0, 0)),
                pl.BlockSpec(memory_space=pl.ANY),
                pl.BlockSpec((LANES, tm), lambda i: (0, i)), pl.BlockSpec((tm, D), row),
                _full_spec(g.shape), _full_spec(b.shape)]
    cap = TOP_K * tm + N_EXPERTS * SEG_ALIGN
    return pl.pallas_call(functools.partial(_combine_kernel, alpha=alpha), name="combine", grid=(n_t,), in_specs=in_specs,
                          out_specs=pl.BlockSpec((tm, D), row),
                          out_shape=jax.ShapeDtypeStruct((T, D), F32),
                          scratch_shapes=[pltpu.VMEM((2, cap, D), F32), pltpu.SemaphoreType.DMA((2,))],
                          compiler_params=_cparams())(seg, seg, ys, info, x2, g, b)


def _tile(n, pref):
    return pref if n % pref == 0 else n


def _layer(x2d, mem2d, pos2d, B, S, depth, w_in, sinks, w_attn_o, lam_re, lam_im, log_dt, b_re, b_im,
           c_re, c_im, d_skip, w_glu_a, w_glu_b, w_out, ln1_g, ln1_b, wq_c, wk_c, wv_c, wo_c, ln2_g,
           ln2_b, w_router, b_router, w_gate_up, b_gate_up, w_down, b_down, ln3_g, ln3_b):
    T, D = x2d.shape
    alpha = (2 * depth) ** 0.25
    rep = N_Q_HEADS // N_KV_HEADS
    q_w = N_Q_HEADS * HEAD_DIM
    kv_w = N_KV_HEADS * HEAD_DIM
    ssm_w = SSM_GROUP * SSM_GROUPS

    o_k, o_v, o_s = q_w, q_w + kv_w, q_w + 2 * kv_w
    o_ga, o_gs = o_s + ssm_w, o_s + ssm_w + D
    wq = w_in[:, :o_k].reshape(D, N_KV_HEADS, rep, HEAD_DIM).transpose(0, 2, 1, 3).reshape(D, q_w).astype(BF16)
    wk = w_in[:, o_k:o_v].astype(BF16)
    wv = w_in[:, o_v:o_s].astype(BF16)
    wu = w_in[:, o_s:o_ga].astype(BF16)
    wga = w_in[:, o_ga:o_gs].astype(BF16)
    wgs = w_in[:, o_gs:].astype(BF16)
    wo_attn = w_attn_o.reshape(N_KV_HEADS, rep, HEAD_DIM, D).transpose(1, 0, 2, 3).reshape(q_w, D).astype(BF16)
    half = HEAD_DIM // 2
    inv_freq = jnp.power(ROPE_THETA, -jnp.arange(half, dtype=F32) / half)
    invf = jnp.tile(inv_freq, LANES // half)[None, :]
    sink_rows = jnp.repeat(sinks.astype(F32).reshape(N_KV_HEADS, rep), WINDOW, axis=1)
    sink_cols = jnp.full((N_KV_HEADS, rep * WINDOW, 2 * WINDOW), NEG_BIG, F32).at[:, :, 0].set(sink_rows)

    q2, k, v, ut, ga, gs = _inproj(x2d, pos2d, invf, wq, wk, wv, wu, wga, wgs, _tile(T, ROW_TILE))
    att = _swa(q2, k, v, sink_cols, wo_attn, ga, B, S, _tile(S, ROW_TILE))

    lam_row = jnp.stack([jnp.concatenate([lam_re, lam_re], -1), jnp.concatenate([lam_im, lam_im], -1)], 1)
    lam_col = jnp.swapaxes(lam_row, 1, 2)
    bt_re = jnp.swapaxes(b_re, 1, 2)
    bt_im = jnp.swapaxes(b_im, 1, 2)
    bt_a = jnp.concatenate([bt_re, bt_im], -1)
    bt_b = jnp.concatenate([-bt_im, bt_re], -1)
    ct_re = jnp.swapaxes(c_re, 1, 2)
    ct_im = jnp.swapaxes(c_im, 1, 2)
    ct_a = jnp.concatenate([ct_re, -ct_im], 1)
    ct_b = jnp.concatenate([-ct_im, -ct_re], 1)
    dsk = jnp.repeat(d_skip.reshape(SSM_GROUPS, SSM_GROUP), SSM_GROUP, axis=1)[:, :, None]
    yt = _ssm(ut, lam_row, lam_col, log_dt.reshape(SSM_GROUPS, 1, 1), bt_a, bt_b, c_re, c_im, ct_a, ct_b, dsk,
              S // LANES)

    x1 = _mix(yt, att, gs, x2d, w_glu_a.astype(BF16), w_glu_b.astype(BF16), w_out.astype(BF16),
              ln1_g[None, :], ln1_b[None, :], alpha, _tile(T, ROW_TILE))

    kc, vc = _memkv(mem2d, wk_c.astype(BF16), wv_c.astype(BF16), _tile(mem2d.shape[0], MEM_TILE))
    wr = jnp.zeros((LANES, D), F32).at[:N_EXPERTS, :].set(w_router.T).astype(BF16)
    br = jnp.full((LANES, 1), NEG_BIG, F32).at[:N_EXPERTS, 0].set(b_router)
    x2, logits = _cross(x1, kc, vc, wq_c.astype(BF16), wo_c.astype(BF16), ln2_g[None, :], ln2_b[None, :],
                        wr, br, alpha, S, _tile(S, ROW_TILE))

    tm_r = _tile(T, ROUTE_TILE)
    n_tiles = T // tm_r
    info, segf, counts = _route(logits, tm_r)
    total = counts[:, 0].astype(jnp.int32)
    padded = (total + MOE_BLOCK - 1) // MOE_BLOCK * MOE_BLOCK
    ends_pad = jnp.cumsum(padded)
    start_pad = ends_pad - padded
    n_blocks = -(-(T * TOP_K + n_tiles * N_EXPERTS * (SEG_ALIGN - 1)) // MOE_BLOCK) + N_EXPERTS
    segi = segf.astype(jnp.int32)
    seg_n, seg_off, seg_start = segi[:, :, 0], segi[:, :, 1], segi[:, :, 2] + start_pad[None, :]
    seg_tot = jnp.broadcast_to(jnp.sum(seg_n, axis=1, keepdims=True), seg_n.shape)
    seg = jnp.stack([seg_n, seg_off, seg_start, seg_tot], axis=1)
    seg = jnp.pad(seg, ((0, 0), (0, 0), (0, LANES - N_EXPERTS)))
    block_start = jnp.arange(n_blocks, dtype=jnp.int32) * MOE_BLOCK
    bexp = jnp.minimum(jnp.sum(block_start[:, None] >= ends_pad[None, :], axis=1), N_EXPERTS - 1).astype(jnp.int32)
    n_used = (ends_pad[-1] // MOE_BLOCK).astype(jnp.int32)[None]
    n_gu_tiles = b_gate_up.shape[1] // GU_TILE
    bgu = b_gate_up.reshape(N_EXPERTS, n_gu_tiles, LANES, 2).transpose(0, 1, 3, 2).reshape(N_EXPERTS, 1, -1)
    xs = _dispatch(ends_pad, seg, x2, info, n_blocks * MOE_BLOCK, tm_r)
    has_rows = padded > 0
    eid = jnp.arange(N_EXPERTS, dtype=jnp.int32)
    later = jnp.where(has_rows[None, :] & (eid[None, :] > eid[:, None]), eid[None, :], N_EXPERTS)
    next_used = jnp.min(later, axis=1)
    next_used = jnp.where(next_used < N_EXPERTS, next_used, -1).astype(jnp.int32)
    slot_of = ((jnp.cumsum(has_rows.astype(jnp.int32)) - 1) % 2).astype(jnp.int32)
    ys = _moe(bexp, n_used, next_used[bexp], slot_of[bexp], xs, w_gate_up, bgu, w_down, b_down[:, None, :])
    return _combine(seg, ys, info, x2, ln3_g[None, :], ln3_b[None, :], alpha, tm_r)


def kernel(x, mem, positions, w_in, sinks, w_attn_o, lam_re, lam_im, log_dt, b_re, b_im, c_re, c_im, d_skip,
           w_glu_a, w_glu_b, w_out, ln1_g, ln1_b, wq_c, wk_c, wv_c, wo_c, ln2_g, ln2_b, w_router, b_router,
           w_gate_up, b_gate_up, w_down, b_down, ln3_g, ln3_b):
    B, S, D = x.shape
    depth = w_in.shape[0]
    x2d = x.reshape(B * S, D)
    mem2d = mem.reshape(-1, D)
    pos2d = positions.reshape(B * S, 1)
    per_layer = (w_in, sinks, w_attn_o, lam_re, lam_im, log_dt, b_re, b_im, c_re, c_im, d_skip, w_glu_a,
                 w_glu_b, w_out, ln1_g, ln1_b, wq_c, wk_c, wv_c, wo_c, ln2_g, ln2_b, w_router, b_router,
                 w_gate_up, b_gate_up, w_down, b_down, ln3_g, ln3_b)
    for l in range(depth):
        x2d = _layer(x2d, mem2d, pos2d, B, S, depth, *(w[l] for w in per_layer))
    return x2d.reshape(B, S, D)
```

```python
import functools

import jax
import jax.numpy as jnp
from jax import lax
from jax.experimental import pallas as pl
from jax.experimental.pallas import tpu as pltpu

N_Q_HEADS = 16
N_KV_HEADS = 2
HEAD_DIM = 64
WINDOW = 128
ROPE_THETA = 10000.0
SSM_GROUP = 16
SSM_GROUPS = 32
SSM_STATE = 64
N_CROSS_HEADS = 4
N_EXPERTS = 32
TOP_K = 4
SWIGLU_ALPHA = 1.702
SWIGLU_LIMIT = 7.0
MOE_BLOCK = 512
ROUTE_TILE = 512
ROW_TILE = 1024
MEM_TILE = 512
LN_EPS = 1e-5

LANES = 128
VMEM_LIMIT_BYTES = 56 * 1024 * 1024

NEG_BIG = -1e30
BF16 = jnp.bfloat16
F32 = jnp.float32
HI = lax.Precision.HIGHEST


def _cparams(n_axes=1):
    return pltpu.CompilerParams(dimension_semantics=("arbitrary",) * n_axes,
                                vmem_limit_bytes=VMEM_LIMIT_BYTES)


def _full_spec(shape):
    n = len(shape)
    return pl.BlockSpec(shape, lambda *_: (0,) * n, pipeline_mode=pl.Buffered(1))


def _dot(a, b):
    return jnp.dot(a, b, preferred_element_type=F32)


def _dot_nt(a, b):
    return lax.dot_general(a, b, (((1,), (1,)), ((), ())), preferred_element_type=F32)


def _layer_norm(y, g, b):
    mu = jnp.mean(y, axis=-1, keepdims=True)
    d = y - mu
    var = jnp.mean(d * d, axis=-1, keepdims=True)
    return d * lax.rsqrt(var + LN_EPS) * g + b


def _rope(t, cos, sin_signed, first_half):
    half = HEAD_DIM // 2
    partner = jnp.where(first_half, pltpu.roll(t, LANES - half, axis=1), pltpu.roll(t, half, axis=1))
    return t * cos + partner * sin_signed


def _inproj_kernel(x_ref, pos_ref, invf_ref, wq_ref, wk_ref, wv_ref, wu_ref, wga_ref, wgs_ref,
                   q2_ref, k_ref, v_ref, ut_ref, ga_ref, gs_ref):
    tm = x_ref.shape[0]
    xb = x_ref[...].astype(BF16)
    ang = pos_ref[...].astype(F32) * invf_ref[...]
    cos = jnp.cos(ang)
    sin = jnp.sin(ang)
    first_half = (lax.broadcasted_iota(jnp.int32, (tm, LANES), 1) % HEAD_DIM) < (HEAD_DIM // 2)
    first_half_w = (lax.broadcasted_iota(jnp.int32, (WINDOW, LANES), 1) % HEAD_DIM) < (HEAD_DIM // 2)
    sin_signed = jnp.where(first_half, -sin, sin)

    q = _dot(xb, wq_ref[...])
    n_rep = q.shape[1] // LANES
    scale = HEAD_DIM ** -0.5
    for j in range(tm // WINDOW):
        rows = slice(j * WINDOW, (j + 1) * WINDOW)
        for r in range(n_rep):
            t = _rope(q[rows, r * LANES:(r + 1) * LANES], cos[rows], sin_signed[rows], first_half_w)
            base = (j * n_rep + r) * WINDOW
            q2_ref[base:base + WINDOW, :] = (t * scale).astype(BF16)
    k_ref[...] = _rope(_dot(xb, wk_ref[...]), cos, sin_signed, first_half).astype(BF16)
    v_ref[...] = _dot(xb, wv_ref[...]).astype(BF16)
    u = _dot(xb, wu_ref[...])
    for j in range(tm // LANES):
        ut_ref[j] = u[j * LANES:(j + 1) * LANES, :].T
    ga_ref[...] = jax.nn.sigmoid(_dot(xb, wga_ref[...])).astype(BF16)
    gs_ref[...] = jax.nn.sigmoid(_dot(xb, wgs_ref[...])).astype(BF16)


def _inproj(x2d, pos2d, invf, wq, wk, wv, wu, wga, wgs, tm):
    T, D = x2d.shape
    n_rep = wq.shape[1] // LANES
    ssm_w = wu.shape[1]
    row = lambda i: (i, 0)
    out_shape = (
        jax.ShapeDtypeStruct((T * n_rep, LANES), BF16),
        jax.ShapeDtypeStruct((T, LANES), BF16),
        jax.ShapeDtypeStruct((T, LANES), BF16),
        jax.ShapeDtypeStruct((T // LANES, ssm_w, LANES), F32),
        jax.ShapeDtypeStruct((T, D), BF16),
        jax.ShapeDtypeStruct((T, D), BF16),
    )
    in_specs = [pl.BlockSpec((tm, D), row), pl.BlockSpec((tm, 1), row), _full_spec(invf.shape),
                _full_spec(wq.shape), _full_spec(wk.shape), _full_spec(wv.shape),
                _full_spec(wu.shape), _full_spec(wga.shape), _full_spec(wgs.shape)]
    out_specs = (pl.BlockSpec((tm * n_rep, LANES), row), pl.BlockSpec((tm, LANES), row),
                 pl.BlockSpec((tm, LANES), row),
                 pl.BlockSpec((tm // LANES, ssm_w, LANES), lambda i: (i, 0, 0)),
                 pl.BlockSpec((tm, D), row), pl.BlockSpec((tm, D), row))
    return pl.pallas_call(_inproj_kernel, name="inproj", grid=(T // tm,), in_specs=in_specs, out_specs=out_specs,
                          out_shape=out_shape, compiler_params=_cparams())(
        x2d, pos2d, invf, wq, wk, wv, wu, wga, wgs)


def _swa_kernel(q2_ref, kc_ref, kp_ref, vc_ref, vp_ref, fill_ref, wo_ref, ga_ref, out_ref, cat_ref):
    i = pl.program_id(1)
    tq = kc_ref.shape[0]
    n_sub = tq // WINDOW
    rep = N_Q_HEADS // N_KV_HEADS
    rows_all = rep * WINDOW
    kfull = jnp.concatenate([kp_ref[...], kc_ref[...]], axis=0)
    vfull = jnp.concatenate([vp_ref[...], vc_ref[...]], axis=0)
    lane = lax.broadcasted_iota(jnp.int32, (2 * WINDOW, LANES), 1)
    qi = lax.broadcasted_iota(jnp.int32, (rows_all, 2 * WINDOW), 0) % WINDOW
    ci = lax.broadcasted_iota(jnp.int32, (rows_all, 2 * WINDOW), 1)
    local = (ci > qi) & (ci <= qi + WINDOW)
    out_lane = lax.broadcasted_iota(jnp.int32, (rows_all, LANES), 1)
    band_row = lax.broadcasted_iota(jnp.int32, (2 * WINDOW, LANES), 0)
    for j in range(n_sub):
        qs = q2_ref[j * rows_all:(j + 1) * rows_all, :]
        kb = kfull[j * WINDOW:(j + 2) * WINDOW, :]
        vb = vfull[j * WINDOW:(j + 2) * WINDOW, :]
        vb = jnp.where(band_row == 0, jnp.zeros_like(vb), vb)
        mask = local
        if j == 0:
            mask = mask & ((ci >= WINDOW) | (i > 0))
        o = None
        for g in range(N_KV_HEADS):
            in_group = (lane >= g * HEAD_DIM) & (lane < (g + 1) * HEAD_DIM)
            kg = jnp.where(in_group, kb, jnp.zeros_like(kb))
            s = _dot_nt(qs, kg)
            s = jnp.where(mask, s, fill_ref[g])
            m = jnp.max(s, axis=-1, keepdims=True)
            p = jnp.exp(s - m)
            denom = jnp.sum(p, axis=-1, keepdims=True)
            og = _dot(p.astype(BF16), vb) * (1.0 / denom)
            o = og if o is None else jnp.where(out_lane < g * HEAD_DIM, o, og)
        ob = o.astype(BF16)
        for r in range(rep):
            cat_ref[j * WINDOW:(j + 1) * WINDOW, r * LANES:(r + 1) * LANES] = ob[r * WINDOW:(r + 1) * WINDOW, :]
    attn = _dot(cat_ref[...], wo_ref[...])
    out_ref[...] = (attn * ga_ref[...].astype(F32)).astype(BF16)


def _swa(q2, k, v, sink_cols, wo, ga, B, S, tq):
    T, D = ga.shape
    rep = N_Q_HEADS // N_KV_HEADS
    n_i = S // tq
    n_sub = tq // WINDOW
    in_specs = [
        pl.BlockSpec((tq * rep, LANES), lambda b, i: (b * n_i + i, 0)),
        pl.BlockSpec((tq, LANES), lambda b, i: (b * n_i + i, 0)),
        pl.BlockSpec((WINDOW, LANES), lambda b, i: (b * (S // WINDOW) + jnp.maximum(i * n_sub - 1, 0), 0)),
        pl.BlockSpec((tq, LANES), lambda b, i: (b * n_i + i, 0)),
        pl.BlockSpec((WINDOW, LANES), lambda b, i: (b * (S // WINDOW) + jnp.maximum(i * n_sub - 1, 0), 0)),
        _full_spec(sink_cols.shape), _full_spec(wo.shape),
        pl.BlockSpec((tq, D), lambda b, i: (b * n_i + i, 0)),
    ]
    return pl.pallas_call(
        _swa_kernel, name="swa", grid=(B, n_i), in_specs=in_specs,
        out_specs=pl.BlockSpec((tq, D), lambda b, i: (b * n_i + i, 0)),
        out_shape=jax.ShapeDtypeStruct((T, D), BF16),
        scratch_shapes=[pltpu.VMEM((tq, D), BF16)],
        compiler_params=_cparams(2))(q2, k, k, v, v, sink_cols, wo, ga)


def _ssm_prep_kernel(lam_row_ref, lam_col_ref, dt_ref, bt_a_ref, bt_b_ref, c_re_ref, c_im_ref,
                     ct_a_ref, ct_b_ref, dsk_ref, toep_ref, wst_ref, wout_ref, apow_ref):
    L = LANES
    P = SSM_STATE
    dt = jnp.exp(dt_ref[0])
    lr2 = lam_row_ref[0, 0:1, :]
    li2 = lam_row_ref[0, 1:2, :]
    mag = jnp.exp(lr2 * dt)
    ar = mag * jnp.cos(li2 * dt)
    ai = mag * jnp.sin(li2 * dt)
    den = lr2 * lr2 + li2 * li2
    f_re = ((ar - 1.0) * lr2 + ai * li2) / den
    f_im = (ai * lr2 - (ar - 1.0) * li2) / den
    bt_a = bt_a_ref[0]
    bt_b = bt_b_ref[0]
    bb_a = f_re * bt_a + f_im * bt_b
    bb_b = f_re * bt_b - f_im * bt_a

    tau_rev = (L - 1 - lax.broadcasted_iota(jnp.int32, (L, 2 * P), 0)).astype(F32)
    g_mag = jnp.exp(lr2 * dt * tau_rev)
    g_re = g_mag * jnp.cos(li2 * dt * tau_rev)
    g_im = g_mag * jnp.sin(li2 * dt * tau_rev)
    for i in range(SSM_GROUP):
        wst_ref[0, i * L:(i + 1) * L, :] = (g_re * bb_a[i:i + 1, :] + g_im * bb_b[i:i + 1, :]).astype(BF16)

    lane2 = lax.broadcasted_iota(jnp.int32, (1, 2 * P), 1)
    for kk in range(4):
        n = float(L * (1 << kk))
        pm = jnp.exp(lr2 * dt * n)
        p_re = pm * jnp.cos(li2 * dt * n)
        p_im = pm * jnp.sin(li2 * dt * n)
        apow_ref[0, 2 * kk:2 * kk + 1, :] = p_re
        apow_ref[0, 2 * kk + 1:2 * kk + 2, :] = jnp.where(lane2 < P, -p_im, p_im)

    lrc = lam_col_ref[0, :, 0:1]
    lic = lam_col_ref[0, :, 1:2]
    tau = lax.broadcasted_iota(jnp.int32, (2 * P, L), 1).astype(F32)
    e0_mag = jnp.exp(lrc * dt * tau)
    e0_re = e0_mag * jnp.cos(lic * dt * tau)
    e0_im = e0_mag * jnp.sin(lic * dt * tau)
    e1_mag = jnp.exp(lrc * dt * (tau + 1.0))
    e1_re = e1_mag * jnp.cos(lic * dt * (tau + 1.0))
    e1_im = e1_mag * jnp.sin(lic * dt * (tau + 1.0))

    ct_a = ct_a_ref[0]
    ct_b = ct_b_ref[0]
    for o in range(SSM_GROUP):
        wout_ref[0, :, o * L:(o + 1) * L] = (ct_a[:, o:o + 1] * e1_re + ct_b[:, o:o + 1] * e1_im).astype(BF16)

    c_re = c_re_ref[0]
    c_im = c_im_ref[0]
    bb_re = bb_a[:, :P]
    bb_im = bb_a[:, P:]
    m_re = (c_re[:, None, :] * bb_re[None, :, :] - c_im[:, None, :] * bb_im[None, :, :])
    m_im = (c_re[:, None, :] * bb_im[None, :, :] + c_im[:, None, :] * bb_re[None, :, :])
    m_re = m_re.reshape(SSM_GROUP * SSM_GROUP, P)
    m_im = m_im.reshape(SSM_GROUP * SSM_GROUP, P)
    kt = (jnp.dot(m_re, e0_re[:P, :], precision=HI, preferred_element_type=F32)
          - jnp.dot(m_im, e0_im[:P, :], precision=HI, preferred_element_type=F32))
    rowi = lax.broadcasted_iota(jnp.int32, (SSM_GROUP * SSM_GROUP, L), 0)
    coli = lax.broadcasted_iota(jnp.int32, (SSM_GROUP * SSM_GROUP, L), 1)
    dsk = dsk_ref[0]
    kt = kt + jnp.where((coli == 0) & ((rowi // SSM_GROUP) == (rowi % SSM_GROUP)), dsk, 0.0)

    cc = lax.broadcasted_iota(jnp.int32, (L, L), 0)
    cp = lax.broadcasted_iota(jnp.int32, (L, L), 1)
    causal = cp >= cc
    for o in range(SSM_GROUP):
        for i in range(SSM_GROUP):
            kv = kt[o * SSM_GROUP + i:o * SSM_GROUP + i + 1, :]
            blk = pltpu.roll(jnp.broadcast_to(kv, (L, L)), 0, axis=1, stride=1, stride_axis=0)
            blk = jnp.where(causal, blk, 0.0)
            toep_ref[0, i * L:(i + 1) * L, o * L:(o + 1) * L] = blk.astype(BF16)


def _ssm_scan_kernel(ut_ref, toep_ref, wst_ref, wout_ref, apow_ref, yt_ref, *, n_chunks):
    nb = ut_ref.shape[0]
    P = SSM_STATE
    ut2 = ut_ref.reshape(nb * SSM_GROUP, LANES)
    lhs = jnp.concatenate([ut2[pl.ds(i, nb, stride=SSM_GROUP), :] for i in range(SSM_GROUP)],
                          axis=1).astype(BF16)
    y = _dot(lhs, toep_ref[0])
    st = _dot(lhs, wst_ref[0])
    srow = lax.broadcasted_iota(jnp.int32, (nb, 2 * P), 0) % n_chunks
    kk = 0
    while (1 << kk) < n_chunks:
        d = 1 << kk
        pa = apow_ref[0, 2 * kk:2 * kk + 1, :]
        pb = apow_ref[0, 2 * kk + 1:2 * kk + 2, :]
        prev = pltpu.roll(st, d, axis=0)
        prev = prev * pa + pltpu.roll(prev, P, axis=1) * pb
        st = st + jnp.where(srow >= d, prev, 0.0)
        kk += 1
    h0 = jnp.where(srow >= 1, pltpu.roll(st, 1, axis=0), 0.0)
    y = y + _dot(h0.astype(BF16), wout_ref[0])
    for o in range(SSM_GROUP):
        yt_ref[o] = y[:, o * LANES:(o + 1) * LANES]


def _ssm_kernel(*refs, n_chunks):
    params, (ut_ref, yt_ref), ops = refs[:10], refs[10:12], refs[12:]
    sets = (ops[0::2], ops[1::2])
    g = pl.program_id(0)

    @pl.when(g == 0)
    def _():
        _ssm_prep_kernel(*params, *sets[0])

    for p in range(2):
        @pl.when((g > 0) & (g % 2 == p))
        def _(p=p):
            _ssm_prep_kernel(*params, *sets[p])
            _ssm_scan_kernel(ut_ref, *sets[1 - p], yt_ref, n_chunks=n_chunks)


def _ssm(ut, lam_row, lam_col, log_dt, bt_a, bt_b, c_re, c_im, ct_a, ct_b, dsk, n_chunks):
    nb, ssm_w, _ = ut.shape
    G = SSM_GROUPS
    KW = SSM_GROUP * LANES
    params = [lam_row, lam_col, log_dt, bt_a, bt_b, c_re, c_im, ct_a, ct_b, dsk]
    build = lambda g: (jnp.minimum(g, G - 1), 0, 0)
    in_specs = [pl.BlockSpec((1,) + a.shape[1:], build) for a in params]
    in_specs.append(pl.BlockSpec((nb, SSM_GROUP, LANES), lambda g: (0, jnp.maximum(g - 1, 0), 0)))
    op_shapes = [pltpu.VMEM((1, KW, KW), BF16), pltpu.VMEM((1, KW, 2 * SSM_STATE), BF16),
                 pltpu.VMEM((1, 2 * SSM_STATE, KW), BF16), pltpu.VMEM((1, 8, 2 * SSM_STATE), F32)]
    return pl.pallas_call(
        functools.partial(_ssm_kernel, n_chunks=n_chunks), name="ssm", grid=(G + 1,), in_specs=in_specs,
        out_specs=pl.BlockSpec((SSM_GROUP, nb, LANES), lambda g: (jnp.maximum(g - 1, 0), 0, 0)),
        out_shape=jax.ShapeDtypeStruct((ssm_w, nb, LANES), F32),
        scratch_shapes=[s for s in op_shapes for _ in range(2)],
        compiler_params=_cparams())(*params, ut)


def _mix_kernel(yt_ref, att_ref, gs_ref, x_ref, wa_ref, wb_ref, wo_ref, g_ref, b_ref, out_ref, zt_ref,
                *, alpha):
    n_ch, n_blk = yt_ref.shape[0], yt_ref.shape[1]
    yt2 = yt_ref.reshape(n_ch * n_blk, LANES)
    for j in range(n_blk):
        y = yt2[pl.ds(j, n_ch, stride=n_blk), :]
        z = jax.nn.gelu(y, approximate=True)
        zt_ref[j * LANES:(j + 1) * LANES, :] = z.T.astype(BF16)
    z = zt_ref[...]
    ssm_out = _dot(z, wa_ref[...]) * jax.nn.sigmoid(_dot(z, wb_ref[...]))
    mixed = att_ref[...].astype(F32) + gs_ref[...].astype(F32) * ssm_out
    y = alpha * x_ref[...] + _dot(mixed.astype(BF16), wo_ref[...])
    out_ref[...] = _layer_norm(y, g_ref[...], b_ref[...])


def _mix(yt, att, gs, x2d, wa, wb, wo, g, b, alpha, tm):
    T, D = x2d.shape
    ssm_w = yt.shape[0]
    row = lambda i: (i, 0)
    in_specs = [pl.BlockSpec((ssm_w, tm // LANES, LANES), lambda i: (0, i, 0)),
                pl.BlockSpec((tm, D), row), pl.BlockSpec((tm, D), row), pl.BlockSpec((tm, D), row),
                _full_spec(wa.shape), _full_spec(wb.shape), _full_spec(wo.shape),
                _full_spec(g.shape), _full_spec(b.shape)]
    return pl.pallas_call(functools.partial(_mix_kernel, alpha=alpha), name="mix", grid=(T // tm,), in_specs=in_specs,
                          out_specs=pl.BlockSpec((tm, D), row),
                          out_shape=jax.ShapeDtypeStruct((T, D), F32),
                          scratch_shapes=[pltpu.VMEM((tm, ssm_w), BF16)],
                          compiler_params=_cparams())(yt, att, gs, x2d, wa, wb, wo, g, b)


def _memkv_kernel(m_ref, wk_ref, wv_ref, k_ref, v_ref):
    mb = m_ref[...].astype(BF16)
    k_ref[...] = _dot(mb, wk_ref[...]).astype(BF16)
    v_ref[...] = _dot(mb, wv_ref[...]).astype(BF16)


def _memkv(mem2d, wk, wv, tm):
    R, D = mem2d.shape
    row = lambda i: (i, 0)
    return pl.pallas_call(_memkv_kernel, name="memkv", grid=(R // tm,),
                          in_specs=[pl.BlockSpec((tm, D), row), _full_spec(wk.shape), _full_spec(wv.shape)],
                          out_specs=(pl.BlockSpec((tm, D), row), pl.BlockSpec((tm, D), row)),
                          out_shape=(jax.ShapeDtypeStruct((R, D), BF16), jax.ShapeDtypeStruct((R, D), BF16)),
                          compiler_params=_cparams())(mem2d, wk, wv)


def _cross_kernel(x_ref, k_ref, v_ref, wq_ref, wo_ref, g_ref, b_ref, wr_ref, br_ref, out_ref, logit_ref,
                  cat_ref, *, alpha):
    x = x_ref[...]
    D = x.shape[1]
    hd = D // N_CROSS_HEADS
    q = (_dot(x.astype(BF16), wq_ref[...]) * (hd ** -0.5)).astype(BF16)
    for h in range(N_CROSS_HEADS):
        cols = slice(h * hd, (h + 1) * hd)
        s = _dot_nt(q[:, cols], k_ref[:, cols])
        m = jnp.max(s, axis=-1, keepdims=True)
        p = jnp.exp(s - m)
        w = (p / jnp.sum(p, axis=-1, keepdims=True)).astype(BF16)
        cat_ref[:, cols] = _dot(w, v_ref[:, cols]).astype(BF16)
    y = alpha * x + _dot(cat_ref[...], wo_ref[...])
    x2 = _layer_norm(y, g_ref[...], b_ref[...])
    out_ref[...] = x2
    logit_ref[...] = _dot_nt(wr_ref[...], x2.astype(BF16)) + br_ref[...]


def _cross(x1, kc, vc, wq, wo, g, b, wr, br, alpha, S, tm):
    T, D = x1.shape
    n_mem = kc.shape[0] // (T // S)
    row = lambda i: (i, 0)
    per_b = S // tm
    in_specs = [pl.BlockSpec((tm, D), row),
                pl.BlockSpec((n_mem, D), lambda i: (i // per_b, 0)),
                pl.BlockSpec((n_mem, D), lambda i: (i // per_b, 0)),
                _full_spec(wq.shape), _full_spec(wo.shape), _full_spec(g.shape), _full_spec(b.shape),
                _full_spec(wr.shape), _full_spec(br.shape)]
    return pl.pallas_call(functools.partial(_cross_kernel, alpha=alpha), name="cross", grid=(T // tm,), in_specs=in_specs,
                          out_specs=(pl.BlockSpec((tm, D), row), pl.BlockSpec((LANES, tm), lambda i: (0, i))),
                          out_shape=(jax.ShapeDtypeStruct((T, D), F32), jax.ShapeDtypeStruct((LANES, T), F32)),
                          scratch_shapes=[pltpu.VMEM((tm, D), BF16)],
                          compiler_params=_cparams())(x1, kc, vc, wq, wo, g, b, wr, br)


SEG_ALIGN = 8


def _route_kernel(lt_ref, info_ref, seg_ref, count_ref, carry_ref):
    tm = lt_ref.shape[1]
    E = N_EXPERTS

    @pl.when(pl.program_id(0) == 0)
    def _():
        carry_ref[...] = jnp.zeros_like(carry_ref)

    l = lt_ref[0:E, :]
    erow = lax.broadcasted_iota(jnp.int32, (E, tm), 0)
    rank = jnp.zeros((E, tm), F32)
    for e2 in range(E):
        other = l[e2:e2 + 1, :]
        tie = jnp.where(erow > e2, 1.0, 0.0)
        rank = rank + jnp.where(other > l, 1.0, jnp.where(other == l, tie, 0.0))
    chosen = rank < float(TOP_K)
    onehot = jnp.where(chosen, 1.0, 0.0)
    p = jnp.where(chosen, jnp.exp(l - jnp.max(l, axis=0, keepdims=True)), 0.0)
    gate = p / jnp.sum(p, axis=0, keepdims=True)
    a = lax.broadcasted_iota(jnp.int32, (tm, tm), 0)
    b = lax.broadcasted_iota(jnp.int32, (tm, tm), 1)
    cum = _dot(onehot.astype(BF16), jnp.where(a < b, 1.0, 0.0).astype(BF16))
    n_seg = jnp.floor((jnp.sum(onehot, axis=1, keepdims=True) + (SEG_ALIGN - 1.0)) * (1.0 / SEG_ALIGN)) * SEG_ALIGN
    n_seg = jnp.broadcast_to(n_seg, (E, LANES))
    erow_l = lax.broadcasted_iota(jnp.int32, (E, LANES), 0)
    seg_off = jnp.zeros((E, LANES), F32)
    for e2 in range(E - 1):
        seg_off = seg_off + jnp.where(erow_l > e2, n_seg[e2:e2 + 1, :], 0.0)
    pos = cum + seg_off[:, 0:1]
    rows = [jnp.sum(jnp.where(rank == float(k), pos, 0.0), axis=0, keepdims=True) for k in range(TOP_K)]
    rows += [jnp.sum(jnp.where(rank == float(k), gate, 0.0), axis=0, keepdims=True) for k in range(TOP_K)]
    rows.append(jnp.zeros((info_ref.shape[0] - 2 * TOP_K, tm), F32))
    info_ref[...] = jnp.concatenate(rows, axis=0)
    carry = carry_ref[...]
    lane = lax.broadcasted_iota(jnp.int32, (E, LANES), 1)
    seg_ref[0] = jnp.where(lane == 0, n_seg, jnp.where(lane == 1, seg_off, jnp.where(lane == 2, carry, 0.0)))
    carry_ref[...] = carry + n_seg
    count_ref[...] = carry_ref[...]


def _route(logits_t, tm):
    T = logits_t.shape[1]
    E = N_EXPERTS
    col = lambda i: (0, i)
    return pl.pallas_call(_route_kernel, name="route", grid=(T // tm,),
                          in_specs=[pl.BlockSpec((LANES, tm), col)],
                          out_specs=(pl.BlockSpec((LANES, tm), col), pl.BlockSpec((1, E, LANES), lambda i: (i, 0, 0)),
                                     pl.BlockSpec((E, LANES), lambda i: (0, 0))),
                          out_shape=(jax.ShapeDtypeStruct((LANES, T), F32),
                                     jax.ShapeDtypeStruct((T // tm, E, LANES), F32),
                                     jax.ShapeDtypeStruct((E, LANES), F32)),
                          scratch_shapes=[pltpu.VMEM((E, LANES), F32)],
                          compiler_params=_cparams())(logits_t)


GU_TILE = 2 * LANES


SEG_SIZES = tuple(ROUTE_TILE >> s for s in range(ROUTE_TILE.bit_length()) if (ROUTE_TILE >> s) >= SEG_ALIGN)
SEG_LARGE = 128


def _segment_copies(seg_ref, local_ref, global_ref, sem, to_global):
    def pieces(n, off, start, sizes, prio):
        for sz in sizes:
            take = (n & sz) != 0

            @pl.when(take)
            def _(off=off, start=start, sz=sz):
                loc = local_ref.at[pl.ds(pl.multiple_of(off, SEG_ALIGN), sz)]
                glo = global_ref.at[pl.ds(pl.multiple_of(start, SEG_ALIGN), sz)]
                if to_global:
                    pltpu.make_async_copy(loc, glo, sem).start(priority=prio)
                else:
                    pltpu.make_async_copy(glo, loc, sem).start(priority=prio)

            step = jnp.where(take, sz, 0)
            off = off + step
            start = start + step

    n_large = sum(1 for sz in SEG_SIZES if sz >= SEG_LARGE)
    for e in range(N_EXPERTS):
        n = seg_ref[0, 0, e]
        off = seg_ref[0, 1, e]
        start = seg_ref[0, 2, e]

        @pl.when(n >= SEG_LARGE)
        def _(n=n, off=off, start=start, e=e):
            pieces(n, off, start, SEG_SIZES[:n_large], e % 2)

        large = n & ~(SEG_LARGE - 1)
        pieces(n, off + large, start + large, SEG_SIZES[n_large:], e % 2)


def _wait_rows(local_ref, global_ref, sem, n_rows):
    @pl.when(n_rows > 0)
    def _():
        n = pl.multiple_of(n_rows, SEG_ALIGN)
        pltpu.make_async_copy(global_ref.at[pl.ds(0, n)], local_ref.at[pl.ds(0, n)], sem).wait()


def _dispatch_kernel(ends_ref, seg_ref, x_ref, info_ref, xs_hbm, comp, tot, sem):
    i = pl.program_id(0)
    n_steps = pl.num_programs(0)
    slot = i % 2
    tm = x_ref.shape[0]
    cap = comp.shape[1]

    @pl.when(i == 0)
    def _():
        comp[1, 0:MOE_BLOCK, :] = jnp.zeros((MOE_BLOCK, comp.shape[2]), F32)

        def zero_block(b, sem_ref):
            return pltpu.make_async_copy(comp.at[1, pl.ds(0, MOE_BLOCK)],
                                         xs_hbm.at[pl.ds(pl.multiple_of(b * MOE_BLOCK, MOE_BLOCK), MOE_BLOCK)],
                                         sem_ref)

        first_unused = ends_ref[N_EXPERTS - 1] // MOE_BLOCK
        n_all = xs_hbm.shape[0] // MOE_BLOCK
        for act in ("start", "wait"):
            for e in range(N_EXPERTS):
                end = ends_ref[e]
                prev = ends_ref[e - 1] if e > 0 else 0

                @pl.when(end > prev)
                def _(end=end, act=act):
                    getattr(zero_block(end // MOE_BLOCK - 1, sem.at[1]), act)()

            lax.fori_loop(first_unused, n_all, lambda b, c, act=act: (getattr(zero_block(b, sem.at[0]), act)(), c)[1], 0)
        tot[0] = 0
        tot[1] = 0

    _wait_rows(comp.at[slot], xs_hbm, sem.at[slot], tot[slot])
    r = lax.broadcasted_iota(jnp.int32, (cap, tm), 0).astype(F32)
    sel = jnp.zeros((cap, tm), F32)
    for k in range(TOP_K):
        sel = jnp.where(r == info_ref[k:k + 1, :], 1.0, sel)
    sel = sel.astype(BF16)
    comp[slot] = _dot(sel, x_ref[...].astype(BF16))
    _segment_copies(seg_ref, comp.at[slot], xs_hbm, sem.at[slot], to_global=True)
    tot[slot] = seg_ref[0, 3, 0]

    @pl.when(i == n_steps - 1)
    def _():
        _wait_rows(comp.at[slot], xs_hbm, sem.at[slot], tot[slot])
        _wait_rows(comp.at[1 - slot], xs_hbm, sem.at[1 - slot], tot[1 - slot])


def _dispatch(ends_pad, seg, x2, info, n_slots, tm):
    T, D = x2.shape
    row = lambda i, ends: (i, 0)
    grid_spec = pltpu.PrefetchScalarGridSpec(
        num_scalar_prefetch=1, grid=(T // tm,),
        in_specs=[pl.BlockSpec((1, 4, LANES), lambda i, ends: (i, 0, 0), memory_space=pltpu.SMEM),
                  pl.BlockSpec((tm, D), row), pl.BlockSpec((LANES, tm), lambda i, ends: (0, i))],
        out_specs=pl.BlockSpec(memory_space=pl.ANY),
        scratch_shapes=[pltpu.VMEM((2, TOP_K * tm + N_EXPERTS * SEG_ALIGN, D), F32), pltpu.SMEM((2,), jnp.int32),
                        pltpu.SemaphoreType.DMA((2,))])
    return pl.pallas_call(_dispatch_kernel, name="dispatch", grid_spec=grid_spec,
                          out_shape=jax.ShapeDtypeStruct((n_slots, D), F32),
                          compiler_params=_cparams())(ends_pad, seg, x2, info)


def _moe_kernel(bexp_ref, nused_ref, nxt_ref, wslot_ref, x_ref, wgu_hbm, bgu_ref, wd_hbm, bd_ref, y_ref,
                wgu_f, wd_f, wgu_s, wd_s, wsem):
    j = pl.program_id(0)
    n_used = nused_ref[0]
    n_tiles = wgu_s.shape[1] // GU_TILE

    def weight_copies(e, s):
        return (pltpu.make_async_copy(wgu_hbm.at[e], wgu_f.at[s], wsem.at[s]),
                pltpu.make_async_copy(wd_hbm.at[e], wd_f.at[s], wsem.at[s]))

    @pl.when(j >= n_used)
    def _():
        y_ref[...] = jnp.zeros_like(y_ref)

    @pl.when(j < n_used)
    def _():
        @pl.when((j == 0) | (bexp_ref[j] != bexp_ref[jnp.maximum(j - 1, 0)]))
        def _():
            e, s = bexp_ref[j], wslot_ref[j]

            @pl.when(j == 0)
            def _():
                for cp in weight_copies(e, s):
                    cp.start()

            for cp in weight_copies(e, s):
                cp.wait()
            k = lax.broadcasted_iota(jnp.int32, (GU_TILE, GU_TILE), 0)
            n = lax.broadcasted_iota(jnp.int32, (GU_TILE, GU_TILE), 1)
            perm = jnp.where(k == jnp.where(n < LANES, 2 * n, 2 * (n - LANES) + 1), 1.0, 0.0).astype(BF16)
            for t in range(n_tiles):
                cols = slice(t * GU_TILE, (t + 1) * GU_TILE)
                wgu_s[:, cols] = _dot(wgu_f[s, :, cols].astype(BF16), perm).astype(BF16)
            wd_s[...] = wd_f[s].astype(BF16)
            nxt = nxt_ref[j]

            @pl.when(nxt >= 0)
            def _():
                for cp in weight_copies(nxt, 1 - s):
                    cp.start()

        xb = x_ref[...].astype(BF16)
        hs = []
        for t in range(n_tiles):
            cols = slice(t * GU_TILE, (t + 1) * GU_TILE)
            gu = _dot(xb, wgu_s[:, cols]) + bgu_ref[0, :, cols]
            gate = jnp.minimum(gu[:, :LANES], SWIGLU_LIMIT)
            lin = jnp.clip(gu[:, LANES:], -SWIGLU_LIMIT, SWIGLU_LIMIT)
            hs.append((gate * jax.nn.sigmoid(SWIGLU_ALPHA * gate) * (lin + 1.0)).astype(BF16))
        y_ref[...] = _dot(jnp.concatenate(hs, axis=1), wd_s[...]) + bd_ref[0]


def _moe(bexp, n_used, nxt, wslot, xs, wgu, bgu, wd, bd):
    n_blocks = bexp.shape[0]
    D = xs.shape[1]
    F2 = wgu.shape[2]
    F = wd.shape[1]
    wmap = lambda j, be, nu, nx, ws: (be[jnp.minimum(j, nu[0] - 1)], 0, 0)
    blk = lambda j, be, nu, nx, ws: (j, 0)
    used = lambda j, be, nu, nx, ws: (jnp.minimum(j, nu[0] - 1), 0)
    in_specs = [pl.BlockSpec((MOE_BLOCK, D), used),
                pl.BlockSpec(memory_space=pl.ANY), pl.BlockSpec((1, 1, F2), wmap),
                pl.BlockSpec(memory_space=pl.ANY), pl.BlockSpec((1, 1, D), wmap)]
    grid_spec = pltpu.PrefetchScalarGridSpec(
        num_scalar_prefetch=4, grid=(n_blocks,), in_specs=in_specs,
        out_specs=pl.BlockSpec((MOE_BLOCK, D), blk),
        scratch_shapes=[pltpu.VMEM((2, D, F2), F32), pltpu.VMEM((2, F, D), F32),
                        pltpu.VMEM((D, F2), BF16), pltpu.VMEM((F, D), BF16), pltpu.SemaphoreType.DMA((2,))])
    return pl.pallas_call(_moe_kernel, name="moe", grid_spec=grid_spec,
                          out_shape=jax.ShapeDtypeStruct((n_blocks * MOE_BLOCK, D), F32),
                          compiler_params=_cparams())(bexp, n_used, nxt, wslot, xs, wgu, bgu, wd, bd)


def _combine_kernel(seg_ref, seg_nxt_ref, ys_hbm, info_ref, x_ref, g_ref, b_ref, out_ref, comp, sem, *, alpha):
    i = pl.program_id(0)
    n_steps = pl.num_programs(0)
    slot = i % 2
    tm = x_ref.shape[0]
    cap = comp.shape[1]

    @pl.when(i == 0)
    def _():
        comp[...] = jnp.zeros_like(comp)
        _segment_copies(seg_ref, comp.at[0], ys_hbm, sem.at[0], to_global=False)

    @pl.when(i + 1 < n_steps)
    def _():
        _segment_copies(seg_nxt_ref, comp.at[1 - slot], ys_hbm, sem.at[1 - slot], to_global=False)

    _wait_rows(comp.at[slot], ys_hbm, sem.at[slot], seg_ref[0, 3, 0])
    info = info_ref[...].T
    lane = lax.broadcasted_iota(jnp.int32, (tm, cap), 1).astype(F32)
    wmat = jnp.zeros((tm, cap), F32)
    for k in range(TOP_K):
        wmat = jnp.where(lane == info[:, k:k + 1], info[:, TOP_K + k:TOP_K + k + 1], wmat)
    ffn = _dot(wmat.astype(BF16), comp[slot].astype(BF16))
    out_ref[...] = _layer_norm(alpha * x_ref[...] + ffn, g_ref[...], b_ref[...])


def _combine(seg, ys, info, x2, g, b, alpha, tm):
    T, D = x2.shape
    n_t = T // tm
    row = lambda i: (i, 0)
    seg_spec = lambda m: pl.BlockSpec((1, 4, LANES), m, memory_space=pltpu.SMEM)
    in_specs = [seg_spec(lambda i: (i, 0, 0)), seg_spec(lambda i: (jnp.minimum(i + 1, n_t - 1), 0, 0)),
                pl.BlockSpec(memory_space=pl.ANY),
                pl.BlockSpec((LANES, tm), lambda i: (0, i)), pl.BlockSpec((tm, D), row),
                _full_spec(g.shape), _full_spec(b.shape)]
    cap = TOP_K * tm + N_EXPERTS * SEG_ALIGN
    return pl.pallas_call(functools.partial(_combine_kernel, alpha=alpha), name="combine", grid=(n_t,), in_specs=in_specs,
                          out_specs=pl.BlockSpec((tm, D), row),
                          out_shape=jax.ShapeDtypeStruct((T, D), F32),
                          scratch_shapes=[pltpu.VMEM((2, cap, D), F32), pltpu.SemaphoreType.DMA((2,))],
                          compiler_params=_cparams())(seg, seg, ys, info, x2, g, b)


def _tile(n, pref):
    return pref if n % pref == 0 else n


def _layer(x2d, mem2d, pos2d, B, S, depth, w_in, sinks, w_attn_o, lam_re, lam_im, log_dt, b_re, b_im,
           c_re, c_im, d_skip, w_glu_a, w_glu_b, w_out, ln1_g, ln1_b, wq_c, wk_c, wv_c, wo_c, ln2_g,
           ln2_b, w_router, b_router, w_gate_up, b_gate_up, w_down, b_down, ln3_g, ln3_b):
    T, D = x2d.shape
    alpha = (2 * depth) ** 0.25
    rep = N_Q_HEADS // N_KV_HEADS
    q_w = N_Q_HEADS * HEAD_DIM
    kv_w = N_KV_HEADS * HEAD_DIM
    ssm_w = SSM_GROUP * SSM_GROUPS

    o_k, o_v, o_s = q_w, q_w + kv_w, q_w + 2 * kv_w
    o_ga, o_gs = o_s + ssm_w, o_s + ssm_w + D
    wq = w_in[:, :o_k].reshape(D, N_KV_HEADS, rep, HEAD_DIM).transpose(0, 2, 1, 3).reshape(D, q_w).astype(BF16)
    wk = w_in[:, o_k:o_v].astype(BF16)
    wv = w_in[:, o_v:o_s].astype(BF16)
    wu = w_in[:, o_s:o_ga].astype(BF16)
    wga = w_in[:, o_ga:o_gs].astype(BF16)
    wgs = w_in[:, o_gs:].astype(BF16)
    wo_attn = w_attn_o.reshape(N_KV_HEADS, rep, HEAD_DIM, D).transpose(1, 0, 2, 3).reshape(q_w, D).astype(BF16)
    half = HEAD_DIM // 2
    inv_freq = jnp.power(ROPE_THETA, -jnp.arange(half, dtype=F32) / half)
    invf = jnp.tile(inv_freq, LANES // half)[None, :]
    sink_rows = jnp.repeat(sinks.astype(F32).reshape(N_KV_HEADS, rep), WINDOW, axis=1)
    sink_cols = jnp.full((N_KV_HEADS, rep * WINDOW, 2 * WINDOW), NEG_BIG, F32).at[:, :, 0].set(sink_rows)

    q2, k, v, ut, ga, gs = _inproj(x2d, pos2d, invf, wq, wk, wv, wu, wga, wgs, _tile(T, ROW_TILE))
    att = _swa(q2, k, v, sink_cols, wo_attn, ga, B, S, _tile(S, ROW_TILE))

    lam_row = jnp.stack([jnp.concatenate([lam_re, lam_re], -1), jnp.concatenate([lam_im, lam_im], -1)], 1)
    lam_col = jnp.swapaxes(lam_row, 1, 2)
    bt_re = jnp.swapaxes(b_re, 1, 2)
    bt_im = jnp.swapaxes(b_im, 1, 2)
    bt_a = jnp.concatenate([bt_re, bt_im], -1)
    bt_b = jnp.concatenate([-bt_im, bt_re], -1)
    ct_re = jnp.swapaxes(c_re, 1, 2)
    ct_im = jnp.swapaxes(c_im, 1, 2)
    ct_a = jnp.concatenate([ct_re, -ct_im], 1)
    ct_b = jnp.concatenate([-ct_im, -ct_re], 1)
    dsk = jnp.repeat(d_skip.reshape(SSM_GROUPS, SSM_GROUP), SSM_GROUP, axis=1)[:, :, None]
    yt = _ssm(ut, lam_row, lam_col, log_dt.reshape(SSM_GROUPS, 1, 1), bt_a, bt_b, c_re, c_im, ct_a, ct_b, dsk,
              S // LANES)

    x1 = _mix(yt, att, gs, x2d, w_glu_a.astype(BF16), w_glu_b.astype(BF16), w_out.astype(BF16),
              ln1_g[None, :], ln1_b[None, :], alpha, _tile(T, ROW_TILE))

    kc, vc = _memkv(mem2d, wk_c.astype(BF16), wv_c.astype(BF16), _tile(mem2d.shape[0], MEM_TILE))
    wr = jnp.zeros((LANES, D), F32).at[:N_EXPERTS, :].set(w_router.T).astype(BF16)
    br = jnp.full((LANES, 1), NEG_BIG, F32).at[:N_EXPERTS, 0].set(b_router)
    x2, logits = _cross(x1, kc, vc, wq_c.astype(BF16), wo_c.astype(BF16), ln2_g[None, :], ln2_b[None, :],
                        wr, br, alpha, S, _tile(S, ROW_TILE))

    tm_r = _tile(T, ROUTE_TILE)
    n_tiles = T // tm_r
    info, segf, counts = _route(logits, tm_r)
    total = counts[:, 0].astype(jnp.int32)
    padded = (total + MOE_BLOCK - 1) // MOE_BLOCK * MOE_BLOCK
    ends_pad = jnp.cumsum(padded)
    start_pad = ends_pad - padded
    n_blocks = -(-(T * TOP_K + n_tiles * N_EXPERTS * (SEG_ALIGN - 1)) // MOE_BLOCK) + N_EXPERTS
    segi = segf.astype(jnp.int32)
    seg_n, seg_off, seg_start = segi[:, :, 0], segi[:, :, 1], segi[:, :, 2] + start_pad[None, :]
    seg_tot = jnp.broadcast_to(jnp.sum(seg_n, axis=1, keepdims=True), seg_n.shape)
    seg = jnp.stack([seg_n, seg_off, seg_start, seg_tot], axis=1)
    seg = jnp.pad(seg, ((0, 0), (0, 0), (0, LANES - N_EXPERTS)))
    block_start = jnp.arange(n_blocks, dtype=jnp.int32) * MOE_BLOCK
    bexp = jnp.minimum(jnp.sum(block_start[:, None] >= ends_pad[None, :], axis=1), N_EXPERTS - 1).astype(jnp.int32)
    n_used = (ends_pad[-1] // MOE_BLOCK).astype(jnp.int32)[None]
    n_gu_tiles = b_gate_up.shape[1] // GU_TILE
    bgu = b_gate_up.reshape(N_EXPERTS, n_gu_tiles, LANES, 2).transpose(0, 1, 3, 2).reshape(N_EXPERTS, 1, -1)
    xs = _dispatch(ends_pad, seg, x2, info, n_blocks * MOE_BLOCK, tm_r)
    has_rows = padded > 0
    eid = jnp.arange(N_EXPERTS, dtype=jnp.int32)
    later = jnp.min(jnp.where(has_rows[None, :] & (eid[None, :] > bexp[:, None]), eid[None, :], N_EXPERTS), axis=1)
    nxt = jnp.where(later < N_EXPERTS, later, -1).astype(jnp.int32)
    wslot = ((jnp.sum(has_rows[None, :] & (eid[None, :] <= bexp[:, None]), axis=1) - 1) % 2).astype(jnp.int32)
    ys = _moe(bexp, n_used, nxt, wslot, xs, w_gate_up, bgu, w_down, b_down[:, None, :])
    return _combine(seg, ys, info, x2, ln3_g[None, :], ln3_b[None, :], alpha, tm_r)


def kernel(x, mem, positions, w_in, sinks, w_attn_o, lam_re, lam_im, log_dt, b_re, b_im, c_re, c_im, d_skip,
           w_glu_a, w_glu_b, w_out, ln1_g, ln1_b, wq_c, wk_c, wv_c, wo_c, ln2_g, ln2_b, w_router, b_router,
           w_gate_up, b_gate_up, w_down, b_down, ln3_g, ln3_b):
    B, S, D = x.shape
    depth = w_in.shape[0]
    x2d = x.reshape(B * S, D)
    mem2d = mem.reshape(-1, D)
    pos2d = positions.reshape(B * S, 1)
    per_layer = (w_in, sinks, w_attn_o, lam_re, lam_im, log_dt, b_re, b_im, c_re, c_im, d_skip, w_glu_a,
                 w_glu_b, w_out, ln1_g, ln1_b, wq_c, wk_c, wv_c, wo_c, ln2_g, ln2_b, w_router, b_router,
                 w_gate_up, b_gate_up, w_down, b_down, ln3_g, ln3_b)
    for l in range(depth):
        x2d = _layer(x2d, mem2d, pos2d, B, S, depth, *(w[l] for w in per_layer))
    return x2d.reshape(B, S, D)
```
